```python
import math
import jax, jax.numpy as jnp
from jax import lax
import numpy as np

D_MODEL = 1024
BATCH = 8
SEQ = 4096
DEPTH = 1

D_MIX = D_MODEL
D_S5 = D_MIX // 2
D_RWKV = D_MIX - D_S5
S5_GROUP = 16
S5_GROUPS = D_S5 // S5_GROUP
S5_STATE = 64
RWKV_HEAD = 64
RWKV_HEADS = D_RWKV // RWKV_HEAD
DECAY_RANK = 64
ICLR_RANK = 64
D_RWKV_SHIFT = 3 * D_RWKV + DECAY_RANK + ICLR_RANK
D_IN = 2 * D_S5 + D_RWKV_SHIFT + D_RWKV
NORM_EPS = 1e-6
GN_EPS = 64e-5
DT_MIN = 1e-3
DT_MAX = 1e-1

kernel_name = "hymba_s5_rwkv7_hybrid_block"


def rms_norm(x, g):
    xf = x.astype(jnp.float32)
    y = xf * lax.rsqrt(jnp.mean(xf * xf, axis=-1, keepdims=True) + NORM_EPS)
    return (y * g.astype(jnp.float32)).astype(x.dtype)


def token_shift(h, mu):
    h_prev = jnp.pad(h, ((0, 0), (1, 0), (0, 0)))[:, :-1]
    return h + (h_prev - h) * mu


def s5_branch(u, lam_re, lam_im, log_dt, b_re, b_im, c_re, c_im, d, glu_w, glu_b):
    bsz, seqlen, _ = u.shape
    f32 = jnp.float32
    ug = u.reshape(bsz, seqlen, S5_GROUPS, S5_GROUP)
    lam = lax.complex(lam_re.astype(f32), lam_im.astype(f32))
    dt = jnp.exp(log_dt.astype(f32))[:, None]
    lam_bar = jnp.exp(lam * dt)
    b_mat = lax.complex(b_re.astype(f32), b_im.astype(f32))
    b_bar = ((lam_bar - 1.0) / lam)[..., None] * b_mat
    bu = jnp.einsum('blgh,gph->blgp', ug.astype(jnp.complex64), b_bar)
    a = jnp.broadcast_to(lam_bar, bu.shape)

    def combine(left, right):
        a_l, b_l = left
        a_r, b_r = right
        return a_r * a_l, a_r * b_l + b_r

    _, states = lax.associative_scan(combine, (a, bu), axis=1)
    c_mat = lax.complex(c_re.astype(f32), c_im.astype(f32))
    y = jnp.einsum('blgp,ghp->blgh', states, c_mat).real.reshape(bsz, seqlen, D_S5)
    y = y + d.astype(f32) * u
    y = jax.nn.gelu(y)
    return y * jax.nn.sigmoid(y @ glu_w.astype(f32) + glu_b.astype(f32))


def rwkv7_branch(r, k, v, wl, al, w0, w2, a0, a2, k_k, k_a, r_k, ln_w, ln_b):
    bsz, seqlen, _ = r.shape
    f32 = jnp.float32
    w = -jax.nn.softplus(-(w0.astype(f32) + jnp.tanh(wl) @ w2.astype(f32))) - 0.5
    decay = jnp.exp(-jnp.exp(w))
    a = jax.nn.sigmoid(a0.astype(f32) + al @ a2.astype(f32))
    heads = lambda t: t.reshape(bsz, seqlen, RWKV_HEADS, RWKV_HEAD)
    kk = heads(k * k_k.astype(f32))
    kk = kk / jnp.maximum(jnp.linalg.norm(kk, axis=-1, keepdims=True), 1e-12)
    k = k * (1.0 + (a - 1.0) * k_a.astype(f32))
    rh, kh, vh, wh, ah = heads(r), heads(k), heads(v), heads(decay), heads(a)
    tm = lambda t: jnp.moveaxis(t, 1, 0)
    xs = (tm(rh), tm(wh), tm(kh), tm(vh), tm(-kk), tm(kk * ah))

    def step(state, inp):
        r_t, w_t, k_t, v_t, a_t, b_t = inp
        sa = jnp.einsum('bhvk,bhk->bhv', state, a_t)
        state = (state * w_t[:, :, None, :]
                 + sa[..., None] * b_t[:, :, None, :]
                 + v_t[..., None] * k_t[:, :, None, :])
        return state, jnp.einsum('bhvk,bhk->bhv', state, r_t)

    s0 = jnp.zeros((bsz, RWKV_HEADS, RWKV_HEAD, RWKV_HEAD), f32)
    _, ys = lax.scan(step, s0, xs)
    y = jnp.moveaxis(ys, 0, 1)
    mean = jnp.mean(y, axis=-1, keepdims=True)
    var = jnp.mean(jnp.square(y - mean), axis=-1, keepdims=True)
    y = ((y - mean) * lax.rsqrt(var + GN_EPS)).reshape(bsz, seqlen, D_RWKV)
    y = y * ln_w.astype(f32) + ln_b.astype(f32)
    bonus = jnp.sum(rh * kh * r_k.astype(f32), axis=-1, keepdims=True) * vh
    return y + bonus.reshape(bsz, seqlen, D_RWKV)


def setup_inputs(seed: int = 0) -> dict:
    key = jax.random.key(seed)
    ks = jax.random.split(key, 26)
    f32 = jnp.float32
    L = DEPTH
    G, P, H = S5_GROUPS, S5_STATE, S5_GROUP

    def nrm(k, shape, s):
        return jax.random.normal(k, shape, f32) * s

    x = nrm(ks[0], (BATCH, SEQ, D_MODEL), 1.0)
    norm_g = 1.0 + nrm(ks[1], (L, D_MODEL), 0.02)
    w_in = nrm(ks[2], (L, D_MODEL, D_IN), D_MODEL ** -0.5)
    n = jnp.arange(S5_STATE, dtype=f32)
    s5_lam_re = -0.5 * jnp.exp(nrm(ks[3], (L, G, P), 0.05))
    s5_lam_im = math.pi * n + nrm(ks[4], (L, G, P), 0.01)
    s5_log_dt = jax.random.uniform(ks[5], (L, G), f32, math.log(DT_MIN), math.log(DT_MAX))
    s5_b_re = nrm(ks[6], (L, G, P, H), (2 * H) ** -0.5)
    s5_b_im = nrm(ks[7], (L, G, P, H), (2 * H) ** -0.5)
    s5_c_re = nrm(ks[8], (L, G, H, P), (2 * P) ** -0.5)
    s5_c_im = nrm(ks[9], (L, G, H, P), (2 * P) ** -0.5)
    s5_d = nrm(ks[10], (L, D_S5), 0.5)
    s5_glu_w = nrm(ks[11], (L, D_S5, D_S5), D_S5 ** -0.5)
    s5_glu_b = nrm(ks[12], (L, D_S5), 0.01)
    rwkv_mu = jax.random.uniform(ks[13], (L, D_RWKV_SHIFT), f32)
    rwkv_w0 = jnp.linspace(-5.0, -0.5, D_RWKV, dtype=f32)[None] + nrm(ks[14], (L, D_RWKV), 0.1)
    rwkv_w2 = nrm(ks[15], (L, DECAY_RANK, D_RWKV), 0.1 * DECAY_RANK ** -0.5)
    rwkv_a0 = nrm(ks[16], (L, D_RWKV), 0.1)
    rwkv_a2 = nrm(ks[17], (L, ICLR_RANK, D_RWKV), 0.1 * ICLR_RANK ** -0.5)
    rwkv_k_k = 0.85 + nrm(ks[18], (L, D_RWKV), 0.02)
    rwkv_k_a = 1.0 + nrm(ks[19], (L, D_RWKV), 0.02)
    rwkv_r_k = nrm(ks[20], (L, RWKV_HEADS, RWKV_HEAD), 0.1)
    rwkv_ln_w = 1.0 + nrm(ks[21], (L, D_RWKV), 0.02)
    rwkv_ln_b = nrm(ks[22], (L, D_RWKV), 0.01)
    w_out = nrm(ks[23], (L, D_MIX, D_MODEL), D_MIX ** -0.5)
    final_g = 1.0 + nrm(ks[24], (D_MODEL,), 0.02)
    return {"x": x, "norm_g": norm_g, "w_in": w_in,
            "s5_lam_re": s5_lam_re, "s5_lam_im": s5_lam_im, "s5_log_dt": s5_log_dt,
            "s5_b_re": s5_b_re, "s5_b_im": s5_b_im, "s5_c_re": s5_c_re, "s5_c_im": s5_c_im,
            "s5_d": s5_d, "s5_glu_w": s5_glu_w, "s5_glu_b": s5_glu_b,
            "rwkv_mu": rwkv_mu, "rwkv_w0": rwkv_w0, "rwkv_w2": rwkv_w2,
            "rwkv_a0": rwkv_a0, "rwkv_a2": rwkv_a2, "rwkv_k_k": rwkv_k_k, "rwkv_k_a": rwkv_k_a,
            "rwkv_r_k": rwkv_r_k, "rwkv_ln_w": rwkv_ln_w, "rwkv_ln_b": rwkv_ln_b,
            "w_out": w_out, "final_g": final_g}


def reference(x, norm_g, w_in, s5_lam_re, s5_lam_im, s5_log_dt, s5_b_re, s5_b_im, s5_c_re, s5_c_im,
              s5_d, s5_glu_w, s5_glu_b, rwkv_mu, rwkv_w0, rwkv_w2, rwkv_a0, rwkv_a2, rwkv_k_k, rwkv_k_a,
              rwkv_r_k, rwkv_ln_w, rwkv_ln_b, w_out, final_g):
    f32 = jnp.float32
    for l in range(DEPTH):
        h = rms_norm(x, norm_g[l])
        proj = (h @ w_in[l]).astype(f32)
        u, z_s5, rw, z_rwkv = jnp.split(
            proj, [D_S5, 2 * D_S5, 2 * D_S5 + D_RWKV_SHIFT], axis=-1)
        y_s5 = s5_branch(u, s5_lam_re[l], s5_lam_im[l], s5_log_dt[l], s5_b_re[l], s5_b_im[l],
                         s5_c_re[l], s5_c_im[l], s5_d[l], s5_glu_w[l], s5_glu_b[l])
        y_s5 = y_s5 * jax.nn.silu(z_s5)
        rw = token_shift(rw, rwkv_mu[l].astype(f32))
        r, k, v, wl, al = jnp.split(
            rw, [D_RWKV, 2 * D_RWKV, 3 * D_RWKV, 3 * D_RWKV + DECAY_RANK], axis=-1)
        y_rwkv = rwkv7_branch(r, k, v, wl, al, rwkv_w0[l], rwkv_w2[l], rwkv_a0[l], rwkv_a2[l],
                              rwkv_k_k[l], rwkv_k_a[l], rwkv_r_k[l], rwkv_ln_w[l], rwkv_ln_b[l])
        y_rwkv = y_rwkv * jax.nn.silu(z_rwkv)
        y = jnp.concatenate([y_s5, y_rwkv], axis=-1)
        x = (x.astype(f32) + y @ w_out[l].astype(f32)).astype(x.dtype)
    return rms_norm(x, final_g)
```

```python
import functools
import math

import jax
import jax.numpy as jnp
from jax import lax
from jax.experimental import pallas as pl
from jax.experimental.pallas import tpu as pltpu

F32 = jnp.float32
BF16 = jnp.bfloat16

D_MODEL = 1024
D_S5 = 512
D_RWKV = 512
S5_GROUP = 16
S5_GROUPS = 32
S5_STATE = 64
RWKV_HEAD = 64
LOW_RANK = 64
D_SHIFT = 3 * D_RWKV + 2 * LOW_RANK
D_IN = 2 * D_S5 + D_SHIFT + D_RWKV
NORM_EPS = 1e-6
GN_EPS = 64e-5

LANES = 128
TM_IN = 256
TM_OUT = 512
S5_CHUNK = 16
S5_LANE_BLOCKS = D_S5 // LANES
S5_GROUPS_PER_BLOCK = LANES // S5_GROUP
S5_BLOCK_STATE = 2 * S5_GROUPS_PER_BLOCK * S5_STATE
RWKV_CHUNK = 64
RWKV_TB = 512
HEAD_PAIRS = D_RWKV // LANES
VMEM_LIMIT = 56 * 1024 * 1024


def _dot(a, b):
    return jnp.dot(a, b, preferred_element_type=F32)


def _dot_nt(a, b):
    return lax.dot_general(a, b, (((1,), (1,)), ((), ())), preferred_element_type=F32)


def _dot_tn(a, b):
    return lax.dot_general(a, b, (((0,), (0,)), ((), ())), preferred_element_type=F32)


def _split3(x):
    hi = x.astype(BF16)
    r1 = x - hi.astype(F32)
    mid = r1.astype(BF16)
    lo = (r1 - mid.astype(F32)).astype(BF16)
    return hi, mid, lo


def _dot_exact_rhs(x, m):
    hi, mid, lo = _split3(x)
    return _dot(hi, m) + _dot(mid, m) + _dot(lo, m)


def _dot_exact_lhs(m, x):
    hi, mid, lo = _split3(x)
    return _dot(m, hi) + _dot(m, mid) + _dot(m, lo)


def _sigmoid(x):
    return 1.0 / (1.0 + jnp.exp(-x))


def _head_ones():
    r = lax.broadcasted_iota(jnp.int32, (LANES, LANES), 0)
    c = lax.broadcasted_iota(jnp.int32, (LANES, LANES), 1)
    return (r < RWKV_HEAD) == (c < RWKV_HEAD)


def _inproj_kernel(x_ref, g_ref, win_ref, mu_ref, w2a2_ref, w0_ref, a0_ref, kk_ref, ka_ref,
                   u_ref, gate_ref, r_ref, lw_ref, k_ref, v_ref, kkn_ref, b_ref, carry_ref):
    t = pl.program_id(1)

    @pl.when(t == 0)
    def _():
        carry_ref[...] = jnp.zeros_like(carry_ref)

    x = x_ref[0]
    ms = jnp.mean(x * x, axis=-1, keepdims=True)
    h = (x * lax.rsqrt(ms + NORM_EPS) * g_ref[...]).astype(BF16)

    def proj(lo, hi):
        return _dot(h, win_ref[:, lo:hi])

    u_ref[0] = proj(0, D_S5)
    z5 = proj(D_S5, 2 * D_S5)
    gate_ref[0, :, :D_S5] = z5 * _sigmoid(z5)
    zr = proj(2 * D_S5 + D_SHIFT, D_IN)
    gate_ref[0, :, D_S5:] = zr * _sigmoid(zr)

    rw = proj(2 * D_S5, 2 * D_S5 + D_SHIFT)
    prev = pltpu.roll(rw, 1, axis=0)
    row = lax.broadcasted_iota(jnp.int32, rw.shape, 0)
    prev = jnp.where(row == 0, carry_ref[...], prev)
    carry_ref[...] = rw[TM_IN - 1:TM_IN, :]
    sh = rw + (prev - rw) * mu_ref[...]

    r = sh[:, 0:D_RWKV]
    k = sh[:, D_RWKV:2 * D_RWKV]
    v = sh[:, 2 * D_RWKV:3 * D_RWKV]
    wa = sh[:, 3 * D_RWKV:]
    lane = lax.broadcasted_iota(jnp.int32, wa.shape, 1)
    wa = jnp.where(lane < LOW_RANK, jnp.tanh(wa), wa)
    za = _dot(wa.astype(BF16), w2a2_ref[...])
    zw = w0_ref[...] + za[:, :D_RWKV]
    softplus = jnp.maximum(-zw, 0.0) + jnp.log(1.0 + jnp.exp(-jnp.abs(zw)))
    lw = -jnp.exp(-softplus - 0.5)
    iclr = _sigmoid(a0_ref[...] + za[:, D_RWKV:])

    kk = k * kk_ref[...]
    ones = _head_ones().astype(BF16)
    sq = kk * kk
    ss = jnp.concatenate(
        [_dot_exact_rhs(sq[:, LANES * p:LANES * (p + 1)], ones) for p in range(HEAD_PAIRS)], axis=1)
    kkn = kk / jnp.maximum(jnp.sqrt(ss), 1e-12)

    r_ref[0] = r
    lw_ref[0] = lw
    k_ref[0] = k * (1.0 + (iclr - 1.0) * ka_ref[...])
    v_ref[0] = v
    kkn_ref[0] = kkn
    b_ref[0] = kkn * iclr


def _inproj(x, norm_g, w_in, mu, w2a2, w0, a0, k_k, k_a):
    bsz, seqlen, _ = x.shape
    tok = lambda w: pl.BlockSpec((1, TM_IN, w), lambda b, t: (b, t, 0))
    full = lambda a: pl.BlockSpec(a.shape, lambda b, t: (0,) * a.ndim)
    out = lambda w: jax.ShapeDtypeStruct((bsz, seqlen, w), F32)
    params = (norm_g, w_in, mu, w2a2, w0, a0, k_k, k_a)
    return pl.pallas_call(
        _inproj_kernel,
        grid=(bsz, seqlen // TM_IN),
        in_specs=[tok(D_MODEL)] + [full(a) for a in params],
        out_specs=[tok(D_S5), tok(2 * D_S5)] + [tok(D_RWKV)] * 6,
        out_shape=[out(D_S5), out(2 * D_S5)] + [out(D_RWKV)] * 6,
        scratch_shapes=[pltpu.VMEM((1, D_SHIFT), F32)],
        compiler_params=pltpu.CompilerParams(
            dimension_semantics=("arbitrary", "arbitrary"), vmem_limit_bytes=VMEM_LIMIT),
        name="inproj",
    )(x, *params)


def _s5_kernel(u_ref, ein_ref, toep_ref, fout_ref, lam_ref, d_ref, y_ref, e_ref, sp_ref):
    nch = e_ref.shape[0]
    half = S5_BLOCK_STATE // 2
    xs = [u_ref[0, pl.ds(j, nch, stride=S5_CHUNK), :] for j in range(S5_CHUNK)]
    xcat = jnp.concatenate([x.astype(BF16) for x in xs], axis=1)
    e_ref[...] = _dot(xcat, ein_ref[0])

    lre = lam_ref[0, :, :half]
    lim = lam_ref[0, :, half:]

    def step(c, s):
        sre, sim = s
        sp_ref[pl.ds(c, 1), :half] = sre
        sp_ref[pl.ds(c, 1), half:] = sim
        e = e_ref[pl.ds(c, 1), :]
        return (lre * sre - lim * sim + e[:, :half], lre * sim + lim * sre + e[:, half:])

    zero = jnp.zeros((1, half), F32)
    lax.fori_loop(0, nch, step, (zero, zero))

    sp = sp_ref[...].astype(BF16)
    d = d_ref[...]
    pairs = S5_CHUNK // 2
    for tp in range(pairs):
        y = _dot(xcat[:, :2 * LANES * (tp + 1)], toep_ref[0, 2 * LANES * (pairs - 1 - tp):, :])
        y = y + _dot(sp, fout_ref[0, :, 2 * LANES * tp:2 * LANES * (tp + 1)])
        for i in range(2):
            t = 2 * tp + i
            y_ref[0, pl.ds(t, nch, stride=S5_CHUNK), :] = y[:, LANES * i:LANES * (i + 1)] + d * xs[t]


def _s5_operators(lam_re, lam_im, log_dt, b_re, b_im, c_re, c_im):
    c16, q4, gl = S5_CHUNK, S5_LANE_BLOCKS, S5_GROUPS_PER_BLOCK
    lam = lax.complex(lam_re, lam_im)
    z = lam * jnp.exp(log_dt)[:, None]
    lam_bar = jnp.exp(z)
    b_bar = ((lam_bar - 1.0) / lam)[..., None] * lax.complex(b_re, b_im)
    c_mat = lax.complex(c_re, c_im)
    n = jnp.arange(c16 + 1, dtype=F32)
    pw = jnp.exp(z[None] * n[:, None, None])
    eye = jnp.eye(gl, dtype=F32)

    kt = jnp.einsum('gop,tgp,gpi->tgio', c_mat, pw[:c16], b_bar).real
    kt = kt.reshape(c16, q4, gl, S5_GROUP, S5_GROUP)
    kbd = jnp.einsum('tqgio,gk->tqgiko', kt, eye).reshape(c16, q4, LANES, LANES)
    d8 = jnp.arange(c16 // 2)
    lag = 2 * d8[:, None, None] + jnp.arange(2)[None, None, :] - jnp.arange(2)[None, :, None]
    kp = jnp.where((lag >= 0)[..., None, None, None], kbd[jnp.maximum(lag, 0)], 0.0)
    kp = jnp.transpose(kp, (3, 0, 1, 4, 2, 5)).reshape(q4, c16 // 2, 2 * LANES, 2 * LANES)
    toep = kp[:, ::-1].reshape(q4, c16 * LANES, 2 * LANES)

    ein_c = pw[c16 - 1 - jnp.arange(c16)][..., None] * b_bar[None]
    ein_c = ein_c.reshape(c16, q4, gl, S5_STATE, S5_GROUP)
    ein = jnp.stack([ein_c.real, ein_c.imag])
    ein = jnp.einsum('rjqgpi,gk->qjgirkp', ein, eye).reshape(q4, c16 * LANES, S5_BLOCK_STATE)

    fo_c = c_mat[None] * pw[1:c16 + 1][:, :, None, :]
    fo_c = fo_c.reshape(c16, q4, gl, S5_GROUP, S5_STATE)
    fo = jnp.stack([fo_c.real, -fo_c.imag])
    fo = jnp.einsum('rtqgop,gk->qrgptko', fo, eye).reshape(q4, S5_BLOCK_STATE, c16 * LANES)

    lam16 = pw[c16].reshape(q4, 1, gl * S5_STATE)
    lam16 = jnp.concatenate([lam16.real, lam16.imag], axis=-1)
    return ein.astype(BF16), toep.astype(BF16), fo.astype(BF16), lam16


def _s5(u, ein, toep, fout, lam16, d):
    bsz, seqlen, _ = u.shape
    nch = seqlen // S5_CHUNK
    per_q = lambda a: pl.BlockSpec((1,) + a.shape[1:], lambda q, b: (q,) + (0,) * (a.ndim - 1))
    seq = pl.BlockSpec((1, seqlen, LANES), lambda q, b: (b, 0, q))
    return pl.pallas_call(
        _s5_kernel,
        grid=(S5_LANE_BLOCKS, bsz),
        in_specs=[seq, per_q(ein), per_q(toep), per_q(fout), per_q(lam16),
                  pl.BlockSpec((1, LANES), lambda q, b: (0, q))],
        out_specs=seq,
        out_shape=jax.ShapeDtypeStruct(u.shape, F32),
        scratch_shapes=[pltpu.VMEM((nch, S5_BLOCK_STATE), F32), pltpu.VMEM((nch, S5_BLOCK_STATE), F32)],
        compiler_params=pltpu.CompilerParams(
            dimension_semantics=("arbitrary", "arbitrary"), vmem_limit_bytes=VMEM_LIMIT),
        name="s5",
    )(u, ein, toep, fout, lam16, d)


def _stack_heads(x, lo):
    return jnp.concatenate([jnp.where(lo, x, 0.0), jnp.where(lo, 0.0, x)], axis=0)


def _stack_heads_swapped(x, lo):
    return jnp.concatenate([jnp.where(lo, 0.0, x), jnp.where(lo, x, 0.0)], axis=0)


def _rwkv_kernel(r_ref, lw_ref, k_ref, v_ref, kk_ref, b_ref, gate_ref, rk_ref, lnw_ref, lnb_ref,
                 y_ref, s_ref):
    tc = RWKV_CHUNK

    @pl.when(pl.program_id(1) == 0)
    def _():
        s_ref[...] = jnp.zeros_like(s_ref)

    row = lax.broadcasted_iota(jnp.int32, (tc, LANES), 0)
    lane = lax.broadcasted_iota(jnp.int32, (tc, LANES), 1)
    col = jnp.bitwise_and(lane, RWKV_HEAD - 1)
    lo = lane < RWKV_HEAD
    stril = row > col
    tril = row >= col
    eye_pk = (row == col).astype(F32)
    head_bd = _head_ones()
    ones = head_bd.astype(BF16)
    tri = (lax.broadcasted_iota(jnp.int32, (tc, tc), 0)
           >= lax.broadcasted_iota(jnp.int32, (tc, tc), 1)).astype(BF16)

    def chunk(c, carry):
        rows = pl.ds(pl.multiple_of(c * tc, tc), tc)
        lw_all = lw_ref[0, rows, :]
        g_all = _dot_exact_lhs(tri, lw_all)
        for p in range(HEAD_PAIRS):
            ls = slice(LANES * p, LANES * (p + 1))
            r = r_ref[0, rows, ls]
            k = k_ref[0, rows, ls]
            v = v_ref[0, rows, ls]
            kk = kk_ref[0, rows, ls]
            b = b_ref[0, rows, ls]
            lw = lw_all[:, ls]
            g = g_all[:, ls]
            g_end = g[tc - 1:tc, :]
            m = 0.5 * g_end
            e_out = jnp.exp(m - g)
            at = (-kk * jnp.exp(g - lw - m)).astype(BF16)
            rt = (r * jnp.exp(g - m)).astype(BF16)
            bt = (b * e_out).astype(BF16)
            kt = (k * e_out).astype(BF16)

            ar = jnp.concatenate([at, rt], axis=0)
            zero = jnp.zeros_like(ar)
            lo2 = jnp.concatenate([lo, lo], axis=0)
            bk = jnp.concatenate([bt, kt], axis=0)
            kb = jnp.concatenate([kt, bt], axis=0)
            o0 = _dot_nt(jnp.where(lo2, ar, zero), bk)
            o1 = _dot_nt(jnp.where(lo2, zero, ar), kb)
            p_ab = jnp.where(stril, jnp.where(lo, o0[:tc], o1[:tc]), 0.0)
            p_ak = jnp.where(stril, jnp.where(lo, o1[:tc], o0[:tc]), 0.0)
            p_rb = jnp.where(tril, jnp.where(lo, o0[tc:], o1[tc:]), 0.0)
            p_rk = jnp.where(tril, jnp.where(lo, o1[tc:], o0[tc:]), 0.0)

            ap = p_ab
            tinv = eye_pk + p_ab
            for _ in range(int(math.log2(tc)) - 1):
                ap = _dot(ap.astype(BF16), _stack_heads(ap, lo).astype(BF16))
                tinv = tinv + _dot(tinv.astype(BF16), _stack_heads(ap, lo).astype(BF16))

            em = jnp.exp(m)
            s0 = s_ref[p]
            s0s = (s0 * em).astype(BF16)
            x = _dot_nt(at, s0s) + _dot(p_ak.astype(BF16), _stack_heads_swapped(v, lo).astype(BF16))
            u = _dot(tinv.astype(BF16), _stack_heads(x, lo).astype(BF16))
            uv = jnp.concatenate([_stack_heads(u, lo), _stack_heads_swapped(v, lo)], axis=0).astype(BF16)
            y = _dot_nt(rt, s0s) + _dot(jnp.concatenate([p_rb, p_rk], axis=1).astype(BF16), uv)
            upd = _dot_tn(jnp.concatenate([u, v], axis=0).astype(BF16), bk)
            s_ref[p] = s0 * (em * em) + jnp.where(head_bd, upd, 0.0) * em

            mean = _dot_exact_rhs(y, ones) * (1.0 / RWKV_HEAD)
            dev = y - mean
            var = _dot_exact_rhs(dev * dev, ones) * (1.0 / RWKV_HEAD)
            yn = dev * lax.rsqrt(var + GN_EPS) * lnw_ref[:, ls] + lnb_ref[:, ls]
            bonus = _dot_exact_rhs(r * k * rk_ref[:, ls], ones) * v
            y_ref[0, rows, ls] = (yn + bonus) * gate_ref[0, rows, ls]
        return carry

    lax.fori_loop(0, RWKV_TB // tc, chunk, 0)


def _rwkv(r, lw, k, v, kkn, b, gate, r_k, ln_w, ln_b):
    bsz, seqlen, _ = r.shape
    tok = pl.BlockSpec((1, RWKV_TB, D_RWKV), lambda bi, t: (bi, t, 0))
    gate_spec = pl.BlockSpec((1, RWKV_TB, D_RWKV), lambda bi, t: (bi, t, 1))
    vec = pl.BlockSpec((1, D_RWKV), lambda bi, t: (0, 0))
    return pl.pallas_call(
        _rwkv_kernel,
        grid=(bsz, seqlen // RWKV_TB),
        in_specs=[tok] * 6 + [gate_spec] + [vec] * 3,
        out_specs=tok,
        out_shape=jax.ShapeDtypeStruct(r.shape, F32),
        scratch_shapes=[pltpu.VMEM((HEAD_PAIRS, LANES, LANES), F32)],
        compiler_params=pltpu.CompilerParams(
            dimension_semantics=("arbitrary", "arbitrary"), vmem_limit_bytes=VMEM_LIMIT),
        name="rwkv7",
    )(r, lw, k, v, kkn, b, gate, r_k, ln_w, ln_b)


def _out_kernel(y5_ref, gate_ref, yr_ref, x_ref, gluw_ref, glub_ref, wout_ref, fg_ref, o_ref):
    y = y5_ref[0]
    y = 0.5 * y * (1.0 + jnp.tanh(math.sqrt(2.0 / math.pi) * (y + 0.044715 * (y * y * y))))
    glu = _sigmoid(_dot(y.astype(BF16), gluw_ref[...]) + glub_ref[...])
    ys = y * glu * gate_ref[0]
    ycat = jnp.concatenate([ys, yr_ref[0]], axis=-1).astype(BF16)
    xn = x_ref[0] + _dot(ycat, wout_ref[...])
    ms = jnp.mean(xn * xn, axis=-1, keepdims=True)
    o_ref[0] = xn * lax.rsqrt(ms + NORM_EPS) * fg_ref[...]


def _outproj(y5, gate, yr, x, glu_w, glu_b, w_out, final_g):
    bsz, seqlen, _ = x.shape
    tok = lambda w: pl.BlockSpec((1, TM_OUT, w), lambda b, t: (b, t, 0))
    full = lambda a: pl.BlockSpec(a.shape, lambda b, t: (0,) * a.ndim)
    return pl.pallas_call(
        _out_kernel,
        grid=(bsz, seqlen // TM_OUT),
        in_specs=[tok(D_S5), tok(D_S5), tok(D_RWKV), tok(D_MODEL),
                  full(glu_w), full(glu_b), full(w_out), full(final_g)],
        out_specs=tok(D_MODEL),
        out_shape=jax.ShapeDtypeStruct(x.shape, F32),
        compiler_params=pltpu.CompilerParams(
            dimension_semantics=("arbitrary", "arbitrary"), vmem_limit_bytes=VMEM_LIMIT),
        name="outproj",
    )(y5, gate, yr, x, glu_w, glu_b, w_out, final_g)


@jax.jit
def _forward(x, norm_g, w_in, s5_lam_re, s5_lam_im, s5_log_dt, s5_b_re, s5_b_im, s5_c_re, s5_c_im,
             s5_d, s5_glu_w, s5_glu_b, rwkv_mu, rwkv_w0, rwkv_w2, rwkv_a0, rwkv_a2, rwkv_k_k, rwkv_k_a,
             rwkv_r_k, rwkv_ln_w, rwkv_ln_b, w_out, final_g):
    assert w_in.shape[0] == 1, "single-layer block"
    for l in range(1):
        zeros = jnp.zeros((LOW_RANK, D_RWKV), F32)
        w2a2 = jnp.concatenate(
            [jnp.concatenate([rwkv_w2[l], zeros], axis=1), jnp.concatenate([zeros, rwkv_a2[l]], axis=1)],
            axis=0).astype(BF16)
        row = lambda a: a.reshape(1, -1).astype(F32)
        u, gate, r, lw, k, v, kkn, b = _inproj(
            x, row(norm_g[l]), w_in[l].astype(BF16), row(rwkv_mu[l]), w2a2, row(rwkv_w0[l]),
            row(rwkv_a0[l]), row(rwkv_k_k[l]), row(rwkv_k_a[l]))
        ein, toep, fout, lam16 = _s5_operators(
            s5_lam_re[l], s5_lam_im[l], s5_log_dt[l], s5_b_re[l], s5_b_im[l], s5_c_re[l], s5_c_im[l])
        y5 = _s5(u, ein, toep, fout, lam16, row(s5_d[l]))
        yr = _rwkv(r, lw, k, v, kkn, b, gate, row(rwkv_r_k[l]), row(rwkv_ln_w[l]), row(rwkv_ln_b[l]))
        x = _outproj(y5, gate, yr, x, s5_glu_w[l].astype(BF16), row(s5_glu_b[l]),
                     w_out[l].astype(BF16), row(final_g))
    return x


def kernel(x, norm_g, w_in, s5_lam_re, s5_lam_im, s5_log_dt, s5_b_re, s5_b_im, s5_c_re, s5_c_im, s5_d, s5_glu_w, s5_glu_b, rwkv_mu, rwkv_w0, rwkv_w2, rwkv_a0, rwkv_a2, rwkv_k_k, rwkv_k_a, rwkv_r_k, rwkv_ln_w, rwkv_ln_b, w_out, final_g):
    return _forward(x, norm_g, w_in, s5_lam_re, s5_lam_im, s5_log_dt, s5_b_re, s5_b_im, s5_c_re, s5_c_im,
                    s5_d, s5_glu_w, s5_glu_b, rwkv_mu, rwkv_w0, rwkv_w2, rwkv_a0, rwkv_a2, rwkv_k_k,
                    rwkv_k_a, rwkv_r_k, rwkv_ln_w, rwkv_ln_b, w_out, final_g)
```

```python
import math

import jax
import jax.numpy as jnp
from jax import lax
from jax.experimental import pallas as pl
from jax.experimental.pallas import tpu as pltpu

F32 = jnp.float32
BF16 = jnp.bfloat16

D_MODEL = 1024
D_S5 = 512
D_RWKV = 512
S5_GROUP = 16
S5_GROUPS = 32
S5_STATE = 64
RWKV_HEAD = 64
LOW_RANK = 64
D_SHIFT = 3 * D_RWKV + 2 * LOW_RANK
D_IN = 2 * D_S5 + D_SHIFT + D_RWKV
NORM_EPS = 1e-6
GN_EPS = 64e-5

LANES = 128
TM_IN = 256
TM_OUT = 512
S5_CHUNK = 16
S5_LANE_BLOCKS = D_S5 // LANES
S5_GROUPS_PER_BLOCK = LANES // S5_GROUP
S5_BLOCK_STATE = 2 * S5_GROUPS_PER_BLOCK * S5_STATE
RWKV_CHUNK = 64
RWKV_TB = 512
RWKV_PREP_CHUNKS = 2
HEAD_PAIRS = D_RWKV // LANES
VMEM_LIMIT = 56 * 1024 * 1024


def _dot(a, b):
    return jnp.dot(a, b, preferred_element_type=F32)


def _dot_nt(a, b):
    return lax.dot_general(a, b, (((1,), (1,)), ((), ())), preferred_element_type=F32)


def _dot_tn(a, b):
    return lax.dot_general(a, b, (((0,), (0,)), ((), ())), preferred_element_type=F32)


def _split2(x):
    hi = x.astype(BF16)
    return hi, (x - hi.astype(F32)).astype(BF16)


def _split3(x):
    hi = x.astype(BF16)
    r1 = x - hi.astype(F32)
    mid = r1.astype(BF16)
    lo = (r1 - mid.astype(F32)).astype(BF16)
    return hi, mid, lo


def _dot_exact_rhs(x, m):
    hi, lo = _split2(x)
    return _dot(hi, m) + _dot(lo, m)


def _dot_exact_lhs(m, x):
    hi, mid, lo = _split3(x)
    return _dot(m, hi) + _dot(m, mid) + _dot(m, lo)


def _sigmoid(x):
    return 1.0 / (1.0 + jnp.exp(-x))


def _head_ones():
    r = lax.broadcasted_iota(jnp.int32, (LANES, LANES), 0)
    c = lax.broadcasted_iota(jnp.int32, (LANES, LANES), 1)
    return (r < RWKV_HEAD) == (c < RWKV_HEAD)


def _inproj_kernel(x_ref, g_ref, win_ref, mu_ref, w2a2_ref, w0_ref, a0_ref, kk_ref, ka_ref,
                   u_ref, gate_ref, r_ref, lw_ref, k_ref, v_ref, kkn_ref, b_ref, carry_ref):
    t = pl.program_id(1)

    @pl.when(t == 0)
    def _():
        carry_ref[...] = jnp.zeros_like(carry_ref)

    x = x_ref[0]
    ms = jnp.mean(x * x, axis=-1, keepdims=True)
    h = (x * lax.rsqrt(ms + NORM_EPS) * g_ref[...]).astype(BF16)

    def proj(lo, hi):
        return _dot(h, win_ref[:, lo:hi])

    u_ref[0] = proj(0, D_S5)
    z5 = proj(D_S5, 2 * D_S5)
    gate_ref[0, :, :D_S5] = z5 * _sigmoid(z5)
    zr = proj(2 * D_S5 + D_SHIFT, D_IN)
    gate_ref[0, :, D_S5:] = zr * _sigmoid(zr)

    rw = proj(2 * D_S5, 2 * D_S5 + D_SHIFT)
    prev = pltpu.roll(rw, 1, axis=0)
    row = lax.broadcasted_iota(jnp.int32, rw.shape, 0)
    prev = jnp.where(row == 0, carry_ref[...], prev)
    carry_ref[...] = rw[TM_IN - 1:TM_IN, :]
    sh = rw + (prev - rw) * mu_ref[...]

    r = sh[:, 0:D_RWKV]
    k = sh[:, D_RWKV:2 * D_RWKV]
    v = sh[:, 2 * D_RWKV:3 * D_RWKV]
    wa = sh[:, 3 * D_RWKV:]
    lane = lax.broadcasted_iota(jnp.int32, wa.shape, 1)
    wa = jnp.where(lane < LOW_RANK, jnp.tanh(wa), wa)
    za = _dot(wa.astype(BF16), w2a2_ref[...])
    zw = w0_ref[...] + za[:, :D_RWKV]
    softplus = jnp.maximum(-zw, 0.0) + jnp.log(1.0 + jnp.exp(-jnp.abs(zw)))
    lw = -jnp.exp(-softplus - 0.5)
    iclr = _sigmoid(a0_ref[...] + za[:, D_RWKV:])

    kk = k * kk_ref[...]
    ones = _head_ones().astype(BF16)
    sq = kk * kk
    ss = jnp.concatenate(
        [_dot_exact_rhs(sq[:, LANES * p:LANES * (p + 1)], ones) for p in range(HEAD_PAIRS)], axis=1)
    kkn = kk / jnp.maximum(jnp.sqrt(ss), 1e-12)

    r_ref[0] = r
    lw_ref[0] = lw
    k_ref[0] = k * (1.0 + (iclr - 1.0) * ka_ref[...])
    v_ref[0] = v
    kkn_ref[0] = kkn
    b_ref[0] = kkn * iclr


def _inproj(x, norm_g, w_in, mu, w2a2, w0, a0, k_k, k_a):
    bsz, seqlen, _ = x.shape
    tok = lambda w: pl.BlockSpec((1, TM_IN, w), lambda b, t: (b, t, 0))
    full = lambda a: pl.BlockSpec(a.shape, lambda b, t: (0,) * a.ndim)
    out = lambda w: jax.ShapeDtypeStruct((bsz, seqlen, w), F32)
    params = (norm_g, w_in, mu, w2a2, w0, a0, k_k, k_a)
    return pl.pallas_call(
        _inproj_kernel,
        grid=(bsz, seqlen // TM_IN),
        in_specs=[tok(D_MODEL)] + [full(a) for a in params],
        out_specs=[tok(D_S5), tok(2 * D_S5)] + [tok(D_RWKV)] * 6,
        out_shape=[out(D_S5), out(2 * D_S5)] + [out(D_RWKV)] * 6,
        scratch_shapes=[pltpu.VMEM((1, D_SHIFT), F32)],
        compiler_params=pltpu.CompilerParams(
            dimension_semantics=("arbitrary", "arbitrary"), vmem_limit_bytes=VMEM_LIMIT),
        name="inproj",
    )(x, *params)


def _s5_kernel(u_ref, ein_ref, toep_ref, fout_ref, lam_ref, d_ref, y_ref, e_ref, sp_ref):
    nch = e_ref.shape[0]
    half = S5_BLOCK_STATE // 2
    xs = [u_ref[0, pl.ds(j, nch, stride=S5_CHUNK), :] for j in range(S5_CHUNK)]
    xcat = jnp.concatenate([x.astype(BF16) for x in xs], axis=1)
    e_ref[...] = _dot(xcat, ein_ref[0])

    lre = lam_ref[0, :, :half]
    lim = lam_ref[0, :, half:]

    def step(c, s):
        sre, sim = s
        sp_ref[pl.ds(c, 1), :half] = sre
        sp_ref[pl.ds(c, 1), half:] = sim
        e = e_ref[pl.ds(c, 1), :]
        return (lre * sre - lim * sim + e[:, :half], lre * sim + lim * sre + e[:, half:])

    zero = jnp.zeros((1, half), F32)
    lax.fori_loop(0, nch, step, (zero, zero))

    sp = sp_ref[...].astype(BF16)
    d = d_ref[...]
    pairs = S5_CHUNK // 2
    for tp in range(pairs):
        y = _dot(xcat[:, :2 * LANES * (tp + 1)], toep_ref[0, 2 * LANES * (pairs - 1 - tp):, :])
        y = y + _dot(sp, fout_ref[0, :, 2 * LANES * tp:2 * LANES * (tp + 1)])
        for i in range(2):
            t = 2 * tp + i
            y_ref[0, pl.ds(t, nch, stride=S5_CHUNK), :] = y[:, LANES * i:LANES * (i + 1)] + d * xs[t]


def _cmul(ar, ai, br, bi):
    return ar * br - ai * bi, ar * bi + ai * br


def _s5_operators(lam_re, lam_im, log_dt, b_re, b_im, c_re, c_im):
    c16, q4, gl = S5_CHUNK, S5_LANE_BLOCKS, S5_GROUPS_PER_BLOCK
    hp = lax.Precision.HIGHEST
    dt = jnp.exp(log_dt)[:, None]
    n = jnp.arange(c16 + 1, dtype=F32)[:, None, None]
    mag = jnp.exp(n * (lam_re * dt)[None])
    ang = n * (lam_im * dt)[None]
    pw_r, pw_i = mag * jnp.cos(ang), mag * jnp.sin(ang)
    den = lam_re * lam_re + lam_im * lam_im
    q_r, q_i = _cmul(pw_r[1] - 1.0, pw_i[1], lam_re / den, -lam_im / den)
    bb_r, bb_i = _cmul(q_r[..., None], q_i[..., None], b_re, b_im)
    eye = jnp.eye(gl, dtype=F32)

    w_r, w_i = _cmul(pw_r[:c16, :, :, None], pw_i[:c16, :, :, None], bb_r[None], bb_i[None])
    kt = (jnp.einsum('gop,tgpi->tgio', c_re, w_r, precision=hp)
          - jnp.einsum('gop,tgpi->tgio', c_im, w_i, precision=hp))
    kt = kt.reshape(c16, q4, gl, S5_GROUP, S5_GROUP)
    kbd = (kt[:, :, :, :, None, :] * eye[None, None, :, None, :, None]).reshape(c16, q4, LANES, LANES)
    d8 = jnp.arange(c16 // 2)
    lag = 2 * d8[:, None, None] + jnp.arange(2)[None, None, :] - jnp.arange(2)[None, :, None]
    kp = jnp.where((lag >= 0)[..., None, None, None], kbd[jnp.maximum(lag, 0)], 0.0)
    kp = jnp.transpose(kp, (3, 0, 1, 4, 2, 5)).reshape(q4, c16 // 2, 2 * LANES, 2 * LANES)
    toep = kp[:, ::-1].reshape(q4, c16 * LANES, 2 * LANES)

    back = c16 - 1 - jnp.arange(c16)
    e_r, e_i = _cmul(pw_r[back][..., None], pw_i[back][..., None], bb_r[None], bb_i[None])
    ein = jnp.stack([e_r, e_i]).reshape(2, c16, q4, gl, S5_STATE, S5_GROUP)
    ein = jnp.transpose(ein, (2, 1, 3, 5, 0, 4))
    ein = ein[:, :, :, :, :, None, :] * eye[None, None, :, None, None, :, None]
    ein = ein.reshape(q4, c16 * LANES, S5_BLOCK_STATE)

    f_r, f_i = _cmul(c_re[None], c_im[None], pw_r[1:, :, None, :], pw_i[1:, :, None, :])
    fo = jnp.stack([f_r, -f_i]).reshape(2, c16, q4, gl, S5_GROUP, S5_STATE)
    fo = jnp.transpose(fo, (2, 0, 3, 5, 1, 4))
    fo = fo[:, :, :, :, :, None, :] * eye[None, None, :, None, None, :, None]
    fo = fo.reshape(q4, S5_BLOCK_STATE, c16 * LANES)

    lam16 = jnp.concatenate([pw_r[c16].reshape(q4, 1, gl * S5_STATE),
                             pw_i[c16].reshape(q4, 1, gl * S5_STATE)], axis=-1)
    return ein.astype(BF16), toep.astype(BF16), fo.astype(BF16), lam16


def _s5(u, ein, toep, fout, lam16, d):
    bsz, seqlen, _ = u.shape
    nch = seqlen // S5_CHUNK
    per_q = lambda a: pl.BlockSpec((1,) + a.shape[1:], lambda q, b: (q,) + (0,) * (a.ndim - 1))
    seq = pl.BlockSpec((1, seqlen, LANES), lambda q, b: (b, 0, q))
    return pl.pallas_call(
        _s5_kernel,
        grid=(S5_LANE_BLOCKS, bsz),
        in_specs=[seq, per_q(ein), per_q(toep), per_q(fout), per_q(lam16),
                  pl.BlockSpec((1, LANES), lambda q, b: (0, q))],
        out_specs=seq,
        out_shape=jax.ShapeDtypeStruct(u.shape, F32),
        scratch_shapes=[pltpu.VMEM((nch, S5_BLOCK_STATE), F32), pltpu.VMEM((nch, S5_BLOCK_STATE), F32)],
        compiler_params=pltpu.CompilerParams(
            dimension_semantics=("arbitrary", "arbitrary"), vmem_limit_bytes=VMEM_LIMIT),
        name="s5",
    )(u, ein, toep, fout, lam16, d)


def _stack_heads(x, lo):
    return jnp.concatenate([jnp.where(lo, x, 0.0), jnp.where(lo, 0.0, x)], axis=0)


def _stack_heads_swapped(x, lo):
    return jnp.concatenate([jnp.where(lo, 0.0, x), jnp.where(lo, x, 0.0)], axis=0)


def _rwkv_kernel(r_ref, lw_ref, k_ref, v_ref, kk_ref, b_ref, gate_ref, rk_ref, lnw_ref, lnb_ref,
                 y_ref, s_ref, w1_ref, w2_ref, rt_ref, bk_ref, prb_ref, yv_ref, em_ref):
    tc = RWKV_CHUNK
    nchunks = RWKV_TB // tc

    @pl.when(pl.program_id(1) == 0)
    def _():
        s_ref[...] = jnp.zeros_like(s_ref)

    row = lax.broadcasted_iota(jnp.int32, (tc, LANES), 0)
    lane = lax.broadcasted_iota(jnp.int32, (tc, LANES), 1)
    col = jnp.bitwise_and(lane, RWKV_HEAD - 1)
    lo = lane < RWKV_HEAD
    lo2 = jnp.concatenate([lo, lo], axis=0)
    stril = row > col
    tril = row >= col
    eye_pk = (row == col).astype(F32)
    head_bd = _head_ones()
    ones = head_bd.astype(BF16)
    tri = (lax.broadcasted_iota(jnp.int32, (tc, tc), 0)
           >= lax.broadcasted_iota(jnp.int32, (tc, tc), 1)).astype(BF16)
    bf = lambda t: t.astype(BF16)

    def prepare(i, carry):
        units = []
        for j in range(RWKV_PREP_CHUNKS):
            c = i * RWKV_PREP_CHUNKS + j
            rows = pl.ds(pl.multiple_of(c * tc, tc), tc)
            lw_all = lw_ref[0, rows, :]
            g_all = _dot_exact_lhs(tri, lw_all)
            for p in range(HEAD_PAIRS):
                ls = slice(LANES * p, LANES * (p + 1))
                units.append(dict(c=c, p=p, rows=rows, ls=ls, lw=lw_all[:, ls], g=g_all[:, ls]))
        for d in units:
            rows, ls, g = d["rows"], d["ls"], d["g"]
            m = 0.5 * g[tc - 1:tc, :]
            e_out = jnp.exp(m - g)
            d["at"] = bf(-kk_ref[0, rows, ls] * jnp.exp(g - d["lw"] - m))
            d["rt"] = bf(r_ref[0, rows, ls] * jnp.exp(g - m))
            bt = bf(b_ref[0, rows, ls] * e_out)
            kt = bf(k_ref[0, rows, ls] * e_out)
            d["bk"] = jnp.concatenate([bt, kt], axis=0)
            d["kb"] = jnp.concatenate([kt, bt], axis=0)
            d["vsw"] = bf(_stack_heads_swapped(v_ref[0, rows, ls], lo))
            em_ref[d["c"], d["p"]] = jnp.broadcast_to(jnp.exp(m), (8, LANES))
        for d in units:
            ar = jnp.concatenate([d["at"], d["rt"]], axis=0)
            zero = jnp.zeros_like(ar)
            d["o0"] = _dot_nt(jnp.where(lo2, ar, zero), d["bk"])
            d["o1"] = _dot_nt(jnp.where(lo2, zero, ar), d["kb"])
        for d in units:
            o0, o1 = d["o0"], d["o1"]
            d["ap"] = jnp.where(stril, jnp.where(lo, o0[:tc], o1[:tc]), 0.0)
            d["p_ak"] = bf(jnp.where(stril, jnp.where(lo, o1[:tc], o0[:tc]), 0.0))
            p_rb = bf(jnp.where(tril, jnp.where(lo, o0[tc:], o1[tc:]), 0.0))
            d["p_rk"] = bf(jnp.where(tril, jnp.where(lo, o1[tc:], o0[tc:]), 0.0))
            prb_ref[d["c"], d["p"]] = p_rb
            d["tinv"] = eye_pk + d["ap"]
        for _ in range(int(math.log2(tc)) - 1):
            for d in units:
                d["ap"] = _dot(bf(d["ap"]), bf(_stack_heads(d["ap"], lo)))
            for d in units:
                d["tinv"] = d["tinv"] + _dot(bf(d["tinv"]), bf(_stack_heads(d["ap"], lo)))
        for d in units:
            d["tinv"] = bf(d["tinv"])
            d["xv"] = _dot(d["p_ak"], d["vsw"])
            w1_ref[d["c"], d["p"]] = bf(_dot(d["tinv"], _stack_heads(d["at"], lo)))
            yv_ref[d["c"], d["p"]] = _dot(d["p_rk"], d["vsw"])
            rt_ref[d["c"], d["p"]] = d["rt"]
            bk_ref[d["c"], d["p"]] = d["bk"]
        for d in units:
            w2_ref[d["c"], d["p"]] = _dot(d["tinv"], bf(_stack_heads(d["xv"], lo)))
        return carry

    lax.fori_loop(0, nchunks // RWKV_PREP_CHUNKS, prepare, 0)

    def recur(c, carry):
        rows = pl.ds(pl.multiple_of(c * tc, tc), tc)
        units = [dict(p=p, ls=slice(LANES * p, LANES * (p + 1))) for p in range(HEAD_PAIRS)]
        for d in units:
            p = d["p"]
            d["em"] = em_ref[c, p, 0:1, :]
            d["s0"] = s_ref[p]
            d["s0s"] = bf(d["s0"] * d["em"])
            d["v"] = v_ref[0, rows, d["ls"]]
        for d in units:
            d["u"] = _dot_nt(w1_ref[c, d["p"]], d["s0s"]) + w2_ref[c, d["p"]]
        for d in units:
            p, em = d["p"], d["em"]
            upd = _dot_tn(bf(jnp.concatenate([d["u"], d["v"]], axis=0)), bk_ref[c, p])
            s_ref[p] = d["s0"] * (em * em) + jnp.where(head_bd, upd, 0.0) * em
        for d in units:
            p = d["p"]
            d["y"] = (_dot_nt(rt_ref[c, p], d["s0s"]) + _dot(prb_ref[c, p], bf(_stack_heads(d["u"], lo)))
                      + yv_ref[c, p])
        for d in units:
            ls, y, v = d["ls"], d["y"], d["v"]
            rkr = r_ref[0, rows, ls] * k_ref[0, rows, ls] * rk_ref[:, ls]
            sums = _dot_exact_rhs(jnp.concatenate([y, rkr], axis=0), ones)
            dev = y - sums[:tc] * (1.0 / RWKV_HEAD)
            var = _dot_exact_rhs(dev * dev, ones) * (1.0 / RWKV_HEAD)
            yn = dev * lax.rsqrt(var + GN_EPS) * lnw_ref[:, ls] + lnb_ref[:, ls]
            y_ref[0, rows, ls] = (yn + sums[tc:] * v) * gate_ref[0, rows, ls]
        return carry

    lax.fori_loop(0, nchunks, recur, 0)


def _rwkv(r, lw, k, v, kkn, b, gate, r_k, ln_w, ln_b):
    bsz, seqlen, _ = r.shape
    nchunks = RWKV_TB // RWKV_CHUNK
    tok = pl.BlockSpec((1, RWKV_TB, D_RWKV), lambda bi, t: (bi, t, 0))
    gate_spec = pl.BlockSpec((1, RWKV_TB, D_RWKV), lambda bi, t: (bi, t, 1))
    vec = pl.BlockSpec((1, D_RWKV), lambda bi, t: (0, 0))
    unit = lambda rows, cols, dt: pltpu.VMEM((nchunks, HEAD_PAIRS, rows, cols), dt)
    return pl.pallas_call(
        _rwkv_kernel,
        grid=(bsz, seqlen // RWKV_TB),
        in_specs=[tok] * 6 + [gate_spec] + [vec] * 3,
        out_specs=tok,
        out_shape=jax.ShapeDtypeStruct(r.shape, F32),
        scratch_shapes=[
            pltpu.VMEM((HEAD_PAIRS, LANES, LANES), F32),
            unit(RWKV_CHUNK, LANES, BF16),
            unit(RWKV_CHUNK, LANES, F32),
            unit(RWKV_CHUNK, LANES, BF16),
            unit(2 * RWKV_CHUNK, LANES, BF16),
            unit(RWKV_CHUNK, LANES, BF16),
            unit(RWKV_CHUNK, LANES, F32),
            unit(8, LANES, F32),
        ],
        compiler_params=pltpu.CompilerParams(
            dimension_semantics=("arbitrary", "arbitrary"), vmem_limit_bytes=VMEM_LIMIT),
        name="rwkv7",
    )(r, lw, k, v, kkn, b, gate, r_k, ln_w, ln_b)


def _out_kernel(y5_ref, gate_ref, yr_ref, x_ref, gluw_ref, glub_ref, wout_ref, fg_ref, o_ref):
    y = y5_ref[0]
    y = 0.5 * y * (1.0 + jnp.tanh(math.sqrt(2.0 / math.pi) * (y + 0.044715 * (y * y * y))))
    glu = _sigmoid(_dot(y.astype(BF16), gluw_ref[...]) + glub_ref[...])
    ys = y * glu * gate_ref[0]
    ycat = jnp.concatenate([ys, yr_ref[0]], axis=-1).astype(BF16)
    xn = x_ref[0] + _dot(ycat, wout_ref[...])
    ms = jnp.mean(xn * xn, axis=-1, keepdims=True)
    o_ref[0] = xn * lax.rsqrt(ms + NORM_EPS) * fg_ref[...]


def _outproj(y5, gate, yr, x, glu_w, glu_b, w_out, final_g):
    bsz, seqlen, _ = x.shape
    tok = lambda w: pl.BlockSpec((1, TM_OUT, w), lambda b, t: (b, t, 0))
    full = lambda a: pl.BlockSpec(a.shape, lambda b, t: (0,) * a.ndim)
    return pl.pallas_call(
        _out_kernel,
        grid=(bsz, seqlen // TM_OUT),
        in_specs=[tok(D_S5), tok(D_S5), tok(D_RWKV), tok(D_MODEL),
                  full(glu_w), full(glu_b), full(w_out), full(final_g)],
        out_specs=tok(D_MODEL),
        out_shape=jax.ShapeDtypeStruct(x.shape, F32),
        compiler_params=pltpu.CompilerParams(
            dimension_semantics=("arbitrary", "arbitrary"), vmem_limit_bytes=VMEM_LIMIT),
        name="outproj",
    )(y5, gate, yr, x, glu_w, glu_b, w_out, final_g)


@jax.jit
def _forward(x, norm_g, w_in, s5_lam_re, s5_lam_im, s5_log_dt, s5_b_re, s5_b_im, s5_c_re, s5_c_im,
             s5_d, s5_glu_w, s5_glu_b, rwkv_mu, rwkv_w0, rwkv_w2, rwkv_a0, rwkv_a2, rwkv_k_k, rwkv_k_a,
             rwkv_r_k, rwkv_ln_w, rwkv_ln_b, w_out, final_g):
    assert w_in.shape[0] == 1, "single-layer block"
    for l in range(1):
        zeros = jnp.zeros((LOW_RANK, D_RWKV), F32)
        w2a2 = jnp.concatenate(
            [jnp.concatenate([rwkv_w2[l], zeros], axis=1), jnp.concatenate([zeros, rwkv_a2[l]], axis=1)],
            axis=0).astype(BF16)
        row = lambda a: a.reshape(1, -1).astype(F32)
        u, gate, r, lw, k, v, kkn, b = _inproj(
            x, row(norm_g[l]), w_in[l].astype(BF16), row(rwkv_mu[l]), w2a2, row(rwkv_w0[l]),
            row(rwkv_a0[l]), row(rwkv_k_k[l]), row(rwkv_k_a[l]))
        ein, toep, fout, lam16 = _s5_operators(
            s5_lam_re[l], s5_lam_im[l], s5_log_dt[l], s5_b_re[l], s5_b_im[l], s5_c_re[l], s5_c_im[l])
        y5 = _s5(u, ein, toep, fout, lam16, row(s5_d[l]))
        yr = _rwkv(r, lw, k, v, kkn, b, gate, row(rwkv_r_k[l]), row(rwkv_ln_w[l]), row(rwkv_ln_b[l]))
        x = _outproj(y5, gate, yr, x, s5_glu_w[l].astype(BF16), row(s5_glu_b[l]),
                     w_out[l].astype(BF16), row(final_g))
    return x


def kernel(x, norm_g, w_in, s5_lam_re, s5_lam_im, s5_log_dt, s5_b_re, s5_b_im, s5_c_re, s5_c_im, s5_d, s5_glu_w, s5_glu_b, rwkv_mu, rwkv_w0, rwkv_w2, rwkv_a0, rwkv_a2, rwkv_k_k, rwkv_k_a, rwkv_r_k, rwkv_ln_w, rwkv_ln_b, w_out, final_g):
    return _forward(x, norm_g, w_in, s5_lam_re, s5_lam_im, s5_log_dt, s5_b_re, s5_b_im, s5_c_re, s5_c_im,
                    s5_d, s5_glu_w, s5_glu_b, rwkv_mu, rwkv_w0, rwkv_w2, rwkv_a0, rwkv_a2, rwkv_k_k,
                    rwkv_k_a, rwkv_r_k, rwkv_ln_w, rwkv_ln_b, w_out, final_g)
```

```python
import math

import jax
import jax.numpy as jnp
from jax import lax
from jax.experimental import pallas as pl
from jax.experimental.pallas import tpu as pltpu

F32 = jnp.float32
BF16 = jnp.bfloat16

D_MODEL = 1024
D_S5 = 512
D_RWKV = 512
S5_GROUP = 16
S5_GROUPS = 32
S5_STATE = 64
RWKV_HEAD = 64
LOW_RANK = 64
D_SHIFT = 3 * D_RWKV + 2 * LOW_RANK
D_IN = 2 * D_S5 + D_SHIFT + D_RWKV
NORM_EPS = 1e-6
GN_EPS = 64e-5

LANES = 128
TM_IN = 256
TM_OUT = 512
S5_CHUNK = 16
S5_LANE_BLOCKS = D_S5 // LANES
S5_GROUPS_PER_BLOCK = LANES // S5_GROUP
S5_BLOCK_STATE = 2 * S5_GROUPS_PER_BLOCK * S5_STATE
RWKV_CHUNK = 64
RWKV_TB = 512
RWKV_PREP_CHUNKS = 4
RWKV_OUT_CHUNKS = 2
HEAD_PAIRS = D_RWKV // LANES
RWKV_QUADS = HEAD_PAIRS // 2
VMEM_LIMIT = 56 * 1024 * 1024


def _dot(a, b):
    return jnp.dot(a, b, preferred_element_type=F32)


def _dot_nt(a, b):
    return lax.dot_general(a, b, (((1,), (1,)), ((), ())), preferred_element_type=F32)


def _dot_tn(a, b):
    return lax.dot_general(a, b, (((0,), (0,)), ((), ())), preferred_element_type=F32)


def _split2(x):
    hi = x.astype(BF16)
    return hi, (x - hi.astype(F32)).astype(BF16)


def _dot_exact_rhs(x, m):
    hi, lo = _split2(x)
    return _dot(hi, m) + _dot(lo, m)


def _dot_exact_lhs(m, x):
    hi, lo = _split2(x)
    return _dot(m, hi) + _dot(m, lo)


def _sigmoid(x):
    return 1.0 / (1.0 + jnp.exp(-x))


def _head_ones():
    r = lax.broadcasted_iota(jnp.int32, (LANES, LANES), 0)
    c = lax.broadcasted_iota(jnp.int32, (LANES, LANES), 1)
    return (r < RWKV_HEAD) == (c < RWKV_HEAD)


def _inproj_kernel(x_ref, g_ref, win_ref, mu_ref, w2a2_ref, w0_ref, a0_ref, kk_ref, ka_ref,
                   u_ref, gate_ref, r_ref, lw_ref, k_ref, v_ref, kkn_ref, b_ref, carry_ref):
    t = pl.program_id(1)

    @pl.when(t == 0)
    def _():
        carry_ref[...] = jnp.zeros_like(carry_ref)

    x = x_ref[0]
    ms = jnp.mean(x * x, axis=-1, keepdims=True)
    h = (x * lax.rsqrt(ms + NORM_EPS) * g_ref[...]).astype(BF16)

    def proj(lo, hi):
        return _dot(h, win_ref[:, lo:hi])

    u_ref[0] = proj(0, D_S5)
    z5 = proj(D_S5, 2 * D_S5)
    gate_ref[0, :, :D_S5] = z5 * _sigmoid(z5)
    zr = proj(2 * D_S5 + D_SHIFT, D_IN)
    gate_ref[0, :, D_S5:] = zr * _sigmoid(zr)

    rw = proj(2 * D_S5, 2 * D_S5 + D_SHIFT)
    prev = pltpu.roll(rw, 1, axis=0)
    row = lax.broadcasted_iota(jnp.int32, rw.shape, 0)
    prev = jnp.where(row == 0, carry_ref[...], prev)
    carry_ref[...] = rw[TM_IN - 1:TM_IN, :]
    sh = rw + (prev - rw) * mu_ref[...]

    r = sh[:, 0:D_RWKV]
    k = sh[:, D_RWKV:2 * D_RWKV]
    v = sh[:, 2 * D_RWKV:3 * D_RWKV]
    wa = sh[:, 3 * D_RWKV:]
    lane = lax.broadcasted_iota(jnp.int32, wa.shape, 1)
    wa = jnp.where(lane < LOW_RANK, jnp.tanh(wa), wa)
    za = _dot(wa.astype(BF16), w2a2_ref[...])
    zw = w0_ref[...] + za[:, :D_RWKV]
    softplus = jnp.maximum(-zw, 0.0) + jnp.log(1.0 + jnp.exp(-jnp.abs(zw)))
    lw = -jnp.exp(-softplus - 0.5)
    iclr = _sigmoid(a0_ref[...] + za[:, D_RWKV:])

    kk = k * kk_ref[...]
    ones = _head_ones().astype(BF16)
    sq = kk * kk
    ss = jnp.concatenate(
        [_dot_exact_rhs(sq[:, LANES * p:LANES * (p + 1)], ones) for p in range(HEAD_PAIRS)], axis=1)
    kkn = kk / jnp.maximum(jnp.sqrt(ss), 1e-12)

    r_ref[0] = r
    lw_ref[0] = lw
    k_ref[0] = k * (1.0 + (iclr - 1.0) * ka_ref[...])
    v_ref[0] = v
    kkn_ref[0] = kkn
    b_ref[0] = kkn * iclr


def _inproj(x, norm_g, w_in, mu, w2a2, w0, a0, k_k, k_a):
    bsz, seqlen, _ = x.shape
    tok = lambda w: pl.BlockSpec((1, TM_IN, w), lambda b, t: (b, t, 0))
    full = lambda a: pl.BlockSpec(a.shape, lambda b, t: (0,) * a.ndim)
    out = lambda w: jax.ShapeDtypeStruct((bsz, seqlen, w), F32)
    params = (norm_g, w_in, mu, w2a2, w0, a0, k_k, k_a)
    return pl.pallas_call(
        _inproj_kernel,
        grid=(bsz, seqlen // TM_IN),
        in_specs=[tok(D_MODEL)] + [full(a) for a in params],
        out_specs=[tok(D_S5), tok(2 * D_S5)] + [tok(D_RWKV)] * 6,
        out_shape=[out(D_S5), out(2 * D_S5)] + [out(D_RWKV)] * 6,
        scratch_shapes=[pltpu.VMEM((1, D_SHIFT), F32)],
        compiler_params=pltpu.CompilerParams(
            dimension_semantics=("arbitrary", "arbitrary"), vmem_limit_bytes=VMEM_LIMIT),
        name="inproj",
    )(x, *params)


def _s5_kernel(u_ref, ein_ref, toep_ref, fout_ref, lam_ref, d_ref, y_ref, e_ref, sp_ref):
    nch = e_ref.shape[0]
    half = S5_BLOCK_STATE // 2
    xs = [u_ref[0, pl.ds(j, nch, stride=S5_CHUNK), :] for j in range(S5_CHUNK)]
    xcat = jnp.concatenate([x.astype(BF16) for x in xs], axis=1)
    e_ref[...] = _dot(xcat, ein_ref[0])

    lre = lam_ref[0, :, :half]
    lim = lam_ref[0, :, half:]

    def step(c, s):
        sre, sim = s
        sp_ref[pl.ds(c, 1), :half] = sre
        sp_ref[pl.ds(c, 1), half:] = sim
        e = e_ref[pl.ds(c, 1), :]
        return (lre * sre - lim * sim + e[:, :half], lre * sim + lim * sre + e[:, half:])

    zero = jnp.zeros((1, half), F32)
    lax.fori_loop(0, nch, step, (zero, zero))

    sp = sp_ref[...].astype(BF16)
    d = d_ref[...]
    pairs = S5_CHUNK // 2
    for tp in range(pairs):
        y = _dot(xcat[:, :2 * LANES * (tp + 1)], toep_ref[0, 2 * LANES * (pairs - 1 - tp):, :])
        y = y + _dot(sp, fout_ref[0, :, 2 * LANES * tp:2 * LANES * (tp + 1)])
        for i in range(2):
            t = 2 * tp + i
            y_ref[0, pl.ds(t, nch, stride=S5_CHUNK), :] = y[:, LANES * i:LANES * (i + 1)] + d * xs[t]


def _cmul(ar, ai, br, bi):
    return ar * br - ai * bi, ar * bi + ai * br


def _block_diag_expand(compact, width, inner, transposed):
    k = compact.shape[1] if transposed else compact.shape[2]
    group_w = S5_GROUPS_PER_BLOCK * inner
    kk = jnp.arange(k)
    ww = jnp.arange(width)
    sel = ((kk[:, None] // inner == ww[None, :] // group_w)
           & (kk[:, None] % inner == ww[None, :] % inner)).astype(F32)
    wide_group = (ww % group_w) // inner
    if transposed:
        full = jnp.matmul(sel.T, compact)
        narrow_group = (jnp.arange(compact.shape[2]) % LANES) // S5_GROUP
        keep = wide_group[:, None] == narrow_group[None, :]
    else:
        full = jnp.matmul(compact, sel)
        narrow_group = (jnp.arange(compact.shape[1]) % LANES) // S5_GROUP
        keep = narrow_group[:, None] == wide_group[None, :]
    return jnp.where(keep, full, 0.0)


def _s5_operators(lam_re, lam_im, log_dt, b_re, b_im, c_re, c_im):
    c16, q4, gl = S5_CHUNK, S5_LANE_BLOCKS, S5_GROUPS_PER_BLOCK
    hp = lax.Precision.HIGHEST
    dt = jnp.exp(log_dt)[:, None]
    n = jnp.arange(c16 + 1, dtype=F32)[:, None, None]
    mag = jnp.exp(n * (lam_re * dt)[None])
    ang = n * (lam_im * dt)[None]
    pw_r, pw_i = mag * jnp.cos(ang), mag * jnp.sin(ang)
    den = lam_re * lam_re + lam_im * lam_im
    q_r, q_i = _cmul(pw_r[1] - 1.0, pw_i[1], lam_re / den, -lam_im / den)
    bb_r, bb_i = _cmul(q_r[..., None], q_i[..., None], b_re, b_im)

    w_r, w_i = _cmul(pw_r[:c16, :, :, None], pw_i[:c16, :, :, None], bb_r[None], bb_i[None])
    kt = (jnp.einsum('gop,tgpi->tgio', c_re, w_r, precision=hp)
          - jnp.einsum('gop,tgpi->tgio', c_im, w_i, precision=hp))
    d8 = jnp.arange(c16 // 2)
    lag = 2 * d8[:, None, None] + jnp.arange(2)[None, None, :] - jnp.arange(2)[None, :, None]
    kp = jnp.where((lag >= 0)[..., None, None, None], kt[jnp.maximum(lag, 0)], 0.0)
    kp = kp.reshape(c16 // 2, 2, 2, q4, gl, S5_GROUP, S5_GROUP)
    kp = jnp.transpose(kp, (3, 0, 1, 4, 5, 2, 6))[:, ::-1]
    toep = _block_diag_expand(kp.reshape(q4, c16 * LANES, 2 * S5_GROUP), 2 * LANES, S5_GROUP, False)

    back = c16 - 1 - jnp.arange(c16)
    e_r, e_i = _cmul(pw_r[back][..., None], pw_i[back][..., None], bb_r[None], bb_i[None])
    ein = jnp.stack([e_r, e_i]).reshape(2, c16, q4, gl, S5_STATE, S5_GROUP)
    ein = jnp.transpose(ein, (2, 1, 3, 5, 0, 4)).reshape(q4, c16 * LANES, 2 * S5_STATE)
    ein = _block_diag_expand(ein, S5_BLOCK_STATE, S5_STATE, False)

    f_r, f_i = _cmul(c_re[None], c_im[None], pw_r[1:, :, None, :], pw_i[1:, :, None, :])
    fo = jnp.stack([f_r, -f_i]).reshape(2, c16, q4, gl, S5_GROUP, S5_STATE)
    fo = jnp.transpose(fo, (2, 0, 5, 1, 3, 4)).reshape(q4, 2 * S5_STATE, c16 * LANES)
    fo = _block_diag_expand(fo, S5_BLOCK_STATE, S5_STATE, True)

    lam16 = jnp.concatenate([pw_r[c16].reshape(q4, 1, gl * S5_STATE),
                             pw_i[c16].reshape(q4, 1, gl * S5_STATE)], axis=-1)
    return ein.astype(BF16), toep.astype(BF16), fo.astype(BF16), lam16


def _s5(u, ein, toep, fout, lam16, d):
    bsz, seqlen, _ = u.shape
    nch = seqlen // S5_CHUNK
    per_q = lambda a: pl.BlockSpec((1,) + a.shape[1:], lambda q, b: (q,) + (0,) * (a.ndim - 1))
    seq = pl.BlockSpec((1, seqlen, LANES), lambda q, b: (b, 0, q))
    return pl.pallas_call(
        _s5_kernel,
        grid=(S5_LANE_BLOCKS, bsz),
        in_specs=[seq, per_q(ein), per_q(toep), per_q(fout), per_q(lam16),
                  pl.BlockSpec((1, LANES), lambda q, b: (0, q))],
        out_specs=seq,
        out_shape=jax.ShapeDtypeStruct(u.shape, F32),
        scratch_shapes=[pltpu.VMEM((nch, S5_BLOCK_STATE), F32), pltpu.VMEM((nch, S5_BLOCK_STATE), F32)],
        compiler_params=pltpu.CompilerParams(
            dimension_semantics=("arbitrary", "arbitrary"), vmem_limit_bytes=VMEM_LIMIT),
        name="s5",
    )(u, ein, toep, fout, lam16, d)


def _stack_heads(x, lo):
    return jnp.concatenate([jnp.where(lo, x, 0.0), jnp.where(lo, 0.0, x)], axis=0)


def _head_block_diag(x, head_id, order):
    return jnp.concatenate([jnp.where(head_id == h, x, 0.0) for h in order], axis=0)


def _rwkv_kernel(r_ref, lw_ref, k_ref, v_ref, kk_ref, b_ref, gate_ref, rk_ref, lnw_ref, lnb_ref,
                 y_ref, s_ref, wr_ref, w2_ref, bkt_ref, prb_ref, yv_ref, e2_ref, u_ref):
    tc = RWKV_CHUNK
    nchunks = RWKV_TB // tc
    qw = 2 * LANES

    @pl.when(pl.program_id(1) == 0)
    def _():
        s_ref[...] = jnp.zeros_like(s_ref)

    row = lax.broadcasted_iota(jnp.int32, (tc, qw), 0)
    lane = lax.broadcasted_iota(jnp.int32, (tc, qw), 1)
    col = jnp.bitwise_and(lane, RWKV_HEAD - 1)
    head_id = lane // RWKV_HEAD
    heads = tuple(range(qw // RWKV_HEAD))
    heads_swapped = (1, 0, 3, 2)
    eye_q = (row == col).astype(F32)
    row_p = lax.broadcasted_iota(jnp.int32, (tc, LANES), 0)
    lane_p = lax.broadcasted_iota(jnp.int32, (tc, LANES), 1)
    col_p = jnp.bitwise_and(lane_p, RWKV_HEAD - 1)
    lo_p = lane_p < RWKV_HEAD
    lo_p2 = lax.broadcasted_iota(jnp.int32, (2 * tc, LANES), 1) < RWKV_HEAD
    stril = row_p > col_p
    tril = row_p >= col_p
    state_bd = _head_ones()
    r4 = lax.broadcasted_iota(jnp.int32, (qw, qw), 0)
    l4 = lax.broadcasted_iota(jnp.int32, (qw, qw), 1)
    ones = ((r4 // RWKV_HEAD) == (l4 // RWKV_HEAD)).astype(BF16)
    tri = (lax.broadcasted_iota(jnp.int32, (tc, tc), 0)
           >= lax.broadcasted_iota(jnp.int32, (tc, tc), 1)).astype(BF16)
    bf = lambda t: t.astype(BF16)
    up, dn = slice(0, tc), slice(tc, 2 * tc)

    def prepare(i, carry):
        units = []
        for j in range(RWKV_PREP_CHUNKS):
            c = i * RWKV_PREP_CHUNKS + j
            rows = pl.ds(pl.multiple_of(c * tc, tc), tc)
            lw_all = lw_ref[0, rows, :]
            g_all = _dot_exact_lhs(tri, lw_all)
            for q in range(RWKV_QUADS):
                qs = slice(qw * q, qw * (q + 1))
                units.append(dict(c=c, q=q, rows=rows, qs=qs, lw=lw_all[:, qs], g=g_all[:, qs]))
        for d in units:
            rows, qs, g = d["rows"], d["qs"], d["g"]
            m = 0.5 * g[tc - 1:tc, :]
            e_out = jnp.exp(m - g)
            em = jnp.exp(m)
            rt = r_ref[0, rows, qs] * jnp.exp(g - m)
            bt = b_ref[0, rows, qs] * e_out
            kt = k_ref[0, rows, qs] * e_out
            d["em"] = em
            d["at"] = bf(-kk_ref[0, rows, qs] * jnp.exp(g - d["lw"] - m))
            d["rt"], d["bt"], d["kt"] = bf(rt), bf(bt), bf(kt)
            d["rts"] = bf(rt * em)
            d["vsw"] = bf(_head_block_diag(v_ref[0, rows, qs], head_id, heads_swapped))
            bkt, e2 = [], []
            for j in range(2):
                ps = slice(LANES * j, LANES * (j + 1))
                bks = jnp.concatenate([bt[:, ps], kt[:, ps]], axis=0) * em[:, ps]
                bkt.append(bf(jnp.transpose(bks)))
                e2.append(jnp.transpose(jnp.broadcast_to(em[:, ps] * em[:, ps], (LANES, LANES))))
            bkt_ref[d["c"], d["q"]] = jnp.concatenate(bkt, axis=0)
            e2_ref[d["c"], d["q"]] = jnp.concatenate(e2, axis=1)
        for d in units:
            d["o"] = []
            for j in range(2):
                ps = slice(LANES * j, LANES * (j + 1))
                bk = jnp.concatenate([d["bt"][:, ps], d["kt"][:, ps]], axis=0)
                kb = jnp.concatenate([d["kt"][:, ps], d["bt"][:, ps]], axis=0)
                rhs = jnp.concatenate([jnp.where(lo_p2, bk, 0.0), jnp.where(lo_p2, 0.0, kb)], axis=0)
                lhs = jnp.concatenate([d["at"][:, ps], d["rt"][:, ps]], axis=0)
                d["o"].append(_dot_nt(lhs, rhs))
        for d in units:
            h0 = [o[:, :LANES] for o in d["o"]]
            h1 = [o[:, LANES:] for o in d["o"]]

            def pick(rws, first, second, mask):
                return jnp.concatenate(
                    [jnp.where(mask, jnp.where(lo_p, first[j][rws], second[j][rws]), 0.0) for j in range(2)],
                    axis=1)

            d["ap"] = pick(up, h0, h1, stril)
            d["p_ak"] = bf(pick(up, h1, h0, stril))
            p_rb = bf(pick(dn, h0, h1, tril))
            d["p_rk"] = bf(pick(dn, h1, h0, tril))
            prb_ref[d["c"], d["q"]] = p_rb
            d["tinv"] = eye_q + d["ap"]
        levels = int(math.log2(tc)) - 1
        for d in units:
            d["ap"] = _dot(bf(d["ap"]), bf(_head_block_diag(d["ap"], head_id, heads)))
        for lvl in range(levels):
            for d in units:
                rhs = bf(_head_block_diag(d["ap"], head_id, heads))
                if lvl < levels - 1:
                    out = _dot(bf(jnp.concatenate([d["tinv"], d["ap"]], axis=0)), rhs)
                    d["tinv"], d["ap"] = d["tinv"] + out[:tc], out[tc:]
                else:
                    d["tinv"] = d["tinv"] + _dot(bf(d["tinv"]), rhs)
        for d in units:
            d["tinv"] = bf(d["tinv"])
            out = _dot(jnp.concatenate([d["p_ak"], d["p_rk"]], axis=0), d["vsw"])
            d["xv"] = out[:tc]
            yv_ref[d["c"], d["q"]] = out[tc:]
        for d in units:
            w1s, w2 = [], []
            for j in range(2):
                ps = slice(LANES * j, LANES * (j + 1))
                rhs = jnp.concatenate([_stack_heads(d["at"][:, ps], lo_p),
                                       bf(_stack_heads(d["xv"][:, ps], lo_p))], axis=1)
                w = _dot(d["tinv"][:, ps], rhs)
                w1s.append(bf(w[:, :LANES] * d["em"][:, ps]))
                w2.append(w[:, LANES:])
            wr_ref[d["c"], d["q"]] = jnp.concatenate(w1s + [d["rts"][:, :LANES], d["rts"][:, LANES:]], axis=0)
            w2_ref[d["c"], d["q"]] = jnp.concatenate(w2, axis=1)
        return carry

    lax.fori_loop(0, nchunks // RWKV_PREP_CHUNKS, prepare, 0)

    def recur(c, carry):
        rows = pl.ds(pl.multiple_of(c * tc, tc), tc)
        units = [dict(q=q, qs=slice(qw * q, qw * (q + 1))) for q in range(RWKV_QUADS)]
        for d in units:
            d["h0"] = s_ref[d["q"]]
        for d in units:
            out = _dot(wr_ref[c, d["q"]], bf(d["h0"]))
            d["u"] = (jnp.concatenate([out[:tc, :LANES], out[tc:2 * tc, LANES:]], axis=1)
                      + w2_ref[c, d["q"]])
            d["yr"] = jnp.concatenate([out[2 * tc:3 * tc, :LANES], out[3 * tc:, LANES:]], axis=1)
        for d in units:
            q = d["q"]
            uv = bf(jnp.concatenate([d["u"], v_ref[0, rows, d["qs"]]], axis=0))
            upd = _dot(bkt_ref[c, q], uv)
            upd = jnp.concatenate([jnp.where(state_bd, upd[:LANES, :LANES], 0.0),
                                   jnp.where(state_bd, upd[LANES:, LANES:], 0.0)], axis=1)
            s_ref[q] = d["h0"] * e2_ref[c, q] + upd
            u_ref[c, q] = bf(d["u"])
            yv_ref[c, q] = yv_ref[c, q] + d["yr"]
        return carry

    lax.fori_loop(0, nchunks, recur, 0)

    def emit(i, carry):
        units = []
        for j in range(RWKV_OUT_CHUNKS):
            c = i * RWKV_OUT_CHUNKS + j
            rows = pl.ds(pl.multiple_of(c * tc, tc), tc)
            units += [dict(c=c, q=q, rows=rows, qs=slice(qw * q, qw * (q + 1))) for q in range(RWKV_QUADS)]
        for d in units:
            c, q = d["c"], d["q"]
            d["y"] = yv_ref[c, q] + _dot(prb_ref[c, q], _head_block_diag(u_ref[c, q], head_id, heads))
        for d in units:
            rows, qs, y = d["rows"], d["qs"], d["y"]
            rkr = r_ref[0, rows, qs] * k_ref[0, rows, qs] * rk_ref[:, qs]
            d["sums"] = _dot_exact_rhs(jnp.concatenate([y, rkr], axis=0), ones)
        for d in units:
            d["dev"] = d["y"] - d["sums"][:tc] * (1.0 / RWKV_HEAD)
            d["var"] = _dot_exact_rhs(d["dev"] * d["dev"], ones) * (1.0 / RWKV_HEAD)
        for d in units:
            rows, qs = d["rows"], d["qs"]
            yn = d["dev"] * lax.rsqrt(d["var"] + GN_EPS) * lnw_ref[:, qs] + lnb_ref[:, qs]
            y_ref[0, rows, qs] = (yn + d["sums"][tc:] * v_ref[0, rows, qs]) * gate_ref[0, rows, qs]
        return carry

    lax.fori_loop(0, nchunks // RWKV_OUT_CHUNKS, emit, 0)


def _rwkv(r, lw, k, v, kkn, b, gate, r_k, ln_w, ln_b):
    bsz, seqlen, _ = r.shape
    nchunks = RWKV_TB // RWKV_CHUNK
    tok = pl.BlockSpec((1, RWKV_TB, D_RWKV), lambda bi, t: (bi, t, 0))
    gate_spec = pl.BlockSpec((1, RWKV_TB, D_RWKV), lambda bi, t: (bi, t, 1))
    vec = pl.BlockSpec((1, D_RWKV), lambda bi, t: (0, 0))
    unit = lambda rows, cols, dt: pltpu.VMEM((nchunks, RWKV_QUADS, rows, cols), dt)
    return pl.pallas_call(
        _rwkv_kernel,
        grid=(bsz, seqlen // RWKV_TB),
        in_specs=[tok] * 6 + [gate_spec] + [vec] * 3,
        out_specs=tok,
        out_shape=jax.ShapeDtypeStruct(r.shape, F32),
        scratch_shapes=[
            pltpu.VMEM((RWKV_QUADS, LANES, 2 * LANES), F32),
            unit(4 * RWKV_CHUNK, LANES, BF16),
            unit(RWKV_CHUNK, 2 * LANES, F32),
            unit(2 * LANES, 2 * RWKV_CHUNK, BF16),
            unit(RWKV_CHUNK, 2 * LANES, BF16),
            unit(RWKV_CHUNK, 2 * LANES, F32),
            unit(LANES, 2 * LANES, F32),
            unit(RWKV_CHUNK, 2 * LANES, BF16),
        ],
        compiler_params=pltpu.CompilerParams(
            dimension_semantics=("arbitrary", "arbitrary"), vmem_limit_bytes=VMEM_LIMIT),
        name="rwkv7",
    )(r, lw, k, v, kkn, b, gate, r_k, ln_w, ln_b)


def _out_kernel(y5_ref, gate_ref, yr_ref, x_ref, gluw_ref, glub_ref, wout_ref, fg_ref, o_ref):
    y = y5_ref[0]
    y = 0.5 * y * (1.0 + jnp.tanh(math.sqrt(2.0 / math.pi) * (y + 0.044715 * (y * y * y))))
    glu = _sigmoid(_dot(y.astype(BF16), gluw_ref[...]) + glub_ref[...])
    ys = y * glu * gate_ref[0]
    ycat = jnp.concatenate([ys, yr_ref[0]], axis=-1).astype(BF16)
    xn = x_ref[0] + _dot(ycat, wout_ref[...])
    ms = jnp.mean(xn * xn, axis=-1, keepdims=True)
    o_ref[0] = xn * lax.rsqrt(ms + NORM_EPS) * fg_ref[...]


def _outproj(y5, gate, yr, x, glu_w, glu_b, w_out, final_g):
    bsz, seqlen, _ = x.shape
    tok = lambda w: pl.BlockSpec((1, TM_OUT, w), lambda b, t: (b, t, 0))
    full = lambda a: pl.BlockSpec(a.shape, lambda b, t: (0,) * a.ndim)
    return pl.pallas_call(
        _out_kernel,
        grid=(bsz, seqlen // TM_OUT),
        in_specs=[tok(D_S5), tok(D_S5), tok(D_RWKV), tok(D_MODEL),
                  full(glu_w), full(glu_b), full(w_out), full(final_g)],
        out_specs=tok(D_MODEL),
        out_shape=jax.ShapeDtypeStruct(x.shape, F32),
        compiler_params=pltpu.CompilerParams(
            dimension_semantics=("arbitrary", "arbitrary"), vmem_limit_bytes=VMEM_LIMIT),
        name="outproj",
    )(y5, gate, yr, x, glu_w, glu_b, w_out, final_g)


@jax.jit
def _forward(x, norm_g, w_in, s5_lam_re, s5_lam_im, s5_log_dt, s5_b_re, s5_b_im, s5_c_re, s5_c_im,
             s5_d, s5_glu_w, s5_glu_b, rwkv_mu, rwkv_w0, rwkv_w2, rwkv_a0, rwkv_a2, rwkv_k_k, rwkv_k_a,
             rwkv_r_k, rwkv_ln_w, rwkv_ln_b, w_out, final_g):
    assert w_in.shape[0] == 1, "single-layer block"
    for l in range(1):
        zeros = jnp.zeros((LOW_RANK, D_RWKV), F32)
        w2a2 = jnp.concatenate(
            [jnp.concatenate([rwkv_w2[l], zeros], axis=1), jnp.concatenate([zeros, rwkv_a2[l]], axis=1)],
            axis=0).astype(BF16)
        row = lambda a: a.reshape(1, -1).astype(F32)
        u, gate, r, lw, k, v, kkn, b = _inproj(
            x, row(norm_g[l]), w_in[l].astype(BF16), row(rwkv_mu[l]), w2a2, row(rwkv_w0[l]),
            row(rwkv_a0[l]), row(rwkv_k_k[l]), row(rwkv_k_a[l]))
        ein, toep, fout, lam16 = _s5_operators(
            s5_lam_re[l], s5_lam_im[l], s5_log_dt[l], s5_b_re[l], s5_b_im[l], s5_c_re[l], s5_c_im[l])
        y5 = _s5(u, ein, toep, fout, lam16, row(s5_d[l]))
        yr = _rwkv(r, lw, k, v, kkn, b, gate, row(rwkv_r_k[l]), row(rwkv_ln_w[l]), row(rwkv_ln_b[l]))
        x = _outproj(y5, gate, yr, x, s5_glu_w[l].astype(BF16), row(s5_glu_b[l]),
                     w_out[l].astype(BF16), row(final_g))
    return x


def kernel(x, norm_g, w_in, s5_lam_re, s5_lam_im, s5_log_dt, s5_b_re, s5_b_im, s5_c_re, s5_c_im, s5_d, s5_glu_w, s5_glu_b, rwkv_mu, rwkv_w0, rwkv_w2, rwkv_a0, rwkv_a2, rwkv_k_k, rwkv_k_a, rwkv_r_k, rwkv_ln_w, rwkv_ln_b, w_out, final_g):
    return _forward(x, norm_g, w_in, s5_lam_re, s5_lam_im, s5_log_dt, s5_b_re, s5_b_im, s5_c_re, s5_c_im,
                    s5_d, s5_glu_w, s5_glu_b, rwkv_mu, rwkv_w0, rwkv_w2, rwkv_a0, rwkv_a2, rwkv_k_k,
                    rwkv_k_a, rwkv_r_k, rwkv_ln_w, rwkv_ln_b, w_out, final_g)
```

```python
import math

import jax
import jax.numpy as jnp
from jax import lax
from jax.experimental import pallas as pl
from jax.experimental.pallas import tpu as pltpu

F32 = jnp.float32
BF16 = jnp.bfloat16

D_MODEL = 1024
D_S5 = 512
D_RWKV = 512
S5_GROUP = 16
S5_GROUPS = 32
S5_STATE = 64
RWKV_HEAD = 64
LOW_RANK = 64
D_SHIFT = 3 * D_RWKV + 2 * LOW_RANK
D_IN = 2 * D_S5 + D_SHIFT + D_RWKV
NORM_EPS = 1e-6
GN_EPS = 64e-5

LANES = 128
TM_IN = 512
IN_SUB = 128
TM_OUT = 512
S5_CHUNK = 16
S5_LANE_BLOCKS = D_S5 // LANES
S5_GROUPS_PER_BLOCK = LANES // S5_GROUP
S5_BLOCK_STATE = 2 * S5_GROUPS_PER_BLOCK * S5_STATE
RWKV_CHUNK = 64
RWKV_TB = 512
RWKV_PREP_CHUNKS = 4
RWKV_OUT_CHUNKS = 2
HEAD_PAIRS = D_RWKV // LANES
RWKV_QUADS = HEAD_PAIRS // 2
VMEM_LIMIT = 56 * 1024 * 1024


def _dot(a, b):
    return jnp.dot(a, b, preferred_element_type=F32)


def _dot_nt(a, b):
    return lax.dot_general(a, b, (((1,), (1,)), ((), ())), preferred_element_type=F32)


def _dot_tn(a, b):
    return lax.dot_general(a, b, (((0,), (0,)), ((), ())), preferred_element_type=F32)


def _split2(x):
    hi = x.astype(BF16)
    return hi, (x - hi.astype(F32)).astype(BF16)


def _dot_exact_rhs(x, m):
    hi, lo = _split2(x)
    return _dot(hi, m) + _dot(lo, m)


def _dot_exact_lhs(m, x):
    hi, lo = _split2(x)
    return _dot(m, hi) + _dot(m, lo)


def _sigmoid(x):
    return 0.5 + 0.5 * jnp.tanh(0.5 * x)


def _head_ones():
    r = lax.broadcasted_iota(jnp.int32, (LANES, LANES), 0)
    c = lax.broadcasted_iota(jnp.int32, (LANES, LANES), 1)
    return (r < RWKV_HEAD) == (c < RWKV_HEAD)


def _inproj_kernel(x_ref, g_ref, win_ref, mu_ref, w2a2_ref, w0_ref, a0_ref, kk_ref, ka_ref,
                   u_ref, gate_ref, r_ref, lw_ref, k_ref, v_ref, kkn_ref, b_ref, carry_ref):
    t = pl.program_id(1)

    @pl.when(t == 0)
    def _():
        carry_ref[...] = jnp.zeros_like(carry_ref)

    ones = _head_ones().astype(BF16)
    carry = carry_ref[...]
    for s in range(TM_IN // IN_SUB):
        rs = slice(IN_SUB * s, IN_SUB * (s + 1))
        x = x_ref[0, rs, :]
        ms = jnp.mean(x * x, axis=-1, keepdims=True)
        h = (x * lax.rsqrt(ms + NORM_EPS) * g_ref[...]).astype(BF16)

        def proj(lo, hi):
            return _dot(h, win_ref[:, lo:hi])

        u_ref[0, rs, :] = proj(0, D_S5)
        z5 = proj(D_S5, 2 * D_S5)
        gate_ref[0, rs, :D_S5] = z5 * _sigmoid(z5)
        zr = proj(2 * D_S5 + D_SHIFT, D_IN)
        gate_ref[0, rs, D_S5:] = zr * _sigmoid(zr)

        rw = proj(2 * D_S5, 2 * D_S5 + D_SHIFT)
        prev = pltpu.roll(rw, 1, axis=0)
        row = lax.broadcasted_iota(jnp.int32, rw.shape, 0)
        prev = jnp.where(row == 0, carry, prev)
        carry = rw[IN_SUB - 1:IN_SUB, :]
        sh = rw + (prev - rw) * mu_ref[...]

        r = sh[:, 0:D_RWKV]
        k = sh[:, D_RWKV:2 * D_RWKV]
        v = sh[:, 2 * D_RWKV:3 * D_RWKV]
        wa = sh[:, 3 * D_RWKV:]
        lane = lax.broadcasted_iota(jnp.int32, wa.shape, 1)
        wa = jnp.where(lane < LOW_RANK, jnp.tanh(wa), wa)
        za = _dot(wa.astype(BF16), w2a2_ref[...])
        lw = -math.exp(-0.5) * _sigmoid(w0_ref[...] + za[:, :D_RWKV])
        iclr = _sigmoid(a0_ref[...] + za[:, D_RWKV:])

        kk = k * kk_ref[...]
        sq = kk * kk
        ss = jnp.concatenate(
            [_dot_exact_rhs(sq[:, LANES * p:LANES * (p + 1)], ones) for p in range(HEAD_PAIRS)], axis=1)
        kkn = kk * lax.rsqrt(jnp.maximum(ss, 1e-24))

        r_ref[0, rs, :] = r
        lw_ref[0, rs, :] = lw
        k_ref[0, rs, :] = k * (1.0 + (iclr - 1.0) * ka_ref[...])
        v_ref[0, rs, :] = v
        kkn_ref[0, rs, :] = kkn
        b_ref[0, rs, :] = kkn * iclr
    carry_ref[...] = carry


def _inproj(x, norm_g, w_in, mu, w2a2, w0, a0, k_k, k_a):
    bsz, seqlen, _ = x.shape
    tok = lambda w: pl.BlockSpec((1, TM_IN, w), lambda b, t: (b, t, 0))
    full = lambda a: pl.BlockSpec(a.shape, lambda b, t: (0,) * a.ndim)
    out = lambda w: jax.ShapeDtypeStruct((bsz, seqlen, w), F32)
    params = (norm_g, w_in, mu, w2a2, w0, a0, k_k, k_a)
    return pl.pallas_call(
        _inproj_kernel,
        grid=(bsz, seqlen // TM_IN),
        in_specs=[tok(D_MODEL)] + [full(a) for a in params],
        out_specs=[tok(D_S5), tok(2 * D_S5)] + [tok(D_RWKV)] * 6,
        out_shape=[out(D_S5), out(2 * D_S5)] + [out(D_RWKV)] * 6,
        scratch_shapes=[pltpu.VMEM((1, D_SHIFT), F32)],
        compiler_params=pltpu.CompilerParams(
            dimension_semantics=("arbitrary", "arbitrary"), vmem_limit_bytes=VMEM_LIMIT),
        name="inproj",
    )(x, *params)


def _s5_kernel(u_ref, ein_ref, toep_ref, fout_ref, lam_ref, d_ref, y_ref, e_ref, sp_ref):
    nch = e_ref.shape[0]
    half = S5_BLOCK_STATE // 2
    xs = [u_ref[0, pl.ds(j, nch, stride=S5_CHUNK), :] for j in range(S5_CHUNK)]
    xcat = jnp.concatenate([x.astype(BF16) for x in xs], axis=1)
    e_ref[...] = _dot(xcat, ein_ref[0])

    lre = lam_ref[0, :, :half]
    lim = lam_ref[0, :, half:]

    def step(c, s):
        sre, sim = s
        sp_ref[pl.ds(c, 1), :half] = sre
        sp_ref[pl.ds(c, 1), half:] = sim
        e = e_ref[pl.ds(c, 1), :]
        return (lre * sre - lim * sim + e[:, :half], lre * sim + lim * sre + e[:, half:])

    zero = jnp.zeros((1, half), F32)
    lax.fori_loop(0, nch, step, (zero, zero))

    sp = sp_ref[...].astype(BF16)
    d = d_ref[...]
    pairs = S5_CHUNK // 2
    for tp in range(pairs):
        y = _dot(xcat[:, :2 * LANES * (tp + 1)], toep_ref[0, 2 * LANES * (pairs - 1 - tp):, :])
        y = y + _dot(sp, fout_ref[0, :, 2 * LANES * tp:2 * LANES * (tp + 1)])
        for i in range(2):
            t = 2 * tp + i
            y_ref[0, pl.ds(t, nch, stride=S5_CHUNK), :] = y[:, LANES * i:LANES * (i + 1)] + d * xs[t]


def _cmul(ar, ai, br, bi):
    return ar * br - ai * bi, ar * bi + ai * br


def _block_diag_expand(compact, width, inner, transposed):
    k = compact.shape[1] if transposed else compact.shape[2]
    group_w = S5_GROUPS_PER_BLOCK * inner
    kk = jnp.arange(k)
    ww = jnp.arange(width)
    sel = ((kk[:, None] // inner == ww[None, :] // group_w)
           & (kk[:, None] % inner == ww[None, :] % inner)).astype(F32)
    wide_group = (ww % group_w) // inner
    if transposed:
        full = jnp.matmul(sel.T, compact)
        narrow_group = (jnp.arange(compact.shape[2]) % LANES) // S5_GROUP
        keep = wide_group[:, None] == narrow_group[None, :]
    else:
        full = jnp.matmul(compact, sel)
        narrow_group = (jnp.arange(compact.shape[1]) % LANES) // S5_GROUP
        keep = narrow_group[:, None] == wide_group[None, :]
    return jnp.where(keep, full, 0.0)


def _s5_operators(lam_re, lam_im, log_dt, b_re, b_im, c_re, c_im):
    c16, q4, gl = S5_CHUNK, S5_LANE_BLOCKS, S5_GROUPS_PER_BLOCK
    hp = lax.Precision.HIGHEST
    dt = jnp.exp(log_dt)[:, None]
    n = jnp.arange(c16 + 1, dtype=F32)[:, None, None]
    mag = jnp.exp(n * (lam_re * dt)[None])
    ang = n * (lam_im * dt)[None]
    pw_r, pw_i = mag * jnp.cos(ang), mag * jnp.sin(ang)
    den = lam_re * lam_re + lam_im * lam_im
    q_r, q_i = _cmul(pw_r[1] - 1.0, pw_i[1], lam_re / den, -lam_im / den)
    bb_r, bb_i = _cmul(q_r[..., None], q_i[..., None], b_re, b_im)

    w_r, w_i = _cmul(pw_r[:c16, :, :, None], pw_i[:c16, :, :, None], bb_r[None], bb_i[None])
    kt = (jnp.einsum('gop,tgpi->tgio', c_re, w_r, precision=hp)
          - jnp.einsum('gop,tgpi->tgio', c_im, w_i, precision=hp))
    d8 = jnp.arange(c16 // 2)
    lag = 2 * d8[:, None, None] + jnp.arange(2)[None, None, :] - jnp.arange(2)[None, :, None]
    kp = jnp.where((lag >= 0)[..., None, None, None], kt[jnp.maximum(lag, 0)], 0.0)
    kp = kp.reshape(c16 // 2, 2, 2, q4, gl, S5_GROUP, S5_GROUP)
    kp = jnp.transpose(kp, (3, 0, 1, 4, 5, 2, 6))[:, ::-1]
    toep = _block_diag_expand(kp.reshape(q4, c16 * LANES, 2 * S5_GROUP), 2 * LANES, S5_GROUP, False)

    back = c16 - 1 - jnp.arange(c16)
    e_r, e_i = _cmul(pw_r[back][..., None], pw_i[back][..., None], bb_r[None], bb_i[None])
    ein = jnp.stack([e_r, e_i]).reshape(2, c16, q4, gl, S5_STATE, S5_GROUP)
    ein = jnp.transpose(ein, (2, 1, 3, 5, 0, 4)).reshape(q4, c16 * LANES, 2 * S5_STATE)
    ein = _block_diag_expand(ein, S5_BLOCK_STATE, S5_STATE, False)

    f_r, f_i = _cmul(c_re[None], c_im[None], pw_r[1:, :, None, :], pw_i[1:, :, None, :])
    fo = jnp.stack([f_r, -f_i]).reshape(2, c16, q4, gl, S5_GROUP, S5_STATE)
    fo = jnp.transpose(fo, (2, 0, 5, 1, 3, 4)).reshape(q4, 2 * S5_STATE, c16 * LANES)
    fo = _block_diag_expand(fo, S5_BLOCK_STATE, S5_STATE, True)

    lam16 = jnp.concatenate([pw_r[c16].reshape(q4, 1, gl * S5_STATE),
                             pw_i[c16].reshape(q4, 1, gl * S5_STATE)], axis=-1)
    return ein.astype(BF16), toep.astype(BF16), fo.astype(BF16), lam16


def _s5(u, ein, toep, fout, lam16, d):
    bsz, seqlen, _ = u.shape
    nch = seqlen // S5_CHUNK
    per_q = lambda a: pl.BlockSpec((1,) + a.shape[1:], lambda q, b: (q,) + (0,) * (a.ndim - 1))
    seq = pl.BlockSpec((1, seqlen, LANES), lambda q, b: (b, 0, q))
    return pl.pallas_call(
        _s5_kernel,
        grid=(S5_LANE_BLOCKS, bsz),
        in_specs=[seq, per_q(ein), per_q(toep), per_q(fout), per_q(lam16),
                  pl.BlockSpec((1, LANES), lambda q, b: (0, q))],
        out_specs=seq,
        out_shape=jax.ShapeDtypeStruct(u.shape, F32),
        scratch_shapes=[pltpu.VMEM((nch, S5_BLOCK_STATE), F32), pltpu.VMEM((nch, S5_BLOCK_STATE), F32)],
        compiler_params=pltpu.CompilerParams(
            dimension_semantics=("arbitrary", "arbitrary"), vmem_limit_bytes=VMEM_LIMIT),
        name="s5",
    )(u, ein, toep, fout, lam16, d)


def _stack_heads(x, lo):
    return jnp.concatenate([jnp.where(lo, x, 0.0), jnp.where(lo, 0.0, x)], axis=0)


def _alternate(*stage_generators):
    pending = list(stage_generators)
    while pending:
        for g in list(pending):
            if next(g, StopIteration) is StopIteration:
                pending.remove(g)


def _head_block_diag(x, head_id, order):
    return jnp.concatenate([jnp.where(head_id == h, x, 0.0) for h in order], axis=0)


def _rwkv_kernel(r_ref, lw_ref, k_ref, v_ref, kk_ref, b_ref, gate_ref, rk_ref, lnw_ref, lnb_ref,
                 y_ref, s_ref, wr_ref, w2_ref, bkt_ref, prb_ref, yv_ref, e2_ref, u_ref):
    tc = RWKV_CHUNK
    nchunks = RWKV_TB // tc
    qw = 2 * LANES

    @pl.when(pl.program_id(1) == 0)
    def _():
        s_ref[...] = jnp.zeros_like(s_ref)

    row = lax.broadcasted_iota(jnp.int32, (tc, qw), 0)
    lane = lax.broadcasted_iota(jnp.int32, (tc, qw), 1)
    col = jnp.bitwise_and(lane, RWKV_HEAD - 1)
    head_id = lane // RWKV_HEAD
    heads = tuple(range(qw // RWKV_HEAD))
    heads_swapped = (1, 0, 3, 2)
    eye_q = (row == col).astype(F32)
    row_p = lax.broadcasted_iota(jnp.int32, (tc, LANES), 0)
    lane_p = lax.broadcasted_iota(jnp.int32, (tc, LANES), 1)
    col_p = jnp.bitwise_and(lane_p, RWKV_HEAD - 1)
    lo_p = lane_p < RWKV_HEAD
    lo_p2 = lax.broadcasted_iota(jnp.int32, (2 * tc, LANES), 1) < RWKV_HEAD
    stril = row_p > col_p
    tril = row_p >= col_p
    state_bd = _head_ones()
    r4 = lax.broadcasted_iota(jnp.int32, (qw, qw), 0)
    l4 = lax.broadcasted_iota(jnp.int32, (qw, qw), 1)
    ones = ((r4 // RWKV_HEAD) == (l4 // RWKV_HEAD)).astype(BF16)
    tri = (lax.broadcasted_iota(jnp.int32, (tc, tc), 0)
           >= lax.broadcasted_iota(jnp.int32, (tc, tc), 1)).astype(BF16)
    bf = lambda t: t.astype(BF16)
    up, dn = slice(0, tc), slice(tc, 2 * tc)

    def prepare(i, carry):
        units = []
        for j in range(RWKV_PREP_CHUNKS):
            c = i * RWKV_PREP_CHUNKS + j
            rows = pl.ds(pl.multiple_of(c * tc, tc), tc)
            lw_all = lw_ref[0, rows, :]
            g_all = _dot_exact_lhs(tri, lw_all)
            for q in range(RWKV_QUADS):
                qs = slice(qw * q, qw * (q + 1))
                units.append(dict(c=c, q=q, rows=rows, qs=qs, lw=lw_all[:, qs], g=g_all[:, qs]))
        for d in units:
            rows, qs, g = d["rows"], d["qs"], d["g"]
            m = 0.5 * g[tc - 1:tc, :]
            e_out = jnp.exp(m - g)
            em = jnp.exp(m)
            rt = r_ref[0, rows, qs] * jnp.exp(g - m)
            bt = b_ref[0, rows, qs] * e_out
            kt = k_ref[0, rows, qs] * e_out
            d["em"] = em
            d["at"] = bf(-kk_ref[0, rows, qs] * jnp.exp(g - d["lw"] - m))
            d["rt"], d["bt"], d["kt"] = bf(rt), bf(bt), bf(kt)
            d["rts"] = bf(rt * em)
            d["vsw"] = bf(_head_block_diag(v_ref[0, rows, qs], head_id, heads_swapped))
            bkt, e2 = [], []
            for j in range(2):
                ps = slice(LANES * j, LANES * (j + 1))
                bks = jnp.concatenate([bt[:, ps], kt[:, ps]], axis=0) * em[:, ps]
                bkt.append(bf(jnp.transpose(bks)))
                e2.append(jnp.transpose(jnp.broadcast_to(em[:, ps] * em[:, ps], (LANES, LANES))))
            bkt_ref[d["c"], d["q"]] = jnp.concatenate(bkt, axis=0)
            e2_ref[d["c"], d["q"]] = jnp.concatenate(e2, axis=1)
        for d in units:
            d["o"] = []
            for j in range(2):
                ps = slice(LANES * j, LANES * (j + 1))
                bk = jnp.concatenate([d["bt"][:, ps], d["kt"][:, ps]], axis=0)
                kb = jnp.concatenate([d["kt"][:, ps], d["bt"][:, ps]], axis=0)
                rhs = jnp.concatenate([jnp.where(lo_p2, bk, 0.0), jnp.where(lo_p2, 0.0, kb)], axis=0)
                lhs = jnp.concatenate([d["at"][:, ps], d["rt"][:, ps]], axis=0)
                d["o"].append(_dot_nt(lhs, rhs))
        for d in units:
            h0 = [o[:, :LANES] for o in d["o"]]
            h1 = [o[:, LANES:] for o in d["o"]]

            def pick(rws, first, second, mask):
                return jnp.concatenate(
                    [jnp.where(mask, jnp.where(lo_p, first[j][rws], second[j][rws]), 0.0) for j in range(2)],
                    axis=1)

            d["ap"] = pick(up, h0, h1, stril)
            d["p_ak"] = bf(pick(up, h1, h0, stril))
            p_rb = bf(pick(dn, h0, h1, tril))
            d["p_rk"] = bf(pick(dn, h1, h0, tril))
            prb_ref[d["c"], d["q"]] = p_rb
            d["tinv"] = eye_q + d["ap"]
        levels = int(math.log2(tc)) - 1
        for d in units:
            d["ap"] = _dot(bf(d["ap"]), bf(_head_block_diag(d["ap"], head_id, heads)))
        for lvl in range(levels):
            for d in units:
                rhs = bf(_head_block_diag(d["ap"], head_id, heads))
                if lvl < levels - 1:
                    out = _dot(bf(jnp.concatenate([d["tinv"], d["ap"]], axis=0)), rhs)
                    d["tinv"], d["ap"] = d["tinv"] + out[:tc], out[tc:]
                else:
                    d["tinv"] = d["tinv"] + _dot(bf(d["tinv"]), rhs)
        for d in units:
            d["tinv"] = bf(d["tinv"])
            out = _dot(jnp.concatenate([d["p_ak"], d["p_rk"]], axis=0), d["vsw"])
            d["xv"] = out[:tc]
            yv_ref[d["c"], d["q"]] = out[tc:]
        for d in units:
            w1s, w2 = [], []
            for j in range(2):
                ps = slice(LANES * j, LANES * (j + 1))
                rhs = jnp.concatenate([_stack_heads(d["at"][:, ps], lo_p),
                                       bf(_stack_heads(d["xv"][:, ps], lo_p))], axis=1)
                w = _dot(d["tinv"][:, ps], rhs)
                w1s.append(bf(w[:, :LANES] * d["em"][:, ps]))
                w2.append(w[:, LANES:])
            wr_ref[d["c"], d["q"]] = jnp.concatenate(w1s + [d["rts"][:, :LANES], d["rts"][:, LANES:]], axis=0)
            w2_ref[d["c"], d["q"]] = jnp.concatenate(w2, axis=1)
        return carry

    lax.fori_loop(0, nchunks // RWKV_PREP_CHUNKS, prepare, 0)

    def recur_step(c):
        rows = pl.ds(pl.multiple_of(c * tc, tc), tc)
        units = [dict(q=q, qs=slice(qw * q, qw * (q + 1))) for q in range(RWKV_QUADS)]
        for d in units:
            d["h0"] = s_ref[d["q"]]
        for d in units:
            out = _dot(wr_ref[c, d["q"]], bf(d["h0"]))
            d["u"] = (jnp.concatenate([out[:tc, :LANES], out[tc:2 * tc, LANES:]], axis=1)
                      + w2_ref[c, d["q"]])
            d["yr"] = jnp.concatenate([out[2 * tc:3 * tc, :LANES], out[3 * tc:, LANES:]], axis=1)
        yield
        for d in units:
            q = d["q"]
            uv = bf(jnp.concatenate([d["u"], v_ref[0, rows, d["qs"]]], axis=0))
            upd = _dot(bkt_ref[c, q], uv)
            upd = jnp.concatenate([jnp.where(state_bd, upd[:LANES, :LANES], 0.0),
                                   jnp.where(state_bd, upd[LANES:, LANES:], 0.0)], axis=1)
            s_ref[q] = d["h0"] * e2_ref[c, q] + upd
            u_ref[c, q] = bf(d["u"])
            yv_ref[c, q] = yv_ref[c, q] + d["yr"]

    def emit_step(c):
        rows = pl.ds(pl.multiple_of(c * tc, tc), tc)
        units = [dict(q=q, qs=slice(qw * q, qw * (q + 1))) for q in range(RWKV_QUADS)]
        for d in units:
            q = d["q"]
            d["y"] = yv_ref[c, q] + _dot(prb_ref[c, q], _head_block_diag(u_ref[c, q], head_id, heads))
        yield
        for d in units:
            qs, y = d["qs"], d["y"]
            rkr = r_ref[0, rows, qs] * k_ref[0, rows, qs] * rk_ref[:, qs]
            d["sums"] = _dot_exact_rhs(jnp.concatenate([y, rkr], axis=0), ones)
        yield
        for d in units:
            d["dev"] = d["y"] - d["sums"][:tc] * (1.0 / RWKV_HEAD)
            d["var"] = _dot_exact_rhs(d["dev"] * d["dev"], ones) * (1.0 / RWKV_HEAD)
        for d in units:
            qs = d["qs"]
            yn = d["dev"] * lax.rsqrt(d["var"] + GN_EPS) * lnw_ref[:, qs] + lnb_ref[:, qs]
            y_ref[0, rows, qs] = (yn + d["sums"][tc:] * v_ref[0, rows, qs]) * gate_ref[0, rows, qs]

    def recur_and_emit(c, carry):
        _alternate(emit_step(c - 1), recur_step(c))
        return carry

    _alternate(recur_step(0))
    lax.fori_loop(1, nchunks, recur_and_emit, 0)
    _alternate(emit_step(nchunks - 1))


def _rwkv(r, lw, k, v, kkn, b, gate, r_k, ln_w, ln_b):
    bsz, seqlen, _ = r.shape
    nchunks = RWKV_TB // RWKV_CHUNK
    tok = pl.BlockSpec((1, RWKV_TB, D_RWKV), lambda bi, t: (bi, t, 0))
    gate_spec = pl.BlockSpec((1, RWKV_TB, D_RWKV), lambda bi, t: (bi, t, 1))
    vec = pl.BlockSpec((1, D_RWKV), lambda bi, t: (0, 0))
    unit = lambda rows, cols, dt: pltpu.VMEM((nchunks, RWKV_QUADS, rows, cols), dt)
    return pl.pallas_call(
        _rwkv_kernel,
        grid=(bsz, seqlen // RWKV_TB),
        in_specs=[tok] * 6 + [gate_spec] + [vec] * 3,
        out_specs=tok,
        out_shape=jax.ShapeDtypeStruct(r.shape, F32),
        scratch_shapes=[
            pltpu.VMEM((RWKV_QUADS, LANES, 2 * LANES), F32),
            unit(4 * RWKV_CHUNK, LANES, BF16),
            unit(RWKV_CHUNK, 2 * LANES, F32),
            unit(2 * LANES, 2 * RWKV_CHUNK, BF16),
            unit(RWKV_CHUNK, 2 * LANES, BF16),
            unit(RWKV_CHUNK, 2 * LANES, F32),
            unit(LANES, 2 * LANES, F32),
            unit(RWKV_CHUNK, 2 * LANES, BF16),
        ],
        compiler_params=pltpu.CompilerParams(
            dimension_semantics=("arbitrary", "arbitrary"), vmem_limit_bytes=VMEM_LIMIT),
        name="rwkv7",
    )(r, lw, k, v, kkn, b, gate, r_k, ln_w, ln_b)


def _out_kernel(y5_ref, gate_ref, yr_ref, x_ref, gluw_ref, glub_ref, wout_ref, fg_ref, o_ref):
    for s in range(TM_OUT // IN_SUB):
        rs = slice(IN_SUB * s, IN_SUB * (s + 1))
        y = y5_ref[0, rs, :]
        y = 0.5 * y * (1.0 + jnp.tanh(math.sqrt(2.0 / math.pi) * (y + 0.044715 * (y * y * y))))
        glu = _sigmoid(_dot(y.astype(BF16), gluw_ref[...]) + glub_ref[...])
        ys = y * glu * gate_ref[0, rs, :]
        ycat = jnp.concatenate([ys, yr_ref[0, rs, :]], axis=-1).astype(BF16)
        xn = x_ref[0, rs, :] + _dot(ycat, wout_ref[...])
        ms = jnp.mean(xn * xn, axis=-1, keepdims=True)
        o_ref[0, rs, :] = xn * lax.rsqrt(ms + NORM_EPS) * fg_ref[...]


def _outproj(y5, gate, yr, x, glu_w, glu_b, w_out, final_g):
    bsz, seqlen, _ = x.shape
    tok = lambda w: pl.BlockSpec((1, TM_OUT, w), lambda b, t: (b, t, 0))
    full = lambda a: pl.BlockSpec(a.shape, lambda b, t: (0,) * a.ndim)
    return pl.pallas_call(
        _out_kernel,
        grid=(bsz, seqlen // TM_OUT),
        in_specs=[tok(D_S5), tok(D_S5), tok(D_RWKV), tok(D_MODEL),
                  full(glu_w), full(glu_b), full(w_out), full(final_g)],
        out_specs=tok(D_MODEL),
        out_shape=jax.ShapeDtypeStruct(x.shape, F32),
        compiler_params=pltpu.CompilerParams(
            dimension_semantics=("arbitrary", "arbitrary"), vmem_limit_bytes=VMEM_LIMIT),
        name="outproj",
    )(y5, gate, yr, x, glu_w, glu_b, w_out, final_g)


@jax.jit
def _forward(x, norm_g, w_in, s5_lam_re, s5_lam_im, s5_log_dt, s5_b_re, s5_b_im, s5_c_re, s5_c_im,
             s5_d, s5_glu_w, s5_glu_b, rwkv_mu, rwkv_w0, rwkv_w2, rwkv_a0, rwkv_a2, rwkv_k_k, rwkv_k_a,
             rwkv_r_k, rwkv_ln_w, rwkv_ln_b, w_out, final_g):
    assert w_in.shape[0] == 1, "single-layer block"
    for l in range(1):
        zeros = jnp.zeros((LOW_RANK, D_RWKV), F32)
        w2a2 = jnp.concatenate(
            [jnp.concatenate([rwkv_w2[l], zeros], axis=1), jnp.concatenate([zeros, rwkv_a2[l]], axis=1)],
            axis=0).astype(BF16)
        row = lambda a: a.reshape(1, -1).astype(F32)
        u, gate, r, lw, k, v, kkn, b = _inproj(
            x, row(norm_g[l]), w_in[l].astype(BF16), row(rwkv_mu[l]), w2a2, row(rwkv_w0[l]),
            row(rwkv_a0[l]), row(rwkv_k_k[l]), row(rwkv_k_a[l]))
        ein, toep, fout, lam16 = _s5_operators(
            s5_lam_re[l], s5_lam_im[l], s5_log_dt[l], s5_b_re[l], s5_b_im[l], s5_c_re[l], s5_c_im[l])
        y5 = _s5(u, ein, toep, fout, lam16, row(s5_d[l]))
        yr = _rwkv(r, lw, k, v, kkn, b, gate, row(rwkv_r_k[l]), row(rwkv_ln_w[l]), row(rwkv_ln_b[l]))
        x = _outproj(y5, gate, yr, x, s5_glu_w[l].astype(BF16), row(s5_glu_b[l]),
                     w_out[l].astype(BF16), row(final_g))
    return x


def kernel(x, norm_g, w_in, s5_lam_re, s5_lam_im, s5_log_dt, s5_b_re, s5_b_im, s5_c_re, s5_c_im, s5_d, s5_glu_w, s5_glu_b, rwkv_mu, rwkv_w0, rwkv_w2, rwkv_a0, rwkv_a2, rwkv_k_k, rwkv_k_a, rwkv_r_k, rwkv_ln_w, rwkv_ln_b, w_out, final_g):
    return _forward(x, norm_g, w_in, s5_lam_re, s5_lam_im, s5_log_dt, s5_b_re, s5_b_im, s5_c_re, s5_c_im,
                    s5_d, s5_glu_w, s5_glu_b, rwkv_mu, rwkv_w0, rwkv_w2, rwkv_a0, rwkv_a2, rwkv_k_k,
                    rwkv_k_a, rwkv_r_k, rwkv_ln_w, rwkv_ln_b, w_out, final_g)
```

```python
import math

import jax
import jax.numpy as jnp
from jax import lax
from jax.experimental import pallas as pl
from jax.experimental.pallas import tpu as pltpu

F32 = jnp.float32
BF16 = jnp.bfloat16

D_MODEL = 1024
D_S5 = 512
D_RWKV = 512
S5_GROUP = 16
S5_GROUPS = 32
S5_STATE = 64
RWKV_HEAD = 64
LOW_RANK = 64
D_SHIFT = 3 * D_RWKV + 2 * LOW_RANK
D_IN = 2 * D_S5 + D_SHIFT + D_RWKV
NORM_EPS = 1e-6
GN_EPS = 64e-5

LANES = 128
TM_IN = 512
IN_SUB = 128
TM_OUT = 512
S5_CHUNK = 16
S5_LANE_BLOCKS = D_S5 // LANES
S5_GROUPS_PER_BLOCK = LANES // S5_GROUP
S5_BLOCK_STATE = 2 * S5_GROUPS_PER_BLOCK * S5_STATE
RWKV_CHUNK = 64
RWKV_TB = 512
RWKV_PREP_CHUNKS = 4
RWKV_OUT_CHUNKS = 2
HEAD_PAIRS = D_RWKV // LANES
RWKV_QUADS = HEAD_PAIRS // 2
VMEM_LIMIT = 56 * 1024 * 1024


def _dot(a, b):
    return jnp.dot(a, b, preferred_element_type=F32)


def _dot_nt(a, b):
    return lax.dot_general(a, b, (((1,), (1,)), ((), ())), preferred_element_type=F32)


def _dot_tn(a, b):
    return lax.dot_general(a, b, (((0,), (0,)), ((), ())), preferred_element_type=F32)


def _split2(x):
    hi = x.astype(BF16)
    return hi, (x - hi.astype(F32)).astype(BF16)


def _dot_exact_rhs(x, m):
    hi, lo = _split2(x)
    return _dot(hi, m) + _dot(lo, m)


def _dot_exact_lhs(m, x):
    hi, lo = _split2(x)
    return _dot(m, hi) + _dot(m, lo)


def _sigmoid(x):
    return 0.5 + 0.5 * jnp.tanh(0.5 * x)


def _head_ones():
    r = lax.broadcasted_iota(jnp.int32, (LANES, LANES), 0)
    c = lax.broadcasted_iota(jnp.int32, (LANES, LANES), 1)
    return (r < RWKV_HEAD) == (c < RWKV_HEAD)


def _inproj_kernel(x_ref, g_ref, win_ref, mu_ref, w2a2_ref, w0_ref, a0_ref, kk_ref, ka_ref,
                   u_ref, gate_ref, r_ref, lw_ref, k_ref, v_ref, kkn_ref, b_ref, carry_ref):
    t = pl.program_id(1)

    @pl.when(t == 0)
    def _():
        carry_ref[...] = jnp.zeros_like(carry_ref)

    ones = _head_ones().astype(BF16)
    carry = carry_ref[...]
    for s in range(TM_IN // IN_SUB):
        rs = slice(IN_SUB * s, IN_SUB * (s + 1))
        x = x_ref[0, rs, :]
        ms = jnp.mean(x * x, axis=-1, keepdims=True)
        h = (x * lax.rsqrt(ms + NORM_EPS) * g_ref[...]).astype(BF16)

        def proj(lo, hi):
            return _dot(h, win_ref[:, lo:hi])

        u_ref[0, rs, :] = proj(0, D_S5)
        z5 = proj(D_S5, 2 * D_S5)
        gate_ref[0, rs, :D_S5] = (z5 * _sigmoid(z5)).astype(BF16)
        zr = proj(2 * D_S5 + D_SHIFT, D_IN)
        gate_ref[0, rs, D_S5:] = (zr * _sigmoid(zr)).astype(BF16)

        rw = proj(2 * D_S5, 2 * D_S5 + D_SHIFT)
        prev = pltpu.roll(rw, 1, axis=0)
        row = lax.broadcasted_iota(jnp.int32, rw.shape, 0)
        prev = jnp.where(row == 0, carry, prev)
        carry = rw[IN_SUB - 1:IN_SUB, :]
        sh = rw + (prev - rw) * mu_ref[...]

        r = sh[:, 0:D_RWKV]
        k = sh[:, D_RWKV:2 * D_RWKV]
        v = sh[:, 2 * D_RWKV:3 * D_RWKV]
        wa = sh[:, 3 * D_RWKV:]
        lane = lax.broadcasted_iota(jnp.int32, wa.shape, 1)
        wa = jnp.where(lane < LOW_RANK, jnp.tanh(wa), wa)
        za = _dot(wa.astype(BF16), w2a2_ref[...])
        lw = -math.exp(-0.5) * _sigmoid(w0_ref[...] + za[:, :D_RWKV])
        iclr = _sigmoid(a0_ref[...] + za[:, D_RWKV:])

        kk = k * kk_ref[...]
        sq = kk * kk
        ss = jnp.concatenate(
            [_dot_exact_rhs(sq[:, LANES * p:LANES * (p + 1)], ones) for p in range(HEAD_PAIRS)], axis=1)
        kkn = kk * lax.rsqrt(jnp.maximum(ss, 1e-24))

        r_ref[0, rs, :] = r.astype(BF16)
        lw_ref[0, rs, :] = lw
        k_ref[0, rs, :] = (k * (1.0 + (iclr - 1.0) * ka_ref[...])).astype(BF16)
        v_ref[0, rs, :] = v.astype(BF16)
        kkn_ref[0, rs, :] = kkn.astype(BF16)
        b_ref[0, rs, :] = (kkn * iclr).astype(BF16)
    carry_ref[...] = carry


def _inproj(x, norm_g, w_in, mu, w2a2, w0, a0, k_k, k_a):
    bsz, seqlen, _ = x.shape
    tok = lambda w: pl.BlockSpec((1, TM_IN, w), lambda b, t: (b, t, 0))
    full = lambda a: pl.BlockSpec(a.shape, lambda b, t: (0,) * a.ndim)
    out = lambda w, dt: jax.ShapeDtypeStruct((bsz, seqlen, w), dt)
    params = (norm_g, w_in, mu, w2a2, w0, a0, k_k, k_a)
    return pl.pallas_call(
        _inproj_kernel,
        grid=(bsz, seqlen // TM_IN),
        in_specs=[tok(D_MODEL)] + [full(a) for a in params],
        out_specs=[tok(D_S5), tok(2 * D_S5)] + [tok(D_RWKV)] * 6,
        out_shape=[out(D_S5, F32), out(2 * D_S5, BF16)]
        + [out(D_RWKV, F32 if i == 1 else BF16) for i in range(6)],
        scratch_shapes=[pltpu.VMEM((1, D_SHIFT), F32)],
        compiler_params=pltpu.CompilerParams(
            dimension_semantics=("arbitrary", "arbitrary"), vmem_limit_bytes=VMEM_LIMIT),
        name="inproj",
    )(x, *params)


def _s5_kernel(u_ref, ein_ref, toep_ref, fout_ref, lam_ref, d_ref, y_ref, e_ref, sp_ref):
    nch = e_ref.shape[0]
    half = S5_BLOCK_STATE // 2
    xs = [u_ref[0, pl.ds(j, nch, stride=S5_CHUNK), :] for j in range(S5_CHUNK)]
    xcat = jnp.concatenate([x.astype(BF16) for x in xs], axis=1)
    e_ref[...] = _dot(xcat, ein_ref[0])

    lre = lam_ref[0, :, :half]
    lim = lam_ref[0, :, half:]

    def step(c, s):
        sre, sim = s
        sp_ref[pl.ds(c, 1), :half] = sre
        sp_ref[pl.ds(c, 1), half:] = sim
        e = e_ref[pl.ds(c, 1), :]
        return (lre * sre - lim * sim + e[:, :half], lre * sim + lim * sre + e[:, half:])

    zero = jnp.zeros((1, half), F32)
    lax.fori_loop(0, nch, step, (zero, zero))

    sp = sp_ref[...].astype(BF16)
    d = d_ref[...]
    pairs = S5_CHUNK // 2
    for tp in range(pairs):
        y = _dot(xcat[:, :2 * LANES * (tp + 1)], toep_ref[0, 2 * LANES * (pairs - 1 - tp):, :])
        y = y + _dot(sp, fout_ref[0, :, 2 * LANES * tp:2 * LANES * (tp + 1)])
        for i in range(2):
            t = 2 * tp + i
            y_ref[0, pl.ds(t, nch, stride=S5_CHUNK), :] = y[:, LANES * i:LANES * (i + 1)] + d * xs[t]


def _cmul(ar, ai, br, bi):
    return ar * br - ai * bi, ar * bi + ai * br


def _block_diag_expand(compact, width, inner, transposed):
    k = compact.shape[1] if transposed else compact.shape[2]
    group_w = S5_GROUPS_PER_BLOCK * inner
    kk = jnp.arange(k)
    ww = jnp.arange(width)
    sel = ((kk[:, None] // inner == ww[None, :] // group_w)
           & (kk[:, None] % inner == ww[None, :] % inner)).astype(F32)
    wide_group = (ww % group_w) // inner
    if transposed:
        full = jnp.matmul(sel.T, compact)
        narrow_group = (jnp.arange(compact.shape[2]) % LANES) // S5_GROUP
        keep = wide_group[:, None] == narrow_group[None, :]
    else:
        full = jnp.matmul(compact, sel)
        narrow_group = (jnp.arange(compact.shape[1]) % LANES) // S5_GROUP
        keep = narrow_group[:, None] == wide_group[None, :]
    return jnp.where(keep, full, 0.0)


def _s5_operators(lam_re, lam_im, log_dt, b_re, b_im, c_re, c_im):
    c16, q4, gl = S5_CHUNK, S5_LANE_BLOCKS, S5_GROUPS_PER_BLOCK
    hp = lax.Precision.HIGHEST
    dt = jnp.exp(log_dt)[:, None]
    n = jnp.arange(c16 + 1, dtype=F32)[:, None, None]
    mag = jnp.exp(n * (lam_re * dt)[None])
    ang = n * (lam_im * dt)[None]
    pw_r, pw_i = mag * jnp.cos(ang), mag * jnp.sin(ang)
    den = lam_re * lam_re + lam_im * lam_im
    q_r, q_i = _cmul(pw_r[1] - 1.0, pw_i[1], lam_re / den, -lam_im / den)
    bb_r, bb_i = _cmul(q_r[..., None], q_i[..., None], b_re, b_im)

    w_r, w_i = _cmul(pw_r[:c16, :, :, None], pw_i[:c16, :, :, None], bb_r[None], bb_i[None])
    kt = (jnp.einsum('gop,tgpi->tgio', c_re, w_r, precision=hp)
          - jnp.einsum('gop,tgpi->tgio', c_im, w_i, precision=hp))
    d8 = jnp.arange(c16 // 2)
    lag = 2 * d8[:, None, None] + jnp.arange(2)[None, None, :] - jnp.arange(2)[None, :, None]
    kp = jnp.where((lag >= 0)[..., None, None, None], kt[jnp.maximum(lag, 0)], 0.0)
    kp = kp.reshape(c16 // 2, 2, 2, q4, gl, S5_GROUP, S5_GROUP)
    kp = jnp.transpose(kp, (3, 0, 1, 4, 5, 2, 6))[:, ::-1]
    toep = _block_diag_expand(kp.reshape(q4, c16 * LANES, 2 * S5_GROUP), 2 * LANES, S5_GROUP, False)

    back = c16 - 1 - jnp.arange(c16)
    e_r, e_i = _cmul(pw_r[back][..., None], pw_i[back][..., None], bb_r[None], bb_i[None])
    ein = jnp.stack([e_r, e_i]).reshape(2, c16, q4, gl, S5_STATE, S5_GROUP)
    ein = jnp.transpose(ein, (2, 1, 3, 5, 0, 4)).reshape(q4, c16 * LANES, 2 * S5_STATE)
    ein = _block_diag_expand(ein, S5_BLOCK_STATE, S5_STATE, False)

    f_r, f_i = _cmul(c_re[None], c_im[None], pw_r[1:, :, None, :], pw_i[1:, :, None, :])
    fo = jnp.stack([f_r, -f_i]).reshape(2, c16, q4, gl, S5_GROUP, S5_STATE)
    fo = jnp.transpose(fo, (2, 0, 5, 1, 3, 4)).reshape(q4, 2 * S5_STATE, c16 * LANES)
    fo = _block_diag_expand(fo, S5_BLOCK_STATE, S5_STATE, True)

    lam16 = jnp.concatenate([pw_r[c16].reshape(q4, 1, gl * S5_STATE),
                             pw_i[c16].reshape(q4, 1, gl * S5_STATE)], axis=-1)
    return ein.astype(BF16), toep.astype(BF16), fo.astype(BF16), lam16


def _s5(u, ein, toep, fout, lam16, d):
    bsz, seqlen, _ = u.shape
    nch = seqlen // S5_CHUNK
    per_q = lambda a: pl.BlockSpec((1,) + a.shape[1:], lambda q, b: (q,) + (0,) * (a.ndim - 1))
    seq = pl.BlockSpec((1, seqlen, LANES), lambda q, b: (b, 0, q))
    return pl.pallas_call(
        _s5_kernel,
        grid=(S5_LANE_BLOCKS, bsz),
        in_specs=[seq, per_q(ein), per_q(toep), per_q(fout), per_q(lam16),
                  pl.BlockSpec((1, LANES), lambda q, b: (0, q))],
        out_specs=seq,
        out_shape=jax.ShapeDtypeStruct(u.shape, F32),
        scratch_shapes=[pltpu.VMEM((nch, S5_BLOCK_STATE), F32), pltpu.VMEM((nch, S5_BLOCK_STATE), F32)],
        compiler_params=pltpu.CompilerParams(
            dimension_semantics=("arbitrary", "arbitrary"), vmem_limit_bytes=VMEM_LIMIT),
        name="s5",
    )(u, ein, toep, fout, lam16, d)


def _stack_heads(x, lo):
    return jnp.concatenate([jnp.where(lo, x, 0.0), jnp.where(lo, 0.0, x)], axis=0)


def _alternate(*stage_generators):
    pending = list(stage_generators)
    while pending:
        for g in list(pending):
            if next(g, StopIteration) is StopIteration:
                pending.remove(g)


def _head_block_diag(x, head_id, order):
    return jnp.concatenate([jnp.where(head_id == h, x, 0.0) for h in order], axis=0)


def _rwkv_kernel(r_ref, lw_ref, k_ref, v_ref, kk_ref, b_ref, gate_ref, rk_ref, lnw_ref, lnb_ref,
                 y_ref, s_ref, wr_ref, w2_ref, bkt_ref, prb_ref, yv_ref, e2_ref, u_ref):
    tc = RWKV_CHUNK
    nchunks = RWKV_TB // tc
    qw = 2 * LANES

    @pl.when(pl.program_id(1) == 0)
    def _():
        s_ref[...] = jnp.zeros_like(s_ref)

    row = lax.broadcasted_iota(jnp.int32, (tc, qw), 0)
    lane = lax.broadcasted_iota(jnp.int32, (tc, qw), 1)
    col = jnp.bitwise_and(lane, RWKV_HEAD - 1)
    head_id = lane // RWKV_HEAD
    heads = tuple(range(qw // RWKV_HEAD))
    heads_swapped = (1, 0, 3, 2)
    eye_q = (row == col).astype(F32)
    row_p = lax.broadcasted_iota(jnp.int32, (tc, LANES), 0)
    lane_p = lax.broadcasted_iota(jnp.int32, (tc, LANES), 1)
    col_p = jnp.bitwise_and(lane_p, RWKV_HEAD - 1)
    lo_p = lane_p < RWKV_HEAD
    lo_p2 = lax.broadcasted_iota(jnp.int32, (2 * tc, LANES), 1) < RWKV_HEAD
    stril = row_p > col_p
    tril = row_p >= col_p
    state_bd = _head_ones()
    r4 = lax.broadcasted_iota(jnp.int32, (qw, qw), 0)
    l4 = lax.broadcasted_iota(jnp.int32, (qw, qw), 1)
    ones = ((r4 // RWKV_HEAD) == (l4 // RWKV_HEAD)).astype(BF16)
    tri = (lax.broadcasted_iota(jnp.int32, (tc, tc), 0)
           >= lax.broadcasted_iota(jnp.int32, (tc, tc), 1)).astype(BF16)
    bf = lambda t: t.astype(BF16)
    up, dn = slice(0, tc), slice(tc, 2 * tc)

    def prepare(i, carry):
        units = []
        for j in range(RWKV_PREP_CHUNKS):
            c = i * RWKV_PREP_CHUNKS + j
            rows = pl.ds(pl.multiple_of(c * tc, tc), tc)
            lw_all = lw_ref[0, rows, :]
            g_all = _dot_exact_lhs(tri, lw_all)
            for q in range(RWKV_QUADS):
                qs = slice(qw * q, qw * (q + 1))
                units.append(dict(c=c, q=q, rows=rows, qs=qs, lw=lw_all[:, qs], g=g_all[:, qs]))
        for d in units:
            rows, qs, g = d["rows"], d["qs"], d["g"]
            m = 0.5 * g[tc - 1:tc, :]
            e_out = jnp.exp(m - g)
            em = jnp.exp(m)
            rt = r_ref[0, rows, qs] * jnp.exp(g - m)
            bt = b_ref[0, rows, qs] * e_out
            kt = k_ref[0, rows, qs] * e_out
            d["em"] = em
            d["at"] = bf(-kk_ref[0, rows, qs] * jnp.exp(g - d["lw"] - m))
            d["rt"], d["bt"], d["kt"] = bf(rt), bf(bt), bf(kt)
            d["rts"] = bf(rt * em)
            d["vsw"] = bf(_head_block_diag(v_ref[0, rows, qs], head_id, heads_swapped))
            bkt, e2 = [], []
            for j in range(2):
                ps = slice(LANES * j, LANES * (j + 1))
                bks = jnp.concatenate([bt[:, ps], kt[:, ps]], axis=0) * em[:, ps]
                bkt.append(bf(jnp.transpose(bks)))
                e2.append(jnp.transpose(jnp.broadcast_to(em[:, ps] * em[:, ps], (LANES, LANES))))
            bkt_ref[d["c"], d["q"]] = jnp.concatenate(bkt, axis=0)
            e2_ref[d["c"], d["q"]] = jnp.concatenate(e2, axis=1)
        for d in units:
            d["o"] = []
            for j in range(2):
                ps = slice(LANES * j, LANES * (j + 1))
                bk = jnp.concatenate([d["bt"][:, ps], d["kt"][:, ps]], axis=0)
                kb = jnp.concatenate([d["kt"][:, ps], d["bt"][:, ps]], axis=0)
                rhs = jnp.concatenate([jnp.where(lo_p2, bk, 0.0), jnp.where(lo_p2, 0.0, kb)], axis=0)
                lhs = jnp.concatenate([d["at"][:, ps], d["rt"][:, ps]], axis=0)
                d["o"].append(_dot_nt(lhs, rhs))
        for d in units:
            h0 = [o[:, :LANES] for o in d["o"]]
            h1 = [o[:, LANES:] for o in d["o"]]

            def pick(rws, first, second, mask):
                return jnp.concatenate(
                    [jnp.where(mask, jnp.where(lo_p, first[j][rws], second[j][rws]), 0.0) for j in range(2)],
                    axis=1)

            d["ap"] = pick(up, h0, h1, stril)
            d["p_ak"] = bf(pick(up, h1, h0, stril))
            p_rb = bf(pick(dn, h0, h1, tril))
            d["p_rk"] = bf(pick(dn, h1, h0, tril))
            prb_ref[d["c"], d["q"]] = p_rb
            d["tinv"] = eye_q + d["ap"]
        levels = int(math.log2(tc)) - 1
        for d in units:
            d["ap"] = _dot(bf(d["ap"]), bf(_head_block_diag(d["ap"], head_id, heads)))
        for lvl in range(levels):
            for d in units:
                rhs = bf(_head_block_diag(d["ap"], head_id, heads))
                if lvl < levels - 1:
                    out = _dot(bf(jnp.concatenate([d["tinv"], d["ap"]], axis=0)), rhs)
                    d["tinv"], d["ap"] = d["tinv"] + out[:tc], out[tc:]
                else:
                    d["tinv"] = d["tinv"] + _dot(bf(d["tinv"]), rhs)
        for d in units:
            d["tinv"] = bf(d["tinv"])
            out = _dot(jnp.concatenate([d["p_ak"], d["p_rk"]], axis=0), d["vsw"])
            d["xv"] = out[:tc]
            yv_ref[d["c"], d["q"]] = out[tc:]
        for d in units:
            w1s, w2 = [], []
            for j in range(2):
                ps = slice(LANES * j, LANES * (j + 1))
                rhs = jnp.concatenate([_stack_heads(d["at"][:, ps], lo_p),
                                       bf(_stack_heads(d["xv"][:, ps], lo_p))], axis=1)
                w = _dot(d["tinv"][:, ps], rhs)
                w1s.append(bf(w[:, :LANES] * d["em"][:, ps]))
                w2.append(w[:, LANES:])
            wr_ref[d["c"], d["q"]] = jnp.concatenate(w1s + [d["rts"][:, :LANES], d["rts"][:, LANES:]], axis=0)
            w2_ref[d["c"], d["q"]] = jnp.concatenate(w2, axis=1)
        return carry

    lax.fori_loop(0, nchunks // RWKV_PREP_CHUNKS, prepare, 0)

    def recur_step(c):
        rows = pl.ds(pl.multiple_of(c * tc, tc), tc)
        units = [dict(q=q, qs=slice(qw * q, qw * (q + 1))) for q in range(RWKV_QUADS)]
        for d in units:
            d["h0"] = s_ref[d["q"]]
        for d in units:
            out = _dot(wr_ref[c, d["q"]], bf(d["h0"]))
            d["u"] = (jnp.concatenate([out[:tc, :LANES], out[tc:2 * tc, LANES:]], axis=1)
                      + w2_ref[c, d["q"]])
            d["yr"] = jnp.concatenate([out[2 * tc:3 * tc, :LANES], out[3 * tc:, LANES:]], axis=1)
        yield
        for d in units:
            q = d["q"]
            uv = jnp.concatenate([bf(d["u"]), v_ref[0, rows, d["qs"]]], axis=0)
            upd = _dot(bkt_ref[c, q], uv)
            upd = jnp.concatenate([jnp.where(state_bd, upd[:LANES, :LANES], 0.0),
                                   jnp.where(state_bd, upd[LANES:, LANES:], 0.0)], axis=1)
            s_ref[q] = d["h0"] * e2_ref[c, q] + upd
            u_ref[c, q] = bf(d["u"])
            yv_ref[c, q] = yv_ref[c, q] + d["yr"]

    def emit_step(c):
        rows = pl.ds(pl.multiple_of(c * tc, tc), tc)
        units = [dict(q=q, qs=slice(qw * q, qw * (q + 1))) for q in range(RWKV_QUADS)]
        for d in units:
            q = d["q"]
            d["y"] = yv_ref[c, q] + _dot(prb_ref[c, q], _head_block_diag(u_ref[c, q], head_id, heads))
        yield
        for d in units:
            qs, y = d["qs"], d["y"]
            rkr = r_ref[0, rows, qs].astype(F32) * k_ref[0, rows, qs] * rk_ref[:, qs]
            d["sums"] = _dot_exact_rhs(jnp.concatenate([y, rkr], axis=0), ones)
        yield
        for d in units:
            d["dev"] = d["y"] - d["sums"][:tc] * (1.0 / RWKV_HEAD)
            d["var"] = _dot_exact_rhs(d["dev"] * d["dev"], ones) * (1.0 / RWKV_HEAD)
        for d in units:
            qs = d["qs"]
            yn = d["dev"] * lax.rsqrt(d["var"] + GN_EPS) * lnw_ref[:, qs] + lnb_ref[:, qs]
            y_ref[0, rows, qs] = bf((yn + d["sums"][tc:] * v_ref[0, rows, qs]) * gate_ref[0, rows, qs])

    def recur_and_emit(c, carry):
        _alternate(emit_step(c - 1), recur_step(c))
        return carry

    _alternate(recur_step(0))
    lax.fori_loop(1, nchunks, recur_and_emit, 0)
    _alternate(emit_step(nchunks - 1))


def _rwkv(r, lw, k, v, kkn, b, gate, r_k, ln_w, ln_b):
    bsz, seqlen, _ = r.shape
    nchunks = RWKV_TB // RWKV_CHUNK
    tok = pl.BlockSpec((1, RWKV_TB, D_RWKV), lambda bi, t: (bi, t, 0))
    gate_spec = pl.BlockSpec((1, RWKV_TB, D_RWKV), lambda bi, t: (bi, t, 1))
    vec = pl.BlockSpec((1, D_RWKV), lambda bi, t: (0, 0))
    unit = lambda rows, cols, dt: pltpu.VMEM((nchunks, RWKV_QUADS, rows, cols), dt)
    return pl.pallas_call(
        _rwkv_kernel,
        grid=(bsz, seqlen // RWKV_TB),
        in_specs=[tok] * 6 + [gate_spec] + [vec] * 3,
        out_specs=tok,
        out_shape=jax.ShapeDtypeStruct(r.shape, BF16),
        scratch_shapes=[
            pltpu.VMEM((RWKV_QUADS, LANES, 2 * LANES), F32),
            unit(4 * RWKV_CHUNK, LANES, BF16),
            unit(RWKV_CHUNK, 2 * LANES, F32),
            unit(2 * LANES, 2 * RWKV_CHUNK, BF16),
            unit(RWKV_CHUNK, 2 * LANES, BF16),
            unit(RWKV_CHUNK, 2 * LANES, F32),
            unit(LANES, 2 * LANES, F32),
            unit(RWKV_CHUNK, 2 * LANES, BF16),
        ],
        compiler_params=pltpu.CompilerParams(
            dimension_semantics=("arbitrary", "arbitrary"), vmem_limit_bytes=VMEM_LIMIT),
        name="rwkv7",
    )(r, lw, k, v, kkn, b, gate, r_k, ln_w, ln_b)


def _out_kernel(y5_ref, gate_ref, yr_ref, x_ref, gluw_ref, glub_ref, wout_ref, fg_ref, o_ref):
    for s in range(TM_OUT // IN_SUB):
        rs = slice(IN_SUB * s, IN_SUB * (s + 1))
        y = y5_ref[0, rs, :]
        y = 0.5 * y * (1.0 + jnp.tanh(math.sqrt(2.0 / math.pi) * (y + 0.044715 * (y * y * y))))
        glu = _sigmoid(_dot(y.astype(BF16), gluw_ref[...]) + glub_ref[...])
        ys = y * glu * gate_ref[0, rs, :]
        ycat = jnp.concatenate([ys.astype(BF16), yr_ref[0, rs, :]], axis=-1)
        xn = x_ref[0, rs, :] + _dot(ycat, wout_ref[...])
        ms = jnp.mean(xn * xn, axis=-1, keepdims=True)
        o_ref[0, rs, :] = xn * lax.rsqrt(ms + NORM_EPS) * fg_ref[...]


def _outproj(y5, gate, yr, x, glu_w, glu_b, w_out, final_g):
    bsz, seqlen, _ = x.shape
    tok = lambda w: pl.BlockSpec((1, TM_OUT, w), lambda b, t: (b, t, 0))
    full = lambda a: pl.BlockSpec(a.shape, lambda b, t: (0,) * a.ndim)
    return pl.pallas_call(
        _out_kernel,
        grid=(bsz, seqlen // TM_OUT),
        in_specs=[tok(D_S5), tok(D_S5), tok(D_RWKV), tok(D_MODEL),
                  full(glu_w), full(glu_b), full(w_out), full(final_g)],
        out_specs=tok(D_MODEL),
        out_shape=jax.ShapeDtypeStruct(x.shape, F32),
        compiler_params=pltpu.CompilerParams(
            dimension_semantics=("arbitrary", "arbitrary"), vmem_limit_bytes=VMEM_LIMIT),
        name="outproj",
    )(y5, gate, yr, x, glu_w, glu_b, w_out, final_g)


@jax.jit
def _forward(x, norm_g, w_in, s5_lam_re, s5_lam_im, s5_log_dt, s5_b_re, s5_b_im, s5_c_re, s5_c_im,
             s5_d, s5_glu_w, s5_glu_b, rwkv_mu, rwkv_w0, rwkv_w2, rwkv_a0, rwkv_a2, rwkv_k_k, rwkv_k_a,
             rwkv_r_k, rwkv_ln_w, rwkv_ln_b, w_out, final_g):
    assert w_in.shape[0] == 1, "single-layer block"
    for l in range(1):
        zeros = jnp.zeros((LOW_RANK, D_RWKV), F32)
        w2a2 = jnp.concatenate(
            [jnp.concatenate([rwkv_w2[l], zeros], axis=1), jnp.concatenate([zeros, rwkv_a2[l]], axis=1)],
            axis=0).astype(BF16)
        row = lambda a: a.reshape(1, -1).astype(F32)
        u, gate, r, lw, k, v, kkn, b = _inproj(
            x, row(norm_g[l]), w_in[l].astype(BF16), row(rwkv_mu[l]), w2a2, row(rwkv_w0[l]),
            row(rwkv_a0[l]), row(rwkv_k_k[l]), row(rwkv_k_a[l]))
        ein, toep, fout, lam16 = _s5_operators(
            s5_lam_re[l], s5_lam_im[l], s5_log_dt[l], s5_b_re[l], s5_b_im[l], s5_c_re[l], s5_c_im[l])
        y5 = _s5(u, ein, toep, fout, lam16, row(s5_d[l]))
        yr = _rwkv(r, lw, k, v, kkn, b, gate, row(rwkv_r_k[l]), row(rwkv_ln_w[l]), row(rwkv_ln_b[l]))
        x = _outproj(y5, gate, yr, x, s5_glu_w[l].astype(BF16), row(s5_glu_b[l]),
                     w_out[l].astype(BF16), row(final_g))
    return x


def kernel(x, norm_g, w_in, s5_lam_re, s5_lam_im, s5_log_dt, s5_b_re, s5_b_im, s5_c_re, s5_c_im, s5_d, s5_glu_w, s5_glu_b, rwkv_mu, rwkv_w0, rwkv_w2, rwkv_a0, rwkv_a2, rwkv_k_k, rwkv_k_a, rwkv_r_k, rwkv_ln_w, rwkv_ln_b, w_out, final_g):
    return _forward(x, norm_g, w_in, s5_lam_re, s5_lam_im, s5_log_dt, s5_b_re, s5_b_im, s5_c_re, s5_c_im,
                    s5_d, s5_glu_w, s5_glu_b, rwkv_mu, rwkv_w0, rwkv_w2, rwkv_a0, rwkv_a2, rwkv_k_k,
                    rwkv_k_a, rwkv_r_k, rwkv_ln_w, rwkv_ln_b, w_out, final_g)
```

```python
import math

import jax
import jax.numpy as jnp
from jax import lax
from jax.experimental import pallas as pl
from jax.experimental.pallas import tpu as pltpu

F32 = jnp.float32
BF16 = jnp.bfloat16

D_MODEL = 1024
D_S5 = 512
D_RWKV = 512
S5_GROUP = 16
S5_GROUPS = 32
S5_STATE = 64
RWKV_HEAD = 64
LOW_RANK = 64
D_SHIFT = 3 * D_RWKV + 2 * LOW_RANK
D_IN = 2 * D_S5 + D_SHIFT + D_RWKV
NORM_EPS = 1e-6
GN_EPS = 64e-5

LANES = 128
TM_IN = 512
IN_SUB = 128
TM_OUT = 512
S5_CHUNK = 16
S5_LANE_BLOCKS = D_S5 // LANES
S5_GROUPS_PER_BLOCK = LANES // S5_GROUP
S5_BLOCK_STATE = 2 * S5_GROUPS_PER_BLOCK * S5_STATE
RWKV_CHUNK = 64
RWKV_TB = 512
RWKV_PREP_CHUNKS = 4
RWKV_OUT_CHUNKS = 2
HEAD_PAIRS = D_RWKV // LANES
RWKV_QUADS = HEAD_PAIRS // 2
VMEM_LIMIT = 56 * 1024 * 1024


def _dot(a, b):
    return jnp.dot(a, b, preferred_element_type=F32)


def _dot_nt(a, b):
    return lax.dot_general(a, b, (((1,), (1,)), ((), ())), preferred_element_type=F32)


def _dot_tn(a, b):
    return lax.dot_general(a, b, (((0,), (0,)), ((), ())), preferred_element_type=F32)


def _split2(x):
    hi = x.astype(BF16)
    return hi, (x - hi.astype(F32)).astype(BF16)


def _dot_exact_rhs(x, m):
    hi, lo = _split2(x)
    return _dot(hi, m) + _dot(lo, m)


def _dot_exact_lhs(m, x):
    hi, lo = _split2(x)
    return _dot(m, hi) + _dot(m, lo)


def _sigmoid(x):
    return 0.5 + 0.5 * jnp.tanh(0.5 * x)


def _head_ones():
    r = lax.broadcasted_iota(jnp.int32, (LANES, LANES), 0)
    c = lax.broadcasted_iota(jnp.int32, (LANES, LANES), 1)
    return (r < RWKV_HEAD) == (c < RWKV_HEAD)


def _inproj_kernel(x_ref, g_ref, win_ref, mu_ref, w2a2_ref, w0_ref, a0_ref, kk_ref, ka_ref,
                   u_ref, gate_ref, r_ref, lw_ref, k_ref, v_ref, kkn_ref, b_ref, carry_ref):
    t = pl.program_id(1)

    @pl.when(t == 0)
    def _():
        carry_ref[...] = jnp.zeros_like(carry_ref)

    ones = _head_ones().astype(BF16)
    carry = carry_ref[...]
    for s in range(TM_IN // IN_SUB):
        rs = slice(IN_SUB * s, IN_SUB * (s + 1))
        x = x_ref[0, rs, :]
        ms = jnp.mean(x * x, axis=-1, keepdims=True)
        h = (x * lax.rsqrt(ms + NORM_EPS) * g_ref[...]).astype(BF16)

        def proj(lo, hi):
            return _dot(h, win_ref[:, lo:hi])

        u_ref[0, rs, :] = proj(0, D_S5)
        z5 = proj(D_S5, 2 * D_S5)
        gate_ref[0, rs, :D_S5] = (z5 * _sigmoid(z5)).astype(BF16)
        zr = proj(2 * D_S5 + D_SHIFT, D_IN)
        gate_ref[0, rs, D_S5:] = (zr * _sigmoid(zr)).astype(BF16)

        rw = proj(2 * D_S5, 2 * D_S5 + D_SHIFT)
        prev = pltpu.roll(rw, 1, axis=0)
        row = lax.broadcasted_iota(jnp.int32, rw.shape, 0)
        prev = jnp.where(row == 0, carry, prev)
        carry = rw[IN_SUB - 1:IN_SUB, :]
        sh = rw + (prev - rw) * mu_ref[...]

        r = sh[:, 0:D_RWKV]
        k = sh[:, D_RWKV:2 * D_RWKV]
        v = sh[:, 2 * D_RWKV:3 * D_RWKV]
        wa = sh[:, 3 * D_RWKV:]
        lane = lax.broadcasted_iota(jnp.int32, wa.shape, 1)
        wa = jnp.where(lane < LOW_RANK, jnp.tanh(wa), wa)
        za = _dot(wa.astype(BF16), w2a2_ref[...])
        lw = -math.exp(-0.5) * _sigmoid(w0_ref[...] + za[:, :D_RWKV])
        iclr = _sigmoid(a0_ref[...] + za[:, D_RWKV:])

        kk = k * kk_ref[...]
        sq = kk * kk
        ss = jnp.concatenate(
            [_dot_exact_rhs(sq[:, LANES * p:LANES * (p + 1)], ones) for p in range(HEAD_PAIRS)], axis=1)
        kkn = kk * lax.rsqrt(jnp.maximum(ss, 1e-24))

        r_ref[0, rs, :] = r.astype(BF16)
        lw_ref[0, rs, :] = lw
        k_ref[0, rs, :] = (k * (1.0 + (iclr - 1.0) * ka_ref[...])).astype(BF16)
        v_ref[0, rs, :] = v.astype(BF16)
        kkn_ref[0, rs, :] = kkn.astype(BF16)
        b_ref[0, rs, :] = (kkn * iclr).astype(BF16)
    carry_ref[...] = carry


def _inproj(x, norm_g, w_in, mu, w2a2, w0, a0, k_k, k_a):
    bsz, seqlen, _ = x.shape
    tok = lambda w: pl.BlockSpec((1, TM_IN, w), lambda b, t: (b, t, 0))
    full = lambda a: pl.BlockSpec(a.shape, lambda b, t: (0,) * a.ndim)
    out = lambda w, dt: jax.ShapeDtypeStruct((bsz, seqlen, w), dt)
    params = (norm_g, w_in, mu, w2a2, w0, a0, k_k, k_a)
    return pl.pallas_call(
        _inproj_kernel,
        grid=(bsz, seqlen // TM_IN),
        in_specs=[tok(D_MODEL)] + [full(a) for a in params],
        out_specs=[tok(D_S5), tok(2 * D_S5)] + [tok(D_RWKV)] * 6,
        out_shape=[out(D_S5, F32), out(2 * D_S5, BF16)]
        + [out(D_RWKV, F32 if i == 1 else BF16) for i in range(6)],
        scratch_shapes=[pltpu.VMEM((1, D_SHIFT), F32)],
        compiler_params=pltpu.CompilerParams(
            dimension_semantics=("arbitrary", "arbitrary"), vmem_limit_bytes=VMEM_LIMIT),
        name="inproj",
    )(x, *params)


def _s5_kernel(u_ref, ein_ref, toep_ref, fout_ref, lam_ref, d_ref, y_ref, e_ref, sp_ref):
    nch = e_ref.shape[0]
    half = S5_BLOCK_STATE // 2
    xs = [u_ref[0, pl.ds(j, nch, stride=S5_CHUNK), :] for j in range(S5_CHUNK)]
    xcat = jnp.concatenate([x.astype(BF16) for x in xs], axis=1)
    e_ref[...] = _dot(xcat, ein_ref[0])

    lre = lam_ref[0, :, :half]
    lim = lam_ref[0, :, half:]

    def step(c, s):
        sre, sim = s
        sp_ref[pl.ds(c, 1), :half] = sre
        sp_ref[pl.ds(c, 1), half:] = sim
        e = e_ref[pl.ds(c, 1), :]
        return (lre * sre - lim * sim + e[:, :half], lre * sim + lim * sre + e[:, half:])

    zero = jnp.zeros((1, half), F32)
    lax.fori_loop(0, nch, step, (zero, zero))

    sp = sp_ref[...].astype(BF16)
    d = d_ref[...]
    pairs = S5_CHUNK // 2
    for tp in range(pairs):
        y = _dot(xcat[:, :2 * LANES * (tp + 1)], toep_ref[0, 2 * LANES * (pairs - 1 - tp):, :])
        y = y + _dot(sp, fout_ref[0, :, 2 * LANES * tp:2 * LANES * (tp + 1)])
        for i in range(2):
            t = 2 * tp + i
            y_ref[0, pl.ds(t, nch, stride=S5_CHUNK), :] = y[:, LANES * i:LANES * (i + 1)] + d * xs[t]


def _cmul(ar, ai, br, bi):
    return ar * br - ai * bi, ar * bi + ai * br


def _block_diag_expand(compact, width, inner, transposed):
    k = compact.shape[1] if transposed else compact.shape[2]
    group_w = S5_GROUPS_PER_BLOCK * inner
    kk = jnp.arange(k)
    ww = jnp.arange(width)
    sel = ((kk[:, None] // inner == ww[None, :] // group_w)
           & (kk[:, None] % inner == ww[None, :] % inner)).astype(F32)
    wide_group = (ww % group_w) // inner
    if transposed:
        full = jnp.matmul(sel.T, compact)
        narrow_group = (jnp.arange(compact.shape[2]) % LANES) // S5_GROUP
        keep = wide_group[:, None] == narrow_group[None, :]
    else:
        full = jnp.matmul(compact, sel)
        narrow_group = (jnp.arange(compact.shape[1]) % LANES) // S5_GROUP
        keep = narrow_group[:, None] == wide_group[None, :]
    return jnp.where(keep, full, 0.0)


def _s5_operators(lam_re, lam_im, log_dt, b_re, b_im, c_re, c_im):
    c16, q4, gl = S5_CHUNK, S5_LANE_BLOCKS, S5_GROUPS_PER_BLOCK
    hp = lax.Precision.HIGHEST
    dt = jnp.exp(log_dt)[:, None]
    n = jnp.arange(c16 + 1, dtype=F32)[:, None, None]
    mag = jnp.exp(n * (lam_re * dt)[None])
    ang = n * (lam_im * dt)[None]
    pw_r, pw_i = mag * jnp.cos(ang), mag * jnp.sin(ang)
    den = lam_re * lam_re + lam_im * lam_im
    q_r, q_i = _cmul(pw_r[1] - 1.0, pw_i[1], lam_re / den, -lam_im / den)
    bb_r, bb_i = _cmul(q_r[..., None], q_i[..., None], b_re, b_im)

    w_r, w_i = _cmul(pw_r[:c16, :, :, None], pw_i[:c16, :, :, None], bb_r[None], bb_i[None])
    kt = (jnp.einsum('gop,tgpi->tgio', c_re, w_r, precision=hp)
          - jnp.einsum('gop,tgpi->tgio', c_im, w_i, precision=hp))
    d8 = jnp.arange(c16 // 2)
    lag = 2 * d8[:, None, None] + jnp.arange(2)[None, None, :] - jnp.arange(2)[None, :, None]
    kp = jnp.where((lag >= 0)[..., None, None, None], kt[jnp.maximum(lag, 0)], 0.0)
    kp = kp.reshape(c16 // 2, 2, 2, q4, gl, S5_GROUP, S5_GROUP)
    kp = jnp.transpose(kp, (3, 0, 1, 4, 5, 2, 6))[:, ::-1]
    toep = _block_diag_expand(kp.reshape(q4, c16 * LANES, 2 * S5_GROUP), 2 * LANES, S5_GROUP, False)

    back = c16 - 1 - jnp.arange(c16)
    e_r, e_i = _cmul(pw_r[back][..., None], pw_i[back][..., None], bb_r[None], bb_i[None])
    ein = jnp.stack([e_r, e_i]).reshape(2, c16, q4, gl, S5_STATE, S5_GROUP)
    ein = jnp.transpose(ein, (2, 1, 3, 5, 0, 4)).reshape(q4, c16 * LANES, 2 * S5_STATE)
    ein = _block_diag_expand(ein, S5_BLOCK_STATE, S5_STATE, False)

    f_r, f_i = _cmul(c_re[None], c_im[None], pw_r[1:, :, None, :], pw_i[1:, :, None, :])
    fo = jnp.stack([f_r, -f_i]).reshape(2, c16, q4, gl, S5_GROUP, S5_STATE)
    fo = jnp.transpose(fo, (2, 0, 5, 1, 3, 4)).reshape(q4, 2 * S5_STATE, c16 * LANES)
    fo = _block_diag_expand(fo, S5_BLOCK_STATE, S5_STATE, True)

    lam16 = jnp.concatenate([pw_r[c16].reshape(q4, 1, gl * S5_STATE),
                             pw_i[c16].reshape(q4, 1, gl * S5_STATE)], axis=-1)
    return ein.astype(BF16), toep.astype(BF16), fo.astype(BF16), lam16


def _s5(u, ein, toep, fout, lam16, d):
    bsz, seqlen, _ = u.shape
    nch = seqlen // S5_CHUNK
    per_q = lambda a: pl.BlockSpec((1,) + a.shape[1:], lambda q, b: (q,) + (0,) * (a.ndim - 1))
    seq = pl.BlockSpec((1, seqlen, LANES), lambda q, b: (b, 0, q))
    return pl.pallas_call(
        _s5_kernel,
        grid=(S5_LANE_BLOCKS, bsz),
        in_specs=[seq, per_q(ein), per_q(toep), per_q(fout), per_q(lam16),
                  pl.BlockSpec((1, LANES), lambda q, b: (0, q))],
        out_specs=seq,
        out_shape=jax.ShapeDtypeStruct(u.shape, F32),
        scratch_shapes=[pltpu.VMEM((nch, S5_BLOCK_STATE), F32), pltpu.VMEM((nch, S5_BLOCK_STATE), F32)],
        compiler_params=pltpu.CompilerParams(
            dimension_semantics=("arbitrary", "arbitrary"), vmem_limit_bytes=VMEM_LIMIT),
        name="s5",
    )(u, ein, toep, fout, lam16, d)


def _stack_heads(x, lo):
    return jnp.concatenate([jnp.where(lo, x, 0.0), jnp.where(lo, 0.0, x)], axis=0)


def _alternating(*stage_generators):
    pending = list(stage_generators)
    while pending:
        for g in list(pending):
            if next(g, StopIteration) is StopIteration:
                pending.remove(g)
            else:
                yield


def _alternate(*stage_generators):
    for _ in _alternating(*stage_generators):
        pass


def _head_block_diag(x, head_id, order):
    return jnp.concatenate([jnp.where(head_id == h, x, 0.0) for h in order], axis=0)


def _rwkv_kernel(r_ref, lw_ref, k_ref, v_ref, kk_ref, b_ref, gate_ref, rk_ref, lnw_ref, lnb_ref,
                 y_ref, s_ref, wr_ref, w2_ref, bkt_ref, prb_ref, yv_ref, e2_ref, u_ref):
    tc = RWKV_CHUNK
    nchunks = RWKV_TB // tc
    qw = 2 * LANES

    @pl.when(pl.program_id(1) == 0)
    def _():
        s_ref[...] = jnp.zeros_like(s_ref)

    row = lax.broadcasted_iota(jnp.int32, (tc, qw), 0)
    lane = lax.broadcasted_iota(jnp.int32, (tc, qw), 1)
    col = jnp.bitwise_and(lane, RWKV_HEAD - 1)
    head_id = lane // RWKV_HEAD
    heads = tuple(range(qw // RWKV_HEAD))
    heads_swapped = (1, 0, 3, 2)
    eye_q = (row == col).astype(F32)
    row_p = lax.broadcasted_iota(jnp.int32, (tc, LANES), 0)
    lane_p = lax.broadcasted_iota(jnp.int32, (tc, LANES), 1)
    col_p = jnp.bitwise_and(lane_p, RWKV_HEAD - 1)
    lo_p = lane_p < RWKV_HEAD
    lo_p2 = lax.broadcasted_iota(jnp.int32, (2 * tc, LANES), 1) < RWKV_HEAD
    stril = row_p > col_p
    tril = row_p >= col_p
    state_bd = _head_ones()
    r4 = lax.broadcasted_iota(jnp.int32, (qw, qw), 0)
    l4 = lax.broadcasted_iota(jnp.int32, (qw, qw), 1)
    ones = ((r4 // RWKV_HEAD) == (l4 // RWKV_HEAD)).astype(BF16)
    tri = (lax.broadcasted_iota(jnp.int32, (tc, tc), 0)
           >= lax.broadcasted_iota(jnp.int32, (tc, tc), 1)).astype(BF16)
    bf = lambda t: t.astype(BF16)
    up, dn = slice(0, tc), slice(tc, 2 * tc)

    def prepare(chunks):
        units = []
        for c in chunks:
            rows = slice(c * tc, (c + 1) * tc)
            lw_all = lw_ref[0, rows, :]
            g_all = _dot_exact_lhs(tri, lw_all)
            for q in range(RWKV_QUADS):
                qs = slice(qw * q, qw * (q + 1))
                units.append(dict(c=c, q=q, rows=rows, qs=qs, lw=lw_all[:, qs], g=g_all[:, qs]))
        yield
        for d in units:
            rows, qs, g = d["rows"], d["qs"], d["g"]
            m = 0.5 * g[tc - 1:tc, :]
            e_out = jnp.exp(m - g)
            em = jnp.exp(m)
            rt = r_ref[0, rows, qs] * jnp.exp(g - m)
            bt = b_ref[0, rows, qs] * e_out
            kt = k_ref[0, rows, qs] * e_out
            d["em"] = em
            d["at"] = bf(-kk_ref[0, rows, qs] * jnp.exp(g - d["lw"] - m))
            d["rt"], d["bt"], d["kt"] = bf(rt), bf(bt), bf(kt)
            d["rts"] = bf(rt * em)
            d["vsw"] = bf(_head_block_diag(v_ref[0, rows, qs], head_id, heads_swapped))
            bkt, e2 = [], []
            for j in range(2):
                ps = slice(LANES * j, LANES * (j + 1))
                bks = jnp.concatenate([bt[:, ps], kt[:, ps]], axis=0) * em[:, ps]
                bkt.append(bf(jnp.transpose(bks)))
                e2.append(jnp.transpose(jnp.broadcast_to(em[:, ps] * em[:, ps], (LANES, LANES))))
            bkt_ref[d["c"], d["q"]] = jnp.concatenate(bkt, axis=0)
            e2_ref[d["c"], d["q"]] = jnp.concatenate(e2, axis=1)
        yield
        for d in units:
            d["o"] = []
            for j in range(2):
                ps = slice(LANES * j, LANES * (j + 1))
                bk = jnp.concatenate([d["bt"][:, ps], d["kt"][:, ps]], axis=0)
                kb = jnp.concatenate([d["kt"][:, ps], d["bt"][:, ps]], axis=0)
                rhs = jnp.concatenate([jnp.where(lo_p2, bk, 0.0), jnp.where(lo_p2, 0.0, kb)], axis=0)
                lhs = jnp.concatenate([d["at"][:, ps], d["rt"][:, ps]], axis=0)
                d["o"].append(_dot_nt(lhs, rhs))
        yield
        for d in units:
            h0 = [o[:, :LANES] for o in d["o"]]
            h1 = [o[:, LANES:] for o in d["o"]]

            def pick(rws, first, second, mask):
                return jnp.concatenate(
                    [jnp.where(mask, jnp.where(lo_p, first[j][rws], second[j][rws]), 0.0) for j in range(2)],
                    axis=1)

            d["ap"] = pick(up, h0, h1, stril)
            d["p_ak"] = bf(pick(up, h1, h0, stril))
            p_rb = bf(pick(dn, h0, h1, tril))
            d["p_rk"] = bf(pick(dn, h1, h0, tril))
            prb_ref[d["c"], d["q"]] = p_rb
            d["tinv"] = eye_q + d["ap"]
        yield
        levels = int(math.log2(tc)) - 1
        for d in units:
            d["ap"] = _dot(bf(d["ap"]), bf(_head_block_diag(d["ap"], head_id, heads)))
        for lvl in range(levels):
            yield
            for d in units:
                rhs = bf(_head_block_diag(d["ap"], head_id, heads))
                if lvl < levels - 1:
                    out = _dot(bf(jnp.concatenate([d["tinv"], d["ap"]], axis=0)), rhs)
                    d["tinv"], d["ap"] = d["tinv"] + out[:tc], out[tc:]
                else:
                    d["tinv"] = d["tinv"] + _dot(bf(d["tinv"]), rhs)
        yield
        for d in units:
            d["tinv"] = bf(d["tinv"])
            out = _dot(jnp.concatenate([d["p_ak"], d["p_rk"]], axis=0), d["vsw"])
            d["xv"] = out[:tc]
            yv_ref[d["c"], d["q"]] = out[tc:]
        yield
        for d in units:
            w1s, w2 = [], []
            for j in range(2):
                ps = slice(LANES * j, LANES * (j + 1))
                rhs = jnp.concatenate([_stack_heads(d["at"][:, ps], lo_p),
                                       bf(_stack_heads(d["xv"][:, ps], lo_p))], axis=1)
                w = _dot(d["tinv"][:, ps], rhs)
                w1s.append(bf(w[:, :LANES] * d["em"][:, ps]))
                w2.append(w[:, LANES:])
            wr_ref[d["c"], d["q"]] = jnp.concatenate(w1s + [d["rts"][:, :LANES], d["rts"][:, LANES:]], axis=0)
            w2_ref[d["c"], d["q"]] = jnp.concatenate(w2, axis=1)

    def recur_step(c):
        rows = slice(c * tc, (c + 1) * tc)
        units = [dict(q=q, qs=slice(qw * q, qw * (q + 1))) for q in range(RWKV_QUADS)]
        for d in units:
            d["h0"] = s_ref[d["q"]]
        for d in units:
            out = _dot(wr_ref[c, d["q"]], bf(d["h0"]))
            d["u"] = (jnp.concatenate([out[:tc, :LANES], out[tc:2 * tc, LANES:]], axis=1)
                      + w2_ref[c, d["q"]])
            d["yr"] = jnp.concatenate([out[2 * tc:3 * tc, :LANES], out[3 * tc:, LANES:]], axis=1)
        yield
        for d in units:
            q = d["q"]
            uv = jnp.concatenate([bf(d["u"]), v_ref[0, rows, d["qs"]]], axis=0)
            upd = _dot(bkt_ref[c, q], uv)
            upd = jnp.concatenate([jnp.where(state_bd, upd[:LANES, :LANES], 0.0),
                                   jnp.where(state_bd, upd[LANES:, LANES:], 0.0)], axis=1)
            s_ref[q] = d["h0"] * e2_ref[c, q] + upd
            u_ref[c, q] = bf(d["u"])
            yv_ref[c, q] = yv_ref[c, q] + d["yr"]

    def emit_step(c):
        rows = slice(c * tc, (c + 1) * tc)
        units = [dict(q=q, qs=slice(qw * q, qw * (q + 1))) for q in range(RWKV_QUADS)]
        for d in units:
            q = d["q"]
            d["y"] = yv_ref[c, q] + _dot(prb_ref[c, q], _head_block_diag(u_ref[c, q], head_id, heads))
        yield
        for d in units:
            qs, y = d["qs"], d["y"]
            rkr = r_ref[0, rows, qs].astype(F32) * k_ref[0, rows, qs] * rk_ref[:, qs]
            d["sums"] = _dot_exact_rhs(jnp.concatenate([y, rkr], axis=0), ones)
        yield
        for d in units:
            d["dev"] = d["y"] - d["sums"][:tc] * (1.0 / RWKV_HEAD)
            d["var"] = _dot_exact_rhs(d["dev"] * d["dev"], ones) * (1.0 / RWKV_HEAD)
        for d in units:
            qs = d["qs"]
            yn = d["dev"] * lax.rsqrt(d["var"] + GN_EPS) * lnw_ref[:, qs] + lnb_ref[:, qs]
            y_ref[0, rows, qs] = bf((yn + d["sums"][tc:] * v_ref[0, rows, qs]) * gate_ref[0, rows, qs])

    def chain(chunks):
        for c in chunks:
            yield from _alternating(*([emit_step(c - 1)] if c > 0 else []), recur_step(c))

    groups = [list(range(g, g + RWKV_PREP_CHUNKS)) for g in range(0, nchunks, RWKV_PREP_CHUNKS)]
    _alternate(prepare(groups[0]))
    for prev, nxt in zip(groups[:-1], groups[1:]):
        _alternate(prepare(nxt), chain(prev))
    _alternate(chain(groups[-1]))
    _alternate(emit_step(nchunks - 1))


def _rwkv(r, lw, k, v, kkn, b, gate, r_k, ln_w, ln_b):
    bsz, seqlen, _ = r.shape
    nchunks = RWKV_TB // RWKV_CHUNK
    tok = pl.BlockSpec((1, RWKV_TB, D_RWKV), lambda bi, t: (bi, t, 0))
    gate_spec = pl.BlockSpec((1, RWKV_TB, D_RWKV), lambda bi, t: (bi, t, 1))
    vec = pl.BlockSpec((1, D_RWKV), lambda bi, t: (0, 0))
    unit = lambda rows, cols, dt: pltpu.VMEM((nchunks, RWKV_QUADS, rows, cols), dt)
    return pl.pallas_call(
        _rwkv_kernel,
        grid=(bsz, seqlen // RWKV_TB),
        in_specs=[tok] * 6 + [gate_spec] + [vec] * 3,
        out_specs=tok,
        out_shape=jax.ShapeDtypeStruct(r.shape, BF16),
        scratch_shapes=[
            pltpu.VMEM((RWKV_QUADS, LANES, 2 * LANES), F32),
            unit(4 * RWKV_CHUNK, LANES, BF16),
            unit(RWKV_CHUNK, 2 * LANES, F32),
            unit(2 * LANES, 2 * RWKV_CHUNK, BF16),
            unit(RWKV_CHUNK, 2 * LANES, BF16),
            unit(RWKV_CHUNK, 2 * LANES, F32),
            unit(LANES, 2 * LANES, F32),
            unit(RWKV_CHUNK, 2 * LANES, BF16),
        ],
        compiler_params=pltpu.CompilerParams(
            dimension_semantics=("arbitrary", "arbitrary"), vmem_limit_bytes=VMEM_LIMIT),
        name="rwkv7",
    )(r, lw, k, v, kkn, b, gate, r_k, ln_w, ln_b)


def _out_kernel(y5_ref, gate_ref, yr_ref, x_ref, gluw_ref, glub_ref, wout_ref, fg_ref, o_ref):
    for s in range(TM_OUT // IN_SUB):
        rs = slice(IN_SUB * s, IN_SUB * (s + 1))
        y = y5_ref[0, rs, :]
        y = 0.5 * y * (1.0 + jnp.tanh(math.sqrt(2.0 / math.pi) * (y + 0.044715 * (y * y * y))))
        glu = _sigmoid(_dot(y.astype(BF16), gluw_ref[...]) + glub_ref[...])
        ys = y * glu * gate_ref[0, rs, :]
        ycat = jnp.concatenate([ys.astype(BF16), yr_ref[0, rs, :]], axis=-1)
        xn = x_ref[0, rs, :] + _dot(ycat, wout_ref[...])
        ms = jnp.mean(xn * xn, axis=-1, keepdims=True)
        o_ref[0, rs, :] = xn * lax.rsqrt(ms + NORM_EPS) * fg_ref[...]


def _outproj(y5, gate, yr, x, glu_w, glu_b, w_out, final_g):
    bsz, seqlen, _ = x.shape
    tok = lambda w: pl.BlockSpec((1, TM_OUT, w), lambda b, t: (b, t, 0))
    full = lambda a: pl.BlockSpec(a.shape, lambda b, t: (0,) * a.ndim)
    return pl.pallas_call(
        _out_kernel,
        grid=(bsz, seqlen // TM_OUT),
        in_specs=[tok(D_S5), tok(D_S5), tok(D_RWKV), tok(D_MODEL),
                  full(glu_w), full(glu_b), full(w_out), full(final_g)],
        out_specs=tok(D_MODEL),
        out_shape=jax.ShapeDtypeStruct(x.shape, F32),
        compiler_params=pltpu.CompilerParams(
            dimension_semantics=("arbitrary", "arbitrary"), vmem_limit_bytes=VMEM_LIMIT),
        name="outproj",
    )(y5, gate, yr, x, glu_w, glu_b, w_out, final_g)


@jax.jit
def _forward(x, norm_g, w_in, s5_lam_re, s5_lam_im, s5_log_dt, s5_b_re, s5_b_im, s5_c_re, s5_c_im,
             s5_d, s5_glu_w, s5_glu_b, rwkv_mu, rwkv_w0, rwkv_w2, rwkv_a0, rwkv_a2, rwkv_k_k, rwkv_k_a,
             rwkv_r_k, rwkv_ln_w, rwkv_ln_b, w_out, final_g):
    assert w_in.shape[0] == 1, "single-layer block"
    for l in range(1):
        zeros = jnp.zeros((LOW_RANK, D_RWKV), F32)
        w2a2 = jnp.concatenate(
            [jnp.concatenate([rwkv_w2[l], zeros], axis=1), jnp.concatenate([zeros, rwkv_a2[l]], axis=1)],
            axis=0).astype(BF16)
        row = lambda a: a.reshape(1, -1).astype(F32)
        u, gate, r, lw, k, v, kkn, b = _inproj(
            x, row(norm_g[l]), w_in[l].astype(BF16), row(rwkv_mu[l]), w2a2, row(rwkv_w0[l]),
            row(rwkv_a0[l]), row(rwkv_k_k[l]), row(rwkv_k_a[l]))
        ein, toep, fout, lam16 = _s5_operators(
            s5_lam_re[l], s5_lam_im[l], s5_log_dt[l], s5_b_re[l], s5_b_im[l], s5_c_re[l], s5_c_im[l])
        y5 = _s5(u, ein, toep, fout, lam16, row(s5_d[l]))
        yr = _rwkv(r, lw, k, v, kkn, b, gate, row(rwkv_r_k[l]), row(rwkv_ln_w[l]), row(rwkv_ln_b[l]))
        x = _outproj(y5, gate, yr, x, s5_glu_w[l].astype(BF16), row(s5_glu_b[l]),
                     w_out[l].astype(BF16), row(final_g))
    return x


def kernel(x, norm_g, w_in, s5_lam_re, s5_lam_im, s5_log_dt, s5_b_re, s5_b_im, s5_c_re, s5_c_im, s5_d, s5_glu_w, s5_glu_b, rwkv_mu, rwkv_w0, rwkv_w2, rwkv_a0, rwkv_a2, rwkv_k_k, rwkv_k_a, rwkv_r_k, rwkv_ln_w, rwkv_ln_b, w_out, final_g):
    return _forward(x, norm_g, w_in, s5_lam_re, s5_lam_im, s5_log_dt, s5_b_re, s5_b_im, s5_c_re, s5_c_im,
                    s5_d, s5_glu_w, s5_glu_b, rwkv_mu, rwkv_w0, rwkv_w2, rwkv_a0, rwkv_a2, rwkv_k_k,
                    rwkv_k_a, rwkv_r_k, rwkv_ln_w, rwkv_ln_b, w_out, final_g)
```

```python
import math

import jax
import jax.numpy as jnp
from jax import lax
from jax.experimental import pallas as pl
from jax.experimental.pallas import tpu as pltpu

F32 = jnp.float32
BF16 = jnp.bfloat16

D_MODEL = 1024
D_S5 = 512
D_RWKV = 512
S5_GROUP = 16
S5_GROUPS = 32
S5_STATE = 64
RWKV_HEAD = 64
LOW_RANK = 64
D_SHIFT = 3 * D_RWKV + 2 * LOW_RANK
D_IN = 2 * D_S5 + D_SHIFT + D_RWKV
NORM_EPS = 1e-6
GN_EPS = 64e-5

LANES = 128
TM_IN = 1024
IN_SUB = 128
SUB_TILE_SKEW = 2
TM_OUT = 1024
S5_CHUNK = 16
S5_LANE_BLOCKS = D_S5 // LANES
S5_GROUPS_PER_BLOCK = LANES // S5_GROUP
S5_BLOCK_STATE = 2 * S5_GROUPS_PER_BLOCK * S5_STATE
RWKV_CHUNK = 64
RWKV_TB = 512
RWKV_PREP_CHUNKS = 4
RWKV_OUT_CHUNKS = 2
HEAD_PAIRS = D_RWKV // LANES
RWKV_QUADS = HEAD_PAIRS // 2
VMEM_LIMIT = 56 * 1024 * 1024


def _dot(a, b):
    return jnp.dot(a, b, preferred_element_type=F32)


def _dot_nt(a, b):
    return lax.dot_general(a, b, (((1,), (1,)), ((), ())), preferred_element_type=F32)


def _dot_tn(a, b):
    return lax.dot_general(a, b, (((0,), (0,)), ((), ())), preferred_element_type=F32)


def _split2(x):
    hi = x.astype(BF16)
    return hi, (x - hi.astype(F32)).astype(BF16)


def _dot_exact_rhs(x, m):
    hi, lo = _split2(x)
    return _dot(hi, m) + _dot(lo, m)


def _dot_exact_lhs(m, x):
    hi, lo = _split2(x)
    return _dot(m, hi) + _dot(m, lo)


def _sigmoid(x):
    return 0.5 + 0.5 * jnp.tanh(0.5 * x)


def _head_ones():
    r = lax.broadcasted_iota(jnp.int32, (LANES, LANES), 0)
    c = lax.broadcasted_iota(jnp.int32, (LANES, LANES), 1)
    return (r < RWKV_HEAD) == (c < RWKV_HEAD)


def _alternating(*stage_generators):
    pending = list(stage_generators)
    while pending:
        for g in list(pending):
            if next(g, StopIteration) is StopIteration:
                pending.remove(g)
            else:
                yield


def _alternate(*stage_generators):
    for _ in _alternating(*stage_generators):
        pass


def _delayed(stage_generator, stages):
    for _ in range(stages):
        yield
    yield from stage_generator


def _inproj_kernel(x_ref, g_ref, win_ref, mu_ref, w2a2_ref, w0_ref, a0_ref, kk_ref, ka_ref,
                   u_ref, gate_ref, r_ref, lw_ref, k_ref, v_ref, kkn_ref, b_ref, carry_ref):
    t = pl.program_id(1)

    @pl.when(t == 0)
    def _():
        carry_ref[...] = jnp.zeros_like(carry_ref)

    ones = _head_ones().astype(BF16)
    carries = {-1: carry_ref[...]}

    def sub_tile(s):
        rs = slice(IN_SUB * s, IN_SUB * (s + 1))
        x = x_ref[0, rs, :]
        ms = jnp.mean(x * x, axis=-1, keepdims=True)
        h = (x * lax.rsqrt(ms + NORM_EPS) * g_ref[...]).astype(BF16)

        def proj(lo, hi):
            return _dot(h, win_ref[:, lo:hi])

        yield
        u_ref[0, rs, :] = proj(0, D_S5)
        yield
        z5 = proj(D_S5, 2 * D_S5)
        gate_ref[0, rs, :D_S5] = (z5 * _sigmoid(z5)).astype(BF16)
        yield
        zr = proj(2 * D_S5 + D_SHIFT, D_IN)
        gate_ref[0, rs, D_S5:] = (zr * _sigmoid(zr)).astype(BF16)
        yield
        rw = proj(2 * D_S5, 2 * D_S5 + D_SHIFT)
        carries[s] = rw[IN_SUB - 1:IN_SUB, :]
        yield
        prev = pltpu.roll(rw, 1, axis=0)
        row = lax.broadcasted_iota(jnp.int32, rw.shape, 0)
        prev = jnp.where(row == 0, carries[s - 1], prev)
        sh = rw + (prev - rw) * mu_ref[...]
        r = sh[:, 0:D_RWKV]
        k = sh[:, D_RWKV:2 * D_RWKV]
        v = sh[:, 2 * D_RWKV:3 * D_RWKV]
        wa = sh[:, 3 * D_RWKV:]
        lane = lax.broadcasted_iota(jnp.int32, wa.shape, 1)
        wa = jnp.where(lane < LOW_RANK, jnp.tanh(wa), wa)
        za = _dot(wa.astype(BF16), w2a2_ref[...])
        r_ref[0, rs, :] = r.astype(BF16)
        v_ref[0, rs, :] = v.astype(BF16)
        yield
        lw_ref[0, rs, :] = -math.exp(-0.5) * _sigmoid(w0_ref[...] + za[:, :D_RWKV])
        iclr = _sigmoid(a0_ref[...] + za[:, D_RWKV:])
        k_ref[0, rs, :] = (k * (1.0 + (iclr - 1.0) * ka_ref[...])).astype(BF16)
        kk = k * kk_ref[...]
        sq = kk * kk
        ss = jnp.concatenate(
            [_dot_exact_rhs(sq[:, LANES * p:LANES * (p + 1)], ones) for p in range(HEAD_PAIRS)], axis=1)
        yield
        kkn = kk * lax.rsqrt(jnp.maximum(ss, 1e-24))
        kkn_ref[0, rs, :] = kkn.astype(BF16)
        b_ref[0, rs, :] = (kkn * iclr).astype(BF16)

    n_sub = TM_IN // IN_SUB
    _alternate(*[_delayed(sub_tile(s), SUB_TILE_SKEW * s) for s in range(n_sub)])
    carry_ref[...] = carries[n_sub - 1]


def _inproj(x, norm_g, w_in, mu, w2a2, w0, a0, k_k, k_a):
    bsz, seqlen, _ = x.shape
    tok = lambda w: pl.BlockSpec((1, TM_IN, w), lambda b, t: (b, t, 0))
    full = lambda a: pl.BlockSpec(a.shape, lambda b, t: (0,) * a.ndim)
    out = lambda w, dt: jax.ShapeDtypeStruct((bsz, seqlen, w), dt)
    params = (norm_g, w_in, mu, w2a2, w0, a0, k_k, k_a)
    return pl.pallas_call(
        _inproj_kernel,
        grid=(bsz, seqlen // TM_IN),
        in_specs=[tok(D_MODEL)] + [full(a) for a in params],
        out_specs=[tok(D_S5), tok(2 * D_S5)] + [tok(D_RWKV)] * 6,
        out_shape=[out(D_S5, F32), out(2 * D_S5, BF16)]
        + [out(D_RWKV, F32 if i == 1 else BF16) for i in range(6)],
        scratch_shapes=[pltpu.VMEM((1, D_SHIFT), F32)],
        compiler_params=pltpu.CompilerParams(
            dimension_semantics=("arbitrary", "arbitrary"), vmem_limit_bytes=VMEM_LIMIT),
        name="inproj",
    )(x, *params)


def _s5_kernel(u_ref, ein_ref, toep_ref, fout_ref, lam_ref, d_ref, y_ref, e_ref, sp_ref):
    nch = e_ref.shape[0]
    half = S5_BLOCK_STATE // 2
    xs = [u_ref[0, pl.ds(j, nch, stride=S5_CHUNK), :] for j in range(S5_CHUNK)]
    xcat = jnp.concatenate([x.astype(BF16) for x in xs], axis=1)
    e_ref[...] = _dot(xcat, ein_ref[0])

    lre = lam_ref[0, :, :half]
    lim = lam_ref[0, :, half:]

    def step(c, s):
        sre, sim = s
        sp_ref[pl.ds(c, 1), :half] = sre
        sp_ref[pl.ds(c, 1), half:] = sim
        e = e_ref[pl.ds(c, 1), :]
        return (lre * sre - lim * sim + e[:, :half], lre * sim + lim * sre + e[:, half:])

    zero = jnp.zeros((1, half), F32)
    lax.fori_loop(0, nch, step, (zero, zero))

    sp = sp_ref[...].astype(BF16)
    d = d_ref[...]
    pairs = S5_CHUNK // 2
    for tp in range(pairs):
        y = _dot(xcat[:, :2 * LANES * (tp + 1)], toep_ref[0, 2 * LANES * (pairs - 1 - tp):, :])
        y = y + _dot(sp, fout_ref[0, :, 2 * LANES * tp:2 * LANES * (tp + 1)])
        for i in range(2):
            t = 2 * tp + i
            y_ref[0, pl.ds(t, nch, stride=S5_CHUNK), :] = y[:, LANES * i:LANES * (i + 1)] + d * xs[t]


def _cmul(ar, ai, br, bi):
    return ar * br - ai * bi, ar * bi + ai * br


def _block_diag_expand(compact, width, inner, transposed):
    k = compact.shape[1] if transposed else compact.shape[2]
    group_w = S5_GROUPS_PER_BLOCK * inner
    kk = jnp.arange(k)
    ww = jnp.arange(width)
    sel = ((kk[:, None] // inner == ww[None, :] // group_w)
           & (kk[:, None] % inner == ww[None, :] % inner)).astype(F32)
    wide_group = (ww % group_w) // inner
    if transposed:
        full = jnp.matmul(sel.T, compact)
        narrow_group = (jnp.arange(compact.shape[2]) % LANES) // S5_GROUP
        keep = wide_group[:, None] == narrow_group[None, :]
    else:
        full = jnp.matmul(compact, sel)
        narrow_group = (jnp.arange(compact.shape[1]) % LANES) // S5_GROUP
        keep = narrow_group[:, None] == wide_group[None, :]
    return jnp.where(keep, full, 0.0)


def _s5_operators(lam_re, lam_im, log_dt, b_re, b_im, c_re, c_im):
    c16, q4, gl = S5_CHUNK, S5_LANE_BLOCKS, S5_GROUPS_PER_BLOCK
    hp = lax.Precision.HIGHEST
    dt = jnp.exp(log_dt)[:, None]
    n = jnp.arange(c16 + 1, dtype=F32)[:, None, None]
    mag = jnp.exp(n * (lam_re * dt)[None])
    ang = n * (lam_im * dt)[None]
    pw_r, pw_i = mag * jnp.cos(ang), mag * jnp.sin(ang)
    den = lam_re * lam_re + lam_im * lam_im
    q_r, q_i = _cmul(pw_r[1] - 1.0, pw_i[1], lam_re / den, -lam_im / den)
    bb_r, bb_i = _cmul(q_r[..., None], q_i[..., None], b_re, b_im)

    w_r, w_i = _cmul(pw_r[:c16, :, :, None], pw_i[:c16, :, :, None], bb_r[None], bb_i[None])
    kt = (jnp.einsum('gop,tgpi->tgio', c_re, w_r, precision=hp)
          - jnp.einsum('gop,tgpi->tgio', c_im, w_i, precision=hp))
    d8 = jnp.arange(c16 // 2)
    lag = 2 * d8[:, None, None] + jnp.arange(2)[None, None, :] - jnp.arange(2)[None, :, None]
    kp = jnp.where((lag >= 0)[..., None, None, None], kt[jnp.maximum(lag, 0)], 0.0)
    kp = kp.reshape(c16 // 2, 2, 2, q4, gl, S5_GROUP, S5_GROUP)
    kp = jnp.transpose(kp, (3, 0, 1, 4, 5, 2, 6))[:, ::-1]
    toep = _block_diag_expand(kp.reshape(q4, c16 * LANES, 2 * S5_GROUP), 2 * LANES, S5_GROUP, False)

    back = c16 - 1 - jnp.arange(c16)
    e_r, e_i = _cmul(pw_r[back][..., None], pw_i[back][..., None], bb_r[None], bb_i[None])
    ein = jnp.stack([e_r, e_i]).reshape(2, c16, q4, gl, S5_STATE, S5_GROUP)
    ein = jnp.transpose(ein, (2, 1, 3, 5, 0, 4)).reshape(q4, c16 * LANES, 2 * S5_STATE)
    ein = _block_diag_expand(ein, S5_BLOCK_STATE, S5_STATE, False)

    f_r, f_i = _cmul(c_re[None], c_im[None], pw_r[1:, :, None, :], pw_i[1:, :, None, :])
    fo = jnp.stack([f_r, -f_i]).reshape(2, c16, q4, gl, S5_GROUP, S5_STATE)
    fo = jnp.transpose(fo, (2, 0, 5, 1, 3, 4)).reshape(q4, 2 * S5_STATE, c16 * LANES)
    fo = _block_diag_expand(fo, S5_BLOCK_STATE, S5_STATE, True)

    lam16 = jnp.concatenate([pw_r[c16].reshape(q4, 1, gl * S5_STATE),
                             pw_i[c16].reshape(q4, 1, gl * S5_STATE)], axis=-1)
    return ein.astype(BF16), toep.astype(BF16), fo.astype(BF16), lam16


def _s5(u, ein, toep, fout, lam16, d):
    bsz, seqlen, _ = u.shape
    nch = seqlen // S5_CHUNK
    per_q = lambda a: pl.BlockSpec((1,) + a.shape[1:], lambda q, b: (q,) + (0,) * (a.ndim - 1))
    seq = pl.BlockSpec((1, seqlen, LANES), lambda q, b: (b, 0, q))
    return pl.pallas_call(
        _s5_kernel,
        grid=(S5_LANE_BLOCKS, bsz),
        in_specs=[seq, per_q(ein), per_q(toep), per_q(fout), per_q(lam16),
                  pl.BlockSpec((1, LANES), lambda q, b: (0, q))],
        out_specs=seq,
        out_shape=jax.ShapeDtypeStruct(u.shape, F32),
        scratch_shapes=[pltpu.VMEM((nch, S5_BLOCK_STATE), F32), pltpu.VMEM((nch, S5_BLOCK_STATE), F32)],
        compiler_params=pltpu.CompilerParams(
            dimension_semantics=("arbitrary", "arbitrary"), vmem_limit_bytes=VMEM_LIMIT),
        name="s5",
    )(u, ein, toep, fout, lam16, d)


def _stack_heads(x, lo):
    return jnp.concatenate([jnp.where(lo, x, 0.0), jnp.where(lo, 0.0, x)], axis=0)


def _head_block_diag(x, head_id, order):
    return jnp.concatenate([jnp.where(head_id == h, x, 0.0) for h in order], axis=0)


def _rwkv_kernel(r_ref, lw_ref, k_ref, v_ref, kk_ref, b_ref, gate_ref, rk_ref, lnw_ref, lnb_ref,
                 y_ref, s_ref, wr_ref, w2_ref, bkt_ref, prb_ref, yv_ref, e2_ref, u_ref):
    tc = RWKV_CHUNK
    nchunks = RWKV_TB // tc
    qw = 2 * LANES

    @pl.when(pl.program_id(1) == 0)
    def _():
        s_ref[...] = jnp.zeros_like(s_ref)

    row = lax.broadcasted_iota(jnp.int32, (tc, qw), 0)
    lane = lax.broadcasted_iota(jnp.int32, (tc, qw), 1)
    col = jnp.bitwise_and(lane, RWKV_HEAD - 1)
    head_id = lane // RWKV_HEAD
    heads = tuple(range(qw // RWKV_HEAD))
    heads_swapped = (1, 0, 3, 2)
    eye_q = (row == col).astype(F32)
    row_p = lax.broadcasted_iota(jnp.int32, (tc, LANES), 0)
    lane_p = lax.broadcasted_iota(jnp.int32, (tc, LANES), 1)
    col_p = jnp.bitwise_and(lane_p, RWKV_HEAD - 1)
    lo_p = lane_p < RWKV_HEAD
    lo_p2 = lax.broadcasted_iota(jnp.int32, (2 * tc, LANES), 1) < RWKV_HEAD
    stril = row_p > col_p
    tril = row_p >= col_p
    state_bd = _head_ones()
    r4 = lax.broadcasted_iota(jnp.int32, (qw, qw), 0)
    l4 = lax.broadcasted_iota(jnp.int32, (qw, qw), 1)
    ones = ((r4 // RWKV_HEAD) == (l4 // RWKV_HEAD)).astype(BF16)
    tri = (lax.broadcasted_iota(jnp.int32, (tc, tc), 0)
           >= lax.broadcasted_iota(jnp.int32, (tc, tc), 1)).astype(BF16)
    bf = lambda t: t.astype(BF16)
    up, dn = slice(0, tc), slice(tc, 2 * tc)

    def prepare(chunks):
        units = []
        for c in chunks:
            rows = slice(c * tc, (c + 1) * tc)
            lw_all = lw_ref[0, rows, :]
            g_all = _dot_exact_lhs(tri, lw_all)
            for q in range(RWKV_QUADS):
                qs = slice(qw * q, qw * (q + 1))
                units.append(dict(c=c, q=q, rows=rows, qs=qs, lw=lw_all[:, qs], g=g_all[:, qs]))
        yield
        for d in units:
            rows, qs, g = d["rows"], d["qs"], d["g"]
            m = 0.5 * g[tc - 1:tc, :]
            e_out = jnp.exp(m - g)
            em = jnp.exp(m)
            rt = r_ref[0, rows, qs] * jnp.exp(g - m)
            bt = b_ref[0, rows, qs] * e_out
            kt = k_ref[0, rows, qs] * e_out
            d["em"] = em
            d["at"] = bf(-kk_ref[0, rows, qs] * jnp.exp(g - d["lw"] - m))
            d["rt"], d["bt"], d["kt"] = bf(rt), bf(bt), bf(kt)
            d["rts"] = bf(rt * em)
            d["vsw"] = bf(_head_block_diag(v_ref[0, rows, qs], head_id, heads_swapped))
            bkt, e2 = [], []
            for j in range(2):
                ps = slice(LANES * j, LANES * (j + 1))
                bks = jnp.concatenate([bt[:, ps], kt[:, ps]], axis=0) * em[:, ps]
                bkt.append(bf(jnp.transpose(bks)))
                e2.append(jnp.transpose(jnp.broadcast_to(em[:, ps] * em[:, ps], (LANES, LANES))))
            bkt_ref[d["c"], d["q"]] = jnp.concatenate(bkt, axis=0)
            e2_ref[d["c"], d["q"]] = jnp.concatenate(e2, axis=1)
        yield
        for d in units:
            d["o"] = []
            for j in range(2):
                ps = slice(LANES * j, LANES * (j + 1))
                bk = jnp.concatenate([d["bt"][:, ps], d["kt"][:, ps]], axis=0)
                kb = jnp.concatenate([d["kt"][:, ps], d["bt"][:, ps]], axis=0)
                rhs = jnp.concatenate([jnp.where(lo_p2, bk, 0.0), jnp.where(lo_p2, 0.0, kb)], axis=0)
                lhs = jnp.concatenate([d["at"][:, ps], d["rt"][:, ps]], axis=0)
                d["o"].append(_dot_nt(lhs, rhs))
        yield
        for d in units:
            h0 = [o[:, :LANES] for o in d["o"]]
            h1 = [o[:, LANES:] for o in d["o"]]

            def pick(rws, first, second, mask):
                return jnp.concatenate(
                    [jnp.where(mask, jnp.where(lo_p, first[j][rws], second[j][rws]), 0.0) for j in range(2)],
                    axis=1)

            d["ap"] = pick(up, h0, h1, stril)
            d["p_ak"] = bf(pick(up, h1, h0, stril))
            p_rb = bf(pick(dn, h0, h1, tril))
            d["p_rk"] = bf(pick(dn, h1, h0, tril))
            prb_ref[d["c"], d["q"]] = p_rb
            d["tinv"] = eye_q + d["ap"]
        yield
        levels = int(math.log2(tc)) - 1
        for d in units:
            d["ap"] = _dot(bf(d["ap"]), bf(_head_block_diag(d["ap"], head_id, heads)))
        for lvl in range(levels):
            yield
            for d in units:
                rhs = bf(_head_block_diag(d["ap"], head_id, heads))
                if lvl < levels - 1:
                    out = _dot(bf(jnp.concatenate([d["tinv"], d["ap"]], axis=0)), rhs)
                    d["tinv"], d["ap"] = d["tinv"] + out[:tc], out[tc:]
                else:
                    d["tinv"] = d["tinv"] + _dot(bf(d["tinv"]), rhs)
        yield
        for d in units:
            d["tinv"] = bf(d["tinv"])
            out = _dot(jnp.concatenate([d["p_ak"], d["p_rk"]], axis=0), d["vsw"])
            d["xv"] = out[:tc]
            yv_ref[d["c"], d["q"]] = out[tc:]
        yield
        for d in units:
            w1s, w2 = [], []
            for j in range(2):
                ps = slice(LANES * j, LANES * (j + 1))
                rhs = jnp.concatenate([_stack_heads(d["at"][:, ps], lo_p),
                                       bf(_stack_heads(d["xv"][:, ps], lo_p))], axis=1)
                w = _dot(d["tinv"][:, ps], rhs)
                w1s.append(bf(w[:, :LANES] * d["em"][:, ps]))
                w2.append(w[:, LANES:])
            wr_ref[d["c"], d["q"]] = jnp.concatenate(w1s + [d["rts"][:, :LANES], d["rts"][:, LANES:]], axis=0)
            w2_ref[d["c"], d["q"]] = jnp.concatenate(w2, axis=1)

    def recur_step(c):
        rows = slice(c * tc, (c + 1) * tc)
        units = [dict(q=q, qs=slice(qw * q, qw * (q + 1))) for q in range(RWKV_QUADS)]
        for d in units:
            d["h0"] = s_ref[d["q"]]
        for d in units:
            out = _dot(wr_ref[c, d["q"]], bf(d["h0"]))
            d["u"] = (jnp.concatenate([out[:tc, :LANES], out[tc:2 * tc, LANES:]], axis=1)
                      + w2_ref[c, d["q"]])
            d["yr"] = jnp.concatenate([out[2 * tc:3 * tc, :LANES], out[3 * tc:, LANES:]], axis=1)
        yield
        for d in units:
            q = d["q"]
            uv = jnp.concatenate([bf(d["u"]), v_ref[0, rows, d["qs"]]], axis=0)
            upd = _dot(bkt_ref[c, q], uv)
            upd = jnp.concatenate([jnp.where(state_bd, upd[:LANES, :LANES], 0.0),
                                   jnp.where(state_bd, upd[LANES:, LANES:], 0.0)], axis=1)
            s_ref[q] = d["h0"] * e2_ref[c, q] + upd
            u_ref[c, q] = bf(d["u"])
            yv_ref[c, q] = yv_ref[c, q] + d["yr"]

    def emit_step(c):
        rows = slice(c * tc, (c + 1) * tc)
        units = [dict(q=q, qs=slice(qw * q, qw * (q + 1))) for q in range(RWKV_QUADS)]
        for d in units:
            q = d["q"]
            d["y"] = yv_ref[c, q] + _dot(prb_ref[c, q], _head_block_diag(u_ref[c, q], head_id, heads))
        yield
        for d in units:
            qs, y = d["qs"], d["y"]
            rkr = r_ref[0, rows, qs].astype(F32) * k_ref[0, rows, qs] * rk_ref[:, qs]
            d["sums"] = _dot_exact_rhs(jnp.concatenate([y, rkr], axis=0), ones)
        yield
        for d in units:
            d["dev"] = d["y"] - d["sums"][:tc] * (1.0 / RWKV_HEAD)
            d["var"] = _dot_exact_rhs(d["dev"] * d["dev"], ones) * (1.0 / RWKV_HEAD)
        for d in units:
            qs = d["qs"]
            yn = d["dev"] * lax.rsqrt(d["var"] + GN_EPS) * lnw_ref[:, qs] + lnb_ref[:, qs]
            y_ref[0, rows, qs] = bf((yn + d["sums"][tc:] * v_ref[0, rows, qs]) * gate_ref[0, rows, qs])

    def chain(chunks):
        for c in chunks:
            yield from _alternating(*([emit_step(c - 1)] if c > 0 else []), recur_step(c))

    groups = [list(range(g, g + RWKV_PREP_CHUNKS)) for g in range(0, nchunks, RWKV_PREP_CHUNKS)]
    _alternate(prepare(groups[0]))
    for prev, nxt in zip(groups[:-1], groups[1:]):
        _alternate(prepare(nxt), chain(prev))
    _alternate(chain(groups[-1]))
    _alternate(emit_step(nchunks - 1))


def _rwkv(r, lw, k, v, kkn, b, gate, r_k, ln_w, ln_b):
    bsz, seqlen, _ = r.shape
    nchunks = RWKV_TB // RWKV_CHUNK
    tok = pl.BlockSpec((1, RWKV_TB, D_RWKV), lambda bi, t: (bi, t, 0))
    gate_spec = pl.BlockSpec((1, RWKV_TB, D_RWKV), lambda bi, t: (bi, t, 1))
    vec = pl.BlockSpec((1, D_RWKV), lambda bi, t: (0, 0))
    unit = lambda rows, cols, dt: pltpu.VMEM((nchunks, RWKV_QUADS, rows, cols), dt)
    return pl.pallas_call(
        _rwkv_kernel,
        grid=(bsz, seqlen // RWKV_TB),
        in_specs=[tok] * 6 + [gate_spec] + [vec] * 3,
        out_specs=tok,
        out_shape=jax.ShapeDtypeStruct(r.shape, BF16),
        scratch_shapes=[
            pltpu.VMEM((RWKV_QUADS, LANES, 2 * LANES), F32),
            unit(4 * RWKV_CHUNK, LANES, BF16),
            unit(RWKV_CHUNK, 2 * LANES, F32),
            unit(2 * LANES, 2 * RWKV_CHUNK, BF16),
            unit(RWKV_CHUNK, 2 * LANES, BF16),
            unit(RWKV_CHUNK, 2 * LANES, F32),
            unit(LANES, 2 * LANES, F32),
            unit(RWKV_CHUNK, 2 * LANES, BF16),
        ],
        compiler_params=pltpu.CompilerParams(
            dimension_semantics=("arbitrary", "arbitrary"), vmem_limit_bytes=VMEM_LIMIT),
        name="rwkv7",
    )(r, lw, k, v, kkn, b, gate, r_k, ln_w, ln_b)


def _out_kernel(y5_ref, gate_ref, yr_ref, x_ref, gluw_ref, glub_ref, wout_ref, fg_ref, o_ref):
    def sub_tile(s):
        rs = slice(IN_SUB * s, IN_SUB * (s + 1))
        y = y5_ref[0, rs, :]
        y = 0.5 * y * (1.0 + jnp.tanh(math.sqrt(2.0 / math.pi) * (y + 0.044715 * (y * y * y))))
        yield
        glu = _sigmoid(_dot(y.astype(BF16), gluw_ref[...]) + glub_ref[...])
        ys = y * glu * gate_ref[0, rs, :]
        ycat = jnp.concatenate([ys.astype(BF16), yr_ref[0, rs, :]], axis=-1)
        yield
        xn = x_ref[0, rs, :] + _dot(ycat, wout_ref[...])
        yield
        ms = jnp.mean(xn * xn, axis=-1, keepdims=True)
        o_ref[0, rs, :] = xn * lax.rsqrt(ms + NORM_EPS) * fg_ref[...]

    _alternate(*[_delayed(sub_tile(s), s) for s in range(TM_OUT // IN_SUB)])


def _outproj(y5, gate, yr, x, glu_w, glu_b, w_out, final_g):
    bsz, seqlen, _ = x.shape
    tok = lambda w: pl.BlockSpec((1, TM_OUT, w), lambda b, t: (b, t, 0))
    full = lambda a: pl.BlockSpec(a.shape, lambda b, t: (0,) * a.ndim)
    return pl.pallas_call(
        _out_kernel,
        grid=(bsz, seqlen // TM_OUT),
        in_specs=[tok(D_S5), tok(D_S5), tok(D_RWKV), tok(D_MODEL),
                  full(glu_w), full(glu_b), full(w_out), full(final_g)],
        out_specs=tok(D_MODEL),
        out_shape=jax.ShapeDtypeStruct(x.shape, F32),
        compiler_params=pltpu.CompilerParams(
            dimension_semantics=("arbitrary", "arbitrary"), vmem_limit_bytes=VMEM_LIMIT),
        name="outproj",
    )(y5, gate, yr, x, glu_w, glu_b, w_out, final_g)


@jax.jit
def _forward(x, norm_g, w_in, s5_lam_re, s5_lam_im, s5_log_dt, s5_b_re, s5_b_im, s5_c_re, s5_c_im,
             s5_d, s5_glu_w, s5_glu_b, rwkv_mu, rwkv_w0, rwkv_w2, rwkv_a0, rwkv_a2, rwkv_k_k, rwkv_k_a,
             rwkv_r_k, rwkv_ln_w, rwkv_ln_b, w_out, final_g):
    assert w_in.shape[0] == 1, "single-layer block"
    for l in range(1):
        zeros = jnp.zeros((LOW_RANK, D_RWKV), F32)
        w2a2 = jnp.concatenate(
            [jnp.concatenate([rwkv_w2[l], zeros], axis=1), jnp.concatenate([zeros, rwkv_a2[l]], axis=1)],
            axis=0).astype(BF16)
        row = lambda a: a.reshape(1, -1).astype(F32)
        u, gate, r, lw, k, v, kkn, b = _inproj(
            x, row(norm_g[l]), w_in[l].astype(BF16), row(rwkv_mu[l]), w2a2, row(rwkv_w0[l]),
            row(rwkv_a0[l]), row(rwkv_k_k[l]), row(rwkv_k_a[l]))
        ein, toep, fout, lam16 = _s5_operators(
            s5_lam_re[l], s5_lam_im[l], s5_log_dt[l], s5_b_re[l], s5_b_im[l], s5_c_re[l], s5_c_im[l])
        y5 = _s5(u, ein, toep, fout, lam16, row(s5_d[l]))
        yr = _rwkv(r, lw, k, v, kkn, b, gate, row(rwkv_r_k[l]), row(rwkv_ln_w[l]), row(rwkv_ln_b[l]))
        x = _outproj(y5, gate, yr, x, s5_glu_w[l].astype(BF16), row(s5_glu_b[l]),
                     w_out[l].astype(BF16), row(final_g))
    return x


def kernel(x, norm_g, w_in, s5_lam_re, s5_lam_im, s5_log_dt, s5_b_re, s5_b_im, s5_c_re, s5_c_im, s5_d, s5_glu_w, s5_glu_b, rwkv_mu, rwkv_w0, rwkv_w2, rwkv_a0, rwkv_a2, rwkv_k_k, rwkv_k_a, rwkv_r_k, rwkv_ln_w, rwkv_ln_b, w_out, final_g):
    return _forward(x, norm_g, w_in, s5_lam_re, s5_lam_im, s5_log_dt, s5_b_re, s5_b_im, s5_c_re, s5_c_im,
                    s5_d, s5_glu_w, s5_glu_b, rwkv_mu, rwkv_w0, rwkv_w2, rwkv_a0, rwkv_a2, rwkv_k_k,
                    rwkv_k_a, rwkv_r_k, rwkv_ln_w, rwkv_ln_b, w_out, final_g)
```

```python
import math

import jax
import jax.numpy as jnp
from jax import lax
from jax.experimental import pallas as pl
from jax.experimental.pallas import tpu as pltpu

F32 = jnp.float32
BF16 = jnp.bfloat16

D_MODEL = 1024
D_S5 = 512
D_RWKV = 512
S5_GROUP = 16
S5_GROUPS = 32
S5_STATE = 64
RWKV_HEAD = 64
LOW_RANK = 64
D_SHIFT = 3 * D_RWKV + 2 * LOW_RANK
D_IN = 2 * D_S5 + D_SHIFT + D_RWKV
NORM_EPS = 1e-6
GN_EPS = 64e-5

LANES = 128
TM_IN = 1024
IN_SUB = 128
SUB_TILE_SKEW = 2
TM_OUT = 1024
S5_CHUNK = 16
S5_BATCH = 2
S5_PIPE_SKEW = 6
S5_LANE_BLOCKS = D_S5 // LANES
S5_GROUPS_PER_BLOCK = LANES // S5_GROUP
S5_BLOCK_STATE = 2 * S5_GROUPS_PER_BLOCK * S5_STATE
RWKV_CHUNK = 64
RWKV_TB = 512
RWKV_PREP_CHUNKS = 4
RWKV_OUT_CHUNKS = 2
HEAD_PAIRS = D_RWKV // LANES
RWKV_QUADS = HEAD_PAIRS // 2
VMEM_LIMIT = 56 * 1024 * 1024


def _dot(a, b):
    return jnp.dot(a, b, preferred_element_type=F32)


def _dot_nt(a, b):
    return lax.dot_general(a, b, (((1,), (1,)), ((), ())), preferred_element_type=F32)


def _dot_tn(a, b):
    return lax.dot_general(a, b, (((0,), (0,)), ((), ())), preferred_element_type=F32)


def _split2(x):
    hi = x.astype(BF16)
    return hi, (x - hi.astype(F32)).astype(BF16)


def _dot_exact_rhs(x, m):
    hi, lo = _split2(x)
    out = _dot(jnp.concatenate([hi, lo], axis=0), m)
    return out[:x.shape[0]] + out[x.shape[0]:]


def _dot_exact_lhs(m, x):
    hi, lo = _split2(x)
    return _dot(m, hi) + _dot(m, lo)


def _sigmoid(x):
    return 0.5 + 0.5 * jnp.tanh(0.5 * x)


def _head_ones():
    r = lax.broadcasted_iota(jnp.int32, (LANES, LANES), 0)
    c = lax.broadcasted_iota(jnp.int32, (LANES, LANES), 1)
    return (r < RWKV_HEAD) == (c < RWKV_HEAD)


def _alternating(*stage_generators):
    pending = list(stage_generators)
    while pending:
        for g in list(pending):
            if next(g, StopIteration) is StopIteration:
                pending.remove(g)
            else:
                yield


def _alternate(*stage_generators):
    for _ in _alternating(*stage_generators):
        pass


def _delayed(stage_generator, stages):
    for _ in range(stages):
        yield
    yield from stage_generator


def _inproj_kernel(x_ref, g_ref, win_ref, mu_ref, w2a2_ref, w0_ref, a0_ref, kk_ref, ka_ref,
                   u_ref, gate_ref, r_ref, lw_ref, k_ref, v_ref, kkn_ref, b_ref, carry_ref):
    t = pl.program_id(1)

    @pl.when(t == 0)
    def _():
        carry_ref[...] = jnp.zeros_like(carry_ref)

    ones = _head_ones().astype(BF16)
    carries = {-1: carry_ref[...]}

    def sub_tile(s):
        rs = slice(IN_SUB * s, IN_SUB * (s + 1))
        x = x_ref[0, rs, :]
        ms = jnp.mean(x * x, axis=-1, keepdims=True)
        h = (x * lax.rsqrt(ms + NORM_EPS) * g_ref[...]).astype(BF16)

        def proj(lo, hi):
            return _dot(h, win_ref[:, lo:hi])

        yield
        u_ref[0, rs, :] = proj(0, D_S5)
        yield
        z5 = proj(D_S5, 2 * D_S5)
        gate_ref[0, rs, :D_S5] = (z5 * _sigmoid(z5)).astype(BF16)
        yield
        zr = proj(2 * D_S5 + D_SHIFT, D_IN)
        gate_ref[0, rs, D_S5:] = (zr * _sigmoid(zr)).astype(BF16)
        yield
        rw = proj(2 * D_S5, 2 * D_S5 + D_SHIFT)
        carries[s] = rw[IN_SUB - 1:IN_SUB, :]
        yield
        prev = pltpu.roll(rw, 1, axis=0)
        row = lax.broadcasted_iota(jnp.int32, rw.shape, 0)
        prev = jnp.where(row == 0, carries[s - 1], prev)
        sh = rw + (prev - rw) * mu_ref[...]
        r = sh[:, 0:D_RWKV]
        k = sh[:, D_RWKV:2 * D_RWKV]
        v = sh[:, 2 * D_RWKV:3 * D_RWKV]
        wa = sh[:, 3 * D_RWKV:]
        lane = lax.broadcasted_iota(jnp.int32, wa.shape, 1)
        wa = jnp.where(lane < LOW_RANK, jnp.tanh(wa), wa)
        za = _dot(wa.astype(BF16), w2a2_ref[...])
        r_ref[0, rs, :] = r.astype(BF16)
        v_ref[0, rs, :] = v.astype(BF16)
        yield
        lw_ref[0, rs, :] = -math.exp(-0.5) * _sigmoid(w0_ref[...] + za[:, :D_RWKV])
        iclr = _sigmoid(a0_ref[...] + za[:, D_RWKV:])
        k_ref[0, rs, :] = (k * (1.0 + (iclr - 1.0) * ka_ref[...])).astype(BF16)
        kk = k * kk_ref[...]
        sq = kk * kk
        ss = jnp.concatenate(
            [_dot_exact_rhs(sq[:, LANES * p:LANES * (p + 1)], ones) for p in range(HEAD_PAIRS)], axis=1)
        yield
        kkn = kk * lax.rsqrt(jnp.maximum(ss, 1e-24))
        kkn_ref[0, rs, :] = kkn.astype(BF16)
        b_ref[0, rs, :] = (kkn * iclr).astype(BF16)

    n_sub = TM_IN // IN_SUB
    _alternate(*[_delayed(sub_tile(s), SUB_TILE_SKEW * s) for s in range(n_sub)])
    carry_ref[...] = carries[n_sub - 1]


def _inproj(x, norm_g, w_in, mu, w2a2, w0, a0, k_k, k_a):
    bsz, seqlen, _ = x.shape
    tok = lambda w: pl.BlockSpec((1, TM_IN, w), lambda b, t: (b, t, 0))
    full = lambda a: pl.BlockSpec(a.shape, lambda b, t: (0,) * a.ndim)
    out = lambda w, dt: jax.ShapeDtypeStruct((bsz, seqlen, w), dt)
    params = (norm_g, w_in, mu, w2a2, w0, a0, k_k, k_a)
    return pl.pallas_call(
        _inproj_kernel,
        grid=(bsz, seqlen // TM_IN),
        in_specs=[tok(D_MODEL)] + [full(a) for a in params],
        out_specs=[tok(D_S5), tok(2 * D_S5)] + [tok(D_RWKV)] * 6,
        out_shape=[out(D_S5, F32), out(2 * D_S5, BF16)]
        + [out(D_RWKV, F32 if i == 1 else BF16) for i in range(6)],
        scratch_shapes=[pltpu.VMEM((1, D_SHIFT), F32)],
        compiler_params=pltpu.CompilerParams(
            dimension_semantics=("arbitrary", "arbitrary"), vmem_limit_bytes=VMEM_LIMIT),
        name="inproj",
    )(x, *params)


def _s5_kernel(u_ref, ein_ref, toep_ref, fout_ref, lam_ref, d_ref, y_ref, e_ref, sp_ref):
    nch = e_ref.shape[1]
    half = S5_BLOCK_STATE // 2
    pairs = S5_CHUNK // 2
    lre = lam_ref[0, :, :half]
    lim = lam_ref[0, :, half:]
    d = d_ref[...]

    def pipeline(bi):
        xs = [u_ref[bi, pl.ds(j, nch, stride=S5_CHUNK), :] for j in range(S5_CHUNK)]
        xcat = jnp.concatenate([x.astype(BF16) for x in xs], axis=1)
        yield
        e_ref[bi] = _dot(xcat, ein_ref[0])
        yield
        intra = {}

        def scan():
            sre = sim = jnp.zeros((1, half), F32)
            for c in range(nch):
                sp_ref[bi, c:c + 1, :half] = sre
                sp_ref[bi, c:c + 1, half:] = sim
                e = e_ref[bi, c:c + 1, :]
                sre, sim = lre * sre - lim * sim + e[:, :half], lre * sim + lim * sre + e[:, half:]
                if (c + 1) % (nch // pairs) == 0:
                    yield

        def toeplitz():
            for tp in range(pairs):
                intra[tp] = _dot(xcat[:, :2 * LANES * (tp + 1)],
                                 toep_ref[0, 2 * LANES * (pairs - 1 - tp):, :])
                yield

        yield from _alternating(scan(), toeplitz())
        sp = sp_ref[bi].astype(BF16)
        for tp in range(pairs):
            y = intra[tp] + _dot(sp, fout_ref[0, :, 2 * LANES * tp:2 * LANES * (tp + 1)])
            for i in range(2):
                t = 2 * tp + i
                y_ref[bi, pl.ds(t, nch, stride=S5_CHUNK), :] = y[:, LANES * i:LANES * (i + 1)] + d * xs[t]
            yield

    _alternate(*[_delayed(pipeline(bi), S5_PIPE_SKEW * bi) for bi in range(S5_BATCH)])


def _cmul(ar, ai, br, bi):
    return ar * br - ai * bi, ar * bi + ai * br


def _block_diag_expand(compact, width, inner, transposed):
    k = compact.shape[1] if transposed else compact.shape[2]
    group_w = S5_GROUPS_PER_BLOCK * inner
    kk = jnp.arange(k)
    ww = jnp.arange(width)
    sel = ((kk[:, None] // inner == ww[None, :] // group_w)
           & (kk[:, None] % inner == ww[None, :] % inner)).astype(F32)
    wide_group = (ww % group_w) // inner
    if transposed:
        full = jnp.matmul(sel.T, compact)
        narrow_group = (jnp.arange(compact.shape[2]) % LANES) // S5_GROUP
        keep = wide_group[:, None] == narrow_group[None, :]
    else:
        full = jnp.matmul(compact, sel)
        narrow_group = (jnp.arange(compact.shape[1]) % LANES) // S5_GROUP
        keep = narrow_group[:, None] == wide_group[None, :]
    return jnp.where(keep, full, 0.0)


def _s5_operators(lam_re, lam_im, log_dt, b_re, b_im, c_re, c_im):
    c16, q4, gl = S5_CHUNK, S5_LANE_BLOCKS, S5_GROUPS_PER_BLOCK
    hp = lax.Precision.HIGHEST
    dt = jnp.exp(log_dt)[:, None]
    n = jnp.arange(c16 + 1, dtype=F32)[:, None, None]
    mag = jnp.exp(n * (lam_re * dt)[None])
    ang = n * (lam_im * dt)[None]
    pw_r, pw_i = mag * jnp.cos(ang), mag * jnp.sin(ang)
    den = lam_re * lam_re + lam_im * lam_im
    q_r, q_i = _cmul(pw_r[1] - 1.0, pw_i[1], lam_re / den, -lam_im / den)
    bb_r, bb_i = _cmul(q_r[..., None], q_i[..., None], b_re, b_im)

    w_r, w_i = _cmul(pw_r[:c16, :, :, None], pw_i[:c16, :, :, None], bb_r[None], bb_i[None])
    kt = (jnp.einsum('gop,tgpi->tgio', c_re, w_r, precision=hp)
          - jnp.einsum('gop,tgpi->tgio', c_im, w_i, precision=hp))
    d8 = jnp.arange(c16 // 2)
    lag = 2 * d8[:, None, None] + jnp.arange(2)[None, None, :] - jnp.arange(2)[None, :, None]
    kp = jnp.where((lag >= 0)[..., None, None, None], kt[jnp.maximum(lag, 0)], 0.0)
    kp = kp.reshape(c16 // 2, 2, 2, q4, gl, S5_GROUP, S5_GROUP)
    kp = jnp.transpose(kp, (3, 0, 1, 4, 5, 2, 6))[:, ::-1]
    toep = _block_diag_expand(kp.reshape(q4, c16 * LANES, 2 * S5_GROUP), 2 * LANES, S5_GROUP, False)

    back = c16 - 1 - jnp.arange(c16)
    e_r, e_i = _cmul(pw_r[back][..., None], pw_i[back][..., None], bb_r[None], bb_i[None])
    ein = jnp.stack([e_r, e_i]).reshape(2, c16, q4, gl, S5_STATE, S5_GROUP)
    ein = jnp.transpose(ein, (2, 1, 3, 5, 0, 4)).reshape(q4, c16 * LANES, 2 * S5_STATE)
    ein = _block_diag_expand(ein, S5_BLOCK_STATE, S5_STATE, False)

    f_r, f_i = _cmul(c_re[None], c_im[None], pw_r[1:, :, None, :], pw_i[1:, :, None, :])
    fo = jnp.stack([f_r, -f_i]).reshape(2, c16, q4, gl, S5_GROUP, S5_STATE)
    fo = jnp.transpose(fo, (2, 0, 5, 1, 3, 4)).reshape(q4, 2 * S5_STATE, c16 * LANES)
    fo = _block_diag_expand(fo, S5_BLOCK_STATE, S5_STATE, True)

    lam16 = jnp.concatenate([pw_r[c16].reshape(q4, 1, gl * S5_STATE),
                             pw_i[c16].reshape(q4, 1, gl * S5_STATE)], axis=-1)
    return ein.astype(BF16), toep.astype(BF16), fo.astype(BF16), lam16


def _s5(u, ein, toep, fout, lam16, d):
    bsz, seqlen, _ = u.shape
    nch = seqlen // S5_CHUNK
    per_q = lambda a: pl.BlockSpec((1,) + a.shape[1:], lambda q, b: (q,) + (0,) * (a.ndim - 1))
    seq = pl.BlockSpec((S5_BATCH, seqlen, LANES), lambda q, b: (b, 0, q))
    state = pltpu.VMEM((S5_BATCH, nch, S5_BLOCK_STATE), F32)
    return pl.pallas_call(
        _s5_kernel,
        grid=(S5_LANE_BLOCKS, bsz // S5_BATCH),
        in_specs=[seq, per_q(ein), per_q(toep), per_q(fout), per_q(lam16),
                  pl.BlockSpec((1, LANES), lambda q, b: (0, q))],
        out_specs=seq,
        out_shape=jax.ShapeDtypeStruct(u.shape, F32),
        scratch_shapes=[state, state],
        compiler_params=pltpu.CompilerParams(
            dimension_semantics=("arbitrary", "arbitrary"), vmem_limit_bytes=VMEM_LIMIT),
        name="s5",
    )(u, ein, toep, fout, lam16, d)


def _stack_heads(x, lo):
    return jnp.concatenate([jnp.where(lo, x, 0.0), jnp.where(lo, 0.0, x)], axis=0)


def _head_block_diag(x, head_id, order):
    return jnp.concatenate([jnp.where(head_id == h, x, 0.0) for h in order], axis=0)


def _rwkv_kernel(r_ref, lw_ref, k_ref, v_ref, kk_ref, b_ref, gate_ref, rk_ref, lnw_ref, lnb_ref,
                 y_ref, s_ref, wr_ref, w2_ref, bkt_ref, prb_ref, yv_ref, e2_ref, u_ref):
    tc = RWKV_CHUNK
    nchunks = RWKV_TB // tc
    qw = 2 * LANES

    @pl.when(pl.program_id(1) == 0)
    def _():
        s_ref[...] = jnp.zeros_like(s_ref)

    row = lax.broadcasted_iota(jnp.int32, (tc, qw), 0)
    lane = lax.broadcasted_iota(jnp.int32, (tc, qw), 1)
    col = jnp.bitwise_and(lane, RWKV_HEAD - 1)
    head_id = lane // RWKV_HEAD
    heads = tuple(range(qw // RWKV_HEAD))
    heads_swapped = (1, 0, 3, 2)
    eye_q = (row == col).astype(F32)
    row_p = lax.broadcasted_iota(jnp.int32, (tc, LANES), 0)
    lane_p = lax.broadcasted_iota(jnp.int32, (tc, LANES), 1)
    col_p = jnp.bitwise_and(lane_p, RWKV_HEAD - 1)
    lo_p = lane_p < RWKV_HEAD
    lo_p2 = lax.broadcasted_iota(jnp.int32, (2 * tc, LANES), 1) < RWKV_HEAD
    stril = row_p > col_p
    tril = row_p >= col_p
    state_bd = _head_ones()
    r4 = lax.broadcasted_iota(jnp.int32, (qw, qw), 0)
    l4 = lax.broadcasted_iota(jnp.int32, (qw, qw), 1)
    ones = ((r4 // RWKV_HEAD) == (l4 // RWKV_HEAD)).astype(BF16)
    tri = (lax.broadcasted_iota(jnp.int32, (tc, tc), 0)
           >= lax.broadcasted_iota(jnp.int32, (tc, tc), 1)).astype(BF16)
    bf = lambda t: t.astype(BF16)
    up, dn = slice(0, tc), slice(tc, 2 * tc)

    def prepare(chunks):
        units = []
        for c in chunks:
            rows = slice(c * tc, (c + 1) * tc)
            lw_all = lw_ref[0, rows, :]
            g_all = _dot_exact_lhs(tri, lw_all)
            for q in range(RWKV_QUADS):
                qs = slice(qw * q, qw * (q + 1))
                units.append(dict(c=c, q=q, rows=rows, qs=qs, lw=lw_all[:, qs], g=g_all[:, qs]))
        yield
        for d in units:
            rows, qs, g = d["rows"], d["qs"], d["g"]
            m = 0.5 * g[tc - 1:tc, :]
            e_out = jnp.exp(m - g)
            em = jnp.exp(m)
            rt = r_ref[0, rows, qs] * jnp.exp(g - m)
            bt = b_ref[0, rows, qs] * e_out
            kt = k_ref[0, rows, qs] * e_out
            d["em"] = em
            d["at"] = bf(-kk_ref[0, rows, qs] * jnp.exp(g - d["lw"] - m))
            d["rt"], d["bt"], d["kt"] = bf(rt), bf(bt), bf(kt)
            d["rts"] = bf(rt * em)
            d["vsw"] = bf(_head_block_diag(v_ref[0, rows, qs], head_id, heads_swapped))
            bkt, e2 = [], []
            for j in range(2):
                ps = slice(LANES * j, LANES * (j + 1))
                bks = jnp.concatenate([bt[:, ps], kt[:, ps]], axis=0) * em[:, ps]
                bkt.append(bf(jnp.transpose(bks)))
                e2.append(jnp.transpose(jnp.broadcast_to(em[:, ps] * em[:, ps], (LANES, LANES))))
            bkt_ref[d["c"], d["q"]] = jnp.concatenate(bkt, axis=0)
            e2_ref[d["c"], d["q"]] = jnp.concatenate(e2, axis=1)
        yield
        for d in units:
            d["o"] = []
            for j in range(2):
                ps = slice(LANES * j, LANES * (j + 1))
                bk = jnp.concatenate([d["bt"][:, ps], d["kt"][:, ps]], axis=0)
                kb = jnp.concatenate([d["kt"][:, ps], d["bt"][:, ps]], axis=0)
                rhs = jnp.concatenate([jnp.where(lo_p2, bk, 0.0), jnp.where(lo_p2, 0.0, kb)], axis=0)
                lhs = jnp.concatenate([d["at"][:, ps], d["rt"][:, ps]], axis=0)
                d["o"].append(_dot_nt(lhs, rhs))
        yield
        for d in units:
            h0 = [o[:, :LANES] for o in d["o"]]
            h1 = [o[:, LANES:] for o in d["o"]]

            def pick(rws, first, second, mask):
                return jnp.concatenate(
                    [jnp.where(mask, jnp.where(lo_p, first[j][rws], second[j][rws]), 0.0) for j in range(2)],
                    axis=1)

            d["ap"] = pick(up, h0, h1, stril)
            d["p_ak"] = bf(pick(up, h1, h0, stril))
            p_rb = bf(pick(dn, h0, h1, tril))
            d["p_rk"] = bf(pick(dn, h1, h0, tril))
            prb_ref[d["c"], d["q"]] = p_rb
            d["tinv"] = eye_q + d["ap"]
        yield
        levels = int(math.log2(tc)) - 1
        for d in units:
            d["ap"] = _dot(bf(d["ap"]), bf(_head_block_diag(d["ap"], head_id, heads)))
        for lvl in range(levels):
            yield
            for d in units:
                rhs = bf(_head_block_diag(d["ap"], head_id, heads))
                if lvl < levels - 1:
                    out = _dot(bf(jnp.concatenate([d["tinv"], d["ap"]], axis=0)), rhs)
                    d["tinv"], d["ap"] = d["tinv"] + out[:tc], out[tc:]
                else:
                    d["tinv"] = d["tinv"] + _dot(bf(d["tinv"]), rhs)
        yield
        for d in units:
            d["tinv"] = bf(d["tinv"])
            out = _dot(jnp.concatenate([d["p_ak"], d["p_rk"]], axis=0), d["vsw"])
            d["xv"] = out[:tc]
            yv_ref[d["c"], d["q"]] = out[tc:]
        yield
        for d in units:
            w1s, w2 = [], []
            for j in range(2):
                ps = slice(LANES * j, LANES * (j + 1))
                rhs = jnp.concatenate([_stack_heads(d["at"][:, ps], lo_p),
                                       bf(_stack_heads(d["xv"][:, ps], lo_p))], axis=1)
                w = _dot(d["tinv"][:, ps], rhs)
                w1s.append(bf(w[:, :LANES] * d["em"][:, ps]))
                w2.append(w[:, LANES:])
            wr_ref[d["c"], d["q"]] = jnp.concatenate(w1s + [d["rts"][:, :LANES], d["rts"][:, LANES:]], axis=0)
            w2_ref[d["c"], d["q"]] = jnp.concatenate(w2, axis=1)

    def recur_step(c):
        rows = slice(c * tc, (c + 1) * tc)
        units = [dict(q=q, qs=slice(qw * q, qw * (q + 1))) for q in range(RWKV_QUADS)]
        for d in units:
            d["h0"] = s_ref[d["q"]]
        for d in units:
            out = _dot(wr_ref[c, d["q"]], bf(d["h0"]))
            d["u"] = (jnp.concatenate([out[:tc, :LANES], out[tc:2 * tc, LANES:]], axis=1)
                      + w2_ref[c, d["q"]])
            d["yr"] = jnp.concatenate([out[2 * tc:3 * tc, :LANES], out[3 * tc:, LANES:]], axis=1)
        yield
        for d in units:
            q = d["q"]
            uv = jnp.concatenate([bf(d["u"]), v_ref[0, rows, d["qs"]]], axis=0)
            upd = _dot(bkt_ref[c, q], uv)
            upd = jnp.concatenate([jnp.where(state_bd, upd[:LANES, :LANES], 0.0),
                                   jnp.where(state_bd, upd[LANES:, LANES:], 0.0)], axis=1)
            s_ref[q] = d["h0"] * e2_ref[c, q] + upd
            u_ref[c, q] = bf(d["u"])
            yv_ref[c, q] = yv_ref[c, q] + d["yr"]

    def emit_step(c):
        rows = slice(c * tc, (c + 1) * tc)
        units = [dict(q=q, qs=slice(qw * q, qw * (q + 1))) for q in range(RWKV_QUADS)]
        for d in units:
            q = d["q"]
            d["y"] = yv_ref[c, q] + _dot(prb_ref[c, q], _head_block_diag(u_ref[c, q], head_id, heads))
        yield
        for d in units:
            qs, y = d["qs"], d["y"]
            rkr = r_ref[0, rows, qs].astype(F32) * k_ref[0, rows, qs] * rk_ref[:, qs]
            d["sums"] = _dot_exact_rhs(jnp.concatenate([y, rkr], axis=0), ones)
        yield
        for d in units:
            d["dev"] = d["y"] - d["sums"][:tc] * (1.0 / RWKV_HEAD)
            d["var"] = _dot_exact_rhs(d["dev"] * d["dev"], ones) * (1.0 / RWKV_HEAD)
        for d in units:
            qs = d["qs"]
            yn = d["dev"] * lax.rsqrt(d["var"] + GN_EPS) * lnw_ref[:, qs] + lnb_ref[:, qs]
            y_ref[0, rows, qs] = bf((yn + d["sums"][tc:] * v_ref[0, rows, qs]) * gate_ref[0, rows, qs])

    def chain(chunks):
        for c in chunks:
            yield from _alternating(*([emit_step(c - 1)] if c > 0 else []), recur_step(c))

    groups = [list(range(g, g + RWKV_PREP_CHUNKS)) for g in range(0, nchunks, RWKV_PREP_CHUNKS)]
    _alternate(prepare(groups[0]))
    for prev, nxt in zip(groups[:-1], groups[1:]):
        _alternate(prepare(nxt), chain(prev))
    _alternate(chain(groups[-1]))
    _alternate(emit_step(nchunks - 1))


def _rwkv(r, lw, k, v, kkn, b, gate, r_k, ln_w, ln_b):
    bsz, seqlen, _ = r.shape
    nchunks = RWKV_TB // RWKV_CHUNK
    tok = pl.BlockSpec((1, RWKV_TB, D_RWKV), lambda bi, t: (bi, t, 0))
    gate_spec = pl.BlockSpec((1, RWKV_TB, D_RWKV), lambda bi, t: (bi, t, 1))
    vec = pl.BlockSpec((1, D_RWKV), lambda bi, t: (0, 0))
    unit = lambda rows, cols, dt: pltpu.VMEM((nchunks, RWKV_QUADS, rows, cols), dt)
    return pl.pallas_call(
        _rwkv_kernel,
        grid=(bsz, seqlen // RWKV_TB),
        in_specs=[tok] * 6 + [gate_spec] + [vec] * 3,
        out_specs=tok,
        out_shape=jax.ShapeDtypeStruct(r.shape, BF16),
        scratch_shapes=[
            pltpu.VMEM((RWKV_QUADS, LANES, 2 * LANES), F32),
            unit(4 * RWKV_CHUNK, LANES, BF16),
            unit(RWKV_CHUNK, 2 * LANES, F32),
            unit(2 * LANES, 2 * RWKV_CHUNK, BF16),
            unit(RWKV_CHUNK, 2 * LANES, BF16),
            unit(RWKV_CHUNK, 2 * LANES, F32),
            unit(LANES, 2 * LANES, F32),
            unit(RWKV_CHUNK, 2 * LANES, BF16),
        ],
        compiler_params=pltpu.CompilerParams(
            dimension_semantics=("arbitrary", "arbitrary"), vmem_limit_bytes=VMEM_LIMIT),
        name="rwkv7",
    )(r, lw, k, v, kkn, b, gate, r_k, ln_w, ln_b)


def _out_kernel(y5_ref, gate_ref, yr_ref, x_ref, gluw_ref, glub_ref, wout_ref, fg_ref, o_ref):
    def sub_tile(s):
        rs = slice(IN_SUB * s, IN_SUB * (s + 1))
        y = y5_ref[0, rs, :]
        y = 0.5 * y * (1.0 + jnp.tanh(math.sqrt(2.0 / math.pi) * (y + 0.044715 * (y * y * y))))
        yield
        glu = _sigmoid(_dot(y.astype(BF16), gluw_ref[...]) + glub_ref[...])
        ys = y * glu * gate_ref[0, rs, :]
        ycat = jnp.concatenate([ys.astype(BF16), yr_ref[0, rs, :]], axis=-1)
        yield
        xn = x_ref[0, rs, :] + _dot(ycat, wout_ref[...])
        yield
        ms = jnp.mean(xn * xn, axis=-1, keepdims=True)
        o_ref[0, rs, :] = xn * lax.rsqrt(ms + NORM_EPS) * fg_ref[...]

    _alternate(*[_delayed(sub_tile(s), s) for s in range(TM_OUT // IN_SUB)])


def _outproj(y5, gate, yr, x, glu_w, glu_b, w_out, final_g):
    bsz, seqlen, _ = x.shape
    tok = lambda w: pl.BlockSpec((1, TM_OUT, w), lambda b, t: (b, t, 0))
    full = lambda a: pl.BlockSpec(a.shape, lambda b, t: (0,) * a.ndim)
    return pl.pallas_call(
        _out_kernel,
        grid=(bsz, seqlen // TM_OUT),
        in_specs=[tok(D_S5), tok(D_S5), tok(D_RWKV), tok(D_MODEL),
                  full(glu_w), full(glu_b), full(w_out), full(final_g)],
        out_specs=tok(D_MODEL),
        out_shape=jax.ShapeDtypeStruct(x.shape, F32),
        compiler_params=pltpu.CompilerParams(
            dimension_semantics=("arbitrary", "arbitrary"), vmem_limit_bytes=VMEM_LIMIT),
        name="outproj",
    )(y5, gate, yr, x, glu_w, glu_b, w_out, final_g)


@jax.jit
def _forward(x, norm_g, w_in, s5_lam_re, s5_lam_im, s5_log_dt, s5_b_re, s5_b_im, s5_c_re, s5_c_im,
             s5_d, s5_glu_w, s5_glu_b, rwkv_mu, rwkv_w0, rwkv_w2, rwkv_a0, rwkv_a2, rwkv_k_k, rwkv_k_a,
             rwkv_r_k, rwkv_ln_w, rwkv_ln_b, w_out, final_g):
    assert w_in.shape[0] == 1, "single-layer block"
    for l in range(1):
        zeros = jnp.zeros((LOW_RANK, D_RWKV), F32)
        w2a2 = jnp.concatenate(
            [jnp.concatenate([rwkv_w2[l], zeros], axis=1), jnp.concatenate([zeros, rwkv_a2[l]], axis=1)],
            axis=0).astype(BF16)
        row = lambda a: a.reshape(1, -1).astype(F32)
        u, gate, r, lw, k, v, kkn, b = _inproj(
            x, row(norm_g[l]), w_in[l].astype(BF16), row(rwkv_mu[l]), w2a2, row(rwkv_w0[l]),
            row(rwkv_a0[l]), row(rwkv_k_k[l]), row(rwkv_k_a[l]))
        ein, toep, fout, lam16 = _s5_operators(
            s5_lam_re[l], s5_lam_im[l], s5_log_dt[l], s5_b_re[l], s5_b_im[l], s5_c_re[l], s5_c_im[l])
        y5 = _s5(u, ein, toep, fout, lam16, row(s5_d[l]))
        yr = _rwkv(r, lw, k, v, kkn, b, gate, row(rwkv_r_k[l]), row(rwkv_ln_w[l]), row(rwkv_ln_b[l]))
        x = _outproj(y5, gate, yr, x, s5_glu_w[l].astype(BF16), row(s5_glu_b[l]),
                     w_out[l].astype(BF16), row(final_g))
    return x


def kernel(x, norm_g, w_in, s5_lam_re, s5_lam_im, s5_log_dt, s5_b_re, s5_b_im, s5_c_re, s5_c_im, s5_d, s5_glu_w, s5_glu_b, rwkv_mu, rwkv_w0, rwkv_w2, rwkv_a0, rwkv_a2, rwkv_k_k, rwkv_k_a, rwkv_r_k, rwkv_ln_w, rwkv_ln_b, w_out, final_g):
    return _forward(x, norm_g, w_in, s5_lam_re, s5_lam_im, s5_log_dt, s5_b_re, s5_b_im, s5_c_re, s5_c_im,
                    s5_d, s5_glu_w, s5_glu_b, rwkv_mu, rwkv_w0, rwkv_w2, rwkv_a0, rwkv_a2, rwkv_k_k,
                    rwkv_k_a, rwkv_r_k, rwkv_ln_w, rwkv_ln_b, w_out, final_g)
```

```python
import math

import jax
import jax.numpy as jnp
from jax import lax
from jax.experimental import pallas as pl
from jax.experimental.pallas import tpu as pltpu

F32 = jnp.float32
BF16 = jnp.bfloat16

D_MODEL = 1024
D_S5 = 512
D_RWKV = 512
S5_GROUP = 16
S5_GROUPS = 32
S5_STATE = 64
RWKV_HEAD = 64
LOW_RANK = 64
D_SHIFT = 3 * D_RWKV + 2 * LOW_RANK
D_IN = 2 * D_S5 + D_SHIFT + D_RWKV
NORM_EPS = 1e-6
GN_EPS = 64e-5

LANES = 128
TM_IN = 1024
IN_SUB = 128
SUB_TILE_SKEW = 2
TM_OUT = 1024
S5_CHUNK = 16
S5_BATCH = 2
S5_PIPE_SKEW = 6
S5_LANE_BLOCKS = D_S5 // LANES
S5_GROUPS_PER_BLOCK = LANES // S5_GROUP
S5_BLOCK_STATE = 2 * S5_GROUPS_PER_BLOCK * S5_STATE
RWKV_CHUNK = 64
RWKV_TB = 1024
RWKV_PREP_CHUNKS = 4
RWKV_OUT_CHUNKS = 2
HEAD_PAIRS = D_RWKV // LANES
RWKV_QUADS = HEAD_PAIRS // 2
VMEM_LIMIT = 56 * 1024 * 1024


def _dot(a, b):
    return jnp.dot(a, b, preferred_element_type=F32)


def _dot_nt(a, b):
    return lax.dot_general(a, b, (((1,), (1,)), ((), ())), preferred_element_type=F32)


def _dot_tn(a, b):
    return lax.dot_general(a, b, (((0,), (0,)), ((), ())), preferred_element_type=F32)


def _split2(x):
    hi = x.astype(BF16)
    return hi, (x - hi.astype(F32)).astype(BF16)


def _dot_exact_rhs(x, m):
    hi, lo = _split2(x)
    out = _dot(jnp.concatenate([hi, lo], axis=0), m)
    return out[:x.shape[0]] + out[x.shape[0]:]


def _dot_exact_lhs(m, x):
    hi, lo = _split2(x)
    return _dot(m, hi) + _dot(m, lo)


def _sigmoid(x):
    return 0.5 + 0.5 * jnp.tanh(0.5 * x)


def _head_ones():
    r = lax.broadcasted_iota(jnp.int32, (LANES, LANES), 0)
    c = lax.broadcasted_iota(jnp.int32, (LANES, LANES), 1)
    return (r < RWKV_HEAD) == (c < RWKV_HEAD)


def _alternating(*stage_generators):
    pending = list(stage_generators)
    while pending:
        for g in list(pending):
            if next(g, StopIteration) is StopIteration:
                pending.remove(g)
            else:
                yield


def _alternate(*stage_generators):
    for _ in _alternating(*stage_generators):
        pass


def _delayed(stage_generator, stages):
    for _ in range(stages):
        yield
    yield from stage_generator


def _inproj_kernel(x_ref, g_ref, win_ref, mu_ref, w2a2_ref, w0_ref, a0_ref, kk_ref, ka_ref,
                   u_ref, gate_ref, r_ref, lw_ref, k_ref, v_ref, kkn_ref, b_ref, carry_ref):
    t = pl.program_id(1)

    @pl.when(t == 0)
    def _():
        carry_ref[...] = jnp.zeros_like(carry_ref)

    ones = _head_ones().astype(BF16)
    carries = {-1: carry_ref[...]}

    def sub_tile(s):
        rs = slice(IN_SUB * s, IN_SUB * (s + 1))
        x = x_ref[0, rs, :]
        ms = jnp.mean(x * x, axis=-1, keepdims=True)
        h = (x * lax.rsqrt(ms + NORM_EPS) * g_ref[...]).astype(BF16)

        def proj(lo, hi):
            return _dot(h, win_ref[:, lo:hi])

        yield
        u_ref[0, rs, :] = proj(0, D_S5)
        yield
        z5 = proj(D_S5, 2 * D_S5)
        gate_ref[0, rs, :D_S5] = (z5 * _sigmoid(z5)).astype(BF16)
        yield
        zr = proj(2 * D_S5 + D_SHIFT, D_IN)
        gate_ref[0, rs, D_S5:] = (zr * _sigmoid(zr)).astype(BF16)
        yield
        rw = proj(2 * D_S5, 2 * D_S5 + D_SHIFT)
        carries[s] = rw[IN_SUB - 1:IN_SUB, :]
        yield
        prev = pltpu.roll(rw, 1, axis=0)
        row = lax.broadcasted_iota(jnp.int32, rw.shape, 0)
        prev = jnp.where(row == 0, carries[s - 1], prev)
        sh = rw + (prev - rw) * mu_ref[...]
        r = sh[:, 0:D_RWKV]
        k = sh[:, D_RWKV:2 * D_RWKV]
        v = sh[:, 2 * D_RWKV:3 * D_RWKV]
        wa = sh[:, 3 * D_RWKV:]
        lane = lax.broadcasted_iota(jnp.int32, wa.shape, 1)
        wa = jnp.where(lane < LOW_RANK, jnp.tanh(wa), wa)
        za = _dot(wa.astype(BF16), w2a2_ref[...])
        r_ref[0, rs, :] = r.astype(BF16)
        v_ref[0, rs, :] = v.astype(BF16)
        yield
        lw_ref[0, rs, :] = -math.exp(-0.5) * _sigmoid(w0_ref[...] + za[:, :D_RWKV])
        iclr = _sigmoid(a0_ref[...] + za[:, D_RWKV:])
        k_ref[0, rs, :] = (k * (1.0 + (iclr - 1.0) * ka_ref[...])).astype(BF16)
        kk = k * kk_ref[...]
        sq = kk * kk
        ss = jnp.concatenate(
            [_dot_exact_rhs(sq[:, LANES * p:LANES * (p + 1)], ones) for p in range(HEAD_PAIRS)], axis=1)
        yield
        kkn = kk * lax.rsqrt(jnp.maximum(ss, 1e-24))
        kkn_ref[0, rs, :] = kkn.astype(BF16)
        b_ref[0, rs, :] = (kkn * iclr).astype(BF16)

    n_sub = TM_IN // IN_SUB
    _alternate(*[_delayed(sub_tile(s), SUB_TILE_SKEW * s) for s in range(n_sub)])
    carry_ref[...] = carries[n_sub - 1]


def _inproj(x, norm_g, w_in, mu, w2a2, w0, a0, k_k, k_a):
    bsz, seqlen, _ = x.shape
    tok = lambda w: pl.BlockSpec((1, TM_IN, w), lambda b, t: (b, t, 0))
    full = lambda a: pl.BlockSpec(a.shape, lambda b, t: (0,) * a.ndim)
    out = lambda w, dt: jax.ShapeDtypeStruct((bsz, seqlen, w), dt)
    params = (norm_g, w_in, mu, w2a2, w0, a0, k_k, k_a)
    return pl.pallas_call(
        _inproj_kernel,
        grid=(bsz, seqlen // TM_IN),
        in_specs=[tok(D_MODEL)] + [full(a) for a in params],
        out_specs=[tok(D_S5), tok(2 * D_S5)] + [tok(D_RWKV)] * 6,
        out_shape=[out(D_S5, F32), out(2 * D_S5, BF16)]
        + [out(D_RWKV, F32 if i == 1 else BF16) for i in range(6)],
        scratch_shapes=[pltpu.VMEM((1, D_SHIFT), F32)],
        compiler_params=pltpu.CompilerParams(
            dimension_semantics=("arbitrary", "arbitrary"), vmem_limit_bytes=VMEM_LIMIT),
        name="inproj",
    )(x, *params)


def _s5_kernel(u_ref, ein_ref, toep_ref, fout_ref, lam_ref, d_ref, y_ref, e_ref, sp_ref):
    nch = e_ref.shape[1]
    half = S5_BLOCK_STATE // 2
    pairs = S5_CHUNK // 2
    lre = lam_ref[0, :, :half]
    lim = lam_ref[0, :, half:]
    d = d_ref[...]

    def pipeline(bi):
        xs = [u_ref[bi, pl.ds(j, nch, stride=S5_CHUNK), :] for j in range(S5_CHUNK)]
        xcat = jnp.concatenate([x.astype(BF16) for x in xs], axis=1)
        yield
        e_ref[bi] = _dot(xcat, ein_ref[0])
        yield
        intra = {}

        def scan():
            sre = sim = jnp.zeros((1, half), F32)
            for c in range(nch):
                sp_ref[bi, c:c + 1, :half] = sre
                sp_ref[bi, c:c + 1, half:] = sim
                e = e_ref[bi, c:c + 1, :]
                sre, sim = lre * sre - lim * sim + e[:, :half], lre * sim + lim * sre + e[:, half:]
                if (c + 1) % (nch // pairs) == 0:
                    yield

        def toeplitz():
            for tp in range(pairs):
                intra[tp] = _dot(xcat[:, :2 * LANES * (tp + 1)],
                                 toep_ref[0, 2 * LANES * (pairs - 1 - tp):, :])
                yield

        yield from _alternating(scan(), toeplitz())
        sp = sp_ref[bi].astype(BF16)
        for tp in range(pairs):
            y = intra[tp] + _dot(sp, fout_ref[0, :, 2 * LANES * tp:2 * LANES * (tp + 1)])
            for i in range(2):
                t = 2 * tp + i
                y_ref[bi, pl.ds(t, nch, stride=S5_CHUNK), :] = y[:, LANES * i:LANES * (i + 1)] + d * xs[t]
            yield

    _alternate(*[_delayed(pipeline(bi), S5_PIPE_SKEW * bi) for bi in range(S5_BATCH)])


def _cmul(ar, ai, br, bi):
    return ar * br - ai * bi, ar * bi + ai * br


def _block_diag_expand(compact, width, inner, transposed):
    k = compact.shape[1] if transposed else compact.shape[2]
    group_w = S5_GROUPS_PER_BLOCK * inner
    kk = jnp.arange(k)
    ww = jnp.arange(width)
    sel = ((kk[:, None] // inner == ww[None, :] // group_w)
           & (kk[:, None] % inner == ww[None, :] % inner)).astype(F32)
    wide_group = (ww % group_w) // inner
    if transposed:
        full = jnp.matmul(sel.T, compact)
        narrow_group = (jnp.arange(compact.shape[2]) % LANES) // S5_GROUP
        keep = wide_group[:, None] == narrow_group[None, :]
    else:
        full = jnp.matmul(compact, sel)
        narrow_group = (jnp.arange(compact.shape[1]) % LANES) // S5_GROUP
        keep = narrow_group[:, None] == wide_group[None, :]
    return jnp.where(keep, full, 0.0)


def _s5_operators(lam_re, lam_im, log_dt, b_re, b_im, c_re, c_im):
    c16, q4, gl = S5_CHUNK, S5_LANE_BLOCKS, S5_GROUPS_PER_BLOCK
    rows = c16 * LANES
    dt = jnp.exp(log_dt)[:, None]
    n = jnp.arange(c16 + 1, dtype=F32)[:, None, None]
    mag = jnp.exp(n * (lam_re * dt)[None])
    ang = n * (lam_im * dt)[None]
    pw_r, pw_i = mag * jnp.cos(ang), mag * jnp.sin(ang)
    den = lam_re * lam_re + lam_im * lam_im
    q_r, q_i = _cmul(pw_r[1] - 1.0, pw_i[1], lam_re / den, -lam_im / den)
    bt_r, bt_i = jnp.swapaxes(b_re, 1, 2), jnp.swapaxes(b_im, 1, 2)
    bb_r, bb_i = _cmul(q_r[:, None, :], q_i[:, None, :], bt_r, bt_i)
    e_r, e_i = _cmul(pw_r[:c16, :, None, :], pw_i[:c16, :, None, :], bb_r[None], bb_i[None])

    def by_block(a):
        a = a.reshape(a.shape[0], q4, gl * S5_GROUP, 2 * S5_STATE)
        return jnp.swapaxes(a, 0, 1).reshape(q4, -1, 2 * S5_STATE)

    ein = by_block(jnp.concatenate([e_r, e_i], axis=-1)[::-1])
    ein = _block_diag_expand(ein, S5_BLOCK_STATE, S5_STATE, False)

    f_r, f_i = _cmul(c_re[None], c_im[None], pw_r[1:, :, None, :], pw_i[1:, :, None, :])
    fo = jnp.swapaxes(by_block(jnp.concatenate([f_r, -f_i], axis=-1)), 1, 2)
    fo = _block_diag_expand(fo, S5_BLOCK_STATE, S5_STATE, True)

    ee = jnp.concatenate([e_r, -e_i], axis=-1)
    ee = jnp.concatenate([ee, jnp.zeros_like(ee[:1])], axis=0)
    cc = jnp.concatenate([c_re, c_im], axis=-1).reshape(q4, LANES, 2 * S5_STATE)
    dr = c16 // 2 - 1 - jnp.arange(c16 // 2)
    row_g = (jnp.arange(rows) % LANES) // S5_GROUP
    col_g = jnp.arange(LANES) // S5_GROUP
    halves = []
    for t2 in range(2):
        lag = 2 * dr[:, None] + t2 - jnp.arange(2)[None, :]
        a = ee[jnp.where(lag >= 0, lag, c16)].reshape(c16, S5_GROUPS, S5_GROUP, 2 * S5_STATE)
        full = jnp.einsum('qrk,qck->qrc', by_block(a), cc, precision=lax.Precision.HIGHEST)
        halves.append(jnp.where(row_g[:, None] == col_g[None, :], full, 0.0))
    toep = jnp.concatenate(halves, axis=-1)

    lam16 = jnp.concatenate([pw_r[c16].reshape(q4, 1, gl * S5_STATE),
                             pw_i[c16].reshape(q4, 1, gl * S5_STATE)], axis=-1)
    return ein.astype(BF16), toep.astype(BF16), fo.astype(BF16), lam16


def _s5(u, ein, toep, fout, lam16, d):
    bsz, seqlen, _ = u.shape
    nch = seqlen // S5_CHUNK
    per_q = lambda a: pl.BlockSpec((1,) + a.shape[1:], lambda q, b: (q,) + (0,) * (a.ndim - 1))
    seq = pl.BlockSpec((S5_BATCH, seqlen, LANES), lambda q, b: (b, 0, q))
    state = pltpu.VMEM((S5_BATCH, nch, S5_BLOCK_STATE), F32)
    return pl.pallas_call(
        _s5_kernel,
        grid=(S5_LANE_BLOCKS, bsz // S5_BATCH),
        in_specs=[seq, per_q(ein), per_q(toep), per_q(fout), per_q(lam16),
                  pl.BlockSpec((1, LANES), lambda q, b: (0, q))],
        out_specs=seq,
        out_shape=jax.ShapeDtypeStruct(u.shape, F32),
        scratch_shapes=[state, state],
        compiler_params=pltpu.CompilerParams(
            dimension_semantics=("arbitrary", "arbitrary"), vmem_limit_bytes=VMEM_LIMIT),
        name="s5",
    )(u, ein, toep, fout, lam16, d)


def _stack_heads(x, lo):
    return jnp.concatenate([jnp.where(lo, x, 0.0), jnp.where(lo, 0.0, x)], axis=0)


def _head_block_diag(x, head_id, order):
    return jnp.concatenate([jnp.where(head_id == h, x, 0.0) for h in order], axis=0)


def _rwkv_kernel(r_ref, lw_ref, k_ref, v_ref, kk_ref, b_ref, gate_ref, rk_ref, lnw_ref, lnb_ref,
                 y_ref, s_ref, wr_ref, w2_ref, bkt_ref, prb_ref, yv_ref, e2_ref, u_ref):
    tc = RWKV_CHUNK
    nchunks = RWKV_TB // tc
    qw = 2 * LANES

    @pl.when(pl.program_id(1) == 0)
    def _():
        s_ref[...] = jnp.zeros_like(s_ref)

    row = lax.broadcasted_iota(jnp.int32, (tc, qw), 0)
    lane = lax.broadcasted_iota(jnp.int32, (tc, qw), 1)
    col = jnp.bitwise_and(lane, RWKV_HEAD - 1)
    head_id = lane // RWKV_HEAD
    heads = tuple(range(qw // RWKV_HEAD))
    heads_swapped = (1, 0, 3, 2)
    eye_q = (row == col).astype(F32)
    row_p = lax.broadcasted_iota(jnp.int32, (tc, LANES), 0)
    lane_p = lax.broadcasted_iota(jnp.int32, (tc, LANES), 1)
    col_p = jnp.bitwise_and(lane_p, RWKV_HEAD - 1)
    lo_p = lane_p < RWKV_HEAD
    lo_p2 = lax.broadcasted_iota(jnp.int32, (2 * tc, LANES), 1) < RWKV_HEAD
    stril = row_p > col_p
    tril = row_p >= col_p
    state_bd = _head_ones()
    r4 = lax.broadcasted_iota(jnp.int32, (qw, qw), 0)
    l4 = lax.broadcasted_iota(jnp.int32, (qw, qw), 1)
    ones = ((r4 // RWKV_HEAD) == (l4 // RWKV_HEAD)).astype(BF16)
    tri = (lax.broadcasted_iota(jnp.int32, (tc, tc), 0)
           >= lax.broadcasted_iota(jnp.int32, (tc, tc), 1)).astype(BF16)
    bf = lambda t: t.astype(BF16)
    up, dn = slice(0, tc), slice(tc, 2 * tc)

    def prepare(chunks):
        units = []
        for c in chunks:
            rows = slice(c * tc, (c + 1) * tc)
            lw_all = lw_ref[0, rows, :]
            g_all = _dot_exact_lhs(tri, lw_all)
            for q in range(RWKV_QUADS):
                qs = slice(qw * q, qw * (q + 1))
                units.append(dict(c=c, q=q, rows=rows, qs=qs, lw=lw_all[:, qs], g=g_all[:, qs]))
        yield
        for d in units:
            rows, qs, g = d["rows"], d["qs"], d["g"]
            m = 0.5 * g[tc - 1:tc, :]
            e_out = jnp.exp(m - g)
            em = jnp.exp(m)
            rt = r_ref[0, rows, qs] * jnp.exp(g - m)
            bt = b_ref[0, rows, qs] * e_out
            kt = k_ref[0, rows, qs] * e_out
            d["em"] = em
            d["at"] = bf(-kk_ref[0, rows, qs] * jnp.exp(g - d["lw"] - m))
            d["rt"], d["bt"], d["kt"] = bf(rt), bf(bt), bf(kt)
            d["rts"] = bf(rt * em)
            d["vsw"] = bf(_head_block_diag(v_ref[0, rows, qs], head_id, heads_swapped))
            bkt, e2 = [], []
            for j in range(2):
                ps = slice(LANES * j, LANES * (j + 1))
                bks = jnp.concatenate([bt[:, ps], kt[:, ps]], axis=0) * em[:, ps]
                bkt.append(bf(jnp.transpose(bks)))
                e2.append(jnp.transpose(jnp.broadcast_to(em[:, ps] * em[:, ps], (LANES, LANES))))
            bkt_ref[d["c"], d["q"]] = jnp.concatenate(bkt, axis=0)
            e2_ref[d["c"], d["q"]] = jnp.concatenate(e2, axis=1)
        yield
        for d in units:
            d["o"] = []
            for j in range(2):
                ps = slice(LANES * j, LANES * (j + 1))
                bk = jnp.concatenate([d["bt"][:, ps], d["kt"][:, ps]], axis=0)
                kb = jnp.concatenate([d["kt"][:, ps], d["bt"][:, ps]], axis=0)
                rhs = jnp.concatenate([jnp.where(lo_p2, bk, 0.0), jnp.where(lo_p2, 0.0, kb)], axis=0)
                lhs = jnp.concatenate([d["at"][:, ps], d["rt"][:, ps]], axis=0)
                d["o"].append(_dot_nt(lhs, rhs))
        yield
        for d in units:
            h0 = [o[:, :LANES] for o in d["o"]]
            h1 = [o[:, LANES:] for o in d["o"]]

            def pick(rws, first, second, mask):
                return jnp.concatenate(
                    [jnp.where(mask, jnp.where(lo_p, first[j][rws], second[j][rws]), 0.0) for j in range(2)],
                    axis=1)

            d["ap"] = pick(up, h0, h1, stril)
            d["p_ak"] = bf(pick(up, h1, h0, stril))
            p_rb = bf(pick(dn, h0, h1, tril))
            d["p_rk"] = bf(pick(dn, h1, h0, tril))
            prb_ref[d["c"], d["q"]] = p_rb
            d["tinv"] = eye_q + d["ap"]
        yield
        levels = int(math.log2(tc)) - 1
        for d in units:
            d["ap"] = _dot(bf(d["ap"]), bf(_head_block_diag(d["ap"], head_id, heads)))
        for lvl in range(levels):
            yield
            for d in units:
                rhs = bf(_head_block_diag(d["ap"], head_id, heads))
                if lvl < levels - 1:
                    out = _dot(bf(jnp.concatenate([d["tinv"], d["ap"]], axis=0)), rhs)
                    d["tinv"], d["ap"] = d["tinv"] + out[:tc], out[tc:]
                else:
                    d["tinv"] = d["tinv"] + _dot(bf(d["tinv"]), rhs)
        yield
        for d in units:
            d["tinv"] = bf(d["tinv"])
            out = _dot(jnp.concatenate([d["p_ak"], d["p_rk"]], axis=0), d["vsw"])
            d["xv"] = out[:tc]
            yv_ref[d["c"], d["q"]] = out[tc:]
        yield
        for d in units:
            w1s, w2 = [], []
            for j in range(2):
                ps = slice(LANES * j, LANES * (j + 1))
                rhs = jnp.concatenate([_stack_heads(d["at"][:, ps], lo_p),
                                       bf(_stack_heads(d["xv"][:, ps], lo_p))], axis=1)
                w = _dot(d["tinv"][:, ps], rhs)
                w1s.append(bf(w[:, :LANES] * d["em"][:, ps]))
                w2.append(w[:, LANES:])
            wr_ref[d["c"], d["q"]] = jnp.concatenate(w1s + [d["rts"][:, :LANES], d["rts"][:, LANES:]], axis=0)
            w2_ref[d["c"], d["q"]] = jnp.concatenate(w2, axis=1)

    def recur_step(c):
        rows = slice(c * tc, (c + 1) * tc)
        units = [dict(q=q, qs=slice(qw * q, qw * (q + 1))) for q in range(RWKV_QUADS)]
        for d in units:
            d["h0"] = s_ref[d["q"]]
        for d in units:
            out = _dot(wr_ref[c, d["q"]], bf(d["h0"]))
            d["u"] = (jnp.concatenate([out[:tc, :LANES], out[tc:2 * tc, LANES:]], axis=1)
                      + w2_ref[c, d["q"]])
            d["yr"] = jnp.concatenate([out[2 * tc:3 * tc, :LANES], out[3 * tc:, LANES:]], axis=1)
        yield
        for d in units:
            q = d["q"]
            uv = jnp.concatenate([bf(d["u"]), v_ref[0, rows, d["qs"]]], axis=0)
            upd = _dot(bkt_ref[c, q], uv)
            upd = jnp.concatenate([jnp.where(state_bd, upd[:LANES, :LANES], 0.0),
                                   jnp.where(state_bd, upd[LANES:, LANES:], 0.0)], axis=1)
            s_ref[q] = d["h0"] * e2_ref[c, q] + upd
            u_ref[c, q] = bf(d["u"])
            yv_ref[c, q] = yv_ref[c, q] + d["yr"]

    def emit_step(c):
        rows = slice(c * tc, (c + 1) * tc)
        units = [dict(q=q, qs=slice(qw * q, qw * (q + 1))) for q in range(RWKV_QUADS)]
        for d in units:
            q = d["q"]
            d["y"] = yv_ref[c, q] + _dot(prb_ref[c, q], _head_block_diag(u_ref[c, q], head_id, heads))
        yield
        for d in units:
            qs, y = d["qs"], d["y"]
            rkr = r_ref[0, rows, qs].astype(F32) * k_ref[0, rows, qs] * rk_ref[:, qs]
            d["sums"] = _dot_exact_rhs(jnp.concatenate([y, rkr], axis=0), ones)
        yield
        for d in units:
            d["dev"] = d["y"] - d["sums"][:tc] * (1.0 / RWKV_HEAD)
            d["var"] = _dot_exact_rhs(d["dev"] * d["dev"], ones) * (1.0 / RWKV_HEAD)
        for d in units:
            qs = d["qs"]
            yn = d["dev"] * lax.rsqrt(d["var"] + GN_EPS) * lnw_ref[:, qs] + lnb_ref[:, qs]
            y_ref[0, rows, qs] = bf((yn + d["sums"][tc:] * v_ref[0, rows, qs]) * gate_ref[0, rows, qs])

    def chain(chunks):
        for c in chunks:
            yield from _alternating(*([emit_step(c - 1)] if c > 0 else []), recur_step(c))

    groups = [list(range(g, g + RWKV_PREP_CHUNKS)) for g in range(0, nchunks, RWKV_PREP_CHUNKS)]
    _alternate(prepare(groups[0]))
    for prev, nxt in zip(groups[:-1], groups[1:]):
        _alternate(prepare(nxt), chain(prev))
    _alternate(chain(groups[-1]))
    _alternate(emit_step(nchunks - 1))


def _rwkv(r, lw, k, v, kkn, b, gate, r_k, ln_w, ln_b):
    bsz, seqlen, _ = r.shape
    nchunks = RWKV_TB // RWKV_CHUNK
    tok = pl.BlockSpec((1, RWKV_TB, D_RWKV), lambda bi, t: (bi, t, 0))
    gate_spec = pl.BlockSpec((1, RWKV_TB, D_RWKV), lambda bi, t: (bi, t, 1))
    vec = pl.BlockSpec((1, D_RWKV), lambda bi, t: (0, 0))
    unit = lambda rows, cols, dt: pltpu.VMEM((nchunks, RWKV_QUADS, rows, cols), dt)
    return pl.pallas_call(
        _rwkv_kernel,
        grid=(bsz, seqlen // RWKV_TB),
        in_specs=[tok] * 6 + [gate_spec] + [vec] * 3,
        out_specs=tok,
        out_shape=jax.ShapeDtypeStruct(r.shape, BF16),
        scratch_shapes=[
            pltpu.VMEM((RWKV_QUADS, LANES, 2 * LANES), F32),
            unit(4 * RWKV_CHUNK, LANES, BF16),
            unit(RWKV_CHUNK, 2 * LANES, F32),
            unit(2 * LANES, 2 * RWKV_CHUNK, BF16),
            unit(RWKV_CHUNK, 2 * LANES, BF16),
            unit(RWKV_CHUNK, 2 * LANES, F32),
            unit(LANES, 2 * LANES, F32),
            unit(RWKV_CHUNK, 2 * LANES, BF16),
        ],
        compiler_params=pltpu.CompilerParams(
            dimension_semantics=("arbitrary", "arbitrary"), vmem_limit_bytes=VMEM_LIMIT),
        name="rwkv7",
    )(r, lw, k, v, kkn, b, gate, r_k, ln_w, ln_b)


def _out_kernel(y5_ref, gate_ref, yr_ref, x_ref, gluw_ref, glub_ref, wout_ref, fg_ref, o_ref):
    def sub_tile(s):
        rs = slice(IN_SUB * s, IN_SUB * (s + 1))
        y = y5_ref[0, rs, :]
        y = 0.5 * y * (1.0 + jnp.tanh(math.sqrt(2.0 / math.pi) * (y + 0.044715 * (y * y * y))))
        yield
        glu = _sigmoid(_dot(y.astype(BF16), gluw_ref[...]) + glub_ref[...])
        ys = y * glu * gate_ref[0, rs, :]
        ycat = jnp.concatenate([ys.astype(BF16), yr_ref[0, rs, :]], axis=-1)
        yield
        xn = x_ref[0, rs, :] + _dot(ycat, wout_ref[...])
        yield
        ms = jnp.mean(xn * xn, axis=-1, keepdims=True)
        o_ref[0, rs, :] = xn * lax.rsqrt(ms + NORM_EPS) * fg_ref[...]

    _alternate(*[_delayed(sub_tile(s), s) for s in range(TM_OUT // IN_SUB)])


def _outproj(y5, gate, yr, x, glu_w, glu_b, w_out, final_g):
    bsz, seqlen, _ = x.shape
    tok = lambda w: pl.BlockSpec((1, TM_OUT, w), lambda b, t: (b, t, 0))
    full = lambda a: pl.BlockSpec(a.shape, lambda b, t: (0,) * a.ndim)
    return pl.pallas_call(
        _out_kernel,
        grid=(bsz, seqlen // TM_OUT),
        in_specs=[tok(D_S5), tok(D_S5), tok(D_RWKV), tok(D_MODEL),
                  full(glu_w), full(glu_b), full(w_out), full(final_g)],
        out_specs=tok(D_MODEL),
        out_shape=jax.ShapeDtypeStruct(x.shape, F32),
        compiler_params=pltpu.CompilerParams(
            dimension_semantics=("arbitrary", "arbitrary"), vmem_limit_bytes=VMEM_LIMIT),
        name="outproj",
    )(y5, gate, yr, x, glu_w, glu_b, w_out, final_g)


@jax.jit
def _forward(x, norm_g, w_in, s5_lam_re, s5_lam_im, s5_log_dt, s5_b_re, s5_b_im, s5_c_re, s5_c_im,
             s5_d, s5_glu_w, s5_glu_b, rwkv_mu, rwkv_w0, rwkv_w2, rwkv_a0, rwkv_a2, rwkv_k_k, rwkv_k_a,
             rwkv_r_k, rwkv_ln_w, rwkv_ln_b, w_out, final_g):
    assert w_in.shape[0] == 1, "single-layer block"
    for l in range(1):
        zeros = jnp.zeros((LOW_RANK, D_RWKV), F32)
        w2a2 = jnp.concatenate(
            [jnp.concatenate([rwkv_w2[l], zeros], axis=1), jnp.concatenate([zeros, rwkv_a2[l]], axis=1)],
            axis=0).astype(BF16)
        row = lambda a: a.reshape(1, -1).astype(F32)
        u, gate, r, lw, k, v, kkn, b = _inproj(
            x, row(norm_g[l]), w_in[l].astype(BF16), row(rwkv_mu[l]), w2a2, row(rwkv_w0[l]),
            row(rwkv_a0[l]), row(rwkv_k_k[l]), row(rwkv_k_a[l]))
        ein, toep, fout, lam16 = _s5_operators(
            s5_lam_re[l], s5_lam_im[l], s5_log_dt[l], s5_b_re[l], s5_b_im[l], s5_c_re[l], s5_c_im[l])
        y5 = _s5(u, ein, toep, fout, lam16, row(s5_d[l]))
        yr = _rwkv(r, lw, k, v, kkn, b, gate, row(rwkv_r_k[l]), row(rwkv_ln_w[l]), row(rwkv_ln_b[l]))
        x = _outproj(y5, gate, yr, x, s5_glu_w[l].astype(BF16), row(s5_glu_b[l]),
                     w_out[l].astype(BF16), row(final_g))
    return x


def kernel(x, norm_g, w_in, s5_lam_re, s5_lam_im, s5_log_dt, s5_b_re, s5_b_im, s5_c_re, s5_c_im, s5_d, s5_glu_w, s5_glu_b, rwkv_mu, rwkv_w0, rwkv_w2, rwkv_a0, rwkv_a2, rwkv_k_k, rwkv_k_a, rwkv_r_k, rwkv_ln_w, rwkv_ln_b, w_out, final_g):
    return _forward(x, norm_g, w_in, s5_lam_re, s5_lam_im, s5_log_dt, s5_b_re, s5_b_im, s5_c_re, s5_c_im,
                    s5_d, s5_glu_w, s5_glu_b, rwkv_mu, rwkv_w0, rwkv_w2, rwkv_a0, rwkv_a2, rwkv_k_k,
                    rwkv_k_a, rwkv_r_k, rwkv_ln_w, rwkv_ln_b, w_out, final_g)
```

```python
import math

import jax
import jax.numpy as jnp
from jax import lax
from jax.experimental import pallas as pl
from jax.experimental.pallas import tpu as pltpu

F32 = jnp.float32
BF16 = jnp.bfloat16

D_MODEL = 1024
D_S5 = 512
D_RWKV = 512
S5_GROUP = 16
S5_GROUPS = 32
S5_STATE = 64
RWKV_HEAD = 64
LOW_RANK = 64
D_SHIFT = 3 * D_RWKV + 2 * LOW_RANK
D_IN = 2 * D_S5 + D_SHIFT + D_RWKV
NORM_EPS = 1e-6
GN_EPS = 64e-5

LANES = 128
TM_IN = 1024
IN_SUB = 256
SUB_TILE_SKEW = 2
TM_OUT = 1024
OUT_SUB = 512
S5_CHUNK = 16
S5_BATCH = 2
S5_PIPE_SKEW = 6
S5_LANE_BLOCKS = D_S5 // LANES
S5_GROUPS_PER_BLOCK = LANES // S5_GROUP
S5_BLOCK_STATE = 2 * S5_GROUPS_PER_BLOCK * S5_STATE
RWKV_CHUNK = 64
RWKV_TB = 1024
RWKV_PREP_CHUNKS = 4
RWKV_OUT_CHUNKS = 2
HEAD_PAIRS = D_RWKV // LANES
RWKV_QUADS = HEAD_PAIRS // 2
VMEM_LIMIT = 56 * 1024 * 1024


def _dot(a, b):
    return jnp.dot(a, b, preferred_element_type=F32)


def _dot_nt(a, b):
    return lax.dot_general(a, b, (((1,), (1,)), ((), ())), preferred_element_type=F32)


def _dot_tn(a, b):
    return lax.dot_general(a, b, (((0,), (0,)), ((), ())), preferred_element_type=F32)


def _split2(x):
    hi = x.astype(BF16)
    return hi, (x - hi.astype(F32)).astype(BF16)


def _head_sums(x, ones):
    return _dot(x.astype(BF16), ones)


def _dot_exact_lhs(m, x):
    hi, lo = _split2(x)
    return _dot(m, hi) + _dot(m, lo)


def _sigmoid(x):
    return 0.5 + 0.5 * jnp.tanh(0.5 * x)


def _head_ones(width):
    r = lax.broadcasted_iota(jnp.int32, (width, width), 0)
    c = lax.broadcasted_iota(jnp.int32, (width, width), 1)
    return (r // RWKV_HEAD) == (c // RWKV_HEAD)


def _alternating(*stage_generators):
    pending = list(stage_generators)
    while pending:
        for g in list(pending):
            if next(g, StopIteration) is StopIteration:
                pending.remove(g)
            else:
                yield


def _alternate(*stage_generators):
    for _ in _alternating(*stage_generators):
        pass


def _delayed(stage_generator, stages):
    for _ in range(stages):
        yield
    yield from stage_generator


def _inproj_kernel(x_ref, g_ref, win_ref, mu_ref, w2a2_ref, w0_ref, a0_ref, kk_ref, ka_ref,
                   u_ref, gate_ref, r_ref, lw_ref, k_ref, v_ref, kkn_ref, b_ref, carry_ref):
    t = pl.program_id(1)

    @pl.when(t == 0)
    def _():
        carry_ref[...] = jnp.zeros_like(carry_ref)

    ones = _head_ones(2 * LANES).astype(BF16)
    carries = {-1: carry_ref[...]}

    def sub_tile(s):
        rs = slice(IN_SUB * s, IN_SUB * (s + 1))
        x = x_ref[0, rs, :]
        ms = jnp.mean(x * x, axis=-1, keepdims=True)
        h = (x * lax.rsqrt(ms + NORM_EPS) * g_ref[...]).astype(BF16)

        def proj(lo, hi):
            return _dot(h, win_ref[:, lo:hi])

        yield
        u_ref[0, rs, :] = proj(0, D_S5)
        yield
        z5 = proj(D_S5, 2 * D_S5)
        gate_ref[0, rs, :D_S5] = (z5 * _sigmoid(z5)).astype(BF16)
        yield
        zr = proj(2 * D_S5 + D_SHIFT, D_IN)
        gate_ref[0, rs, D_S5:] = (zr * _sigmoid(zr)).astype(BF16)
        yield
        rw = proj(2 * D_S5, 2 * D_S5 + D_SHIFT)
        carries[s] = rw[IN_SUB - 1:IN_SUB, :]
        yield
        prev = pltpu.roll(rw, 1, axis=0)
        row = lax.broadcasted_iota(jnp.int32, rw.shape, 0)
        prev = jnp.where(row == 0, carries[s - 1], prev)
        sh = rw + (prev - rw) * mu_ref[...]
        r = sh[:, 0:D_RWKV]
        k = sh[:, D_RWKV:2 * D_RWKV]
        v = sh[:, 2 * D_RWKV:3 * D_RWKV]
        wa = sh[:, 3 * D_RWKV:]
        lane = lax.broadcasted_iota(jnp.int32, wa.shape, 1)
        wa = jnp.where(lane < LOW_RANK, jnp.tanh(wa), wa)
        za = _dot(wa.astype(BF16), w2a2_ref[...])
        r_ref[0, rs, :] = r.astype(BF16)
        v_ref[0, rs, :] = v.astype(BF16)
        yield
        lw_ref[0, rs, :] = -math.exp(-0.5) * _sigmoid(w0_ref[...] + za[:, :D_RWKV])
        iclr = _sigmoid(a0_ref[...] + za[:, D_RWKV:])
        k_ref[0, rs, :] = (k * (1.0 + (iclr - 1.0) * ka_ref[...])).astype(BF16)
        kk = k * kk_ref[...]
        sq = kk * kk
        ss = jnp.concatenate(
            [_head_sums(sq[:, 2 * LANES * q:2 * LANES * (q + 1)], ones) for q in range(RWKV_QUADS)], axis=1)
        yield
        kkn = kk * lax.rsqrt(jnp.maximum(ss, 1e-24))
        kkn_ref[0, rs, :] = kkn.astype(BF16)
        b_ref[0, rs, :] = (kkn * iclr).astype(BF16)

    n_sub = TM_IN // IN_SUB
    _alternate(*[_delayed(sub_tile(s), SUB_TILE_SKEW * s) for s in range(n_sub)])
    carry_ref[...] = carries[n_sub - 1]


def _inproj(x, norm_g, w_in, mu, w2a2, w0, a0, k_k, k_a):
    bsz, seqlen, _ = x.shape
    tok = lambda w: pl.BlockSpec((1, TM_IN, w), lambda b, t: (b, t, 0))
    full = lambda a: pl.BlockSpec(a.shape, lambda b, t: (0,) * a.ndim)
    out = lambda w, dt: jax.ShapeDtypeStruct((bsz, seqlen, w), dt)
    params = (norm_g, w_in, mu, w2a2, w0, a0, k_k, k_a)
    return pl.pallas_call(
        _inproj_kernel,
        grid=(bsz, seqlen // TM_IN),
        in_specs=[tok(D_MODEL)] + [full(a) for a in params],
        out_specs=[tok(D_S5), tok(2 * D_S5)] + [tok(D_RWKV)] * 6,
        out_shape=[out(D_S5, F32), out(2 * D_S5, BF16)]
        + [out(D_RWKV, F32 if i == 1 else BF16) for i in range(6)],
        scratch_shapes=[pltpu.VMEM((1, D_SHIFT), F32)],
        compiler_params=pltpu.CompilerParams(
            dimension_semantics=("arbitrary", "arbitrary"), vmem_limit_bytes=VMEM_LIMIT),
        name="inproj",
    )(x, *params)


def _s5_kernel(u_ref, ein_ref, toep_ref, fout_ref, lam_ref, d_ref, y_ref, e_ref, sp_ref):
    nch = e_ref.shape[1]
    half = S5_BLOCK_STATE // 2
    pairs = S5_CHUNK // 2
    lre = lam_ref[0, :, :half]
    lim = lam_ref[0, :, half:]
    d = d_ref[...]

    def pipeline(bi):
        xs = [u_ref[bi, pl.ds(j, nch, stride=S5_CHUNK), :] for j in range(S5_CHUNK)]
        xcat = jnp.concatenate([x.astype(BF16) for x in xs], axis=1)
        yield
        e_ref[bi] = _dot(xcat, ein_ref[0])
        yield
        intra = {}

        def scan():
            sre = sim = jnp.zeros((1, half), F32)
            for c in range(nch):
                sp_ref[bi, c:c + 1, :half] = sre
                sp_ref[bi, c:c + 1, half:] = sim
                e = e_ref[bi, c:c + 1, :]
                sre, sim = lre * sre - lim * sim + e[:, :half], lre * sim + lim * sre + e[:, half:]
                if (c + 1) % (nch // pairs) == 0:
                    yield

        def toeplitz():
            for tp in range(pairs):
                intra[tp] = _dot(xcat[:, :2 * LANES * (tp + 1)],
                                 toep_ref[0, 2 * LANES * (pairs - 1 - tp):, :])
                yield

        yield from _alternating(scan(), toeplitz())
        sp = sp_ref[bi].astype(BF16)
        for tp in range(pairs):
            y = intra[tp] + _dot(sp, fout_ref[0, :, 2 * LANES * tp:2 * LANES * (tp + 1)])
            for i in range(2):
                t = 2 * tp + i
                y_ref[bi, pl.ds(t, nch, stride=S5_CHUNK), :] = y[:, LANES * i:LANES * (i + 1)] + d * xs[t]
            yield

    _alternate(*[_delayed(pipeline(bi), S5_PIPE_SKEW * bi) for bi in range(S5_BATCH)])


def _cmul(ar, ai, br, bi):
    return ar * br - ai * bi, ar * bi + ai * br


def _block_diag_expand(compact, width, inner, transposed):
    k = compact.shape[1] if transposed else compact.shape[2]
    group_w = S5_GROUPS_PER_BLOCK * inner
    kk = jnp.arange(k)
    ww = jnp.arange(width)
    sel = ((kk[:, None] // inner == ww[None, :] // group_w)
           & (kk[:, None] % inner == ww[None, :] % inner)).astype(F32)
    wide_group = (ww % group_w) // inner
    if transposed:
        full = jnp.matmul(sel.T, compact)
        narrow_group = (jnp.arange(compact.shape[2]) % LANES) // S5_GROUP
        keep = wide_group[:, None] == narrow_group[None, :]
    else:
        full = jnp.matmul(compact, sel)
        narrow_group = (jnp.arange(compact.shape[1]) % LANES) // S5_GROUP
        keep = narrow_group[:, None] == wide_group[None, :]
    return jnp.where(keep, full, 0.0)


def _s5_operators(lam_re, lam_im, log_dt, b_re, b_im, c_re, c_im):
    c16, q4, gl = S5_CHUNK, S5_LANE_BLOCKS, S5_GROUPS_PER_BLOCK
    rows = c16 * LANES
    dt = jnp.exp(log_dt)[:, None]
    n = jnp.arange(c16 + 1, dtype=F32)[:, None, None]
    mag = jnp.exp(n * (lam_re * dt)[None])
    ang = n * (lam_im * dt)[None]
    pw_r, pw_i = mag * jnp.cos(ang), mag * jnp.sin(ang)
    den = lam_re * lam_re + lam_im * lam_im
    q_r, q_i = _cmul(pw_r[1] - 1.0, pw_i[1], lam_re / den, -lam_im / den)
    bt_r, bt_i = jnp.swapaxes(b_re, 1, 2), jnp.swapaxes(b_im, 1, 2)
    bb_r, bb_i = _cmul(q_r[:, None, :], q_i[:, None, :], bt_r, bt_i)
    e_r, e_i = _cmul(pw_r[:c16, :, None, :], pw_i[:c16, :, None, :], bb_r[None], bb_i[None])

    def by_block(a):
        a = a.reshape(a.shape[0], q4, gl * S5_GROUP, 2 * S5_STATE)
        return jnp.swapaxes(a, 0, 1).reshape(q4, -1, 2 * S5_STATE)

    ein = by_block(jnp.concatenate([e_r, e_i], axis=-1)[::-1])
    ein = _block_diag_expand(ein, S5_BLOCK_STATE, S5_STATE, False)

    f_r, f_i = _cmul(c_re[None], c_im[None], pw_r[1:, :, None, :], pw_i[1:, :, None, :])
    fo = jnp.swapaxes(by_block(jnp.concatenate([f_r, -f_i], axis=-1)), 1, 2)
    fo = _block_diag_expand(fo, S5_BLOCK_STATE, S5_STATE, True)

    ee = jnp.concatenate([e_r, -e_i], axis=-1)
    ee = jnp.concatenate([ee, jnp.zeros_like(ee[:1])], axis=0)
    cc = jnp.concatenate([c_re, c_im], axis=-1).reshape(q4, LANES, 2 * S5_STATE)
    dr = c16 // 2 - 1 - jnp.arange(c16 // 2)
    row_g = (jnp.arange(rows) % LANES) // S5_GROUP
    col_g = jnp.arange(LANES) // S5_GROUP
    halves = []
    for t2 in range(2):
        lag = 2 * dr[:, None] + t2 - jnp.arange(2)[None, :]
        a = ee[jnp.where(lag >= 0, lag, c16)].reshape(c16, S5_GROUPS, S5_GROUP, 2 * S5_STATE)
        full = jnp.einsum('qrk,qck->qrc', by_block(a), cc, precision=lax.Precision.HIGHEST)
        halves.append(jnp.where(row_g[:, None] == col_g[None, :], full, 0.0))
    toep = jnp.concatenate(halves, axis=-1)

    lam16 = jnp.concatenate([pw_r[c16].reshape(q4, 1, gl * S5_STATE),
                             pw_i[c16].reshape(q4, 1, gl * S5_STATE)], axis=-1)
    return ein.astype(BF16), toep.astype(BF16), fo.astype(BF16), lam16


def _s5(u, ein, toep, fout, lam16, d):
    bsz, seqlen, _ = u.shape
    nch = seqlen // S5_CHUNK
    per_q = lambda a: pl.BlockSpec((1,) + a.shape[1:], lambda q, b: (q,) + (0,) * (a.ndim - 1))
    seq = pl.BlockSpec((S5_BATCH, seqlen, LANES), lambda q, b: (b, 0, q))
    state = pltpu.VMEM((S5_BATCH, nch, S5_BLOCK_STATE), F32)
    return pl.pallas_call(
        _s5_kernel,
        grid=(S5_LANE_BLOCKS, bsz // S5_BATCH),
        in_specs=[seq, per_q(ein), per_q(toep), per_q(fout), per_q(lam16),
                  pl.BlockSpec((1, LANES), lambda q, b: (0, q))],
        out_specs=seq,
        out_shape=jax.ShapeDtypeStruct(u.shape, F32),
        scratch_shapes=[state, state],
        compiler_params=pltpu.CompilerParams(
            dimension_semantics=("arbitrary", "arbitrary"), vmem_limit_bytes=VMEM_LIMIT),
        name="s5",
    )(u, ein, toep, fout, lam16, d)


def _stack_heads(x, lo):
    return jnp.concatenate([jnp.where(lo, x, 0.0), jnp.where(lo, 0.0, x)], axis=0)


def _head_block_diag(x, head_id, order):
    return jnp.concatenate([jnp.where(head_id == h, x, 0.0) for h in order], axis=0)


def _rwkv_kernel(r_ref, lw_ref, k_ref, v_ref, kk_ref, b_ref, gate_ref, rk_ref, lnw_ref, lnb_ref,
                 y_ref, s_ref, wr_ref, w2_ref, bkt_ref, prb_ref, yv_ref, e2_ref, u_ref):
    tc = RWKV_CHUNK
    nchunks = RWKV_TB // tc
    qw = 2 * LANES

    @pl.when(pl.program_id(1) == 0)
    def _():
        s_ref[...] = jnp.zeros_like(s_ref)

    row = lax.broadcasted_iota(jnp.int32, (tc, qw), 0)
    lane = lax.broadcasted_iota(jnp.int32, (tc, qw), 1)
    col = jnp.bitwise_and(lane, RWKV_HEAD - 1)
    head_id = lane // RWKV_HEAD
    heads = tuple(range(qw // RWKV_HEAD))
    heads_swapped = (1, 0, 3, 2)
    eye_q = (row == col).astype(F32)
    row_p = lax.broadcasted_iota(jnp.int32, (tc, LANES), 0)
    lane_p = lax.broadcasted_iota(jnp.int32, (tc, LANES), 1)
    col_p = jnp.bitwise_and(lane_p, RWKV_HEAD - 1)
    lo_p = lane_p < RWKV_HEAD
    lo_p2 = lax.broadcasted_iota(jnp.int32, (2 * tc, LANES), 1) < RWKV_HEAD
    stril = row_p > col_p
    tril = row_p >= col_p
    state_bd = _head_ones(LANES)
    ones = _head_ones(qw).astype(BF16)
    tri = (lax.broadcasted_iota(jnp.int32, (tc, tc), 0)
           >= lax.broadcasted_iota(jnp.int32, (tc, tc), 1)).astype(BF16)
    bf = lambda t: t.astype(BF16)
    up, dn = slice(0, tc), slice(tc, 2 * tc)

    def prepare(chunks):
        units = []
        for c in chunks:
            rows = slice(c * tc, (c + 1) * tc)
            lw_all = lw_ref[0, rows, :]
            g_all = _dot_exact_lhs(tri, lw_all)
            for q in range(RWKV_QUADS):
                qs = slice(qw * q, qw * (q + 1))
                units.append(dict(c=c, q=q, rows=rows, qs=qs, lw=lw_all[:, qs], g=g_all[:, qs]))
        yield
        for d in units:
            rows, qs, g = d["rows"], d["qs"], d["g"]
            m = 0.5 * g[tc - 1:tc, :]
            e_out = jnp.exp(m - g)
            em = jnp.exp(m)
            rt = r_ref[0, rows, qs] * jnp.exp(g - m)
            bt = b_ref[0, rows, qs] * e_out
            kt = k_ref[0, rows, qs] * e_out
            d["em"] = em
            d["at"] = bf(-kk_ref[0, rows, qs] * jnp.exp(g - d["lw"] - m))
            d["rt"], d["bt"], d["kt"] = bf(rt), bf(bt), bf(kt)
            d["rts"] = bf(rt * em)
            d["vsw"] = bf(_head_block_diag(v_ref[0, rows, qs], head_id, heads_swapped))
            bkt, e2 = [], []
            for j in range(2):
                ps = slice(LANES * j, LANES * (j + 1))
                bks = jnp.concatenate([bt[:, ps], kt[:, ps]], axis=0) * em[:, ps]
                bkt.append(bf(jnp.transpose(bks)))
                e2.append(jnp.transpose(jnp.broadcast_to(em[:, ps] * em[:, ps], (LANES, LANES))))
            bkt_ref[d["c"], d["q"]] = jnp.concatenate(bkt, axis=0)
            e2_ref[d["c"], d["q"]] = jnp.concatenate(e2, axis=1)
        yield
        for d in units:
            d["o"] = []
            for j in range(2):
                ps = slice(LANES * j, LANES * (j + 1))
                bk = jnp.concatenate([d["bt"][:, ps], d["kt"][:, ps]], axis=0)
                kb = jnp.concatenate([d["kt"][:, ps], d["bt"][:, ps]], axis=0)
                rhs = jnp.concatenate([jnp.where(lo_p2, bk, 0.0), jnp.where(lo_p2, 0.0, kb)], axis=0)
                lhs = jnp.concatenate([d["at"][:, ps], d["rt"][:, ps]], axis=0)
                d["o"].append(_dot_nt(lhs, rhs))
        yield
        for d in units:
            h0 = [o[:, :LANES] for o in d["o"]]
            h1 = [o[:, LANES:] for o in d["o"]]

            def pick(rws, first, second, mask):
                return jnp.concatenate(
                    [jnp.where(mask, jnp.where(lo_p, first[j][rws], second[j][rws]), 0.0) for j in range(2)],
                    axis=1)

            d["ap"] = pick(up, h0, h1, stril)
            d["p_ak"] = bf(pick(up, h1, h0, stril))
            p_rb = bf(pick(dn, h0, h1, tril))
            d["p_rk"] = bf(pick(dn, h1, h0, tril))
            prb_ref[d["c"], d["q"]] = p_rb
            d["tinv"] = eye_q + d["ap"]
        yield
        levels = int(math.log2(tc)) - 1
        for d in units:
            d["ap"] = _dot(bf(d["ap"]), bf(_head_block_diag(d["ap"], head_id, heads)))
        for lvl in range(levels):
            yield
            for d in units:
                rhs = bf(_head_block_diag(d["ap"], head_id, heads))
                if lvl < levels - 1:
                    out = _dot(bf(jnp.concatenate([d["tinv"], d["ap"]], axis=0)), rhs)
                    d["tinv"], d["ap"] = d["tinv"] + out[:tc], out[tc:]
                else:
                    d["tinv"] = d["tinv"] + _dot(bf(d["tinv"]), rhs)
        yield
        for d in units:
            d["tinv"] = bf(d["tinv"])
            out = _dot(jnp.concatenate([d["p_ak"], d["p_rk"]], axis=0), d["vsw"])
            d["xv"] = out[:tc]
            yv_ref[d["c"], d["q"]] = out[tc:]
        yield
        for d in units:
            w1s, w2 = [], []
            for j in range(2):
                ps = slice(LANES * j, LANES * (j + 1))
                rhs = jnp.concatenate([_stack_heads(d["at"][:, ps], lo_p),
                                       bf(_stack_heads(d["xv"][:, ps], lo_p))], axis=1)
                w = _dot(d["tinv"][:, ps], rhs)
                w1s.append(bf(w[:, :LANES] * d["em"][:, ps]))
                w2.append(w[:, LANES:])
            wr_ref[d["c"], d["q"]] = jnp.concatenate(w1s + [d["rts"][:, :LANES], d["rts"][:, LANES:]], axis=0)
            w2_ref[d["c"], d["q"]] = jnp.concatenate(w2, axis=1)

    def recur_step(c):
        rows = slice(c * tc, (c + 1) * tc)
        units = [dict(q=q, qs=slice(qw * q, qw * (q + 1))) for q in range(RWKV_QUADS)]
        for d in units:
            d["h0"] = s_ref[d["q"]]
        for d in units:
            out = _dot(wr_ref[c, d["q"]], bf(d["h0"]))
            d["u"] = (jnp.concatenate([out[:tc, :LANES], out[tc:2 * tc, LANES:]], axis=1)
                      + w2_ref[c, d["q"]])
            d["yr"] = jnp.concatenate([out[2 * tc:3 * tc, :LANES], out[3 * tc:, LANES:]], axis=1)
        yield
        for d in units:
            q = d["q"]
            uv = jnp.concatenate([bf(d["u"]), v_ref[0, rows, d["qs"]]], axis=0)
            upd = _dot(bkt_ref[c, q], uv)
            upd = jnp.concatenate([jnp.where(state_bd, upd[:LANES, :LANES], 0.0),
                                   jnp.where(state_bd, upd[LANES:, LANES:], 0.0)], axis=1)
            s_ref[q] = d["h0"] * e2_ref[c, q] + upd
            u_ref[c, q] = bf(d["u"])
            yv_ref[c, q] = yv_ref[c, q] + d["yr"]

    def emit_step(c):
        rows = slice(c * tc, (c + 1) * tc)
        units = [dict(q=q, qs=slice(qw * q, qw * (q + 1))) for q in range(RWKV_QUADS)]
        for d in units:
            q = d["q"]
            d["y"] = yv_ref[c, q] + _dot(prb_ref[c, q], _head_block_diag(u_ref[c, q], head_id, heads))
        yield
        for d in units:
            qs, y = d["qs"], d["y"]
            rkr = r_ref[0, rows, qs].astype(F32) * k_ref[0, rows, qs] * rk_ref[:, qs]
            d["sums"] = _head_sums(jnp.concatenate([y, rkr], axis=0), ones)
        yield
        for d in units:
            d["dev"] = d["y"] - d["sums"][:tc] * (1.0 / RWKV_HEAD)
            d["var"] = _head_sums(d["dev"] * d["dev"], ones) * (1.0 / RWKV_HEAD)
        for d in units:
            qs = d["qs"]
            yn = d["dev"] * lax.rsqrt(d["var"] + GN_EPS) * lnw_ref[:, qs] + lnb_ref[:, qs]
            y_ref[0, rows, qs] = bf((yn + d["sums"][tc:] * v_ref[0, rows, qs]) * gate_ref[0, rows, qs])

    def chain(chunks):
        for c in chunks:
            yield from _alternating(*([emit_step(c - 1)] if c > 0 else []), recur_step(c))

    groups = [list(range(g, g + RWKV_PREP_CHUNKS)) for g in range(0, nchunks, RWKV_PREP_CHUNKS)]
    _alternate(prepare(groups[0]))
    for prev, nxt in zip(groups[:-1], groups[1:]):
        _alternate(prepare(nxt), chain(prev))
    _alternate(chain(groups[-1]))
    _alternate(emit_step(nchunks - 1))


def _rwkv(r, lw, k, v, kkn, b, gate, r_k, ln_w, ln_b):
    bsz, seqlen, _ = r.shape
    nchunks = RWKV_TB // RWKV_CHUNK
    tok = pl.BlockSpec((1, RWKV_TB, D_RWKV), lambda bi, t: (bi, t, 0))
    gate_spec = pl.BlockSpec((1, RWKV_TB, D_RWKV), lambda bi, t: (bi, t, 1))
    vec = pl.BlockSpec((1, D_RWKV), lambda bi, t: (0, 0))
    unit = lambda rows, cols, dt: pltpu.VMEM((nchunks, RWKV_QUADS, rows, cols), dt)
    return pl.pallas_call(
        _rwkv_kernel,
        grid=(bsz, seqlen // RWKV_TB),
        in_specs=[tok] * 6 + [gate_spec] + [vec] * 3,
        out_specs=tok,
        out_shape=jax.ShapeDtypeStruct(r.shape, BF16),
        scratch_shapes=[
            pltpu.VMEM((RWKV_QUADS, LANES, 2 * LANES), F32),
            unit(4 * RWKV_CHUNK, LANES, BF16),
            unit(RWKV_CHUNK, 2 * LANES, F32),
            unit(2 * LANES, 2 * RWKV_CHUNK, BF16),
            unit(RWKV_CHUNK, 2 * LANES, BF16),
            unit(RWKV_CHUNK, 2 * LANES, F32),
            unit(LANES, 2 * LANES, F32),
            unit(RWKV_CHUNK, 2 * LANES, BF16),
        ],
        compiler_params=pltpu.CompilerParams(
            dimension_semantics=("arbitrary", "arbitrary"), vmem_limit_bytes=VMEM_LIMIT),
        name="rwkv7",
    )(r, lw, k, v, kkn, b, gate, r_k, ln_w, ln_b)


def _out_kernel(y5_ref, gate_ref, yr_ref, x_ref, gluw_ref, glub_ref, wout_ref, fg_ref, o_ref):
    def sub_tile(s):
        rs = slice(OUT_SUB * s, OUT_SUB * (s + 1))
        y = y5_ref[0, rs, :]
        y = 0.5 * y * (1.0 + jnp.tanh(math.sqrt(2.0 / math.pi) * (y + 0.044715 * (y * y * y))))
        yield
        glu = _sigmoid(_dot(y.astype(BF16), gluw_ref[...]) + glub_ref[...])
        ys = y * glu * gate_ref[0, rs, :]
        ycat = jnp.concatenate([ys.astype(BF16), yr_ref[0, rs, :]], axis=-1)
        yield
        xn = x_ref[0, rs, :] + _dot(ycat, wout_ref[...])
        yield
        ms = jnp.mean(xn * xn, axis=-1, keepdims=True)
        o_ref[0, rs, :] = xn * lax.rsqrt(ms + NORM_EPS) * fg_ref[...]

    _alternate(*[_delayed(sub_tile(s), SUB_TILE_SKEW * s) for s in range(TM_OUT // OUT_SUB)])


def _outproj(y5, gate, yr, x, glu_w, glu_b, w_out, final_g):
    bsz, seqlen, _ = x.shape
    tok = lambda w: pl.BlockSpec((1, TM_OUT, w), lambda b, t: (b, t, 0))
    full = lambda a: pl.BlockSpec(a.shape, lambda b, t: (0,) * a.ndim)
    return pl.pallas_call(
        _out_kernel,
        grid=(bsz, seqlen // TM_OUT),
        in_specs=[tok(D_S5), tok(D_S5), tok(D_RWKV), tok(D_MODEL),
                  full(glu_w), full(glu_b), full(w_out), full(final_g)],
        out_specs=tok(D_MODEL),
        out_shape=jax.ShapeDtypeStruct(x.shape, F32),
        compiler_params=pltpu.CompilerParams(
            dimension_semantics=("arbitrary", "arbitrary"), vmem_limit_bytes=VMEM_LIMIT),
        name="outproj",
    )(y5, gate, yr, x, glu_w, glu_b, w_out, final_g)


@jax.jit
def _forward(x, norm_g, w_in, s5_lam_re, s5_lam_im, s5_log_dt, s5_b_re, s5_b_im, s5_c_re, s5_c_im,
             s5_d, s5_glu_w, s5_glu_b, rwkv_mu, rwkv_w0, rwkv_w2, rwkv_a0, rwkv_a2, rwkv_k_k, rwkv_k_a,
             rwkv_r_k, rwkv_ln_w, rwkv_ln_b, w_out, final_g):
    assert w_in.shape[0] == 1, "single-layer block"
    for l in range(1):
        zeros = jnp.zeros((LOW_RANK, D_RWKV), F32)
        w2a2 = jnp.concatenate(
            [jnp.concatenate([rwkv_w2[l], zeros], axis=1), jnp.concatenate([zeros, rwkv_a2[l]], axis=1)],
            axis=0).astype(BF16)
        row = lambda a: a.reshape(1, -1).astype(F32)
        u, gate, r, lw, k, v, kkn, b = _inproj(
            x, row(norm_g[l]), w_in[l].astype(BF16), row(rwkv_mu[l]), w2a2, row(rwkv_w0[l]),
            row(rwkv_a0[l]), row(rwkv_k_k[l]), row(rwkv_k_a[l]))
        ein, toep, fout, lam16 = _s5_operators(
            s5_lam_re[l], s5_lam_im[l], s5_log_dt[l], s5_b_re[l], s5_b_im[l], s5_c_re[l], s5_c_im[l])
        y5 = _s5(u, ein, toep, fout, lam16, row(s5_d[l]))
        yr = _rwkv(r, lw, k, v, kkn, b, gate, row(rwkv_r_k[l]), row(rwkv_ln_w[l]), row(rwkv_ln_b[l]))
        x = _outproj(y5, gate, yr, x, s5_glu_w[l].astype(BF16), row(s5_glu_b[l]),
                     w_out[l].astype(BF16), row(final_g))
    return x


def kernel(x, norm_g, w_in, s5_lam_re, s5_lam_im, s5_log_dt, s5_b_re, s5_b_im, s5_c_re, s5_c_im, s5_d, s5_glu_w, s5_glu_b, rwkv_mu, rwkv_w0, rwkv_w2, rwkv_a0, rwkv_a2, rwkv_k_k, rwkv_k_a, rwkv_r_k, rwkv_ln_w, rwkv_ln_b, w_out, final_g):
    return _forward(x, norm_g, w_in, s5_lam_re, s5_lam_im, s5_log_dt, s5_b_re, s5_b_im, s5_c_re, s5_c_im,
                    s5_d, s5_glu_w, s5_glu_b, rwkv_mu, rwkv_w0, rwkv_w2, rwkv_a0, rwkv_a2, rwkv_k_k,
                    rwkv_k_a, rwkv_r_k, rwkv_ln_w, rwkv_ln_b, w_out, final_g)
```

```python
import math

import jax
import jax.numpy as jnp
from jax import lax
from jax.experimental import pallas as pl
from jax.experimental.pallas import tpu as pltpu

F32 = jnp.float32
BF16 = jnp.bfloat16

D_MODEL = 1024
D_S5 = 512
D_RWKV = 512
S5_GROUP = 16
S5_GROUPS = 32
S5_STATE = 64
RWKV_HEAD = 64
LOW_RANK = 64
D_SHIFT = 3 * D_RWKV + 2 * LOW_RANK
D_IN = 2 * D_S5 + D_SHIFT + D_RWKV
NORM_EPS = 1e-6
GN_EPS = 64e-5

LANES = 128
TM_IN = 1024
IN_SUB = 256
SUB_TILE_SKEW = 2
TM_OUT = 1024
OUT_SUB = 512
S5_CHUNK = 16
S5_BATCH = 2
S5_PIPE_SKEW = 6
S5_LANE_BLOCKS = D_S5 // LANES
S5_GROUPS_PER_BLOCK = LANES // S5_GROUP
S5_BLOCK_STATE = 2 * S5_GROUPS_PER_BLOCK * S5_STATE
RWKV_CHUNK = 64
RWKV_TB = 1024
RWKV_PREP_CHUNKS = 4
RWKV_OUT_CHUNKS = 2
HEAD_PAIRS = D_RWKV // LANES
RWKV_QUADS = HEAD_PAIRS // 2
VMEM_LIMIT = 56 * 1024 * 1024


def _dot(a, b):
    return jnp.dot(a, b, preferred_element_type=F32)


def _dot_nt(a, b):
    return lax.dot_general(a, b, (((1,), (1,)), ((), ())), preferred_element_type=F32)


def _dot_tn(a, b):
    return lax.dot_general(a, b, (((0,), (0,)), ((), ())), preferred_element_type=F32)


def _split2(x):
    hi = x.astype(BF16)
    return hi, (x - hi.astype(F32)).astype(BF16)


def _head_sums(x, ones):
    return _dot(x.astype(BF16), ones)


def _dot_exact_lhs(m, x):
    hi, lo = _split2(x)
    return _dot(m, hi) + _dot(m, lo)


def _sigmoid(x):
    return 0.5 + 0.5 * jnp.tanh(0.5 * x)


def _head_ones(width):
    r = lax.broadcasted_iota(jnp.int32, (width, width), 0)
    c = lax.broadcasted_iota(jnp.int32, (width, width), 1)
    return (r // RWKV_HEAD) == (c // RWKV_HEAD)


def _alternating(*stage_generators):
    pending = list(stage_generators)
    while pending:
        for g in list(pending):
            if next(g, StopIteration) is StopIteration:
                pending.remove(g)
            else:
                yield


def _alternate(*stage_generators):
    for _ in _alternating(*stage_generators):
        pass


def _delayed(stage_generator, stages):
    for _ in range(stages):
        yield
    yield from stage_generator


def _inproj_kernel(x_ref, g_ref, win_ref, mu_ref, w2a2_ref, w0_ref, a0_ref, kk_ref, ka_ref,
                   u_ref, gate_ref, r_ref, lw_ref, k_ref, v_ref, kkn_ref, b_ref, carry_ref):
    t = pl.program_id(1)

    @pl.when(t == 0)
    def _():
        carry_ref[...] = jnp.zeros_like(carry_ref)

    ones = _head_ones(2 * LANES).astype(BF16)
    carries = {-1: carry_ref[...]}

    def sub_tile(s):
        rs = slice(IN_SUB * s, IN_SUB * (s + 1))
        x = x_ref[0, rs, :]
        ms = jnp.mean(x * x, axis=-1, keepdims=True)
        h = (x * lax.rsqrt(ms + NORM_EPS) * g_ref[...]).astype(BF16)

        def proj(lo, hi):
            return _dot(h, win_ref[:, lo:hi])

        yield
        u_ref[0, rs, :] = proj(0, D_S5)
        yield
        z5 = proj(D_S5, 2 * D_S5)
        gate_ref[0, rs, :D_S5] = (z5 * _sigmoid(z5)).astype(BF16)
        yield
        zr = proj(2 * D_S5 + D_SHIFT, D_IN)
        gate_ref[0, rs, D_S5:] = (zr * _sigmoid(zr)).astype(BF16)
        yield
        rw = proj(2 * D_S5, 2 * D_S5 + D_SHIFT)
        carries[s] = rw[IN_SUB - 1:IN_SUB, :]
        yield
        prev = pltpu.roll(rw, 1, axis=0)
        row = lax.broadcasted_iota(jnp.int32, rw.shape, 0)
        prev = jnp.where(row == 0, carries[s - 1], prev)
        sh = rw + (prev - rw) * mu_ref[...]
        r = sh[:, 0:D_RWKV]
        k = sh[:, D_RWKV:2 * D_RWKV]
        v = sh[:, 2 * D_RWKV:3 * D_RWKV]
        wa = sh[:, 3 * D_RWKV:]
        lane = lax.broadcasted_iota(jnp.int32, wa.shape, 1)
        wa = jnp.where(lane < LOW_RANK, jnp.tanh(wa), wa)
        za = _dot(wa.astype(BF16), w2a2_ref[...])
        r_ref[0, rs, :] = r.astype(BF16)
        v_ref[0, rs, :] = v.astype(BF16)
        yield
        lw_ref[0, rs, :] = -math.exp(-0.5) * _sigmoid(w0_ref[...] + za[:, :D_RWKV])
        iclr = _sigmoid(a0_ref[...] + za[:, D_RWKV:])
        k_ref[0, rs, :] = (k * (1.0 + (iclr - 1.0) * ka_ref[...])).astype(BF16)
        kk = k * kk_ref[...]
        sq = kk * kk
        ss = jnp.concatenate(
            [_head_sums(sq[:, 2 * LANES * q:2 * LANES * (q + 1)], ones) for q in range(RWKV_QUADS)], axis=1)
        yield
        kkn = kk * lax.rsqrt(jnp.maximum(ss, 1e-24))
        kkn_ref[0, rs, :] = kkn.astype(BF16)
        b_ref[0, rs, :] = (kkn * iclr).astype(BF16)

    n_sub = TM_IN // IN_SUB
    _alternate(*[_delayed(sub_tile(s), SUB_TILE_SKEW * s) for s in range(n_sub)])
    carry_ref[...] = carries[n_sub - 1]


def _inproj(x, norm_g, w_in, mu, w2a2, w0, a0, k_k, k_a):
    bsz, seqlen, _ = x.shape
    tok = lambda w: pl.BlockSpec((1, TM_IN, w), lambda b, t: (b, t, 0))
    full = lambda a: pl.BlockSpec(a.shape, lambda b, t: (0,) * a.ndim)
    out = lambda w, dt: jax.ShapeDtypeStruct((bsz, seqlen, w), dt)
    params = (norm_g, w_in, mu, w2a2, w0, a0, k_k, k_a)
    return pl.pallas_call(
        _inproj_kernel,
        grid=(bsz, seqlen // TM_IN),
        in_specs=[tok(D_MODEL)] + [full(a) for a in params],
        out_specs=[tok(D_S5), tok(2 * D_S5)] + [tok(D_RWKV)] * 6,
        out_shape=[out(D_S5, F32), out(2 * D_S5, BF16)]
        + [out(D_RWKV, F32 if i == 1 else BF16) for i in range(6)],
        scratch_shapes=[pltpu.VMEM((1, D_SHIFT), F32)],
        compiler_params=pltpu.CompilerParams(
            dimension_semantics=("arbitrary", "arbitrary"), vmem_limit_bytes=VMEM_LIMIT),
        name="inproj",
    )(x, *params)


def _s5_kernel(u_ref, ein_ref, toep_ref, fout_ref, lam_ref, d_ref, y_ref, e_ref, sp_ref):
    nch = e_ref.shape[1]
    half = S5_BLOCK_STATE // 2
    pairs = S5_CHUNK // 2
    lre = lam_ref[0, :, :half]
    lim = lam_ref[0, :, half:]
    d = d_ref[...]

    def pipeline(bi):
        xs = [u_ref[bi, pl.ds(j, nch, stride=S5_CHUNK), :] for j in range(S5_CHUNK)]
        xcat = jnp.concatenate([x.astype(BF16) for x in xs], axis=1)
        yield
        e_ref[bi] = _dot(xcat, ein_ref[0])
        yield
        intra = {}

        def scan():
            sre = sim = jnp.zeros((1, half), F32)
            for c in range(nch):
                sp_ref[bi, c:c + 1, :half] = sre
                sp_ref[bi, c:c + 1, half:] = sim
                e = e_ref[bi, c:c + 1, :]
                sre, sim = lre * sre - lim * sim + e[:, :half], lre * sim + lim * sre + e[:, half:]
                if (c + 1) % (nch // pairs) == 0:
                    yield

        def toeplitz():
            for tp in range(pairs):
                intra[tp] = _dot(xcat[:, :2 * LANES * (tp + 1)],
                                 toep_ref[0, 2 * LANES * (pairs - 1 - tp):, :])
                yield

        yield from _alternating(scan(), toeplitz())
        sp = sp_ref[bi].astype(BF16)
        for tp in range(pairs):
            y = intra[tp] + _dot(sp, fout_ref[0, :, 2 * LANES * tp:2 * LANES * (tp + 1)])
            for i in range(2):
                t = 2 * tp + i
                y_ref[bi, pl.ds(t, nch, stride=S5_CHUNK), :] = y[:, LANES * i:LANES * (i + 1)] + d * xs[t]
            yield

    _alternate(*[_delayed(pipeline(bi), S5_PIPE_SKEW * bi) for bi in range(S5_BATCH)])


def _cmul(ar, ai, br, bi):
    return ar * br - ai * bi, ar * bi + ai * br


def _block_diag_expand(compact, width, inner, transposed):
    k = compact.shape[1] if transposed else compact.shape[2]
    group_w = S5_GROUPS_PER_BLOCK * inner
    kk = jnp.arange(k)
    ww = jnp.arange(width)
    sel = ((kk[:, None] // inner == ww[None, :] // group_w)
           & (kk[:, None] % inner == ww[None, :] % inner)).astype(F32)
    wide_group = (ww % group_w) // inner
    if transposed:
        full = jnp.matmul(sel.T, compact)
        narrow_group = (jnp.arange(compact.shape[2]) % LANES) // S5_GROUP
        keep = wide_group[:, None] == narrow_group[None, :]
    else:
        full = jnp.matmul(compact, sel)
        narrow_group = (jnp.arange(compact.shape[1]) % LANES) // S5_GROUP
        keep = narrow_group[:, None] == wide_group[None, :]
    return jnp.where(keep, full, 0.0)


def _s5_operators(lam_re, lam_im, log_dt, b_re, b_im, c_re, c_im):
    c16, q4, gl = S5_CHUNK, S5_LANE_BLOCKS, S5_GROUPS_PER_BLOCK
    rows = c16 * LANES
    dt = jnp.exp(log_dt)[:, None]
    n = jnp.arange(c16 + 1, dtype=F32)[:, None, None]
    mag = jnp.exp(n * (lam_re * dt)[None])
    ang = n * (lam_im * dt)[None]
    pw_r, pw_i = mag * jnp.cos(ang), mag * jnp.sin(ang)
    den = lam_re * lam_re + lam_im * lam_im
    q_r, q_i = _cmul(pw_r[1] - 1.0, pw_i[1], lam_re / den, -lam_im / den)
    bt_r, bt_i = jnp.swapaxes(b_re, 1, 2), jnp.swapaxes(b_im, 1, 2)
    bb_r, bb_i = _cmul(q_r[:, None, :], q_i[:, None, :], bt_r, bt_i)
    e_r, e_i = _cmul(pw_r[:c16, :, None, :], pw_i[:c16, :, None, :], bb_r[None], bb_i[None])

    def by_block(a):
        a = a.reshape(a.shape[0], q4, gl * S5_GROUP, 2 * S5_STATE)
        return jnp.swapaxes(a, 0, 1).reshape(q4, -1, 2 * S5_STATE)

    ein = by_block(jnp.concatenate([e_r, e_i], axis=-1)[::-1])
    ein = _block_diag_expand(ein, S5_BLOCK_STATE, S5_STATE, False)

    f_r, f_i = _cmul(c_re[None], c_im[None], pw_r[1:, :, None, :], pw_i[1:, :, None, :])
    fo = jnp.swapaxes(by_block(jnp.concatenate([f_r, -f_i], axis=-1)), 1, 2)
    fo = _block_diag_expand(fo, S5_BLOCK_STATE, S5_STATE, True)

    ee = jnp.concatenate([e_r, -e_i], axis=-1)
    ee = jnp.concatenate([ee, jnp.zeros_like(ee[:1])], axis=0)
    cc = jnp.concatenate([c_re, c_im], axis=-1).reshape(q4, LANES, 2 * S5_STATE)
    dr = c16 // 2 - 1 - jnp.arange(c16 // 2)
    row_g = (jnp.arange(rows) % LANES) // S5_GROUP
    col_g = jnp.arange(LANES) // S5_GROUP
    halves = []
    for t2 in range(2):
        lag = 2 * dr[:, None] + t2 - jnp.arange(2)[None, :]
        a = ee[jnp.where(lag >= 0, lag, c16)].reshape(c16, S5_GROUPS, S5_GROUP, 2 * S5_STATE)
        full = jnp.einsum('qrk,qck->qrc', by_block(a), cc, precision=lax.Precision.HIGHEST)
        halves.append(jnp.where(row_g[:, None] == col_g[None, :], full, 0.0))
    toep = jnp.concatenate(halves, axis=-1)

    lam16 = jnp.concatenate([pw_r[c16].reshape(q4, 1, gl * S5_STATE),
                             pw_i[c16].reshape(q4, 1, gl * S5_STATE)], axis=-1)
    return ein.astype(BF16), toep.astype(BF16), fo.astype(BF16), lam16


def _s5(u, ein, toep, fout, lam16, d):
    bsz, seqlen, _ = u.shape
    nch = seqlen // S5_CHUNK
    per_q = lambda a: pl.BlockSpec((1,) + a.shape[1:], lambda q, b: (q,) + (0,) * (a.ndim - 1))
    seq = pl.BlockSpec((S5_BATCH, seqlen, LANES), lambda q, b: (b, 0, q))
    state = pltpu.VMEM((S5_BATCH, nch, S5_BLOCK_STATE), F32)
    return pl.pallas_call(
        _s5_kernel,
        grid=(S5_LANE_BLOCKS, bsz // S5_BATCH),
        in_specs=[seq, per_q(ein), per_q(toep), per_q(fout), per_q(lam16),
                  pl.BlockSpec((1, LANES), lambda q, b: (0, q))],
        out_specs=seq,
        out_shape=jax.ShapeDtypeStruct(u.shape, F32),
        scratch_shapes=[state, state],
        compiler_params=pltpu.CompilerParams(
            dimension_semantics=("arbitrary", "arbitrary"), vmem_limit_bytes=VMEM_LIMIT),
        name="s5",
    )(u, ein, toep, fout, lam16, d)


def _stack_heads(x, lo):
    return jnp.concatenate([jnp.where(lo, x, 0.0), jnp.where(lo, 0.0, x)], axis=0)


def _pair_block_diag(x):
    zero = jnp.zeros((x.shape[0], LANES), x.dtype)
    return jnp.concatenate([jnp.concatenate([x[:, :LANES], zero], axis=1),
                            jnp.concatenate([zero, x[:, LANES:]], axis=1)], axis=0)


def _head_block_diag(x, head_id, order):
    return jnp.concatenate([jnp.where(head_id == h, x, 0.0) for h in order], axis=0)


def _rwkv_kernel(r_ref, lw_ref, k_ref, v_ref, kk_ref, b_ref, gate_ref, rk_ref, lnw_ref, lnb_ref,
                 y_ref, s_ref, wr_ref, w2_ref, bkt_ref, prb_ref, yv_ref, e2_ref, u_ref):
    tc = RWKV_CHUNK
    nchunks = RWKV_TB // tc
    qw = 2 * LANES

    @pl.when(pl.program_id(1) == 0)
    def _():
        s_ref[...] = jnp.zeros_like(s_ref)

    row = lax.broadcasted_iota(jnp.int32, (tc, qw), 0)
    lane = lax.broadcasted_iota(jnp.int32, (tc, qw), 1)
    col = jnp.bitwise_and(lane, RWKV_HEAD - 1)
    head_id = lane // RWKV_HEAD
    heads = tuple(range(qw // RWKV_HEAD))
    heads_swapped = (1, 0, 3, 2)
    eye_q = (row == col).astype(F32)
    row_p = lax.broadcasted_iota(jnp.int32, (tc, LANES), 0)
    lane_p = lax.broadcasted_iota(jnp.int32, (tc, LANES), 1)
    col_p = jnp.bitwise_and(lane_p, RWKV_HEAD - 1)
    lo_p = lane_p < RWKV_HEAD
    lo_p2 = lax.broadcasted_iota(jnp.int32, (2 * tc, LANES), 1) < RWKV_HEAD
    stril = row_p > col_p
    tril = row_p >= col_p
    state_bd = ((lax.broadcasted_iota(jnp.int32, (LANES, qw), 0) // RWKV_HEAD)
                == (jnp.bitwise_and(lax.broadcasted_iota(jnp.int32, (LANES, qw), 1), LANES - 1) // RWKV_HEAD))
    ones = _head_ones(qw).astype(BF16)
    tri = (lax.broadcasted_iota(jnp.int32, (tc, tc), 0)
           >= lax.broadcasted_iota(jnp.int32, (tc, tc), 1)).astype(BF16)
    bf = lambda t: t.astype(BF16)
    up, dn = slice(0, tc), slice(tc, 2 * tc)

    def prepare(chunks):
        units = []
        for c in chunks:
            rows = slice(c * tc, (c + 1) * tc)
            lw_all = lw_ref[0, rows, :]
            g_all = _dot_exact_lhs(tri, lw_all)
            for q in range(RWKV_QUADS):
                qs = slice(qw * q, qw * (q + 1))
                units.append(dict(c=c, q=q, rows=rows, qs=qs, lw=lw_all[:, qs], g=g_all[:, qs]))
        yield
        for d in units:
            rows, qs, g = d["rows"], d["qs"], d["g"]
            m = 0.5 * g[tc - 1:tc, :]
            e_out = jnp.exp(m - g)
            em = jnp.exp(m)
            rt = r_ref[0, rows, qs] * jnp.exp(g - m)
            bt = b_ref[0, rows, qs] * e_out
            kt = k_ref[0, rows, qs] * e_out
            d["em"] = em
            d["at"] = bf(-kk_ref[0, rows, qs] * jnp.exp(g - d["lw"] - m))
            d["rt"], d["bt"], d["kt"] = bf(rt), bf(bt), bf(kt)
            d["rts"] = bf(rt * em)
            d["vsw"] = bf(_head_block_diag(v_ref[0, rows, qs], head_id, heads_swapped))
            bkt, e2 = [], []
            for j in range(2):
                ps = slice(LANES * j, LANES * (j + 1))
                bks = jnp.concatenate([bt[:, ps], kt[:, ps]], axis=0) * em[:, ps]
                bkt.append(bf(jnp.transpose(bks)))
                e2.append(jnp.transpose(jnp.broadcast_to(em[:, ps] * em[:, ps], (LANES, LANES))))
            bkt_ref[d["c"], d["q"]] = jnp.concatenate(bkt, axis=1)
            e2_ref[d["c"], d["q"]] = jnp.concatenate(e2, axis=1)
        yield
        for d in units:
            d["o"] = []
            for j in range(2):
                ps = slice(LANES * j, LANES * (j + 1))
                bk = jnp.concatenate([d["bt"][:, ps], d["kt"][:, ps]], axis=0)
                kb = jnp.concatenate([d["kt"][:, ps], d["bt"][:, ps]], axis=0)
                rhs = jnp.concatenate([jnp.where(lo_p2, bk, 0.0), jnp.where(lo_p2, 0.0, kb)], axis=0)
                lhs = jnp.concatenate([d["at"][:, ps], d["rt"][:, ps]], axis=0)
                d["o"].append(_dot_nt(lhs, rhs))
        yield
        for d in units:
            h0 = [o[:, :LANES] for o in d["o"]]
            h1 = [o[:, LANES:] for o in d["o"]]

            def pick(rws, first, second, mask):
                return jnp.concatenate(
                    [jnp.where(mask, jnp.where(lo_p, first[j][rws], second[j][rws]), 0.0) for j in range(2)],
                    axis=1)

            d["ap"] = pick(up, h0, h1, stril)
            d["p_ak"] = bf(pick(up, h1, h0, stril))
            p_rb = bf(pick(dn, h0, h1, tril))
            d["p_rk"] = bf(pick(dn, h1, h0, tril))
            prb_ref[d["c"], d["q"]] = p_rb
            d["tinv"] = eye_q + d["ap"]
        yield
        levels = int(math.log2(tc)) - 1
        for d in units:
            d["ap"] = _dot(bf(d["ap"]), bf(_head_block_diag(d["ap"], head_id, heads)))
        for lvl in range(levels):
            yield
            for d in units:
                rhs = bf(_head_block_diag(d["ap"], head_id, heads))
                if lvl < levels - 1:
                    out = _dot(bf(jnp.concatenate([d["tinv"], d["ap"]], axis=0)), rhs)
                    d["tinv"], d["ap"] = d["tinv"] + out[:tc], out[tc:]
                else:
                    d["tinv"] = d["tinv"] + _dot(bf(d["tinv"]), rhs)
        yield
        for d in units:
            d["tinv"] = bf(d["tinv"])
            out = _dot(jnp.concatenate([d["p_ak"], d["p_rk"]], axis=0), d["vsw"])
            d["xv"] = out[:tc]
            yv_ref[d["c"], d["q"]] = out[tc:]
        yield
        for d in units:
            w1s, w2 = [], []
            for j in range(2):
                ps = slice(LANES * j, LANES * (j + 1))
                rhs = jnp.concatenate([_stack_heads(d["at"][:, ps], lo_p),
                                       bf(_stack_heads(d["xv"][:, ps], lo_p))], axis=1)
                w = _dot(d["tinv"][:, ps], rhs)
                w1s.append(bf(w[:, :LANES] * d["em"][:, ps]))
                w2.append(w[:, LANES:])
            wr_ref[d["c"], d["q"]] = jnp.concatenate([jnp.concatenate(w1s, axis=1), d["rts"]], axis=0)
            w2_ref[d["c"], d["q"]] = jnp.concatenate(w2, axis=1)

    def recur_step(c):
        rows = slice(c * tc, (c + 1) * tc)
        units = [dict(q=q, qs=slice(qw * q, qw * (q + 1))) for q in range(RWKV_QUADS)]
        for d in units:
            d["h0"] = s_ref[d["q"]]
        for d in units:
            out = _dot(wr_ref[c, d["q"]], _pair_block_diag(bf(d["h0"])))
            d["u"] = out[:tc] + w2_ref[c, d["q"]]
            d["yr"] = out[tc:]
        yield
        for d in units:
            q = d["q"]
            uv = jnp.concatenate([bf(d["u"]), v_ref[0, rows, d["qs"]]], axis=0)
            upd = _dot(bkt_ref[c, q], _pair_block_diag(uv))
            s_ref[q] = d["h0"] * e2_ref[c, q] + jnp.where(state_bd, upd, 0.0)
            u_ref[c, q] = bf(d["u"])
            yv_ref[c, q] = yv_ref[c, q] + d["yr"]

    def emit_step(c):
        rows = slice(c * tc, (c + 1) * tc)
        units = [dict(q=q, qs=slice(qw * q, qw * (q + 1))) for q in range(RWKV_QUADS)]
        for d in units:
            q = d["q"]
            d["y"] = yv_ref[c, q] + _dot(prb_ref[c, q], _head_block_diag(u_ref[c, q], head_id, heads))
        yield
        for d in units:
            qs, y = d["qs"], d["y"]
            rkr = r_ref[0, rows, qs].astype(F32) * k_ref[0, rows, qs] * rk_ref[:, qs]
            d["sums"] = _head_sums(jnp.concatenate([y, rkr], axis=0), ones)
        yield
        for d in units:
            d["dev"] = d["y"] - d["sums"][:tc] * (1.0 / RWKV_HEAD)
            d["var"] = _head_sums(d["dev"] * d["dev"], ones) * (1.0 / RWKV_HEAD)
        for d in units:
            qs = d["qs"]
            yn = d["dev"] * lax.rsqrt(d["var"] + GN_EPS) * lnw_ref[:, qs] + lnb_ref[:, qs]
            y_ref[0, rows, qs] = bf((yn + d["sums"][tc:] * v_ref[0, rows, qs]) * gate_ref[0, rows, qs])

    def chain(chunks):
        for c in chunks:
            yield from _alternating(*([emit_step(c - 1)] if c > 0 else []), recur_step(c))

    groups = [list(range(g, g + RWKV_PREP_CHUNKS)) for g in range(0, nchunks, RWKV_PREP_CHUNKS)]
    _alternate(prepare(groups[0]))
    for prev, nxt in zip(groups[:-1], groups[1:]):
        _alternate(prepare(nxt), chain(prev))
    _alternate(chain(groups[-1]))
    _alternate(emit_step(nchunks - 1))


def _rwkv(r, lw, k, v, kkn, b, gate, r_k, ln_w, ln_b):
    bsz, seqlen, _ = r.shape
    nchunks = RWKV_TB // RWKV_CHUNK
    tok = pl.BlockSpec((1, RWKV_TB, D_RWKV), lambda bi, t: (bi, t, 0))
    gate_spec = pl.BlockSpec((1, RWKV_TB, D_RWKV), lambda bi, t: (bi, t, 1))
    vec = pl.BlockSpec((1, D_RWKV), lambda bi, t: (0, 0))
    unit = lambda rows, cols, dt: pltpu.VMEM((nchunks, RWKV_QUADS, rows, cols), dt)
    return pl.pallas_call(
        _rwkv_kernel,
        grid=(bsz, seqlen // RWKV_TB),
        in_specs=[tok] * 6 + [gate_spec] + [vec] * 3,
        out_specs=tok,
        out_shape=jax.ShapeDtypeStruct(r.shape, BF16),
        scratch_shapes=[
            pltpu.VMEM((RWKV_QUADS, LANES, 2 * LANES), F32),
            unit(2 * RWKV_CHUNK, 2 * LANES, BF16),
            unit(RWKV_CHUNK, 2 * LANES, F32),
            unit(LANES, 4 * RWKV_CHUNK, BF16),
            unit(RWKV_CHUNK, 2 * LANES, BF16),
            unit(RWKV_CHUNK, 2 * LANES, F32),
            unit(LANES, 2 * LANES, F32),
            unit(RWKV_CHUNK, 2 * LANES, BF16),
        ],
        compiler_params=pltpu.CompilerParams(
            dimension_semantics=("arbitrary", "arbitrary"), vmem_limit_bytes=VMEM_LIMIT),
        name="rwkv7",
    )(r, lw, k, v, kkn, b, gate, r_k, ln_w, ln_b)


def _out_kernel(y5_ref, gate_ref, yr_ref, x_ref, gluw_ref, glub_ref, wout_ref, fg_ref, o_ref):
    def sub_tile(s):
        rs = slice(OUT_SUB * s, OUT_SUB * (s + 1))
        y = y5_ref[0, rs, :]
        y = 0.5 * y * (1.0 + jnp.tanh(math.sqrt(2.0 / math.pi) * (y + 0.044715 * (y * y * y))))
        yield
        glu = _sigmoid(_dot(y.astype(BF16), gluw_ref[...]) + glub_ref[...])
        ys = y * glu * gate_ref[0, rs, :]
        ycat = jnp.concatenate([ys.astype(BF16), yr_ref[0, rs, :]], axis=-1)
        yield
        xn = x_ref[0, rs, :] + _dot(ycat, wout_ref[...])
        yield
        ms = jnp.mean(xn * xn, axis=-1, keepdims=True)
        o_ref[0, rs, :] = xn * lax.rsqrt(ms + NORM_EPS) * fg_ref[...]

    _alternate(*[_delayed(sub_tile(s), SUB_TILE_SKEW * s) for s in range(TM_OUT // OUT_SUB)])


def _outproj(y5, gate, yr, x, glu_w, glu_b, w_out, final_g):
    bsz, seqlen, _ = x.shape
    tok = lambda w: pl.BlockSpec((1, TM_OUT, w), lambda b, t: (b, t, 0))
    full = lambda a: pl.BlockSpec(a.shape, lambda b, t: (0,) * a.ndim)
    return pl.pallas_call(
        _out_kernel,
        grid=(bsz, seqlen // TM_OUT),
        in_specs=[tok(D_S5), tok(D_S5), tok(D_RWKV), tok(D_MODEL),
                  full(glu_w), full(glu_b), full(w_out), full(final_g)],
        out_specs=tok(D_MODEL),
        out_shape=jax.ShapeDtypeStruct(x.shape, F32),
        compiler_params=pltpu.CompilerParams(
            dimension_semantics=("arbitrary", "arbitrary"), vmem_limit_bytes=VMEM_LIMIT),
        name="outproj",
    )(y5, gate, yr, x, glu_w, glu_b, w_out, final_g)


@jax.jit
def _forward(x, norm_g, w_in, s5_lam_re, s5_lam_im, s5_log_dt, s5_b_re, s5_b_im, s5_c_re, s5_c_im,
             s5_d, s5_glu_w, s5_glu_b, rwkv_mu, rwkv_w0, rwkv_w2, rwkv_a0, rwkv_a2, rwkv_k_k, rwkv_k_a,
             rwkv_r_k, rwkv_ln_w, rwkv_ln_b, w_out, final_g):
    assert w_in.shape[0] == 1, "single-layer block"
    for l in range(1):
        zeros = jnp.zeros((LOW_RANK, D_RWKV), F32)
        w2a2 = jnp.concatenate(
            [jnp.concatenate([rwkv_w2[l], zeros], axis=1), jnp.concatenate([zeros, rwkv_a2[l]], axis=1)],
            axis=0).astype(BF16)
        row = lambda a: a.reshape(1, -1).astype(F32)
        u, gate, r, lw, k, v, kkn, b = _inproj(
            x, row(norm_g[l]), w_in[l].astype(BF16), row(rwkv_mu[l]), w2a2, row(rwkv_w0[l]),
            row(rwkv_a0[l]), row(rwkv_k_k[l]), row(rwkv_k_a[l]))
        ein, toep, fout, lam16 = _s5_operators(
            s5_lam_re[l], s5_lam_im[l], s5_log_dt[l], s5_b_re[l], s5_b_im[l], s5_c_re[l], s5_c_im[l])
        y5 = _s5(u, ein, toep, fout, lam16, row(s5_d[l]))
        yr = _rwkv(r, lw, k, v, kkn, b, gate, row(rwkv_r_k[l]), row(rwkv_ln_w[l]), row(rwkv_ln_b[l]))
        x = _outproj(y5, gate, yr, x, s5_glu_w[l].astype(BF16), row(s5_glu_b[l]),
                     w_out[l].astype(BF16), row(final_g))
    return x


def kernel(x, norm_g, w_in, s5_lam_re, s5_lam_im, s5_log_dt, s5_b_re, s5_b_im, s5_c_re, s5_c_im, s5_d, s5_glu_w, s5_glu_b, rwkv_mu, rwkv_w0, rwkv_w2, rwkv_a0, rwkv_a2, rwkv_k_k, rwkv_k_a, rwkv_r_k, rwkv_ln_w, rwkv_ln_b, w_out, final_g):
    return _forward(x, norm_g, w_in, s5_lam_re, s5_lam_im, s5_log_dt, s5_b_re, s5_b_im, s5_c_re, s5_c_im,
                    s5_d, s5_glu_w, s5_glu_b, rwkv_mu, rwkv_w0, rwkv_w2, rwkv_a0, rwkv_a2, rwkv_k_k,
                    rwkv_k_a, rwkv_r_k, rwkv_ln_w, rwkv_ln_b, w_out, final_g)
```

```python
import math

import jax
import jax.numpy as jnp
from jax import lax
from jax.experimental import pallas as pl
from jax.experimental.pallas import tpu as pltpu

F32 = jnp.float32
BF16 = jnp.bfloat16

D_MODEL = 1024
D_S5 = 512
D_RWKV = 512
S5_GROUP = 16
S5_GROUPS = 32
S5_STATE = 64
RWKV_HEAD = 64
LOW_RANK = 64
D_SHIFT = 3 * D_RWKV + 2 * LOW_RANK
D_IN = 2 * D_S5 + D_SHIFT + D_RWKV
NORM_EPS = 1e-6
GN_EPS = 64e-5
GELU_C = math.sqrt(2.0 / math.pi)

LANES = 128
TM_IN = 1024
IN_SUB = 256
SUB_TILE_SKEW = 2
TM_OUT = 1024
OUT_SUB = 512
S5_CHUNK = 16
S5_BATCH = 2
S5_PIPE_SKEW = 6
S5_LANE_BLOCKS = D_S5 // LANES
S5_GROUPS_PER_BLOCK = LANES // S5_GROUP
S5_BLOCK_STATE = 2 * S5_GROUPS_PER_BLOCK * S5_STATE
RWKV_CHUNK = 64
RWKV_TB = 1024
RWKV_PREP_CHUNKS = 4
HEAD_PAIRS = D_RWKV // LANES
RWKV_QUADS = HEAD_PAIRS // 2
VMEM_LIMIT = 56 * 1024 * 1024


def _dot(a, b):
    return jnp.dot(a, b, preferred_element_type=F32)


def _dot_nt(a, b):
    return lax.dot_general(a, b, (((1,), (1,)), ((), ())), preferred_element_type=F32)


def _dot_tn(a, b):
    return lax.dot_general(a, b, (((0,), (0,)), ((), ())), preferred_element_type=F32)


def _split2(x):
    hi = x.astype(BF16)
    return hi, (x - hi.astype(F32)).astype(BF16)


def _head_sums(x, ones):
    return _dot(x.astype(BF16), ones)


def _dot_exact_lhs(m, x):
    hi, lo = _split2(x)
    return _dot(m, hi) + _dot(m, lo)


def _sigmoid(x):
    return 0.5 + 0.5 * jnp.tanh(0.5 * x)


def _head_ones(width):
    r = lax.broadcasted_iota(jnp.int32, (width, width), 0)
    c = lax.broadcasted_iota(jnp.int32, (width, width), 1)
    return (r // RWKV_HEAD) == (c // RWKV_HEAD)


def _alternating(*stage_generators):
    pending = list(stage_generators)
    while pending:
        for g in list(pending):
            if next(g, StopIteration) is StopIteration:
                pending.remove(g)
            else:
                yield


def _alternate(*stage_generators):
    for _ in _alternating(*stage_generators):
        pass


def _delayed(stage_generator, stages):
    for _ in range(stages):
        yield
    yield from stage_generator


def _inproj_kernel(x_ref, g_ref, win_ref, mu_ref, w2a2_ref, w0_ref, a0_ref, kk_ref, ka_ref,
                   u_ref, gate_ref, r_ref, lw_ref, k_ref, v_ref, kkn_ref, b_ref, carry_ref):
    t = pl.program_id(1)

    @pl.when(t == 0)
    def _():
        carry_ref[...] = jnp.zeros_like(carry_ref)

    ones = _head_ones(2 * LANES).astype(BF16)
    carries = {-1: carry_ref[...]}

    def sub_tile(s):
        rs = slice(IN_SUB * s, IN_SUB * (s + 1))
        x = x_ref[0, rs, :]
        ms = jnp.mean(x * x, axis=-1, keepdims=True)
        h = (x * lax.rsqrt(ms + NORM_EPS) * g_ref[...]).astype(BF16)

        def proj(lo, hi):
            return _dot(h, win_ref[:, lo:hi])

        yield
        u_ref[0, rs, :] = proj(0, D_S5)
        yield
        z5 = proj(D_S5, 2 * D_S5)
        gate_ref[0, rs, :D_S5] = (z5 * _sigmoid(z5)).astype(BF16)
        yield
        zr = proj(2 * D_S5 + D_SHIFT, D_IN)
        gate_ref[0, rs, D_S5:] = (zr * _sigmoid(zr)).astype(BF16)
        yield
        rw = proj(2 * D_S5, 2 * D_S5 + D_SHIFT)
        carries[s] = rw[IN_SUB - 1:IN_SUB, :]
        yield
        prev = pltpu.roll(rw, 1, axis=0)
        row = lax.broadcasted_iota(jnp.int32, rw.shape, 0)
        prev = jnp.where(row == 0, carries[s - 1], prev)
        sh = rw + (prev - rw) * mu_ref[...]
        r = sh[:, 0:D_RWKV]
        k = sh[:, D_RWKV:2 * D_RWKV]
        v = sh[:, 2 * D_RWKV:3 * D_RWKV]
        wa = sh[:, 3 * D_RWKV:]
        lane = lax.broadcasted_iota(jnp.int32, wa.shape, 1)
        wa = jnp.where(lane < LOW_RANK, jnp.tanh(wa), wa)
        za = _dot(wa.astype(BF16), w2a2_ref[...])
        r_ref[0, rs, :] = r.astype(BF16)
        v_ref[0, rs, :] = v.astype(BF16)
        yield
        lw_ref[0, rs, :] = -math.exp(-0.5) * _sigmoid(w0_ref[...] + za[:, :D_RWKV])
        iclr = _sigmoid(a0_ref[...] + za[:, D_RWKV:])
        k_ref[0, rs, :] = (k * (1.0 + (iclr - 1.0) * ka_ref[...])).astype(BF16)
        kk = k * kk_ref[...]
        sq = kk * kk
        ss = jnp.concatenate(
            [_head_sums(sq[:, 2 * LANES * q:2 * LANES * (q + 1)], ones) for q in range(RWKV_QUADS)], axis=1)
        yield
        kkn = kk * lax.rsqrt(jnp.maximum(ss, 1e-24))
        kkn_ref[0, rs, :] = kkn.astype(BF16)
        b_ref[0, rs, :] = (kkn * iclr).astype(BF16)

    n_sub = TM_IN // IN_SUB
    _alternate(*[_delayed(sub_tile(s), SUB_TILE_SKEW * s) for s in range(n_sub)])
    carry_ref[...] = carries[n_sub - 1]


def _inproj(x, norm_g, w_in, mu, w2a2, w0, a0, k_k, k_a):
    bsz, seqlen, _ = x.shape
    tok = lambda w: pl.BlockSpec((1, TM_IN, w), lambda b, t: (b, t, 0))
    full = lambda a: pl.BlockSpec(a.shape, lambda b, t: (0,) * a.ndim)
    out = lambda w, dt: jax.ShapeDtypeStruct((bsz, seqlen, w), dt)
    params = (norm_g, w_in, mu, w2a2, w0, a0, k_k, k_a)
    return pl.pallas_call(
        _inproj_kernel,
        grid=(bsz, seqlen // TM_IN),
        in_specs=[tok(D_MODEL)] + [full(a) for a in params],
        out_specs=[tok(D_S5), tok(2 * D_S5)] + [tok(D_RWKV)] * 6,
        out_shape=[out(D_S5, F32), out(2 * D_S5, BF16)]
        + [out(D_RWKV, F32 if i == 1 else BF16) for i in range(6)],
        scratch_shapes=[pltpu.VMEM((1, D_SHIFT), F32)],
        compiler_params=pltpu.CompilerParams(
            dimension_semantics=("arbitrary", "arbitrary"), vmem_limit_bytes=VMEM_LIMIT),
        name="inproj",
    )(x, *params)


def _s5_kernel(u_ref, ein_ref, toep_ref, fout_ref, lam_ref, d_ref, y_ref, e_ref, sp_ref):
    nch = e_ref.shape[1]
    half = S5_BLOCK_STATE // 2
    pairs = S5_CHUNK // 2
    lre = lam_ref[0, :, :half]
    lim = lam_ref[0, :, half:]
    d = d_ref[...]

    def pipeline(bi):
        xs = [u_ref[bi, pl.ds(j, nch, stride=S5_CHUNK), :] for j in range(S5_CHUNK)]
        xcat = jnp.concatenate([x.astype(BF16) for x in xs], axis=1)
        yield
        e_ref[bi] = _dot(xcat, ein_ref[0])
        yield
        intra = {}

        def scan():
            sre = sim = jnp.zeros((1, half), F32)
            for c in range(nch):
                sp_ref[bi, c:c + 1, :half] = sre
                sp_ref[bi, c:c + 1, half:] = sim
                e = e_ref[bi, c:c + 1, :]
                sre, sim = lre * sre - lim * sim + e[:, :half], lre * sim + lim * sre + e[:, half:]
                if (c + 1) % (nch // pairs) == 0:
                    yield

        def toeplitz():
            for tp in range(pairs):
                intra[tp] = _dot(xcat[:, :2 * LANES * (tp + 1)],
                                 toep_ref[0, 2 * LANES * (pairs - 1 - tp):, :])
                yield

        yield from _alternating(scan(), toeplitz())
        sp = sp_ref[bi].astype(BF16)
        for tp in range(pairs):
            y = intra[tp] + _dot(sp, fout_ref[0, :, 2 * LANES * tp:2 * LANES * (tp + 1)])
            for i in range(2):
                t = 2 * tp + i
                y_ref[bi, pl.ds(t, nch, stride=S5_CHUNK), :] = y[:, LANES * i:LANES * (i + 1)] + d * xs[t]
            yield

    _alternate(*[_delayed(pipeline(bi), S5_PIPE_SKEW * bi) for bi in range(S5_BATCH)])


def _cmul(ar, ai, br, bi):
    return ar * br - ai * bi, ar * bi + ai * br


def _block_diag_expand(compact, width, inner, transposed):
    k = compact.shape[1] if transposed else compact.shape[2]
    group_w = S5_GROUPS_PER_BLOCK * inner
    kk = jnp.arange(k)
    ww = jnp.arange(width)
    sel = ((kk[:, None] // inner == ww[None, :] // group_w)
           & (kk[:, None] % inner == ww[None, :] % inner)).astype(F32)
    wide_group = (ww % group_w) // inner
    if transposed:
        full = jnp.matmul(sel.T, compact)
        narrow_group = (jnp.arange(compact.shape[2]) % LANES) // S5_GROUP
        keep = wide_group[:, None] == narrow_group[None, :]
    else:
        full = jnp.matmul(compact, sel)
        narrow_group = (jnp.arange(compact.shape[1]) % LANES) // S5_GROUP
        keep = narrow_group[:, None] == wide_group[None, :]
    return jnp.where(keep, full, 0.0)


def _s5_operators(lam_re, lam_im, log_dt, b_re, b_im, c_re, c_im):
    c16, q4, gl = S5_CHUNK, S5_LANE_BLOCKS, S5_GROUPS_PER_BLOCK
    rows = c16 * LANES
    dt = jnp.exp(log_dt)[:, None]
    n = jnp.arange(c16 + 1, dtype=F32)[:, None, None]
    mag = jnp.exp(n * (lam_re * dt)[None])
    ang = n * (lam_im * dt)[None]
    pw_r, pw_i = mag * jnp.cos(ang), mag * jnp.sin(ang)
    den = lam_re * lam_re + lam_im * lam_im
    q_r, q_i = _cmul(pw_r[1] - 1.0, pw_i[1], lam_re / den, -lam_im / den)
    bt_r, bt_i = jnp.swapaxes(b_re, 1, 2), jnp.swapaxes(b_im, 1, 2)
    bb_r, bb_i = _cmul(q_r[:, None, :], q_i[:, None, :], bt_r, bt_i)
    e_r, e_i = _cmul(pw_r[:c16, :, None, :], pw_i[:c16, :, None, :], bb_r[None], bb_i[None])

    def by_block(a):
        a = a.reshape(a.shape[0], q4, gl * S5_GROUP, 2 * S5_STATE)
        return jnp.swapaxes(a, 0, 1).reshape(q4, -1, 2 * S5_STATE)

    ein = by_block(jnp.concatenate([e_r, e_i], axis=-1)[::-1])
    ein = _block_diag_expand(ein, S5_BLOCK_STATE, S5_STATE, False)

    f_r, f_i = _cmul(c_re[None], c_im[None], pw_r[1:, :, None, :], pw_i[1:, :, None, :])
    fo = jnp.swapaxes(by_block(jnp.concatenate([f_r, -f_i], axis=-1)), 1, 2)
    fo = _block_diag_expand(fo, S5_BLOCK_STATE, S5_STATE, True)

    ee = jnp.concatenate([e_r, -e_i], axis=-1)
    cc = jnp.concatenate([c_re, c_im], axis=-1).reshape(q4, LANES, 2 * S5_STATE)
    row_g = (jnp.arange(rows) % LANES) // S5_GROUP
    col_g = jnp.arange(LANES) // S5_GROUP
    halves = []
    for t2 in range(2):
        lags = [2 * d + t2 - j2 for d in reversed(range(c16 // 2)) for j2 in range(2)]
        a = jnp.stack([ee[lag] if lag >= 0 else jnp.zeros_like(ee[0]) for lag in lags])
        full = jnp.einsum('qrk,qck->qrc', by_block(a), cc, precision=lax.Precision.HIGHEST)
        halves.append(jnp.where(row_g[:, None] == col_g[None, :], full, 0.0))
    toep = jnp.concatenate(halves, axis=-1)

    lam16 = jnp.concatenate([pw_r[c16].reshape(q4, 1, gl * S5_STATE),
                             pw_i[c16].reshape(q4, 1, gl * S5_STATE)], axis=-1)
    return ein.astype(BF16), toep.astype(BF16), fo.astype(BF16), lam16


def _s5(u, ein, toep, fout, lam16, d):
    bsz, seqlen, _ = u.shape
    nch = seqlen // S5_CHUNK
    per_q = lambda a: pl.BlockSpec((1,) + a.shape[1:], lambda q, b: (q,) + (0,) * (a.ndim - 1))
    seq = pl.BlockSpec((S5_BATCH, seqlen, LANES), lambda q, b: (b, 0, q))
    state = pltpu.VMEM((S5_BATCH, nch, S5_BLOCK_STATE), F32)
    return pl.pallas_call(
        _s5_kernel,
        grid=(S5_LANE_BLOCKS, bsz // S5_BATCH),
        in_specs=[seq, per_q(ein), per_q(toep), per_q(fout), per_q(lam16),
                  pl.BlockSpec((1, LANES), lambda q, b: (0, q))],
        out_specs=seq,
        out_shape=jax.ShapeDtypeStruct(u.shape, F32),
        scratch_shapes=[state, state],
        compiler_params=pltpu.CompilerParams(
            dimension_semantics=("arbitrary", "arbitrary"), vmem_limit_bytes=VMEM_LIMIT),
        name="s5",
    )(u, ein, toep, fout, lam16, d)


def _stack_heads(x, lo):
    return jnp.concatenate([jnp.where(lo, x, 0.0), jnp.where(lo, 0.0, x)], axis=0)


def _pair_block_diag(x):
    zero = jnp.zeros((x.shape[0], LANES), x.dtype)
    return jnp.concatenate([jnp.concatenate([x[:, :LANES], zero], axis=1),
                            jnp.concatenate([zero, x[:, LANES:]], axis=1)], axis=0)


def _head_block_diag(x, head_id, order):
    return jnp.concatenate([jnp.where(head_id == h, x, 0.0) for h in order], axis=0)


def _rwkv_kernel(r_ref, lw_ref, k_ref, v_ref, kk_ref, b_ref, gate_ref, rk_ref, lnw_ref, lnb_ref,
                 y_ref, s_ref, wr_ref, w2_ref, bkt_ref, prb_ref, yv_ref, e2_ref, u_ref):
    tc = RWKV_CHUNK
    nchunks = RWKV_TB // tc
    qw = 2 * LANES

    @pl.when(pl.program_id(1) == 0)
    def _():
        s_ref[...] = jnp.zeros_like(s_ref)

    row = lax.broadcasted_iota(jnp.int32, (tc, qw), 0)
    lane = lax.broadcasted_iota(jnp.int32, (tc, qw), 1)
    col = jnp.bitwise_and(lane, RWKV_HEAD - 1)
    head_id = lane // RWKV_HEAD
    heads = tuple(range(qw // RWKV_HEAD))
    heads_swapped = (1, 0, 3, 2)
    eye_q = (row == col).astype(F32)
    row_p = lax.broadcasted_iota(jnp.int32, (tc, LANES), 0)
    lane_p = lax.broadcasted_iota(jnp.int32, (tc, LANES), 1)
    col_p = jnp.bitwise_and(lane_p, RWKV_HEAD - 1)
    lo_p = lane_p < RWKV_HEAD
    lo_p2 = lax.broadcasted_iota(jnp.int32, (2 * tc, LANES), 1) < RWKV_HEAD
    stril = row_p > col_p
    tril = row_p >= col_p
    state_bd = ((lax.broadcasted_iota(jnp.int32, (LANES, qw), 0) // RWKV_HEAD)
                == (jnp.bitwise_and(lax.broadcasted_iota(jnp.int32, (LANES, qw), 1), LANES - 1) // RWKV_HEAD))
    ones = _head_ones(qw).astype(BF16)
    tri = (lax.broadcasted_iota(jnp.int32, (tc, tc), 0)
           >= lax.broadcasted_iota(jnp.int32, (tc, tc), 1)).astype(BF16)
    bf = lambda t: t.astype(BF16)
    up, dn = slice(0, tc), slice(tc, 2 * tc)

    def prepare(chunks):
        units = []
        for c in chunks:
            rows = slice(c * tc, (c + 1) * tc)
            lw_all = lw_ref[0, rows, :]
            g_all = _dot_exact_lhs(tri, lw_all)
            for q in range(RWKV_QUADS):
                qs = slice(qw * q, qw * (q + 1))
                units.append(dict(c=c, q=q, rows=rows, qs=qs, lw=lw_all[:, qs], g=g_all[:, qs]))
        yield
        for d in units:
            rows, qs, g = d["rows"], d["qs"], d["g"]
            m = 0.5 * g[tc - 1:tc, :]
            e_out = jnp.exp(m - g)
            em = jnp.exp(m)
            rt = r_ref[0, rows, qs] * jnp.exp(g - m)
            bt = b_ref[0, rows, qs] * e_out
            kt = k_ref[0, rows, qs] * e_out
            d["em"] = em
            d["at"] = bf(-kk_ref[0, rows, qs] * jnp.exp(g - d["lw"] - m))
            d["rt"], d["bt"], d["kt"] = bf(rt), bf(bt), bf(kt)
            d["rts"] = bf(rt * em)
            d["vsw"] = bf(_head_block_diag(v_ref[0, rows, qs], head_id, heads_swapped))
            bkt, e2 = [], []
            for j in range(2):
                ps = slice(LANES * j, LANES * (j + 1))
                bks = jnp.concatenate([bt[:, ps], kt[:, ps]], axis=0) * em[:, ps]
                bkt.append(bf(jnp.transpose(bks)))
                e2.append(jnp.transpose(jnp.broadcast_to(em[:, ps] * em[:, ps], (LANES, LANES))))
            bkt_ref[d["c"], d["q"]] = jnp.concatenate(bkt, axis=1)
            e2_ref[d["c"], d["q"]] = jnp.concatenate(e2, axis=1)
        yield
        for d in units:
            d["o"] = []
            for j in range(2):
                ps = slice(LANES * j, LANES * (j + 1))
                bk = jnp.concatenate([d["bt"][:, ps], d["kt"][:, ps]], axis=0)
                kb = jnp.concatenate([d["kt"][:, ps], d["bt"][:, ps]], axis=0)
                rhs = jnp.concatenate([jnp.where(lo_p2, bk, 0.0), jnp.where(lo_p2, 0.0, kb)], axis=0)
                lhs = jnp.concatenate([d["at"][:, ps], d["rt"][:, ps]], axis=0)
                d["o"].append(_dot_nt(lhs, rhs))
        yield
        for d in units:
            h0 = [o[:, :LANES] for o in d["o"]]
            h1 = [o[:, LANES:] for o in d["o"]]

            def pick(rws, first, second, mask):
                return jnp.concatenate(
                    [jnp.where(mask, jnp.where(lo_p, first[j][rws], second[j][rws]), 0.0) for j in range(2)],
                    axis=1)

            d["ap"] = pick(up, h0, h1, stril)
            d["p_ak"] = bf(pick(up, h1, h0, stril))
            p_rb = bf(pick(dn, h0, h1, tril))
            d["p_rk"] = bf(pick(dn, h1, h0, tril))
            prb_ref[d["c"], d["q"]] = p_rb
            d["tinv"] = eye_q + d["ap"]
        yield
        levels = int(math.log2(tc)) - 1
        for d in units:
            d["ap"] = _dot(bf(d["ap"]), bf(_head_block_diag(d["ap"], head_id, heads)))
        for lvl in range(levels):
            yield
            for d in units:
                rhs = bf(_head_block_diag(d["ap"], head_id, heads))
                if lvl < levels - 1:
                    out = _dot(bf(jnp.concatenate([d["tinv"], d["ap"]], axis=0)), rhs)
                    d["tinv"], d["ap"] = d["tinv"] + out[:tc], out[tc:]
                else:
                    d["tinv"] = d["tinv"] + _dot(bf(d["tinv"]), rhs)
        yield
        for d in units:
            d["tinv"] = bf(d["tinv"])
            out = _dot(jnp.concatenate([d["p_ak"], d["p_rk"]], axis=0), d["vsw"])
            d["xv"] = out[:tc]
            yv_ref[d["c"], d["q"]] = out[tc:]
        yield
        for d in units:
            w1s, w2 = [], []
            for j in range(2):
                ps = slice(LANES * j, LANES * (j + 1))
                rhs = jnp.concatenate([_stack_heads(d["at"][:, ps], lo_p),
                                       bf(_stack_heads(d["xv"][:, ps], lo_p))], axis=1)
                w = _dot(d["tinv"][:, ps], rhs)
                w1s.append(bf(w[:, :LANES] * d["em"][:, ps]))
                w2.append(w[:, LANES:])
            wr_ref[d["c"], d["q"]] = jnp.concatenate([jnp.concatenate(w1s, axis=1), d["rts"]], axis=0)
            w2_ref[d["c"], d["q"]] = jnp.concatenate(w2, axis=1)

    def recur_step(c):
        rows = slice(c * tc, (c + 1) * tc)
        units = [dict(q=q, qs=slice(qw * q, qw * (q + 1))) for q in range(RWKV_QUADS)]
        for d in units:
            d["h0"] = s_ref[d["q"]]
        for d in units:
            out = _dot(wr_ref[c, d["q"]], _pair_block_diag(bf(d["h0"])))
            d["u"] = out[:tc] + w2_ref[c, d["q"]]
            d["yr"] = out[tc:]
        yield
        for d in units:
            q = d["q"]
            uv = jnp.concatenate([bf(d["u"]), v_ref[0, rows, d["qs"]]], axis=0)
            upd = _dot(bkt_ref[c, q], _pair_block_diag(uv))
            s_ref[q] = d["h0"] * e2_ref[c, q] + jnp.where(state_bd, upd, 0.0)
            u_ref[c, q] = bf(d["u"])
            yv_ref[c, q] = yv_ref[c, q] + d["yr"]

    def emit_step(c):
        rows = slice(c * tc, (c + 1) * tc)
        units = [dict(q=q, qs=slice(qw * q, qw * (q + 1))) for q in range(RWKV_QUADS)]
        for d in units:
            q = d["q"]
            d["y"] = yv_ref[c, q] + _dot(prb_ref[c, q], _head_block_diag(u_ref[c, q], head_id, heads))
        yield
        stacked = []
        for d in units:
            qs = d["qs"]
            stacked += [d["y"], r_ref[0, rows, qs].astype(F32) * k_ref[0, rows, qs] * rk_ref[:, qs]]
        sums = _head_sums(jnp.concatenate(stacked, axis=0), ones)
        for i, d in enumerate(units):
            d["dev"] = d["y"] - sums[2 * i * tc:(2 * i + 1) * tc] * (1.0 / RWKV_HEAD)
            d["bonus"] = sums[(2 * i + 1) * tc:(2 * i + 2) * tc]
        yield
        var = _head_sums(jnp.concatenate([d["dev"] * d["dev"] for d in units], axis=0), ones)
        for i, d in enumerate(units):
            qs = d["qs"]
            rstd = lax.rsqrt(var[i * tc:(i + 1) * tc] * (1.0 / RWKV_HEAD) + GN_EPS)
            yn = d["dev"] * rstd * lnw_ref[:, qs] + lnb_ref[:, qs]
            y_ref[0, rows, qs] = bf((yn + d["bonus"] * v_ref[0, rows, qs]) * gate_ref[0, rows, qs])

    def chain(chunks):
        for c in chunks:
            yield from _alternating(*([emit_step(c - 1)] if c > 0 else []), recur_step(c))

    groups = [list(range(g, g + RWKV_PREP_CHUNKS)) for g in range(0, nchunks, RWKV_PREP_CHUNKS)]
    _alternate(prepare(groups[0]))
    for prev, nxt in zip(groups[:-1], groups[1:]):
        _alternate(prepare(nxt), chain(prev))
    _alternate(chain(groups[-1]))
    _alternate(emit_step(nchunks - 1))


def _rwkv(r, lw, k, v, kkn, b, gate, r_k, ln_w, ln_b):
    bsz, seqlen, _ = r.shape
    nchunks = RWKV_TB // RWKV_CHUNK
    tok = pl.BlockSpec((1, RWKV_TB, D_RWKV), lambda bi, t: (bi, t, 0))
    gate_spec = pl.BlockSpec((1, RWKV_TB, D_RWKV), lambda bi, t: (bi, t, 1))
    vec = pl.BlockSpec((1, D_RWKV), lambda bi, t: (0, 0))
    unit = lambda rows, cols, dt: pltpu.VMEM((nchunks, RWKV_QUADS, rows, cols), dt)
    return pl.pallas_call(
        _rwkv_kernel,
        grid=(bsz, seqlen // RWKV_TB),
        in_specs=[tok] * 6 + [gate_spec] + [vec] * 3,
        out_specs=tok,
        out_shape=jax.ShapeDtypeStruct(r.shape, BF16),
        scratch_shapes=[
            pltpu.VMEM((RWKV_QUADS, LANES, 2 * LANES), F32),
            unit(2 * RWKV_CHUNK, 2 * LANES, BF16),
            unit(RWKV_CHUNK, 2 * LANES, F32),
            unit(LANES, 4 * RWKV_CHUNK, BF16),
            unit(RWKV_CHUNK, 2 * LANES, BF16),
            unit(RWKV_CHUNK, 2 * LANES, F32),
            unit(LANES, 2 * LANES, F32),
            unit(RWKV_CHUNK, 2 * LANES, BF16),
        ],
        compiler_params=pltpu.CompilerParams(
            dimension_semantics=("arbitrary", "arbitrary"), vmem_limit_bytes=VMEM_LIMIT),
        name="rwkv7",
    )(r, lw, k, v, kkn, b, gate, r_k, ln_w, ln_b)


def _out_kernel(y5_ref, gate_ref, yr_ref, x_ref, gluw_ref, glub_ref, wout_ref, fg_ref, o_ref):
    def sub_tile(s):
        rs = slice(OUT_SUB * s, OUT_SUB * (s + 1))
        y = y5_ref[0, rs, :]
        half = 0.5 * y
        y = half + half * jnp.tanh(y * (GELU_C + (GELU_C * 0.044715) * (y * y)))
        yield
        glu = _sigmoid(_dot(y.astype(BF16), gluw_ref[...]) + glub_ref[...])
        ys = y * glu * gate_ref[0, rs, :]
        ycat = jnp.concatenate([ys.astype(BF16), yr_ref[0, rs, :]], axis=-1)
        yield
        xn = x_ref[0, rs, :] + _dot(ycat, wout_ref[...])
        yield
        ms = jnp.mean(xn * xn, axis=-1, keepdims=True)
        o_ref[0, rs, :] = xn * lax.rsqrt(ms + NORM_EPS) * fg_ref[...]

    _alternate(*[_delayed(sub_tile(s), SUB_TILE_SKEW * s) for s in range(TM_OUT // OUT_SUB)])


def _outproj(y5, gate, yr, x, glu_w, glu_b, w_out, final_g):
    bsz, seqlen, _ = x.shape
    tok = lambda w: pl.BlockSpec((1, TM_OUT, w), lambda b, t: (b, t, 0))
    full = lambda a: pl.BlockSpec(a.shape, lambda b, t: (0,) * a.ndim)
    return pl.pallas_call(
        _out_kernel,
        grid=(bsz, seqlen // TM_OUT),
        in_specs=[tok(D_S5), tok(D_S5), tok(D_RWKV), tok(D_MODEL),
                  full(glu_w), full(glu_b), full(w_out), full(final_g)],
        out_specs=tok(D_MODEL),
        out_shape=jax.ShapeDtypeStruct(x.shape, F32),
        compiler_params=pltpu.CompilerParams(
            dimension_semantics=("arbitrary", "arbitrary"), vmem_limit_bytes=VMEM_LIMIT),
        name="outproj",
    )(y5, gate, yr, x, glu_w, glu_b, w_out, final_g)


@jax.jit
def _forward(x, norm_g, w_in, s5_lam_re, s5_lam_im, s5_log_dt, s5_b_re, s5_b_im, s5_c_re, s5_c_im,
             s5_d, s5_glu_w, s5_glu_b, rwkv_mu, rwkv_w0, rwkv_w2, rwkv_a0, rwkv_a2, rwkv_k_k, rwkv_k_a,
             rwkv_r_k, rwkv_ln_w, rwkv_ln_b, w_out, final_g):
    assert w_in.shape[0] == 1, "single-layer block"
    for l in range(1):
        zeros = jnp.zeros((LOW_RANK, D_RWKV), F32)
        w2a2 = jnp.concatenate(
            [jnp.concatenate([rwkv_w2[l], zeros], axis=1), jnp.concatenate([zeros, rwkv_a2[l]], axis=1)],
            axis=0).astype(BF16)
        row = lambda a: a.reshape(1, -1).astype(F32)
        u, gate, r, lw, k, v, kkn, b = _inproj(
            x, row(norm_g[l]), w_in[l].astype(BF16), row(rwkv_mu[l]), w2a2, row(rwkv_w0[l]),
            row(rwkv_a0[l]), row(rwkv_k_k[l]), row(rwkv_k_a[l]))
        ein, toep, fout, lam16 = _s5_operators(
            s5_lam_re[l], s5_lam_im[l], s5_log_dt[l], s5_b_re[l], s5_b_im[l], s5_c_re[l], s5_c_im[l])
        y5 = _s5(u, ein, toep, fout, lam16, row(s5_d[l]))
        yr = _rwkv(r, lw, k, v, kkn, b, gate, row(rwkv_r_k[l]), row(rwkv_ln_w[l]), row(rwkv_ln_b[l]))
        x = _outproj(y5, gate, yr, x, s5_glu_w[l].astype(BF16), row(s5_glu_b[l]),
                     w_out[l].astype(BF16), row(final_g))
    return x


def kernel(x, norm_g, w_in, s5_lam_re, s5_lam_im, s5_log_dt, s5_b_re, s5_b_im, s5_c_re, s5_c_im, s5_d, s5_glu_w, s5_glu_b, rwkv_mu, rwkv_w0, rwkv_w2, rwkv_a0, rwkv_a2, rwkv_k_k, rwkv_k_a, rwkv_r_k, rwkv_ln_w, rwkv_ln_b, w_out, final_g):
    return _forward(x, norm_g, w_in, s5_lam_re, s5_lam_im, s5_log_dt, s5_b_re, s5_b_im, s5_c_re, s5_c_im,
                    s5_d, s5_glu_w, s5_glu_b, rwkv_mu, rwkv_w0, rwkv_w2, rwkv_a0, rwkv_a2, rwkv_k_k,
                    rwkv_k_a, rwkv_r_k, rwkv_ln_w, rwkv_ln_b, w_out, final_g)
```

```python
import math

import jax
import jax.numpy as jnp
from jax import lax
from jax.experimental import pallas as pl
from jax.experimental.pallas import tpu as pltpu

F32 = jnp.float32
BF16 = jnp.bfloat16

D_MODEL = 1024
D_S5 = 512
D_RWKV = 512
S5_GROUP = 16
S5_GROUPS = 32
S5_STATE = 64
RWKV_HEAD = 64
LOW_RANK = 64
D_SHIFT = 3 * D_RWKV + 2 * LOW_RANK
D_IN = 2 * D_S5 + D_SHIFT + D_RWKV
NORM_EPS = 1e-6
GN_EPS = 64e-5
GELU_C = math.sqrt(2.0 / math.pi)

LANES = 128
TM_IN = 1024
IN_SUB = 256
SUB_TILE_SKEW = 2
TM_OUT = 1024
OUT_SUB = 512
S5_CHUNK = 16
S5_BATCH = 2
S5_PIPE_SKEW = 6
S5_LANE_BLOCKS = D_S5 // LANES
S5_GROUPS_PER_BLOCK = LANES // S5_GROUP
S5_BLOCK_STATE = 2 * S5_GROUPS_PER_BLOCK * S5_STATE
RWKV_CHUNK = 64
RWKV_TB = 2048
RWKV_PREP_CHUNKS = 4
RWKV_RING = 3 * RWKV_PREP_CHUNKS
HEAD_PAIRS = D_RWKV // LANES
RWKV_QUADS = HEAD_PAIRS // 2
VMEM_LIMIT = 56 * 1024 * 1024


def _dot(a, b):
    return jnp.dot(a, b, preferred_element_type=F32)


def _dot_nt(a, b):
    return lax.dot_general(a, b, (((1,), (1,)), ((), ())), preferred_element_type=F32)


def _dot_tn(a, b):
    return lax.dot_general(a, b, (((0,), (0,)), ((), ())), preferred_element_type=F32)


def _split2(x):
    hi = x.astype(BF16)
    return hi, (x - hi.astype(F32)).astype(BF16)


def _head_sums(x, ones):
    return _dot(x.astype(BF16), ones)


def _dot_exact_lhs(m, x):
    hi, lo = _split2(x)
    return _dot(m, hi) + _dot(m, lo)


def _sigmoid(x):
    return 0.5 + 0.5 * jnp.tanh(0.5 * x)


def _head_ones(width):
    r = lax.broadcasted_iota(jnp.int32, (width, width), 0)
    c = lax.broadcasted_iota(jnp.int32, (width, width), 1)
    return (r // RWKV_HEAD) == (c // RWKV_HEAD)


def _alternating(*stage_generators):
    pending = list(stage_generators)
    while pending:
        for g in list(pending):
            if next(g, StopIteration) is StopIteration:
                pending.remove(g)
            else:
                yield


def _alternate(*stage_generators):
    for _ in _alternating(*stage_generators):
        pass


def _delayed(stage_generator, stages):
    for _ in range(stages):
        yield
    yield from stage_generator


def _inproj_kernel(x_ref, g_ref, win_ref, mu_ref, w2a2_ref, w0_ref, a0_ref, kk_ref, ka_ref,
                   u_ref, gate_ref, r_ref, lw_ref, k_ref, v_ref, kkn_ref, b_ref, carry_ref):
    t = pl.program_id(1)

    @pl.when(t == 0)
    def _():
        carry_ref[...] = jnp.zeros_like(carry_ref)

    ones = _head_ones(2 * LANES).astype(BF16)
    carries = {-1: carry_ref[...]}

    def sub_tile(s):
        rs = slice(IN_SUB * s, IN_SUB * (s + 1))
        x = x_ref[0, rs, :]
        ms = jnp.mean(x * x, axis=-1, keepdims=True)
        h = (x * lax.rsqrt(ms + NORM_EPS) * g_ref[...]).astype(BF16)

        def proj(lo, hi):
            return _dot(h, win_ref[:, lo:hi])

        yield
        u_ref[0, rs, :] = proj(0, D_S5)
        yield
        z5 = proj(D_S5, 2 * D_S5)
        gate_ref[0, rs, :D_S5] = (z5 * _sigmoid(z5)).astype(BF16)
        yield
        zr = proj(2 * D_S5 + D_SHIFT, D_IN)
        gate_ref[0, rs, D_S5:] = (zr * _sigmoid(zr)).astype(BF16)
        yield
        rw = proj(2 * D_S5, 2 * D_S5 + D_SHIFT)
        carries[s] = rw[IN_SUB - 1:IN_SUB, :]
        yield
        prev = pltpu.roll(rw, 1, axis=0)
        row = lax.broadcasted_iota(jnp.int32, rw.shape, 0)
        prev = jnp.where(row == 0, carries[s - 1], prev)
        sh = rw + (prev - rw) * mu_ref[...]
        r = sh[:, 0:D_RWKV]
        k = sh[:, D_RWKV:2 * D_RWKV]
        v = sh[:, 2 * D_RWKV:3 * D_RWKV]
        wa = sh[:, 3 * D_RWKV:]
        lane = lax.broadcasted_iota(jnp.int32, wa.shape, 1)
        wa = jnp.where(lane < LOW_RANK, jnp.tanh(wa), wa)
        za = _dot(wa.astype(BF16), w2a2_ref[...])
        r_ref[0, rs, :] = r.astype(BF16)
        v_ref[0, rs, :] = v.astype(BF16)
        yield
        lw_ref[0, rs, :] = -math.exp(-0.5) * _sigmoid(w0_ref[...] + za[:, :D_RWKV])
        iclr = _sigmoid(a0_ref[...] + za[:, D_RWKV:])
        k_ref[0, rs, :] = (k * (1.0 + (iclr - 1.0) * ka_ref[...])).astype(BF16)
        kk = k * kk_ref[...]
        sq = kk * kk
        ss = jnp.concatenate(
            [_head_sums(sq[:, 2 * LANES * q:2 * LANES * (q + 1)], ones) for q in range(RWKV_QUADS)], axis=1)
        yield
        kkn = kk * lax.rsqrt(jnp.maximum(ss, 1e-24))
        kkn_ref[0, rs, :] = kkn.astype(BF16)
        b_ref[0, rs, :] = (kkn * iclr).astype(BF16)

    n_sub = TM_IN // IN_SUB
    _alternate(*[_delayed(sub_tile(s), SUB_TILE_SKEW * s) for s in range(n_sub)])
    carry_ref[...] = carries[n_sub - 1]


def _inproj(x, norm_g, w_in, mu, w2a2, w0, a0, k_k, k_a):
    bsz, seqlen, _ = x.shape
    tok = lambda w: pl.BlockSpec((1, TM_IN, w), lambda b, t: (b, t, 0))
    full = lambda a: pl.BlockSpec(a.shape, lambda b, t: (0,) * a.ndim)
    out = lambda w, dt: jax.ShapeDtypeStruct((bsz, seqlen, w), dt)
    params = (norm_g, w_in, mu, w2a2, w0, a0, k_k, k_a)
    return pl.pallas_call(
        _inproj_kernel,
        grid=(bsz, seqlen // TM_IN),
        in_specs=[tok(D_MODEL)] + [full(a) for a in params],
        out_specs=[tok(D_S5), tok(2 * D_S5)] + [tok(D_RWKV)] * 6,
        out_shape=[out(D_S5, F32), out(2 * D_S5, BF16)]
        + [out(D_RWKV, F32 if i == 1 else BF16) for i in range(6)],
        scratch_shapes=[pltpu.VMEM((1, D_SHIFT), F32)],
        compiler_params=pltpu.CompilerParams(
            dimension_semantics=("arbitrary", "arbitrary"), vmem_limit_bytes=VMEM_LIMIT),
        name="inproj",
    )(x, *params)


def _s5_kernel(u_ref, ein_ref, toep_ref, fout_ref, lam_ref, d_ref, y_ref, e_ref, sp_ref):
    nch = e_ref.shape[1]
    half = S5_BLOCK_STATE // 2
    pairs = S5_CHUNK // 2
    lre = lam_ref[0, :, :half]
    lim = lam_ref[0, :, half:]
    d = d_ref[...]

    def pipeline(bi):
        xs = [u_ref[bi, pl.ds(j, nch, stride=S5_CHUNK), :] for j in range(S5_CHUNK)]
        xcat = jnp.concatenate([x.astype(BF16) for x in xs], axis=1)
        yield
        e_ref[bi] = _dot(xcat, ein_ref[0])
        yield
        intra = {}

        def scan():
            sre = sim = jnp.zeros((1, half), F32)
            for c in range(nch):
                sp_ref[bi, c:c + 1, :half] = sre
                sp_ref[bi, c:c + 1, half:] = sim
                e = e_ref[bi, c:c + 1, :]
                sre, sim = lre * sre - lim * sim + e[:, :half], lre * sim + lim * sre + e[:, half:]
                if (c + 1) % (nch // pairs) == 0:
                    yield

        def toeplitz():
            for tp in range(pairs):
                intra[tp] = _dot(xcat[:, :2 * LANES * (tp + 1)],
                                 toep_ref[0, 2 * LANES * (pairs - 1 - tp):, :])
                yield

        yield from _alternating(scan(), toeplitz())
        sp = sp_ref[bi].astype(BF16)
        for tp in range(pairs):
            y = intra[tp] + _dot(sp, fout_ref[0, :, 2 * LANES * tp:2 * LANES * (tp + 1)])
            for i in range(2):
                t = 2 * tp + i
                y_ref[bi, pl.ds(t, nch, stride=S5_CHUNK), :] = y[:, LANES * i:LANES * (i + 1)] + d * xs[t]
            yield

    _alternate(*[_delayed(pipeline(bi), S5_PIPE_SKEW * bi) for bi in range(S5_BATCH)])


def _cmul(ar, ai, br, bi):
    return ar * br - ai * bi, ar * bi + ai * br


def _block_diag_expand(compact, width, inner, transposed):
    k = compact.shape[1] if transposed else compact.shape[2]
    group_w = S5_GROUPS_PER_BLOCK * inner
    kk = jnp.arange(k)
    ww = jnp.arange(width)
    sel = ((kk[:, None] // inner == ww[None, :] // group_w)
           & (kk[:, None] % inner == ww[None, :] % inner)).astype(F32)
    wide_group = (ww % group_w) // inner
    if transposed:
        full = jnp.matmul(sel.T, compact)
        narrow_group = (jnp.arange(compact.shape[2]) % LANES) // S5_GROUP
        keep = wide_group[:, None] == narrow_group[None, :]
    else:
        full = jnp.matmul(compact, sel)
        narrow_group = (jnp.arange(compact.shape[1]) % LANES) // S5_GROUP
        keep = narrow_group[:, None] == wide_group[None, :]
    return jnp.where(keep, full, 0.0)


def _s5_operators(lam_re, lam_im, log_dt, b_re, b_im, c_re, c_im):
    c16, q4, gl = S5_CHUNK, S5_LANE_BLOCKS, S5_GROUPS_PER_BLOCK
    rows = c16 * LANES
    dt = jnp.exp(log_dt)[:, None]
    n = jnp.arange(c16 + 1, dtype=F32)[:, None, None]
    mag = jnp.exp(n * (lam_re * dt)[None])
    ang = n * (lam_im * dt)[None]
    pw_r, pw_i = mag * jnp.cos(ang), mag * jnp.sin(ang)
    den = lam_re * lam_re + lam_im * lam_im
    q_r, q_i = _cmul(pw_r[1] - 1.0, pw_i[1], lam_re / den, -lam_im / den)
    bt_r, bt_i = jnp.swapaxes(b_re, 1, 2), jnp.swapaxes(b_im, 1, 2)
    bb_r, bb_i = _cmul(q_r[:, None, :], q_i[:, None, :], bt_r, bt_i)
    e_r, e_i = _cmul(pw_r[:c16, :, None, :], pw_i[:c16, :, None, :], bb_r[None], bb_i[None])

    def by_block(a):
        a = a.reshape(a.shape[0], q4, gl * S5_GROUP, 2 * S5_STATE)
        return jnp.swapaxes(a, 0, 1).reshape(q4, -1, 2 * S5_STATE)

    ein = by_block(jnp.concatenate([e_r, e_i], axis=-1)[::-1])
    ein = _block_diag_expand(ein, S5_BLOCK_STATE, S5_STATE, False)

    f_r, f_i = _cmul(c_re[None], c_im[None], pw_r[1:, :, None, :], pw_i[1:, :, None, :])
    fo = jnp.swapaxes(by_block(jnp.concatenate([f_r, -f_i], axis=-1)), 1, 2)
    fo = _block_diag_expand(fo, S5_BLOCK_STATE, S5_STATE, True)

    ee = jnp.concatenate([e_r, -e_i], axis=-1)
    cc = jnp.concatenate([c_re, c_im], axis=-1).reshape(q4, LANES, 2 * S5_STATE)
    row_g = (jnp.arange(rows) % LANES) // S5_GROUP
    col_g = jnp.arange(LANES) // S5_GROUP
    halves = []
    for t2 in range(2):
        lags = [2 * d + t2 - j2 for d in reversed(range(c16 // 2)) for j2 in range(2)]
        a = jnp.stack([ee[lag] if lag >= 0 else jnp.zeros_like(ee[0]) for lag in lags])
        full = jnp.einsum('qrk,qck->qrc', by_block(a), cc, precision=lax.Precision.HIGHEST)
        halves.append(jnp.where(row_g[:, None] == col_g[None, :], full, 0.0))
    toep = jnp.concatenate(halves, axis=-1)

    lam16 = jnp.concatenate([pw_r[c16].reshape(q4, 1, gl * S5_STATE),
                             pw_i[c16].reshape(q4, 1, gl * S5_STATE)], axis=-1)
    return ein.astype(BF16), toep.astype(BF16), fo.astype(BF16), lam16


def _s5(u, ein, toep, fout, lam16, d):
    bsz, seqlen, _ = u.shape
    nch = seqlen // S5_CHUNK
    per_q = lambda a: pl.BlockSpec((1,) + a.shape[1:], lambda q, b: (q,) + (0,) * (a.ndim - 1))
    seq = pl.BlockSpec((S5_BATCH, seqlen, LANES), lambda q, b: (b, 0, q))
    state = pltpu.VMEM((S5_BATCH, nch, S5_BLOCK_STATE), F32)
    return pl.pallas_call(
        _s5_kernel,
        grid=(S5_LANE_BLOCKS, bsz // S5_BATCH),
        in_specs=[seq, per_q(ein), per_q(toep), per_q(fout), per_q(lam16),
                  pl.BlockSpec((1, LANES), lambda q, b: (0, q))],
        out_specs=seq,
        out_shape=jax.ShapeDtypeStruct(u.shape, F32),
        scratch_shapes=[state, state],
        compiler_params=pltpu.CompilerParams(
            dimension_semantics=("arbitrary", "arbitrary"), vmem_limit_bytes=VMEM_LIMIT),
        name="s5",
    )(u, ein, toep, fout, lam16, d)


def _stack_heads(x, lo):
    return jnp.concatenate([jnp.where(lo, x, 0.0), jnp.where(lo, 0.0, x)], axis=0)


def _pair_block_diag(x):
    zero = jnp.zeros((x.shape[0], LANES), x.dtype)
    return jnp.concatenate([jnp.concatenate([x[:, :LANES], zero], axis=1),
                            jnp.concatenate([zero, x[:, LANES:]], axis=1)], axis=0)


def _head_block_diag(x, head_id, order):
    return jnp.concatenate([jnp.where(head_id == h, x, 0.0) for h in order], axis=0)


def _rwkv_kernel(r_ref, lw_ref, k_ref, v_ref, kk_ref, b_ref, gate_ref, rk_ref, lnw_ref, lnb_ref,
                 y_ref, s_ref, wr_ref, w2_ref, bkt_ref, prb_ref, yv_ref, e2_ref, u_ref):
    tc = RWKV_CHUNK
    nchunks = RWKV_TB // tc
    qw = 2 * LANES

    @pl.when(pl.program_id(1) == 0)
    def _():
        s_ref[...] = jnp.zeros_like(s_ref)

    row = lax.broadcasted_iota(jnp.int32, (tc, qw), 0)
    lane = lax.broadcasted_iota(jnp.int32, (tc, qw), 1)
    col = jnp.bitwise_and(lane, RWKV_HEAD - 1)
    head_id = lane // RWKV_HEAD
    heads = tuple(range(qw // RWKV_HEAD))
    heads_swapped = (1, 0, 3, 2)
    eye_q = (row == col).astype(F32)
    row_p = lax.broadcasted_iota(jnp.int32, (tc, LANES), 0)
    lane_p = lax.broadcasted_iota(jnp.int32, (tc, LANES), 1)
    col_p = jnp.bitwise_and(lane_p, RWKV_HEAD - 1)
    lo_p = lane_p < RWKV_HEAD
    lo_p2 = lax.broadcasted_iota(jnp.int32, (2 * tc, LANES), 1) < RWKV_HEAD
    stril = row_p > col_p
    tril = row_p >= col_p
    state_bd = ((lax.broadcasted_iota(jnp.int32, (LANES, qw), 0) // RWKV_HEAD)
                == (jnp.bitwise_and(lax.broadcasted_iota(jnp.int32, (LANES, qw), 1), LANES - 1) // RWKV_HEAD))
    ones = _head_ones(qw).astype(BF16)
    tri = (lax.broadcasted_iota(jnp.int32, (tc, tc), 0)
           >= lax.broadcasted_iota(jnp.int32, (tc, tc), 1)).astype(BF16)
    bf = lambda t: t.astype(BF16)
    up, dn = slice(0, tc), slice(tc, 2 * tc)

    def prepare(chunks):
        units = []
        for c in chunks:
            rows = slice(c * tc, (c + 1) * tc)
            lw_all = lw_ref[0, rows, :]
            g_all = _dot_exact_lhs(tri, lw_all)
            for q in range(RWKV_QUADS):
                qs = slice(qw * q, qw * (q + 1))
                units.append(dict(slot=c % RWKV_RING, q=q, rows=rows, qs=qs, lw=lw_all[:, qs], g=g_all[:, qs]))
        yield
        for d in units:
            rows, qs, g = d["rows"], d["qs"], d["g"]
            m = 0.5 * g[tc - 1:tc, :]
            e_out = jnp.exp(m - g)
            em = jnp.exp(m)
            rt = r_ref[0, rows, qs] * jnp.exp(g - m)
            bt = b_ref[0, rows, qs] * e_out
            kt = k_ref[0, rows, qs] * e_out
            d["em"] = em
            d["at"] = bf(-kk_ref[0, rows, qs] * jnp.exp(g - d["lw"] - m))
            d["rt"], d["bt"], d["kt"] = bf(rt), bf(bt), bf(kt)
            d["rts"] = bf(rt * em)
            d["vsw"] = bf(_head_block_diag(v_ref[0, rows, qs], head_id, heads_swapped))
            bkt, e2 = [], []
            for j in range(2):
                ps = slice(LANES * j, LANES * (j + 1))
                bks = jnp.concatenate([bt[:, ps], kt[:, ps]], axis=0) * em[:, ps]
                bkt.append(bf(jnp.transpose(bks)))
                e2.append(jnp.transpose(jnp.broadcast_to(em[:, ps] * em[:, ps], (LANES, LANES))))
            bkt_ref[d["slot"], d["q"]] = jnp.concatenate(bkt, axis=1)
            e2_ref[d["slot"], d["q"]] = jnp.concatenate(e2, axis=1)
        yield
        for d in units:
            d["o"] = []
            for j in range(2):
                ps = slice(LANES * j, LANES * (j + 1))
                bk = jnp.concatenate([d["bt"][:, ps], d["kt"][:, ps]], axis=0)
                kb = jnp.concatenate([d["kt"][:, ps], d["bt"][:, ps]], axis=0)
                rhs = jnp.concatenate([jnp.where(lo_p2, bk, 0.0), jnp.where(lo_p2, 0.0, kb)], axis=0)
                lhs = jnp.concatenate([d["at"][:, ps], d["rt"][:, ps]], axis=0)
                d["o"].append(_dot_nt(lhs, rhs))
        yield
        for d in units:
            h0 = [o[:, :LANES] for o in d["o"]]
            h1 = [o[:, LANES:] for o in d["o"]]

            def pick(rws, first, second, mask):
                return jnp.concatenate(
                    [jnp.where(mask, jnp.where(lo_p, first[j][rws], second[j][rws]), 0.0) for j in range(2)],
                    axis=1)

            d["ap"] = pick(up, h0, h1, stril)
            d["p_ak"] = bf(pick(up, h1, h0, stril))
            p_rb = bf(pick(dn, h0, h1, tril))
            d["p_rk"] = bf(pick(dn, h1, h0, tril))
            prb_ref[d["slot"], d["q"]] = p_rb
            d["tinv"] = eye_q + d["ap"]
        yield
        levels = int(math.log2(tc)) - 1
        for d in units:
            d["ap"] = _dot(bf(d["ap"]), bf(_head_block_diag(d["ap"], head_id, heads)))
        for lvl in range(levels):
            yield
            for d in units:
                rhs = bf(_head_block_diag(d["ap"], head_id, heads))
                if lvl < levels - 1:
                    out = _dot(bf(jnp.concatenate([d["tinv"], d["ap"]], axis=0)), rhs)
                    d["tinv"], d["ap"] = d["tinv"] + out[:tc], out[tc:]
                else:
                    d["tinv"] = d["tinv"] + _dot(bf(d["tinv"]), rhs)
        yield
        for d in units:
            d["tinv"] = bf(d["tinv"])
            out = _dot(jnp.concatenate([d["p_ak"], d["p_rk"]], axis=0), d["vsw"])
            d["xv"] = out[:tc]
            yv_ref[d["slot"], d["q"]] = out[tc:]
        yield
        for d in units:
            w1s, w2 = [], []
            for j in range(2):
                ps = slice(LANES * j, LANES * (j + 1))
                rhs = jnp.concatenate([_stack_heads(d["at"][:, ps], lo_p),
                                       bf(_stack_heads(d["xv"][:, ps], lo_p))], axis=1)
                w = _dot(d["tinv"][:, ps], rhs)
                w1s.append(bf(w[:, :LANES] * d["em"][:, ps]))
                w2.append(w[:, LANES:])
            wr_ref[d["slot"], d["q"]] = jnp.concatenate([jnp.concatenate(w1s, axis=1), d["rts"]], axis=0)
            w2_ref[d["slot"], d["q"]] = jnp.concatenate(w2, axis=1)

    def recur_step(c):
        rows, slot = slice(c * tc, (c + 1) * tc), c % RWKV_RING
        units = [dict(q=q, qs=slice(qw * q, qw * (q + 1))) for q in range(RWKV_QUADS)]
        for d in units:
            d["h0"] = s_ref[d["q"]]
        for d in units:
            out = _dot(wr_ref[slot, d["q"]], _pair_block_diag(bf(d["h0"])))
            d["u"] = out[:tc] + w2_ref[slot, d["q"]]
            d["yr"] = out[tc:]
        yield
        for d in units:
            q = d["q"]
            uv = jnp.concatenate([bf(d["u"]), v_ref[0, rows, d["qs"]]], axis=0)
            upd = _dot(bkt_ref[slot, q], _pair_block_diag(uv))
            s_ref[q] = d["h0"] * e2_ref[slot, q] + jnp.where(state_bd, upd, 0.0)
            u_ref[slot, q] = bf(d["u"])
            yv_ref[slot, q] = yv_ref[slot, q] + d["yr"]

    def emit_step(c):
        rows, slot = slice(c * tc, (c + 1) * tc), c % RWKV_RING
        units = [dict(q=q, qs=slice(qw * q, qw * (q + 1))) for q in range(RWKV_QUADS)]
        for d in units:
            q = d["q"]
            d["y"] = yv_ref[slot, q] + _dot(prb_ref[slot, q], _head_block_diag(u_ref[slot, q], head_id, heads))
        yield
        stacked = []
        for d in units:
            qs = d["qs"]
            stacked += [d["y"], r_ref[0, rows, qs].astype(F32) * k_ref[0, rows, qs] * rk_ref[:, qs]]
        sums = _head_sums(jnp.concatenate(stacked, axis=0), ones)
        for i, d in enumerate(units):
            d["dev"] = d["y"] - sums[2 * i * tc:(2 * i + 1) * tc] * (1.0 / RWKV_HEAD)
            d["bonus"] = sums[(2 * i + 1) * tc:(2 * i + 2) * tc]
        yield
        var = _head_sums(jnp.concatenate([d["dev"] * d["dev"] for d in units], axis=0), ones)
        for i, d in enumerate(units):
            qs = d["qs"]
            rstd = lax.rsqrt(var[i * tc:(i + 1) * tc] * (1.0 / RWKV_HEAD) + GN_EPS)
            yn = d["dev"] * rstd * lnw_ref[:, qs] + lnb_ref[:, qs]
            y_ref[0, rows, qs] = bf((yn + d["bonus"] * v_ref[0, rows, qs]) * gate_ref[0, rows, qs])

    def chain(chunks):
        for c in chunks:
            yield from _alternating(*([emit_step(c - 1)] if c > 0 else []), recur_step(c))

    groups = [list(range(g, g + RWKV_PREP_CHUNKS)) for g in range(0, nchunks, RWKV_PREP_CHUNKS)]
    _alternate(prepare(groups[0]))
    for prev, nxt in zip(groups[:-1], groups[1:]):
        _alternate(prepare(nxt), chain(prev))
    _alternate(chain(groups[-1]))
    _alternate(emit_step(nchunks - 1))


def _rwkv(r, lw, k, v, kkn, b, gate, r_k, ln_w, ln_b):
    bsz, seqlen, _ = r.shape
    nchunks = RWKV_TB // RWKV_CHUNK
    tok = pl.BlockSpec((1, RWKV_TB, D_RWKV), lambda bi, t: (bi, t, 0))
    gate_spec = pl.BlockSpec((1, RWKV_TB, D_RWKV), lambda bi, t: (bi, t, 1))
    vec = pl.BlockSpec((1, D_RWKV), lambda bi, t: (0, 0))
    unit = lambda rows, cols, dt: pltpu.VMEM((RWKV_RING, RWKV_QUADS, rows, cols), dt)
    return pl.pallas_call(
        _rwkv_kernel,
        grid=(bsz, seqlen // RWKV_TB),
        in_specs=[tok] * 6 + [gate_spec] + [vec] * 3,
        out_specs=tok,
        out_shape=jax.ShapeDtypeStruct(r.shape, BF16),
        scratch_shapes=[
            pltpu.VMEM((RWKV_QUADS, LANES, 2 * LANES), F32),
            unit(2 * RWKV_CHUNK, 2 * LANES, BF16),
            unit(RWKV_CHUNK, 2 * LANES, F32),
            unit(LANES, 4 * RWKV_CHUNK, BF16),
            unit(RWKV_CHUNK, 2 * LANES, BF16),
            unit(RWKV_CHUNK, 2 * LANES, F32),
            unit(LANES, 2 * LANES, F32),
            unit(RWKV_CHUNK, 2 * LANES, BF16),
        ],
        compiler_params=pltpu.CompilerParams(
            dimension_semantics=("arbitrary", "arbitrary"), vmem_limit_bytes=VMEM_LIMIT),
        name="rwkv7",
    )(r, lw, k, v, kkn, b, gate, r_k, ln_w, ln_b)


def _out_kernel(y5_ref, gate_ref, yr_ref, x_ref, gluw_ref, glub_ref, wout_ref, fg_ref, o_ref):
    def sub_tile(s):
        rs = slice(OUT_SUB * s, OUT_SUB * (s + 1))
        y = y5_ref[0, rs, :]
        half = 0.5 * y
        y = half + half * jnp.tanh(y * (GELU_C + (GELU_C * 0.044715) * (y * y)))
        yield
        glu = _sigmoid(_dot(y.astype(BF16), gluw_ref[...]) + glub_ref[...])
        ys = y * glu * gate_ref[0, rs, :]
        ycat = jnp.concatenate([ys.astype(BF16), yr_ref[0, rs, :]], axis=-1)
        yield
        xn = x_ref[0, rs, :] + _dot(ycat, wout_ref[...])
        yield
        ms = jnp.mean(xn * xn, axis=-1, keepdims=True)
        o_ref[0, rs, :] = xn * lax.rsqrt(ms + NORM_EPS) * fg_ref[...]

    _alternate(*[_delayed(sub_tile(s), SUB_TILE_SKEW * s) for s in range(TM_OUT // OUT_SUB)])


def _outproj(y5, gate, yr, x, glu_w, glu_b, w_out, final_g):
    bsz, seqlen, _ = x.shape
    tok = lambda w: pl.BlockSpec((1, TM_OUT, w), lambda b, t: (b, t, 0))
    full = lambda a: pl.BlockSpec(a.shape, lambda b, t: (0,) * a.ndim)
    return pl.pallas_call(
        _out_kernel,
        grid=(bsz, seqlen // TM_OUT),
        in_specs=[tok(D_S5), tok(D_S5), tok(D_RWKV), tok(D_MODEL),
                  full(glu_w), full(glu_b), full(w_out), full(final_g)],
        out_specs=tok(D_MODEL),
        out_shape=jax.ShapeDtypeStruct(x.shape, F32),
        compiler_params=pltpu.CompilerParams(
            dimension_semantics=("arbitrary", "arbitrary"), vmem_limit_bytes=VMEM_LIMIT),
        name="outproj",
    )(y5, gate, yr, x, glu_w, glu_b, w_out, final_g)


@jax.jit
def _forward(x, norm_g, w_in, s5_lam_re, s5_lam_im, s5_log_dt, s5_b_re, s5_b_im, s5_c_re, s5_c_im,
             s5_d, s5_glu_w, s5_glu_b, rwkv_mu, rwkv_w0, rwkv_w2, rwkv_a0, rwkv_a2, rwkv_k_k, rwkv_k_a,
             rwkv_r_k, rwkv_ln_w, rwkv_ln_b, w_out, final_g):
    assert w_in.shape[0] == 1, "single-layer block"
    for l in range(1):
        zeros = jnp.zeros((LOW_RANK, D_RWKV), F32)
        w2a2 = jnp.concatenate(
            [jnp.concatenate([rwkv_w2[l], zeros], axis=1), jnp.concatenate([zeros, rwkv_a2[l]], axis=1)],
            axis=0).astype(BF16)
        row = lambda a: a.reshape(1, -1).astype(F32)
        u, gate, r, lw, k, v, kkn, b = _inproj(
            x, row(norm_g[l]), w_in[l].astype(BF16), row(rwkv_mu[l]), w2a2, row(rwkv_w0[l]),
            row(rwkv_a0[l]), row(rwkv_k_k[l]), row(rwkv_k_a[l]))
        ein, toep, fout, lam16 = _s5_operators(
            s5_lam_re[l], s5_lam_im[l], s5_log_dt[l], s5_b_re[l], s5_b_im[l], s5_c_re[l], s5_c_im[l])
        y5 = _s5(u, ein, toep, fout, lam16, row(s5_d[l]))
        yr = _rwkv(r, lw, k, v, kkn, b, gate, row(rwkv_r_k[l]), row(rwkv_ln_w[l]), row(rwkv_ln_b[l]))
        x = _outproj(y5, gate, yr, x, s5_glu_w[l].astype(BF16), row(s5_glu_b[l]),
                     w_out[l].astype(BF16), row(final_g))
    return x


def kernel(x, norm_g, w_in, s5_lam_re, s5_lam_im, s5_log_dt, s5_b_re, s5_b_im, s5_c_re, s5_c_im, s5_d, s5_glu_w, s5_glu_b, rwkv_mu, rwkv_w0, rwkv_w2, rwkv_a0, rwkv_a2, rwkv_k_k, rwkv_k_a, rwkv_r_k, rwkv_ln_w, rwkv_ln_b, w_out, final_g):
    return _forward(x, norm_g, w_in, s5_lam_re, s5_lam_im, s5_log_dt, s5_b_re, s5_b_im, s5_c_re, s5_c_im,
                    s5_d, s5_glu_w, s5_glu_b, rwkv_mu, rwkv_w0, rwkv_w2, rwkv_a0, rwkv_a2, rwkv_k_k,
                    rwkv_k_a, rwkv_r_k, rwkv_ln_w, rwkv_ln_b, w_out, final_g)
```

```python
import math

import jax
import jax.numpy as jnp
from jax import lax
from jax.experimental import pallas as pl
from jax.experimental.pallas import tpu as pltpu

F32 = jnp.float32
BF16 = jnp.bfloat16

D_MODEL = 1024
D_S5 = 512
D_RWKV = 512
S5_GROUP = 16
S5_GROUPS = 32
S5_STATE = 64
RWKV_HEAD = 64
LOW_RANK = 64
D_SHIFT = 3 * D_RWKV + 2 * LOW_RANK
D_IN = 2 * D_S5 + D_SHIFT + D_RWKV
NORM_EPS = 1e-6
GN_EPS = 64e-5
GELU_C = math.sqrt(2.0 / math.pi)

LANES = 128
TM_IN = 1024
IN_SUB = 256
SUB_TILE_SKEW = 2
TM_OUT = 1024
OUT_SUB = 512
S5_CHUNK = 16
S5_BATCH = 2
S5_PIPE_SKEW = 6
S5_LANE_BLOCKS = D_S5 // LANES
S5_GROUPS_PER_BLOCK = LANES // S5_GROUP
S5_BLOCK_STATE = 2 * S5_GROUPS_PER_BLOCK * S5_STATE
RWKV_CHUNK = 64
RWKV_TB = 2048
RWKV_PREP_CHUNKS = 4
RWKV_RING = 3 * RWKV_PREP_CHUNKS
HEAD_PAIRS = D_RWKV // LANES
RWKV_QUADS = HEAD_PAIRS // 2
VMEM_LIMIT = 56 * 1024 * 1024


def _dot(a, b):
    return jnp.dot(a, b, preferred_element_type=F32)


def _dot_nt(a, b):
    return lax.dot_general(a, b, (((1,), (1,)), ((), ())), preferred_element_type=F32)


def _dot_tn(a, b):
    return lax.dot_general(a, b, (((0,), (0,)), ((), ())), preferred_element_type=F32)


def _split2(x):
    hi = x.astype(BF16)
    return hi, (x - hi.astype(F32)).astype(BF16)


def _head_sums(x, ones):
    return _dot(x.astype(BF16), ones)


def _dot_exact_lhs(m, x):
    hi, lo = _split2(x)
    return _dot(m, hi) + _dot(m, lo)


def _sigmoid(x):
    return 0.5 + 0.5 * jnp.tanh(0.5 * x)


def _gelu(y):
    half = 0.5 * y
    return half + half * jnp.tanh(y * (GELU_C + (GELU_C * 0.044715) * (y * y)))


def _head_ones(width):
    r = lax.broadcasted_iota(jnp.int32, (width, width), 0)
    c = lax.broadcasted_iota(jnp.int32, (width, width), 1)
    return (r // RWKV_HEAD) == (c // RWKV_HEAD)


def _alternating(*stage_generators):
    pending = list(stage_generators)
    while pending:
        for g in list(pending):
            if next(g, StopIteration) is StopIteration:
                pending.remove(g)
            else:
                yield


def _alternate(*stage_generators):
    for _ in _alternating(*stage_generators):
        pass


def _delayed(stage_generator, stages):
    for _ in range(stages):
        yield
    yield from stage_generator


def _inproj_kernel(x_ref, g_ref, win_ref, mu_ref, w2a2_ref, w0_ref, a0_ref, kk_ref, ka_ref,
                   u_ref, gate_ref, r_ref, lw_ref, k_ref, v_ref, kkn_ref, b_ref, carry_ref):
    t = pl.program_id(1)

    @pl.when(t == 0)
    def _():
        carry_ref[...] = jnp.zeros_like(carry_ref)

    ones = _head_ones(2 * LANES).astype(BF16)
    carries = {-1: carry_ref[...]}

    def sub_tile(s):
        rs = slice(IN_SUB * s, IN_SUB * (s + 1))
        x = x_ref[0, rs, :]
        ms = jnp.mean(x * x, axis=-1, keepdims=True)
        h = (x * lax.rsqrt(ms + NORM_EPS) * g_ref[...]).astype(BF16)

        def proj(lo, hi):
            return _dot(h, win_ref[:, lo:hi])

        yield
        u_ref[0, rs, :] = proj(0, D_S5)
        yield
        z5 = proj(D_S5, 2 * D_S5)
        gate_ref[0, rs, :D_S5] = (z5 * _sigmoid(z5)).astype(BF16)
        yield
        zr = proj(2 * D_S5 + D_SHIFT, D_IN)
        gate_ref[0, rs, D_S5:] = (zr * _sigmoid(zr)).astype(BF16)
        yield
        rw = proj(2 * D_S5, 2 * D_S5 + D_SHIFT)
        carries[s] = rw[IN_SUB - 1:IN_SUB, :]
        yield
        prev = pltpu.roll(rw, 1, axis=0)
        row = lax.broadcasted_iota(jnp.int32, rw.shape, 0)
        prev = jnp.where(row == 0, carries[s - 1], prev)
        sh = rw + (prev - rw) * mu_ref[...]
        r = sh[:, 0:D_RWKV]
        k = sh[:, D_RWKV:2 * D_RWKV]
        v = sh[:, 2 * D_RWKV:3 * D_RWKV]
        wa = sh[:, 3 * D_RWKV:]
        lane = lax.broadcasted_iota(jnp.int32, wa.shape, 1)
        wa = jnp.where(lane < LOW_RANK, jnp.tanh(wa), wa)
        za = _dot(wa.astype(BF16), w2a2_ref[...])
        r_ref[0, rs, :] = r.astype(BF16)
        v_ref[0, rs, :] = v.astype(BF16)
        yield
        lw_ref[0, rs, :] = -math.exp(-0.5) * _sigmoid(w0_ref[...] + za[:, :D_RWKV])
        iclr = _sigmoid(a0_ref[...] + za[:, D_RWKV:])
        k_ref[0, rs, :] = (k * (1.0 + (iclr - 1.0) * ka_ref[...])).astype(BF16)
        kk = k * kk_ref[...]
        sq = kk * kk
        ss = jnp.concatenate(
            [_head_sums(sq[:, 2 * LANES * q:2 * LANES * (q + 1)], ones) for q in range(RWKV_QUADS)], axis=1)
        yield
        kkn = kk * lax.rsqrt(jnp.maximum(ss, 1e-24))
        kkn_ref[0, rs, :] = kkn.astype(BF16)
        b_ref[0, rs, :] = (kkn * iclr).astype(BF16)

    n_sub = TM_IN // IN_SUB
    _alternate(*[_delayed(sub_tile(s), SUB_TILE_SKEW * s) for s in range(n_sub)])
    carry_ref[...] = carries[n_sub - 1]


def _inproj(x, norm_g, w_in, mu, w2a2, w0, a0, k_k, k_a):
    bsz, seqlen, _ = x.shape
    tok = lambda w: pl.BlockSpec((1, TM_IN, w), lambda b, t: (b, t, 0))
    full = lambda a: pl.BlockSpec(a.shape, lambda b, t: (0,) * a.ndim)
    out = lambda w, dt: jax.ShapeDtypeStruct((bsz, seqlen, w), dt)
    params = (norm_g, w_in, mu, w2a2, w0, a0, k_k, k_a)
    return pl.pallas_call(
        _inproj_kernel,
        grid=(bsz, seqlen // TM_IN),
        in_specs=[tok(D_MODEL)] + [full(a) for a in params],
        out_specs=[tok(D_S5), tok(2 * D_S5)] + [tok(D_RWKV)] * 6,
        out_shape=[out(D_S5, F32), out(2 * D_S5, BF16)]
        + [out(D_RWKV, F32 if i == 1 else BF16) for i in range(6)],
        scratch_shapes=[pltpu.VMEM((1, D_SHIFT), F32)],
        compiler_params=pltpu.CompilerParams(
            dimension_semantics=("arbitrary", "arbitrary"), vmem_limit_bytes=VMEM_LIMIT),
        name="inproj",
    )(x, *params)


def _s5_kernel(u_ref, ein_ref, toep_ref, fout_ref, lam_ref, d_ref, y_ref, e_ref, sp_ref):
    nch = e_ref.shape[1]
    half = S5_BLOCK_STATE // 2
    pairs = S5_CHUNK // 2
    lre = lam_ref[0, :, :half]
    lim = lam_ref[0, :, half:]
    d = d_ref[...]

    def pipeline(bi):
        xs = [u_ref[bi, pl.ds(j, nch, stride=S5_CHUNK), :] for j in range(S5_CHUNK)]
        xcat = jnp.concatenate([x.astype(BF16) for x in xs], axis=1)
        yield
        e_ref[bi] = _dot(xcat, ein_ref[0])
        yield
        intra = {}

        def scan():
            sre = sim = jnp.zeros((1, half), F32)
            for c in range(nch):
                sp_ref[bi, c:c + 1, :half] = sre
                sp_ref[bi, c:c + 1, half:] = sim
                e = e_ref[bi, c:c + 1, :]
                sre, sim = lre * sre - lim * sim + e[:, :half], lre * sim + lim * sre + e[:, half:]
                if (c + 1) % (nch // pairs) == 0:
                    yield

        def toeplitz():
            for tp in range(pairs):
                intra[tp] = _dot(xcat[:, :2 * LANES * (tp + 1)],
                                 toep_ref[0, 2 * LANES * (pairs - 1 - tp):, :])
                yield

        yield from _alternating(scan(), toeplitz())
        sp = sp_ref[bi].astype(BF16)
        for tp in range(pairs):
            y = intra[tp] + _dot(sp, fout_ref[0, :, 2 * LANES * tp:2 * LANES * (tp + 1)])
            for i in range(2):
                t = 2 * tp + i
                y_ref[bi, pl.ds(t, nch, stride=S5_CHUNK), :] = _gelu(y[:, LANES * i:LANES * (i + 1)] + d * xs[t])
            yield

    _alternate(*[_delayed(pipeline(bi), S5_PIPE_SKEW * bi) for bi in range(S5_BATCH)])


def _cmul(ar, ai, br, bi):
    return ar * br - ai * bi, ar * bi + ai * br


def _block_diag_expand(compact, width, inner, transposed):
    k = compact.shape[1] if transposed else compact.shape[2]
    group_w = S5_GROUPS_PER_BLOCK * inner
    kk = jnp.arange(k)
    ww = jnp.arange(width)
    sel = ((kk[:, None] // inner == ww[None, :] // group_w)
           & (kk[:, None] % inner == ww[None, :] % inner)).astype(F32)
    wide_group = (ww % group_w) // inner
    if transposed:
        full = jnp.matmul(sel.T, compact)
        narrow_group = (jnp.arange(compact.shape[2]) % LANES) // S5_GROUP
        keep = wide_group[:, None] == narrow_group[None, :]
    else:
        full = jnp.matmul(compact, sel)
        narrow_group = (jnp.arange(compact.shape[1]) % LANES) // S5_GROUP
        keep = narrow_group[:, None] == wide_group[None, :]
    return jnp.where(keep, full, 0.0)


def _s5_operators(lam_re, lam_im, log_dt, b_re, b_im, c_re, c_im):
    c16, q4, gl = S5_CHUNK, S5_LANE_BLOCKS, S5_GROUPS_PER_BLOCK
    rows = c16 * LANES
    dt = jnp.exp(log_dt)[:, None]
    n = jnp.arange(c16 + 1, dtype=F32)[:, None, None]
    mag = jnp.exp(n * (lam_re * dt)[None])
    ang = n * (lam_im * dt)[None]
    pw_r, pw_i = mag * jnp.cos(ang), mag * jnp.sin(ang)
    den = lam_re * lam_re + lam_im * lam_im
    q_r, q_i = _cmul(pw_r[1] - 1.0, pw_i[1], lam_re / den, -lam_im / den)
    bt_r, bt_i = jnp.swapaxes(b_re, 1, 2), jnp.swapaxes(b_im, 1, 2)
    bb_r, bb_i = _cmul(q_r[:, None, :], q_i[:, None, :], bt_r, bt_i)
    e_r, e_i = _cmul(pw_r[:c16, :, None, :], pw_i[:c16, :, None, :], bb_r[None], bb_i[None])

    def by_block(a):
        a = a.reshape(a.shape[0], q4, gl * S5_GROUP, 2 * S5_STATE)
        return jnp.swapaxes(a, 0, 1).reshape(q4, -1, 2 * S5_STATE)

    ein = by_block(jnp.concatenate([e_r, e_i], axis=-1)[::-1])
    ein = _block_diag_expand(ein, S5_BLOCK_STATE, S5_STATE, False)

    f_r, f_i = _cmul(c_re[None], c_im[None], pw_r[1:, :, None, :], pw_i[1:, :, None, :])
    fo = jnp.swapaxes(by_block(jnp.concatenate([f_r, -f_i], axis=-1)), 1, 2)
    fo = _block_diag_expand(fo, S5_BLOCK_STATE, S5_STATE, True)

    ee = jnp.concatenate([e_r, -e_i], axis=-1)
    cc = jnp.concatenate([c_re, c_im], axis=-1).reshape(q4, LANES, 2 * S5_STATE)
    row_g = (jnp.arange(rows) % LANES) // S5_GROUP
    col_g = jnp.arange(LANES) // S5_GROUP
    halves = []
    for t2 in range(2):
        lags = [2 * d + t2 - j2 for d in reversed(range(c16 // 2)) for j2 in range(2)]
        a = jnp.stack([ee[lag] if lag >= 0 else jnp.zeros_like(ee[0]) for lag in lags])
        full = jnp.einsum('qrk,qck->qrc', by_block(a), cc, precision=lax.Precision.HIGHEST)
        halves.append(jnp.where(row_g[:, None] == col_g[None, :], full, 0.0))
    toep = jnp.concatenate(halves, axis=-1)

    lam16 = jnp.concatenate([pw_r[c16].reshape(q4, 1, gl * S5_STATE),
                             pw_i[c16].reshape(q4, 1, gl * S5_STATE)], axis=-1)
    return ein.astype(BF16), toep.astype(BF16), fo.astype(BF16), lam16


def _s5(u, ein, toep, fout, lam16, d):
    bsz, seqlen, _ = u.shape
    nch = seqlen // S5_CHUNK
    per_q = lambda a: pl.BlockSpec((1,) + a.shape[1:], lambda q, b: (q,) + (0,) * (a.ndim - 1))
    seq = pl.BlockSpec((S5_BATCH, seqlen, LANES), lambda q, b: (b, 0, q))
    state = pltpu.VMEM((S5_BATCH, nch, S5_BLOCK_STATE), F32)
    return pl.pallas_call(
        _s5_kernel,
        grid=(S5_LANE_BLOCKS, bsz // S5_BATCH),
        in_specs=[seq, per_q(ein), per_q(toep), per_q(fout), per_q(lam16),
                  pl.BlockSpec((1, LANES), lambda q, b: (0, q))],
        out_specs=seq,
        out_shape=jax.ShapeDtypeStruct(u.shape, F32),
        scratch_shapes=[state, state],
        compiler_params=pltpu.CompilerParams(
            dimension_semantics=("arbitrary", "arbitrary"), vmem_limit_bytes=VMEM_LIMIT),
        name="s5",
    )(u, ein, toep, fout, lam16, d)


def _stack_heads(x, lo):
    return jnp.concatenate([jnp.where(lo, x, 0.0), jnp.where(lo, 0.0, x)], axis=0)


def _pair_block_diag(x):
    zero = jnp.zeros((x.shape[0], LANES), x.dtype)
    return jnp.concatenate([jnp.concatenate([x[:, :LANES], zero], axis=1),
                            jnp.concatenate([zero, x[:, LANES:]], axis=1)], axis=0)


def _head_block_diag(x, head_id, order):
    return jnp.concatenate([jnp.where(head_id == h, x, 0.0) for h in order], axis=0)


def _rwkv_kernel(r_ref, lw_ref, k_ref, v_ref, kk_ref, b_ref, gate_ref, rk_ref, lnw_ref, lnb_ref,
                 y_ref, s_ref, wr_ref, w2_ref, bkt_ref, prb_ref, yv_ref, e2_ref, u_ref):
    tc = RWKV_CHUNK
    nchunks = RWKV_TB // tc
    qw = 2 * LANES

    @pl.when(pl.program_id(1) == 0)
    def _():
        s_ref[...] = jnp.zeros_like(s_ref)

    row = lax.broadcasted_iota(jnp.int32, (tc, qw), 0)
    lane = lax.broadcasted_iota(jnp.int32, (tc, qw), 1)
    col = jnp.bitwise_and(lane, RWKV_HEAD - 1)
    head_id = lane // RWKV_HEAD
    heads = tuple(range(qw // RWKV_HEAD))
    heads_swapped = (1, 0, 3, 2)
    eye_q = (row == col).astype(F32)
    row_p = lax.broadcasted_iota(jnp.int32, (tc, LANES), 0)
    lane_p = lax.broadcasted_iota(jnp.int32, (tc, LANES), 1)
    col_p = jnp.bitwise_and(lane_p, RWKV_HEAD - 1)
    lo_p = lane_p < RWKV_HEAD
    lo_p2 = lax.broadcasted_iota(jnp.int32, (2 * tc, LANES), 1) < RWKV_HEAD
    stril = row_p > col_p
    tril = row_p >= col_p
    state_bd = ((lax.broadcasted_iota(jnp.int32, (LANES, qw), 0) // RWKV_HEAD)
                == (jnp.bitwise_and(lax.broadcasted_iota(jnp.int32, (LANES, qw), 1), LANES - 1) // RWKV_HEAD))
    ones = _head_ones(qw).astype(BF16)
    tri = (lax.broadcasted_iota(jnp.int32, (tc, tc), 0)
           >= lax.broadcasted_iota(jnp.int32, (tc, tc), 1)).astype(BF16)
    bf = lambda t: t.astype(BF16)
    up, dn = slice(0, tc), slice(tc, 2 * tc)

    def prepare(chunks):
        units = []
        for c in chunks:
            rows = slice(c * tc, (c + 1) * tc)
            lw_all = lw_ref[0, rows, :]
            g_all = _dot_exact_lhs(tri, lw_all)
            for q in range(RWKV_QUADS):
                qs = slice(qw * q, qw * (q + 1))
                units.append(dict(slot=c % RWKV_RING, q=q, rows=rows, qs=qs, lw=lw_all[:, qs], g=g_all[:, qs]))
        yield
        for d in units:
            rows, qs, g = d["rows"], d["qs"], d["g"]
            m = 0.5 * g[tc - 1:tc, :]
            e_out = jnp.exp(m - g)
            em = jnp.exp(m)
            rt = r_ref[0, rows, qs] * jnp.exp(g - m)
            bt = b_ref[0, rows, qs] * e_out
            kt = k_ref[0, rows, qs] * e_out
            d["em"] = em
            d["at"] = bf(-kk_ref[0, rows, qs] * jnp.exp(g - d["lw"] - m))
            d["rt"], d["bt"], d["kt"] = bf(rt), bf(bt), bf(kt)
            d["rts"] = bf(rt * em)
            d["vsw"] = bf(_head_block_diag(v_ref[0, rows, qs], head_id, heads_swapped))
            bkt, e2 = [], []
            for j in range(2):
                ps = slice(LANES * j, LANES * (j + 1))
                bks = jnp.concatenate([bt[:, ps], kt[:, ps]], axis=0) * em[:, ps]
                bkt.append(bf(jnp.transpose(bks)))
                e2.append(jnp.transpose(jnp.broadcast_to(em[:, ps] * em[:, ps], (LANES, LANES))))
            bkt_ref[d["slot"], d["q"]] = jnp.concatenate(bkt, axis=1)
            e2_ref[d["slot"], d["q"]] = jnp.concatenate(e2, axis=1)
        yield
        for d in units:
            d["o"] = []
            for j in range(2):
                ps = slice(LANES * j, LANES * (j + 1))
                bk = jnp.concatenate([d["bt"][:, ps], d["kt"][:, ps]], axis=0)
                kb = jnp.concatenate([d["kt"][:, ps], d["bt"][:, ps]], axis=0)
                rhs = jnp.concatenate([jnp.where(lo_p2, bk, 0.0), jnp.where(lo_p2, 0.0, kb)], axis=0)
                lhs = jnp.concatenate([d["at"][:, ps], d["rt"][:, ps]], axis=0)
                d["o"].append(_dot_nt(lhs, rhs))
        yield
        for d in units:
            h0 = [o[:, :LANES] for o in d["o"]]
            h1 = [o[:, LANES:] for o in d["o"]]

            def pick(rws, first, second, mask):
                return jnp.concatenate(
                    [jnp.where(mask, jnp.where(lo_p, first[j][rws], second[j][rws]), 0.0) for j in range(2)],
                    axis=1)

            d["ap"] = pick(up, h0, h1, stril)
            d["p_ak"] = bf(pick(up, h1, h0, stril))
            p_rb = bf(pick(dn, h0, h1, tril))
            d["p_rk"] = bf(pick(dn, h1, h0, tril))
            prb_ref[d["slot"], d["q"]] = p_rb
            d["tinv"] = eye_q + d["ap"]
        yield
        levels = int(math.log2(tc)) - 1
        for d in units:
            d["ap"] = _dot(bf(d["ap"]), bf(_head_block_diag(d["ap"], head_id, heads)))
        for lvl in range(levels):
            yield
            for d in units:
                rhs = bf(_head_block_diag(d["ap"], head_id, heads))
                if lvl < levels - 1:
                    out = _dot(bf(jnp.concatenate([d["tinv"], d["ap"]], axis=0)), rhs)
                    d["tinv"], d["ap"] = d["tinv"] + out[:tc], out[tc:]
                else:
                    d["tinv"] = d["tinv"] + _dot(bf(d["tinv"]), rhs)
        yield
        for d in units:
            d["tinv"] = bf(d["tinv"])
            out = _dot(jnp.concatenate([d["p_ak"], d["p_rk"]], axis=0), d["vsw"])
            d["xv"] = out[:tc]
            yv_ref[d["slot"], d["q"]] = out[tc:]
        yield
        for d in units:
            w1s, w2 = [], []
            for j in range(2):
                ps = slice(LANES * j, LANES * (j + 1))
                rhs = jnp.concatenate([_stack_heads(d["at"][:, ps], lo_p),
                                       bf(_stack_heads(d["xv"][:, ps], lo_p))], axis=1)
                w = _dot(d["tinv"][:, ps], rhs)
                w1s.append(bf(w[:, :LANES] * d["em"][:, ps]))
                w2.append(w[:, LANES:])
            wr_ref[d["slot"], d["q"]] = jnp.concatenate([jnp.concatenate(w1s, axis=1), d["rts"]], axis=0)
            w2_ref[d["slot"], d["q"]] = jnp.concatenate(w2, axis=1)

    def recur_step(c):
        rows, slot = slice(c * tc, (c + 1) * tc), c % RWKV_RING
        units = [dict(q=q, qs=slice(qw * q, qw * (q + 1))) for q in range(RWKV_QUADS)]
        for d in units:
            d["h0"] = s_ref[d["q"]]
        for d in units:
            out = _dot(wr_ref[slot, d["q"]], _pair_block_diag(bf(d["h0"])))
            d["u"] = out[:tc] + w2_ref[slot, d["q"]]
            d["yr"] = out[tc:]
        yield
        for d in units:
            q = d["q"]
            uv = jnp.concatenate([bf(d["u"]), v_ref[0, rows, d["qs"]]], axis=0)
            upd = _dot(bkt_ref[slot, q], _pair_block_diag(uv))
            s_ref[q] = d["h0"] * e2_ref[slot, q] + jnp.where(state_bd, upd, 0.0)
            u_ref[slot, q] = bf(d["u"])
            yv_ref[slot, q] = yv_ref[slot, q] + d["yr"]

    def emit_step(c):
        rows, slot = slice(c * tc, (c + 1) * tc), c % RWKV_RING
        units = [dict(q=q, qs=slice(qw * q, qw * (q + 1))) for q in range(RWKV_QUADS)]
        for d in units:
            q = d["q"]
            d["y"] = yv_ref[slot, q] + _dot(prb_ref[slot, q], _head_block_diag(u_ref[slot, q], head_id, heads))
        yield
        stacked = []
        for d in units:
            qs = d["qs"]
            stacked += [d["y"], r_ref[0, rows, qs].astype(F32) * k_ref[0, rows, qs] * rk_ref[:, qs]]
        sums = _head_sums(jnp.concatenate(stacked, axis=0), ones)
        for i, d in enumerate(units):
            d["dev"] = d["y"] - sums[2 * i * tc:(2 * i + 1) * tc] * (1.0 / RWKV_HEAD)
            d["bonus"] = sums[(2 * i + 1) * tc:(2 * i + 2) * tc]
        yield
        var = _head_sums(jnp.concatenate([d["dev"] * d["dev"] for d in units], axis=0), ones)
        for i, d in enumerate(units):
            qs = d["qs"]
            rstd = lax.rsqrt(var[i * tc:(i + 1) * tc] * (1.0 / RWKV_HEAD) + GN_EPS)
            yn = d["dev"] * rstd * lnw_ref[:, qs] + lnb_ref[:, qs]
            y_ref[0, rows, qs] = bf((yn + d["bonus"] * v_ref[0, rows, qs]) * gate_ref[0, rows, qs])

    def chain(chunks):
        for c in chunks:
            yield from _alternating(*([emit_step(c - 1)] if c > 0 else []), recur_step(c))

    groups = [list(range(g, g + RWKV_PREP_CHUNKS)) for g in range(0, nchunks, RWKV_PREP_CHUNKS)]
    _alternate(prepare(groups[0]))
    for prev, nxt in zip(groups[:-1], groups[1:]):
        _alternate(prepare(nxt), chain(prev))
    _alternate(chain(groups[-1]))
    _alternate(emit_step(nchunks - 1))


def _rwkv(r, lw, k, v, kkn, b, gate, r_k, ln_w, ln_b):
    bsz, seqlen, _ = r.shape
    nchunks = RWKV_TB // RWKV_CHUNK
    tok = pl.BlockSpec((1, RWKV_TB, D_RWKV), lambda bi, t: (bi, t, 0))
    gate_spec = pl.BlockSpec((1, RWKV_TB, D_RWKV), lambda bi, t: (bi, t, 1))
    vec = pl.BlockSpec((1, D_RWKV), lambda bi, t: (0, 0))
    unit = lambda rows, cols, dt: pltpu.VMEM((RWKV_RING, RWKV_QUADS, rows, cols), dt)
    return pl.pallas_call(
        _rwkv_kernel,
        grid=(bsz, seqlen // RWKV_TB),
        in_specs=[tok] * 6 + [gate_spec] + [vec] * 3,
        out_specs=tok,
        out_shape=jax.ShapeDtypeStruct(r.shape, BF16),
        scratch_shapes=[
            pltpu.VMEM((RWKV_QUADS, LANES, 2 * LANES), F32),
            unit(2 * RWKV_CHUNK, 2 * LANES, BF16),
            unit(RWKV_CHUNK, 2 * LANES, F32),
            unit(LANES, 4 * RWKV_CHUNK, BF16),
            unit(RWKV_CHUNK, 2 * LANES, BF16),
            unit(RWKV_CHUNK, 2 * LANES, F32),
            unit(LANES, 2 * LANES, F32),
            unit(RWKV_CHUNK, 2 * LANES, BF16),
        ],
        compiler_params=pltpu.CompilerParams(
            dimension_semantics=("arbitrary", "arbitrary"), vmem_limit_bytes=VMEM_LIMIT),
        name="rwkv7",
    )(r, lw, k, v, kkn, b, gate, r_k, ln_w, ln_b)


def _out_kernel(y5_ref, gate_ref, yr_ref, x_ref, gluw_ref, glub_ref, wout_ref, fg_ref, o_ref):
    def sub_tile(s):
        rs = slice(OUT_SUB * s, OUT_SUB * (s + 1))
        y = y5_ref[0, rs, :]
        glu = _sigmoid(_dot(y.astype(BF16), gluw_ref[...]) + glub_ref[...])
        ys = y * glu * gate_ref[0, rs, :]
        ycat = jnp.concatenate([ys.astype(BF16), yr_ref[0, rs, :]], axis=-1)
        yield
        xn = x_ref[0, rs, :] + _dot(ycat, wout_ref[...])
        yield
        ms = jnp.mean(xn * xn, axis=-1, keepdims=True)
        o_ref[0, rs, :] = xn * lax.rsqrt(ms + NORM_EPS) * fg_ref[...]

    _alternate(*[_delayed(sub_tile(s), SUB_TILE_SKEW * s) for s in range(TM_OUT // OUT_SUB)])


def _outproj(y5, gate, yr, x, glu_w, glu_b, w_out, final_g):
    bsz, seqlen, _ = x.shape
    tok = lambda w: pl.BlockSpec((1, TM_OUT, w), lambda b, t: (b, t, 0))
    full = lambda a: pl.BlockSpec(a.shape, lambda b, t: (0,) * a.ndim)
    return pl.pallas_call(
        _out_kernel,
        grid=(bsz, seqlen // TM_OUT),
        in_specs=[tok(D_S5), tok(D_S5), tok(D_RWKV), tok(D_MODEL),
                  full(glu_w), full(glu_b), full(w_out), full(final_g)],
        out_specs=tok(D_MODEL),
        out_shape=jax.ShapeDtypeStruct(x.shape, F32),
        compiler_params=pltpu.CompilerParams(
            dimension_semantics=("arbitrary", "arbitrary"), vmem_limit_bytes=VMEM_LIMIT),
        name="outproj",
    )(y5, gate, yr, x, glu_w, glu_b, w_out, final_g)


@jax.jit
def _forward(x, norm_g, w_in, s5_lam_re, s5_lam_im, s5_log_dt, s5_b_re, s5_b_im, s5_c_re, s5_c_im,
             s5_d, s5_glu_w, s5_glu_b, rwkv_mu, rwkv_w0, rwkv_w2, rwkv_a0, rwkv_a2, rwkv_k_k, rwkv_k_a,
             rwkv_r_k, rwkv_ln_w, rwkv_ln_b, w_out, final_g):
    assert w_in.shape[0] == 1, "single-layer block"
    for l in range(1):
        zeros = jnp.zeros((LOW_RANK, D_RWKV), F32)
        w2a2 = jnp.concatenate(
            [jnp.concatenate([rwkv_w2[l], zeros], axis=1), jnp.concatenate([zeros, rwkv_a2[l]], axis=1)],
            axis=0).astype(BF16)
        row = lambda a: a.reshape(1, -1).astype(F32)
        u, gate, r, lw, k, v, kkn, b = _inproj(
            x, row(norm_g[l]), w_in[l].astype(BF16), row(rwkv_mu[l]), w2a2, row(rwkv_w0[l]),
            row(rwkv_a0[l]), row(rwkv_k_k[l]), row(rwkv_k_a[l]))
        ein, toep, fout, lam16 = _s5_operators(
            s5_lam_re[l], s5_lam_im[l], s5_log_dt[l], s5_b_re[l], s5_b_im[l], s5_c_re[l], s5_c_im[l])
        y5 = _s5(u, ein, toep, fout, lam16, row(s5_d[l]))
        yr = _rwkv(r, lw, k, v, kkn, b, gate, row(rwkv_r_k[l]), row(rwkv_ln_w[l]), row(rwkv_ln_b[l]))
        x = _outproj(y5, gate, yr, x, s5_glu_w[l].astype(BF16), row(s5_glu_b[l]),
                     w_out[l].astype(BF16), row(final_g))
    return x


def kernel(x, norm_g, w_in, s5_lam_re, s5_lam_im, s5_log_dt, s5_b_re, s5_b_im, s5_c_re, s5_c_im, s5_d, s5_glu_w, s5_glu_b, rwkv_mu, rwkv_w0, rwkv_w2, rwkv_a0, rwkv_a2, rwkv_k_k, rwkv_k_a, rwkv_r_k, rwkv_ln_w, rwkv_ln_b, w_out, final_g):
    return _forward(x, norm_g, w_in, s5_lam_re, s5_lam_im, s5_log_dt, s5_b_re, s5_b_im, s5_c_re, s5_c_im,
                    s5_d, s5_glu_w, s5_glu_b, rwkv_mu, rwkv_w0, rwkv_w2, rwkv_a0, rwkv_a2, rwkv_k_k,
                    rwkv_k_a, rwkv_r_k, rwkv_ln_w, rwkv_ln_b, w_out, final_g)
```

```python
import math

import jax
import jax.numpy as jnp
from jax import lax
from jax.experimental import pallas as pl
from jax.experimental.pallas import tpu as pltpu

F32 = jnp.float32
BF16 = jnp.bfloat16

D_MODEL = 1024
D_S5 = 512
D_RWKV = 512
S5_GROUP = 16
S5_GROUPS = 32
S5_STATE = 64
RWKV_HEAD = 64
LOW_RANK = 64
D_SHIFT = 3 * D_RWKV + 2 * LOW_RANK
D_IN = 2 * D_S5 + D_SHIFT + D_RWKV
NORM_EPS = 1e-6
GN_EPS = 64e-5
GELU_C = math.sqrt(2.0 / math.pi)

LANES = 128
TM_IN = 1024
IN_SUB = 256
SUB_TILE_SKEW = 2
TM_OUT = 1024
OUT_SUB = 512
S5_CHUNK = 16
S5_BATCH = 2
S5_PIPE_SKEW = 6
S5_LANE_BLOCKS = D_S5 // LANES
S5_GROUPS_PER_BLOCK = LANES // S5_GROUP
S5_BLOCK_STATE = 2 * S5_GROUPS_PER_BLOCK * S5_STATE
RWKV_CHUNK = 64
RWKV_TB = 2048
RWKV_PREP_CHUNKS = 4
RWKV_RING = 3 * RWKV_PREP_CHUNKS
HEAD_PAIRS = D_RWKV // LANES
RWKV_QUADS = HEAD_PAIRS // 2
VMEM_LIMIT = 56 * 1024 * 1024


def _dot(a, b):
    return jnp.dot(a, b, preferred_element_type=F32)


def _dot_nt(a, b):
    return lax.dot_general(a, b, (((1,), (1,)), ((), ())), preferred_element_type=F32)


def _dot_tn(a, b):
    return lax.dot_general(a, b, (((0,), (0,)), ((), ())), preferred_element_type=F32)


def _split2(x):
    hi = x.astype(BF16)
    return hi, (x - hi.astype(F32)).astype(BF16)


def _head_sums(x, ones):
    return _dot(x.astype(BF16), ones)


def _dot_exact_lhs(m, x):
    hi, lo = _split2(x)
    return _dot(m, hi) + _dot(m, lo)


def _sigmoid(x):
    return 0.5 + 0.5 * jnp.tanh(0.5 * x)


def _head_ones(width):
    r = lax.broadcasted_iota(jnp.int32, (width, width), 0)
    c = lax.broadcasted_iota(jnp.int32, (width, width), 1)
    return (r // RWKV_HEAD) == (c // RWKV_HEAD)


def _alternating(*stage_generators):
    pending = list(stage_generators)
    while pending:
        for g in list(pending):
            if next(g, StopIteration) is StopIteration:
                pending.remove(g)
            else:
                yield


def _alternate(*stage_generators):
    for _ in _alternating(*stage_generators):
        pass


def _delayed(stage_generator, stages):
    for _ in range(stages):
        yield
    yield from stage_generator


def _inproj_kernel(x_ref, g_ref, win_ref, mu_ref, w2a2_ref, w0_ref, a0_ref, kk_ref, ka_ref,
                   u_ref, gate_ref, r_ref, lw_ref, k_ref, v_ref, kkn_ref, b_ref, carry_ref):
    t = pl.program_id(1)

    @pl.when(t == 0)
    def _():
        carry_ref[...] = jnp.zeros_like(carry_ref)

    ones = _head_ones(2 * LANES).astype(BF16)
    carries = {-1: carry_ref[...]}

    def sub_tile(s):
        rs = slice(IN_SUB * s, IN_SUB * (s + 1))
        x = x_ref[0, rs, :]
        ms = jnp.mean(x * x, axis=-1, keepdims=True)
        h = (x * lax.rsqrt(ms + NORM_EPS) * g_ref[...]).astype(BF16)

        def proj(lo, hi):
            return _dot(h, win_ref[:, lo:hi])

        yield
        u_ref[0, rs, :] = proj(0, D_S5)
        yield
        z5 = proj(D_S5, 2 * D_S5)
        gate_ref[0, rs, :D_S5] = (z5 * _sigmoid(z5)).astype(BF16)
        yield
        zr = proj(2 * D_S5 + D_SHIFT, D_IN)
        gate_ref[0, rs, D_S5:] = (zr * _sigmoid(zr)).astype(BF16)
        yield
        rw = proj(2 * D_S5, 2 * D_S5 + D_SHIFT)
        carries[s] = rw[IN_SUB - 1:IN_SUB, :]
        yield
        prev = pltpu.roll(rw, 1, axis=0)
        row = lax.broadcasted_iota(jnp.int32, rw.shape, 0)
        prev = jnp.where(row == 0, carries[s - 1], prev)
        sh = rw + (prev - rw) * mu_ref[...]
        r = sh[:, 0:D_RWKV]
        k = sh[:, D_RWKV:2 * D_RWKV]
        v = sh[:, 2 * D_RWKV:3 * D_RWKV]
        wa = sh[:, 3 * D_RWKV:]
        lane = lax.broadcasted_iota(jnp.int32, wa.shape, 1)
        wa = jnp.where(lane < LOW_RANK, jnp.tanh(wa), wa)
        za = _dot(wa.astype(BF16), w2a2_ref[...])
        r_ref[0, rs, :] = r.astype(BF16)
        v_ref[0, rs, :] = v.astype(BF16)
        yield
        lw_ref[0, rs, :] = -math.exp(-0.5) * _sigmoid(w0_ref[...] + za[:, :D_RWKV])
        iclr = _sigmoid(a0_ref[...] + za[:, D_RWKV:])
        k_ref[0, rs, :] = (k * (1.0 + (iclr - 1.0) * ka_ref[...])).astype(BF16)
        kk = k * kk_ref[...]
        sq = kk * kk
        ss = jnp.concatenate(
            [_head_sums(sq[:, 2 * LANES * q:2 * LANES * (q + 1)], ones) for q in range(RWKV_QUADS)], axis=1)
        yield
        kkn = kk * lax.rsqrt(jnp.maximum(ss, 1e-24))
        kkn_ref[0, rs, :] = kkn.astype(BF16)
        b_ref[0, rs, :] = (kkn * iclr).astype(BF16)

    n_sub = TM_IN // IN_SUB
    _alternate(*[_delayed(sub_tile(s), SUB_TILE_SKEW * s) for s in range(n_sub)])
    carry_ref[...] = carries[n_sub - 1]


def _inproj(x, norm_g, w_in, mu, w2a2, w0, a0, k_k, k_a):
    bsz, seqlen, _ = x.shape
    tok = lambda w: pl.BlockSpec((1, TM_IN, w), lambda b, t: (b, t, 0))
    full = lambda a: pl.BlockSpec(a.shape, lambda b, t: (0,) * a.ndim)
    out = lambda w, dt: jax.ShapeDtypeStruct((bsz, seqlen, w), dt)
    params = (norm_g, w_in, mu, w2a2, w0, a0, k_k, k_a)
    return pl.pallas_call(
        _inproj_kernel,
        grid=(bsz, seqlen // TM_IN),
        in_specs=[tok(D_MODEL)] + [full(a) for a in params],
        out_specs=[tok(D_S5), tok(2 * D_S5)] + [tok(D_RWKV)] * 6,
        out_shape=[out(D_S5, F32), out(2 * D_S5, BF16)]
        + [out(D_RWKV, F32 if i == 1 else BF16) for i in range(6)],
        scratch_shapes=[pltpu.VMEM((1, D_SHIFT), F32)],
        compiler_params=pltpu.CompilerParams(
            dimension_semantics=("arbitrary", "arbitrary"), vmem_limit_bytes=VMEM_LIMIT),
        name="inproj",
    )(x, *params)


def _s5_kernel(u_ref, ein_ref, toep_ref, fout_ref, lam_ref, d_ref, y_ref, e_ref, sp_ref):
    nch = e_ref.shape[1]
    half = S5_BLOCK_STATE // 2
    pairs = S5_CHUNK // 2
    lre = lam_ref[0, :, :half]
    lim = lam_ref[0, :, half:]
    d = d_ref[...]

    def pipeline(bi):
        xs = [u_ref[bi, pl.ds(j, nch, stride=S5_CHUNK), :] for j in range(S5_CHUNK)]
        xcat = jnp.concatenate([x.astype(BF16) for x in xs], axis=1)
        yield
        e_ref[bi] = _dot(xcat, ein_ref[0])
        yield
        intra = {}

        def scan():
            sre = sim = jnp.zeros((1, half), F32)
            for c in range(nch):
                sp_ref[bi, c:c + 1, :half] = sre
                sp_ref[bi, c:c + 1, half:] = sim
                e = e_ref[bi, c:c + 1, :]
                sre, sim = lre * sre - lim * sim + e[:, :half], lre * sim + lim * sre + e[:, half:]
                if (c + 1) % (nch // pairs) == 0:
                    yield

        def toeplitz():
            for tp in range(pairs):
                intra[tp] = _dot(xcat[:, :2 * LANES * (tp + 1)],
                                 toep_ref[0, 2 * LANES * (pairs - 1 - tp):, :])
                yield

        yield from _alternating(scan(), toeplitz())
        sp = sp_ref[bi].astype(BF16)
        for tp in range(pairs):
            y = intra[tp] + _dot(sp, fout_ref[0, :, 2 * LANES * tp:2 * LANES * (tp + 1)])
            for i in range(2):
                t = 2 * tp + i
                y_ref[bi, pl.ds(t, nch, stride=S5_CHUNK), :] = y[:, LANES * i:LANES * (i + 1)] + d * xs[t]
            yield

    _alternate(*[_delayed(pipeline(bi), S5_PIPE_SKEW * bi) for bi in range(S5_BATCH)])


def _cmul(ar, ai, br, bi):
    return ar * br - ai * bi, ar * bi + ai * br


def _block_diag_expand(compact, width, inner, transposed):
    k = compact.shape[1] if transposed else compact.shape[2]
    group_w = S5_GROUPS_PER_BLOCK * inner
    kk = jnp.arange(k)
    ww = jnp.arange(width)
    sel = ((kk[:, None] // inner == ww[None, :] // group_w)
           & (kk[:, None] % inner == ww[None, :] % inner)).astype(F32)
    wide_group = (ww % group_w) // inner
    if transposed:
        full = jnp.matmul(sel.T, compact)
        narrow_group = (jnp.arange(compact.shape[2]) % LANES) // S5_GROUP
        keep = wide_group[:, None] == narrow_group[None, :]
    else:
        full = jnp.matmul(compact, sel)
        narrow_group = (jnp.arange(compact.shape[1]) % LANES) // S5_GROUP
        keep = narrow_group[:, None] == wide_group[None, :]
    return jnp.where(keep, full, 0.0)


def _s5_operators(lam_re, lam_im, log_dt, b_re, b_im, c_re, c_im):
    c16, q4, gl = S5_CHUNK, S5_LANE_BLOCKS, S5_GROUPS_PER_BLOCK
    rows = c16 * LANES
    dt = jnp.exp(log_dt)[:, None]
    n = jnp.arange(c16 + 1, dtype=F32)[:, None, None]
    mag = jnp.exp(n * (lam_re * dt)[None])
    ang = n * (lam_im * dt)[None]
    pw_r, pw_i = mag * jnp.cos(ang), mag * jnp.sin(ang)
    den = lam_re * lam_re + lam_im * lam_im
    q_r, q_i = _cmul(pw_r[1] - 1.0, pw_i[1], lam_re / den, -lam_im / den)
    bt_r, bt_i = jnp.swapaxes(b_re, 1, 2), jnp.swapaxes(b_im, 1, 2)
    bb_r, bb_i = _cmul(q_r[:, None, :], q_i[:, None, :], bt_r, bt_i)
    e_r, e_i = _cmul(pw_r[:c16, :, None, :], pw_i[:c16, :, None, :], bb_r[None], bb_i[None])

    def by_block(a):
        a = a.reshape(a.shape[0], q4, gl * S5_GROUP, 2 * S5_STATE)
        return jnp.swapaxes(a, 0, 1).reshape(q4, -1, 2 * S5_STATE)

    ein = by_block(jnp.concatenate([e_r, e_i], axis=-1)[::-1])
    ein = _block_diag_expand(ein, S5_BLOCK_STATE, S5_STATE, False)

    f_r, f_i = _cmul(c_re[None], c_im[None], pw_r[1:, :, None, :], pw_i[1:, :, None, :])
    fo = jnp.swapaxes(by_block(jnp.concatenate([f_r, -f_i], axis=-1)), 1, 2)
    fo = _block_diag_expand(fo, S5_BLOCK_STATE, S5_STATE, True)

    ee = jnp.concatenate([e_r, -e_i], axis=-1)
    cc = jnp.concatenate([c_re, c_im], axis=-1).reshape(q4, LANES, 2 * S5_STATE)
    row_g = (jnp.arange(rows) % LANES) // S5_GROUP
    col_g = jnp.arange(LANES) // S5_GROUP
    halves = []
    for t2 in range(2):
        lags = [2 * d + t2 - j2 for d in reversed(range(c16 // 2)) for j2 in range(2)]
        a = jnp.stack([ee[lag] if lag >= 0 else jnp.zeros_like(ee[0]) for lag in lags])
        full = jnp.einsum('qrk,qck->qrc', by_block(a), cc, precision=lax.Precision.HIGHEST)
        halves.append(jnp.where(row_g[:, None] == col_g[None, :], full, 0.0))
    toep = jnp.concatenate(halves, axis=-1)

    lam16 = jnp.concatenate([pw_r[c16].reshape(q4, 1, gl * S5_STATE),
                             pw_i[c16].reshape(q4, 1, gl * S5_STATE)], axis=-1)
    return ein.astype(BF16), toep.astype(BF16), fo.astype(BF16), lam16


def _s5(u, ein, toep, fout, lam16, d):
    bsz, seqlen, _ = u.shape
    nch = seqlen // S5_CHUNK
    per_q = lambda a: pl.BlockSpec((1,) + a.shape[1:], lambda q, b: (q,) + (0,) * (a.ndim - 1))
    seq = pl.BlockSpec((S5_BATCH, seqlen, LANES), lambda q, b: (b, 0, q))
    state = pltpu.VMEM((S5_BATCH, nch, S5_BLOCK_STATE), F32)
    return pl.pallas_call(
        _s5_kernel,
        grid=(S5_LANE_BLOCKS, bsz // S5_BATCH),
        in_specs=[seq, per_q(ein), per_q(toep), per_q(fout), per_q(lam16),
                  pl.BlockSpec((1, LANES), lambda q, b: (0, q))],
        out_specs=seq,
        out_shape=jax.ShapeDtypeStruct(u.shape, F32),
        scratch_shapes=[state, state],
        compiler_params=pltpu.CompilerParams(
            dimension_semantics=("arbitrary", "arbitrary"), vmem_limit_bytes=VMEM_LIMIT),
        name="s5",
    )(u, ein, toep, fout, lam16, d)


def _stack_heads(x, lo):
    return jnp.concatenate([jnp.where(lo, x, 0.0), jnp.where(lo, 0.0, x)], axis=0)


def _pair_block_diag(x):
    zero = jnp.zeros((x.shape[0], LANES), x.dtype)
    return jnp.concatenate([jnp.concatenate([x[:, :LANES], zero], axis=1),
                            jnp.concatenate([zero, x[:, LANES:]], axis=1)], axis=0)


def _head_block_diag(x, head_id, order):
    return jnp.concatenate([jnp.where(head_id == h, x, 0.0) for h in order], axis=0)


def _rwkv_kernel(r_ref, lw_ref, k_ref, v_ref, kk_ref, b_ref, gate_ref, rk_ref, lnw_ref, lnb_ref,
                 y_ref, s_ref, wr_ref, w2_ref, bkt_ref, prb_ref, yv_ref, e2_ref, u_ref):
    tc = RWKV_CHUNK
    nchunks = RWKV_TB // tc
    qw = 2 * LANES

    @pl.when(pl.program_id(1) == 0)
    def _():
        s_ref[...] = jnp.zeros_like(s_ref)

    row = lax.broadcasted_iota(jnp.int32, (tc, qw), 0)
    lane = lax.broadcasted_iota(jnp.int32, (tc, qw), 1)
    col = jnp.bitwise_and(lane, RWKV_HEAD - 1)
    head_id = lane // RWKV_HEAD
    heads = tuple(range(qw // RWKV_HEAD))
    heads_swapped = (1, 0, 3, 2)
    eye_q = (row == col).astype(F32)
    row_p = lax.broadcasted_iota(jnp.int32, (tc, LANES), 0)
    lane_p = lax.broadcasted_iota(jnp.int32, (tc, LANES), 1)
    col_p = jnp.bitwise_and(lane_p, RWKV_HEAD - 1)
    lo_p = lane_p < RWKV_HEAD
    top_p2 = lax.broadcasted_iota(jnp.int32, (LANES, LANES), 0) < RWKV_HEAD
    stril = row_p > col_p
    tril = row_p >= col_p
    state_bd = ((lax.broadcasted_iota(jnp.int32, (LANES, qw), 0) // RWKV_HEAD)
                == (jnp.bitwise_and(lax.broadcasted_iota(jnp.int32, (LANES, qw), 1), LANES - 1) // RWKV_HEAD))
    ones = _head_ones(qw).astype(BF16)
    tri = (lax.broadcasted_iota(jnp.int32, (tc, tc), 0)
           >= lax.broadcasted_iota(jnp.int32, (tc, tc), 1)).astype(BF16)
    bf = lambda t: t.astype(BF16)
    up, dn = slice(0, tc), slice(tc, 2 * tc)

    def prepare(chunks):
        units = []
        for c in chunks:
            rows = slice(c * tc, (c + 1) * tc)
            lw_all = lw_ref[0, rows, :]
            g_all = _dot_exact_lhs(tri, lw_all)
            for q in range(RWKV_QUADS):
                qs = slice(qw * q, qw * (q + 1))
                units.append(dict(slot=c % RWKV_RING, q=q, rows=rows, qs=qs, lw=lw_all[:, qs], g=g_all[:, qs]))
        yield
        for d in units:
            rows, qs, g = d["rows"], d["qs"], d["g"]
            m = 0.5 * g[tc - 1:tc, :]
            e_out = jnp.exp(m - g)
            em = jnp.exp(m)
            rt = r_ref[0, rows, qs] * jnp.exp(g - m)
            bt = b_ref[0, rows, qs] * e_out
            kt = k_ref[0, rows, qs] * e_out
            d["em"] = em
            d["at"] = bf(-kk_ref[0, rows, qs] * jnp.exp(g - d["lw"] - m))
            d["rt"], d["bt"], d["kt"] = bf(rt), bf(bt), bf(kt)
            d["rts"] = bf(rt * em)
            d["vsw"] = bf(_head_block_diag(v_ref[0, rows, qs], head_id, heads_swapped))
            bkt, e2, d["bkt"] = [], [], []
            for j in range(2):
                ps = slice(LANES * j, LANES * (j + 1))
                bk_t = jnp.transpose(jnp.concatenate([bt[:, ps], kt[:, ps]], axis=0))
                em_t = jnp.transpose(jnp.broadcast_to(em[:, ps], (LANES, LANES)))
                d["bkt"].append(bk_t)
                bkt.append(bf(bk_t * em_t))
                e2.append(em_t * em_t)
            bkt_ref[d["slot"], d["q"]] = jnp.concatenate(bkt, axis=1)
            e2_ref[d["slot"], d["q"]] = jnp.concatenate(e2, axis=1)
        yield
        for d in units:
            d["o"] = []
            for j in range(2):
                ps = slice(LANES * j, LANES * (j + 1))
                bk_t = d["bkt"][j]
                kb_t = pltpu.roll(bk_t, RWKV_HEAD, axis=1)
                rhs = bf(jnp.concatenate([jnp.where(top_p2, bk_t, 0.0), jnp.where(top_p2, 0.0, kb_t)], axis=1))
                lhs = jnp.concatenate([d["at"][:, ps], d["rt"][:, ps]], axis=0)
                d["o"].append(_dot(lhs, rhs))
        yield
        for d in units:
            h0 = [o[:, :LANES] for o in d["o"]]
            h1 = [o[:, LANES:] for o in d["o"]]

            def pick(rws, first, second, mask):
                return jnp.concatenate(
                    [jnp.where(mask, jnp.where(lo_p, first[j][rws], second[j][rws]), 0.0) for j in range(2)],
                    axis=1)

            d["ap"] = pick(up, h0, h1, stril)
            d["p_ak"] = bf(pick(up, h1, h0, stril))
            p_rb = bf(pick(dn, h0, h1, tril))
            d["p_rk"] = bf(pick(dn, h1, h0, tril))
            prb_ref[d["slot"], d["q"]] = p_rb
            d["tinv"] = eye_q + d["ap"]
        yield
        levels = int(math.log2(tc)) - 1
        for d in units:
            d["ap"] = _dot(bf(d["ap"]), bf(_head_block_diag(d["ap"], head_id, heads)))
        for lvl in range(levels):
            yield
            for d in units:
                rhs = bf(_head_block_diag(d["ap"], head_id, heads))
                if lvl < levels - 1:
                    out = _dot(bf(jnp.concatenate([d["tinv"], d["ap"]], axis=0)), rhs)
                    d["tinv"], d["ap"] = d["tinv"] + out[:tc], out[tc:]
                else:
                    d["tinv"] = d["tinv"] + _dot(bf(d["tinv"]), rhs)
        yield
        for d in units:
            d["tinv"] = bf(d["tinv"])
            out = _dot(jnp.concatenate([d["p_ak"], d["p_rk"]], axis=0), d["vsw"])
            d["xv"] = out[:tc]
            yv_ref[d["slot"], d["q"]] = out[tc:]
        yield
        for d in units:
            w1s, w2 = [], []
            for j in range(2):
                ps = slice(LANES * j, LANES * (j + 1))
                rhs = jnp.concatenate([_stack_heads(d["at"][:, ps], lo_p),
                                       bf(_stack_heads(d["xv"][:, ps], lo_p))], axis=1)
                w = _dot(d["tinv"][:, ps], rhs)
                w1s.append(bf(w[:, :LANES] * d["em"][:, ps]))
                w2.append(w[:, LANES:])
            wr_ref[d["slot"], d["q"]] = jnp.concatenate([jnp.concatenate(w1s, axis=1), d["rts"]], axis=0)
            w2_ref[d["slot"], d["q"]] = jnp.concatenate(w2, axis=1)

    def recur_step(c):
        rows, slot = slice(c * tc, (c + 1) * tc), c % RWKV_RING
        units = [dict(q=q, qs=slice(qw * q, qw * (q + 1))) for q in range(RWKV_QUADS)]
        for d in units:
            d["h0"] = s_ref[d["q"]]
        for d in units:
            out = _dot(wr_ref[slot, d["q"]], _pair_block_diag(bf(d["h0"])))
            d["u"] = out[:tc] + w2_ref[slot, d["q"]]
            d["yr"] = out[tc:]
        yield
        for d in units:
            q = d["q"]
            uv = jnp.concatenate([bf(d["u"]), v_ref[0, rows, d["qs"]]], axis=0)
            upd = _dot(bkt_ref[slot, q], _pair_block_diag(uv))
            s_ref[q] = d["h0"] * e2_ref[slot, q] + jnp.where(state_bd, upd, 0.0)
            u_ref[slot, q] = bf(d["u"])
            yv_ref[slot, q] = yv_ref[slot, q] + d["yr"]

    def emit_step(c):
        rows, slot = slice(c * tc, (c + 1) * tc), c % RWKV_RING
        units = [dict(q=q, qs=slice(qw * q, qw * (q + 1))) for q in range(RWKV_QUADS)]
        for d in units:
            q = d["q"]
            d["y"] = yv_ref[slot, q] + _dot(prb_ref[slot, q], _head_block_diag(u_ref[slot, q], head_id, heads))
        yield
        stacked = []
        for d in units:
            qs = d["qs"]
            stacked += [d["y"], r_ref[0, rows, qs].astype(F32) * k_ref[0, rows, qs] * rk_ref[:, qs]]
        sums = _head_sums(jnp.concatenate(stacked, axis=0), ones)
        for i, d in enumerate(units):
            d["dev"] = d["y"] - sums[2 * i * tc:(2 * i + 1) * tc] * (1.0 / RWKV_HEAD)
            d["bonus"] = sums[(2 * i + 1) * tc:(2 * i + 2) * tc]
        yield
        var = _head_sums(jnp.concatenate([d["dev"] * d["dev"] for d in units], axis=0), ones)
        for i, d in enumerate(units):
            qs = d["qs"]
            rstd = lax.rsqrt(var[i * tc:(i + 1) * tc] * (1.0 / RWKV_HEAD) + GN_EPS)
            yn = d["dev"] * rstd * lnw_ref[:, qs] + lnb_ref[:, qs]
            y_ref[0, rows, qs] = bf((yn + d["bonus"] * v_ref[0, rows, qs]) * gate_ref[0, rows, qs])

    def chain(chunks):
        for c in chunks:
            yield from _alternating(*([emit_step(c - 1)] if c > 0 else []), recur_step(c))

    groups = [list(range(g, g + RWKV_PREP_CHUNKS)) for g in range(0, nchunks, RWKV_PREP_CHUNKS)]
    _alternate(prepare(groups[0]))
    for prev, nxt in zip(groups[:-1], groups[1:]):
        _alternate(prepare(nxt), chain(prev))
    _alternate(chain(groups[-1]))
    _alternate(emit_step(nchunks - 1))


def _rwkv(r, lw, k, v, kkn, b, gate, r_k, ln_w, ln_b):
    bsz, seqlen, _ = r.shape
    nchunks = RWKV_TB // RWKV_CHUNK
    tok = pl.BlockSpec((1, RWKV_TB, D_RWKV), lambda bi, t: (bi, t, 0))
    gate_spec = pl.BlockSpec((1, RWKV_TB, D_RWKV), lambda bi, t: (bi, t, 1))
    vec = pl.BlockSpec((1, D_RWKV), lambda bi, t: (0, 0))
    unit = lambda rows, cols, dt: pltpu.VMEM((RWKV_RING, RWKV_QUADS, rows, cols), dt)
    return pl.pallas_call(
        _rwkv_kernel,
        grid=(bsz, seqlen // RWKV_TB),
        in_specs=[tok] * 6 + [gate_spec] + [vec] * 3,
        out_specs=tok,
        out_shape=jax.ShapeDtypeStruct(r.shape, BF16),
        scratch_shapes=[
            pltpu.VMEM((RWKV_QUADS, LANES, 2 * LANES), F32),
            unit(2 * RWKV_CHUNK, 2 * LANES, BF16),
            unit(RWKV_CHUNK, 2 * LANES, F32),
            unit(LANES, 4 * RWKV_CHUNK, BF16),
            unit(RWKV_CHUNK, 2 * LANES, BF16),
            unit(RWKV_CHUNK, 2 * LANES, F32),
            unit(LANES, 2 * LANES, F32),
            unit(RWKV_CHUNK, 2 * LANES, BF16),
        ],
        compiler_params=pltpu.CompilerParams(
            dimension_semantics=("arbitrary", "arbitrary"), vmem_limit_bytes=VMEM_LIMIT),
        name="rwkv7",
    )(r, lw, k, v, kkn, b, gate, r_k, ln_w, ln_b)


def _out_kernel(y5_ref, gate_ref, yr_ref, x_ref, gluw_ref, glub_ref, wout_ref, fg_ref, o_ref):
    def sub_tile(s):
        rs = slice(OUT_SUB * s, OUT_SUB * (s + 1))
        y = y5_ref[0, rs, :]
        half = 0.5 * y
        y = half + half * jnp.tanh(y * (GELU_C + (GELU_C * 0.044715) * (y * y)))
        yield
        glu = _sigmoid(_dot(y.astype(BF16), gluw_ref[...]) + glub_ref[...])
        ys = y * glu * gate_ref[0, rs, :]
        ycat = jnp.concatenate([ys.astype(BF16), yr_ref[0, rs, :]], axis=-1)
        yield
        xn = x_ref[0, rs, :] + _dot(ycat, wout_ref[...])
        yield
        ms = jnp.mean(xn * xn, axis=-1, keepdims=True)
        o_ref[0, rs, :] = xn * lax.rsqrt(ms + NORM_EPS) * fg_ref[...]

    _alternate(*[_delayed(sub_tile(s), SUB_TILE_SKEW * s) for s in range(TM_OUT // OUT_SUB)])


def _outproj(y5, gate, yr, x, glu_w, glu_b, w_out, final_g):
    bsz, seqlen, _ = x.shape
    tok = lambda w: pl.BlockSpec((1, TM_OUT, w), lambda b, t: (b, t, 0))
    full = lambda a: pl.BlockSpec(a.shape, lambda b, t: (0,) * a.ndim)
    return pl.pallas_call(
        _out_kernel,
        grid=(bsz, seqlen // TM_OUT),
        in_specs=[tok(D_S5), tok(D_S5), tok(D_RWKV), tok(D_MODEL),
                  full(glu_w), full(glu_b), full(w_out), full(final_g)],
        out_specs=tok(D_MODEL),
        out_shape=jax.ShapeDtypeStruct(x.shape, F32),
        compiler_params=pltpu.CompilerParams(
            dimension_semantics=("arbitrary", "arbitrary"), vmem_limit_bytes=VMEM_LIMIT),
        name="outproj",
    )(y5, gate, yr, x, glu_w, glu_b, w_out, final_g)


@jax.jit
def _forward(x, norm_g, w_in, s5_lam_re, s5_lam_im, s5_log_dt, s5_b_re, s5_b_im, s5_c_re, s5_c_im,
             s5_d, s5_glu_w, s5_glu_b, rwkv_mu, rwkv_w0, rwkv_w2, rwkv_a0, rwkv_a2, rwkv_k_k, rwkv_k_a,
             rwkv_r_k, rwkv_ln_w, rwkv_ln_b, w_out, final_g):
    assert w_in.shape[0] == 1, "single-layer block"
    for l in range(1):
        zeros = jnp.zeros((LOW_RANK, D_RWKV), F32)
        w2a2 = jnp.concatenate(
            [jnp.concatenate([rwkv_w2[l], zeros], axis=1), jnp.concatenate([zeros, rwkv_a2[l]], axis=1)],
            axis=0).astype(BF16)
        row = lambda a: a.reshape(1, -1).astype(F32)
        u, gate, r, lw, k, v, kkn, b = _inproj(
            x, row(norm_g[l]), w_in[l].astype(BF16), row(rwkv_mu[l]), w2a2, row(rwkv_w0[l]),
            row(rwkv_a0[l]), row(rwkv_k_k[l]), row(rwkv_k_a[l]))
        ein, toep, fout, lam16 = _s5_operators(
            s5_lam_re[l], s5_lam_im[l], s5_log_dt[l], s5_b_re[l], s5_b_im[l], s5_c_re[l], s5_c_im[l])
        y5 = _s5(u, ein, toep, fout, lam16, row(s5_d[l]))
        yr = _rwkv(r, lw, k, v, kkn, b, gate, row(rwkv_r_k[l]), row(rwkv_ln_w[l]), row(rwkv_ln_b[l]))
        x = _outproj(y5, gate, yr, x, s5_glu_w[l].astype(BF16), row(s5_glu_b[l]),
                     w_out[l].astype(BF16), row(final_g))
    return x


def kernel(x, norm_g, w_in, s5_lam_re, s5_lam_im, s5_log_dt, s5_b_re, s5_b_im, s5_c_re, s5_c_im, s5_d, s5_glu_w, s5_glu_b, rwkv_mu, rwkv_w0, rwkv_w2, rwkv_a0, rwkv_a2, rwkv_k_k, rwkv_k_a, rwkv_r_k, rwkv_ln_w, rwkv_ln_b, w_out, final_g):
    return _forward(x, norm_g, w_in, s5_lam_re, s5_lam_im, s5_log_dt, s5_b_re, s5_b_im, s5_c_re, s5_c_im,
                    s5_d, s5_glu_w, s5_glu_b, rwkv_mu, rwkv_w0, rwkv_w2, rwkv_a0, rwkv_a2, rwkv_k_k,
                    rwkv_k_a, rwkv_r_k, rwkv_ln_w, rwkv_ln_b, w_out, final_g)
```

```python
import math

import jax
import jax.numpy as jnp
from jax import lax
from jax.experimental import pallas as pl
from jax.experimental.pallas import tpu as pltpu

F32 = jnp.float32
BF16 = jnp.bfloat16

D_MODEL = 1024
D_S5 = 512
D_RWKV = 512
S5_GROUP = 16
S5_GROUPS = 32
S5_STATE = 64
RWKV_HEAD = 64
LOW_RANK = 64
D_SHIFT = 3 * D_RWKV + 2 * LOW_RANK
D_IN = 2 * D_S5 + D_SHIFT + D_RWKV
NORM_EPS = 1e-6
GN_EPS = 64e-5
GELU_C = math.sqrt(2.0 / math.pi)

LANES = 128
TM_IN = 1024
IN_SUB = 256
SUB_TILE_SKEW = 2
TM_OUT = 1024
OUT_SUB = 512
S5_CHUNK = 16
S5_BATCH = 2
S5_PIPE_SKEW = 6
S5_LANE_BLOCKS = D_S5 // LANES
S5_GROUPS_PER_BLOCK = LANES // S5_GROUP
S5_BLOCK_STATE = 2 * S5_GROUPS_PER_BLOCK * S5_STATE
RWKV_CHUNK = 64
RWKV_TB = 2048
RWKV_PREP_CHUNKS = 4
RWKV_RING = 3 * RWKV_PREP_CHUNKS
HEAD_PAIRS = D_RWKV // LANES
RWKV_QUADS = HEAD_PAIRS // 2
VMEM_LIMIT = 56 * 1024 * 1024


def _dot(a, b):
    return jnp.dot(a, b, preferred_element_type=F32)


def _dot_nt(a, b):
    return lax.dot_general(a, b, (((1,), (1,)), ((), ())), preferred_element_type=F32)


def _dot_tn(a, b):
    return lax.dot_general(a, b, (((0,), (0,)), ((), ())), preferred_element_type=F32)


def _split2(x):
    hi = x.astype(BF16)
    return hi, (x - hi.astype(F32)).astype(BF16)


def _head_sums(x, ones):
    return _dot(x.astype(BF16), ones)


def _dot_exact_lhs(m, x):
    hi, lo = _split2(x)
    return _dot(m, hi) + _dot(m, lo)


def _sigmoid(x):
    return 0.5 + 0.5 * jnp.tanh(0.5 * x)


def _head_ones(width):
    r = lax.broadcasted_iota(jnp.int32, (width, width), 0)
    c = lax.broadcasted_iota(jnp.int32, (width, width), 1)
    return (r // RWKV_HEAD) == (c // RWKV_HEAD)


def _alternating(*stage_generators):
    pending = list(stage_generators)
    while pending:
        for g in list(pending):
            if next(g, StopIteration) is StopIteration:
                pending.remove(g)
            else:
                yield


def _alternate(*stage_generators):
    for _ in _alternating(*stage_generators):
        pass


def _delayed(stage_generator, stages):
    for _ in range(stages):
        yield
    yield from stage_generator


def _inproj_kernel(x_ref, g_ref, win_ref, mu_ref, w2a2_ref, w0_ref, a0_ref, kk_ref, ka_ref,
                   u_ref, gate_ref, r_ref, lw_ref, k_ref, v_ref, kkn_ref, b_ref, carry_ref):
    t = pl.program_id(1)

    @pl.when(t == 0)
    def _():
        carry_ref[...] = jnp.zeros_like(carry_ref)

    ones = _head_ones(2 * LANES).astype(BF16)
    carries = {-1: carry_ref[...]}

    def sub_tile(s):
        rs = slice(IN_SUB * s, IN_SUB * (s + 1))
        x = x_ref[0, rs, :]
        ms = jnp.mean(x * x, axis=-1, keepdims=True)
        h = (x * lax.rsqrt(ms + NORM_EPS) * g_ref[...]).astype(BF16)

        def proj(lo, hi):
            return _dot(h, win_ref[:, lo:hi])

        yield
        u_ref[0, rs, :] = proj(0, D_S5)
        yield
        z5 = proj(D_S5, 2 * D_S5)
        gate_ref[0, rs, :D_S5] = (z5 * _sigmoid(z5)).astype(BF16)
        yield
        zr = proj(2 * D_S5 + D_SHIFT, D_IN)
        gate_ref[0, rs, D_S5:] = (zr * _sigmoid(zr)).astype(BF16)
        yield
        rw = proj(2 * D_S5, 2 * D_S5 + D_SHIFT)
        carries[s] = rw[IN_SUB - 1:IN_SUB, :]
        yield
        prev = pltpu.roll(rw, 1, axis=0)
        row = lax.broadcasted_iota(jnp.int32, rw.shape, 0)
        prev = jnp.where(row == 0, carries[s - 1], prev)
        sh = rw + (prev - rw) * mu_ref[...]
        r = sh[:, 0:D_RWKV]
        k = sh[:, D_RWKV:2 * D_RWKV]
        v = sh[:, 2 * D_RWKV:3 * D_RWKV]
        wa = sh[:, 3 * D_RWKV:]
        lane = lax.broadcasted_iota(jnp.int32, wa.shape, 1)
        wa = jnp.where(lane < LOW_RANK, jnp.tanh(wa), wa)
        za = _dot(wa.astype(BF16), w2a2_ref[...])
        r_ref[0, rs, :] = r.astype(BF16)
        v_ref[0, rs, :] = v.astype(BF16)
        yield
        lw_ref[0, rs, :] = -math.exp(-0.5) * _sigmoid(w0_ref[...] + za[:, :D_RWKV])
        iclr = _sigmoid(a0_ref[...] + za[:, D_RWKV:])
        k_ref[0, rs, :] = (k * (1.0 + (iclr - 1.0) * ka_ref[...])).astype(BF16)
        kk = k * kk_ref[...]
        sq = kk * kk
        ss = jnp.concatenate(
            [_head_sums(sq[:, 2 * LANES * q:2 * LANES * (q + 1)], ones) for q in range(RWKV_QUADS)], axis=1)
        yield
        kkn = kk * lax.rsqrt(jnp.maximum(ss, 1e-24))
        kkn_ref[0, rs, :] = kkn.astype(BF16)
        b_ref[0, rs, :] = (kkn * iclr).astype(BF16)

    n_sub = TM_IN // IN_SUB
    _alternate(*[_delayed(sub_tile(s), SUB_TILE_SKEW * s) for s in range(n_sub)])
    carry_ref[...] = carries[n_sub - 1]


def _inproj(x, norm_g, w_in, mu, w2a2, w0, a0, k_k, k_a):
    bsz, seqlen, _ = x.shape
    tok = lambda w: pl.BlockSpec((1, TM_IN, w), lambda b, t: (b, t, 0))
    full = lambda a: pl.BlockSpec(a.shape, lambda b, t: (0,) * a.ndim)
    out = lambda w, dt: jax.ShapeDtypeStruct((bsz, seqlen, w), dt)
    params = (norm_g, w_in, mu, w2a2, w0, a0, k_k, k_a)
    return pl.pallas_call(
        _inproj_kernel,
        grid=(bsz, seqlen // TM_IN),
        in_specs=[tok(D_MODEL)] + [full(a) for a in params],
        out_specs=[tok(D_S5), tok(2 * D_S5)] + [tok(D_RWKV)] * 6,
        out_shape=[out(D_S5, F32), out(2 * D_S5, BF16)]
        + [out(D_RWKV, F32 if i == 1 else BF16) for i in range(6)],
        scratch_shapes=[pltpu.VMEM((1, D_SHIFT), F32)],
        compiler_params=pltpu.CompilerParams(
            dimension_semantics=("arbitrary", "arbitrary"), vmem_limit_bytes=VMEM_LIMIT),
        name="inproj",
    )(x, *params)


def _s5_kernel(u_ref, ein_ref, toep_ref, fout_ref, lam_ref, d_ref, y_ref, e_ref, sp_ref):
    nch = e_ref.shape[1]
    half = S5_BLOCK_STATE // 2
    pairs = S5_CHUNK // 2
    lre = lam_ref[0, :, :half]
    lim = lam_ref[0, :, half:]
    d = d_ref[...]

    def pipeline(bi):
        xs = [u_ref[bi, pl.ds(j, nch, stride=S5_CHUNK), :] for j in range(S5_CHUNK)]
        xcat = jnp.concatenate([x.astype(BF16) for x in xs], axis=1)
        yield
        e_ref[bi] = _dot(xcat, ein_ref[0])
        yield
        intra = {}

        def scan():
            sre = sim = jnp.zeros((1, half), F32)
            for c in range(nch):
                sp_ref[bi, c:c + 1, :half] = sre
                sp_ref[bi, c:c + 1, half:] = sim
                e = e_ref[bi, c:c + 1, :]
                sre, sim = lre * sre - lim * sim + e[:, :half], lre * sim + lim * sre + e[:, half:]
                if (c + 1) % (nch // pairs) == 0:
                    yield

        def toeplitz():
            for tp in range(pairs):
                intra[tp] = _dot(xcat[:, :2 * LANES * (tp + 1)],
                                 toep_ref[0, 2 * LANES * (pairs - 1 - tp):, :])
                yield

        yield from _alternating(scan(), toeplitz())
        sp = sp_ref[bi].astype(BF16)
        for tp in range(pairs):
            y = intra[tp] + _dot(sp, fout_ref[0, :, 2 * LANES * tp:2 * LANES * (tp + 1)])
            for i in range(2):
                t = 2 * tp + i
                y_ref[bi, pl.ds(t, nch, stride=S5_CHUNK), :] = y[:, LANES * i:LANES * (i + 1)] + d * xs[t]
            yield

    _alternate(*[_delayed(pipeline(bi), S5_PIPE_SKEW * bi) for bi in range(S5_BATCH)])


def _cmul(ar, ai, br, bi):
    return ar * br - ai * bi, ar * bi + ai * br


def _block_diag_expand(compact, width, inner, transposed):
    k = compact.shape[1] if transposed else compact.shape[2]
    group_w = S5_GROUPS_PER_BLOCK * inner
    kk = jnp.arange(k)
    ww = jnp.arange(width)
    sel = ((kk[:, None] // inner == ww[None, :] // group_w)
           & (kk[:, None] % inner == ww[None, :] % inner)).astype(F32)
    wide_group = (ww % group_w) // inner
    if transposed:
        full = jnp.matmul(sel.T, compact)
        narrow_group = (jnp.arange(compact.shape[2]) % LANES) // S5_GROUP
        keep = wide_group[:, None] == narrow_group[None, :]
    else:
        full = jnp.matmul(compact, sel)
        narrow_group = (jnp.arange(compact.shape[1]) % LANES) // S5_GROUP
        keep = narrow_group[:, None] == wide_group[None, :]
    return jnp.where(keep, full, 0.0)


def _s5_operators(lam_re, lam_im, log_dt, b_re, b_im, c_re, c_im):
    c16, q4, gl = S5_CHUNK, S5_LANE_BLOCKS, S5_GROUPS_PER_BLOCK
    rows = c16 * LANES
    dt = jnp.exp(log_dt)[:, None]
    n = jnp.arange(c16 + 1, dtype=F32)[:, None, None]
    mag = jnp.exp(n * (lam_re * dt)[None])
    ang = n * (lam_im * dt)[None]
    pw_r, pw_i = mag * jnp.cos(ang), mag * jnp.sin(ang)
    den = lam_re * lam_re + lam_im * lam_im
    q_r, q_i = _cmul(pw_r[1] - 1.0, pw_i[1], lam_re / den, -lam_im / den)
    bt_r, bt_i = jnp.swapaxes(b_re, 1, 2), jnp.swapaxes(b_im, 1, 2)
    bb_r, bb_i = _cmul(q_r[:, None, :], q_i[:, None, :], bt_r, bt_i)
    e_r, e_i = _cmul(pw_r[:c16, :, None, :], pw_i[:c16, :, None, :], bb_r[None], bb_i[None])

    def by_block(a):
        a = a.reshape(a.shape[0], q4, gl * S5_GROUP, 2 * S5_STATE)
        return jnp.swapaxes(a, 0, 1).reshape(q4, -1, 2 * S5_STATE)

    ein = by_block(jnp.concatenate([e_r, e_i], axis=-1)[::-1])
    ein = _block_diag_expand(ein, S5_BLOCK_STATE, S5_STATE, False)

    f_r, f_i = _cmul(c_re[None], c_im[None], pw_r[1:, :, None, :], pw_i[1:, :, None, :])
    fo = jnp.swapaxes(by_block(jnp.concatenate([f_r, -f_i], axis=-1)), 1, 2)
    fo = _block_diag_expand(fo, S5_BLOCK_STATE, S5_STATE, True)

    ee = jnp.concatenate([e_r, -e_i], axis=-1)
    cc = jnp.concatenate([c_re, c_im], axis=-1).reshape(q4, LANES, 2 * S5_STATE)
    row_g = (jnp.arange(rows) % LANES) // S5_GROUP
    col_g = jnp.arange(LANES) // S5_GROUP
    halves = []
    for t2 in range(2):
        lags = [2 * d + t2 - j2 for d in reversed(range(c16 // 2)) for j2 in range(2)]
        a = jnp.stack([ee[lag] if lag >= 0 else jnp.zeros_like(ee[0]) for lag in lags])
        full = jnp.einsum('qrk,qck->qrc', by_block(a), cc, precision=lax.Precision.HIGHEST)
        halves.append(jnp.where(row_g[:, None] == col_g[None, :], full, 0.0))
    toep = jnp.concatenate(halves, axis=-1)

    lam16 = jnp.concatenate([pw_r[c16].reshape(q4, 1, gl * S5_STATE),
                             pw_i[c16].reshape(q4, 1, gl * S5_STATE)], axis=-1)
    return ein.astype(BF16), toep.astype(BF16), fo.astype(BF16), lam16


def _s5(u, ein, toep, fout, lam16, d):
    bsz, seqlen, _ = u.shape
    nch = seqlen // S5_CHUNK
    per_q = lambda a: pl.BlockSpec((1,) + a.shape[1:], lambda q, b: (q,) + (0,) * (a.ndim - 1))
    seq = pl.BlockSpec((S5_BATCH, seqlen, LANES), lambda q, b: (b, 0, q))
    state = pltpu.VMEM((S5_BATCH, nch, S5_BLOCK_STATE), F32)
    return pl.pallas_call(
        _s5_kernel,
        grid=(S5_LANE_BLOCKS, bsz // S5_BATCH),
        in_specs=[seq, per_q(ein), per_q(toep), per_q(fout), per_q(lam16),
                  pl.BlockSpec((1, LANES), lambda q, b: (0, q))],
        out_specs=seq,
        out_shape=jax.ShapeDtypeStruct(u.shape, F32),
        scratch_shapes=[state, state],
        compiler_params=pltpu.CompilerParams(
            dimension_semantics=("arbitrary", "arbitrary"), vmem_limit_bytes=VMEM_LIMIT),
        name="s5",
    )(u, ein, toep, fout, lam16, d)


def _stack_heads(x, lo):
    return jnp.concatenate([jnp.where(lo, x, 0.0), jnp.where(lo, 0.0, x)], axis=0)


def _pair_block_diag(x):
    zero = jnp.zeros((x.shape[0], LANES), x.dtype)
    return jnp.concatenate([jnp.concatenate([x[:, :LANES], zero], axis=1),
                            jnp.concatenate([zero, x[:, LANES:]], axis=1)], axis=0)


def _head_block_diag(x, head_id, order):
    return jnp.concatenate([jnp.where(head_id == h, x, 0.0) for h in order], axis=0)


def _rwkv_kernel(r_ref, lw_ref, k_ref, v_ref, kk_ref, b_ref, gate_ref, rk_ref, lnw_ref, lnb_ref,
                 y_ref, s_ref, wr_ref, w2_ref, bkt_ref, prb_ref, yv_ref, e2_ref, u_ref):
    tc = RWKV_CHUNK
    nchunks = RWKV_TB // tc
    qw = 2 * LANES

    @pl.when(pl.program_id(1) == 0)
    def _():
        s_ref[...] = jnp.zeros_like(s_ref)

    row = lax.broadcasted_iota(jnp.int32, (tc, qw), 0)
    lane = lax.broadcasted_iota(jnp.int32, (tc, qw), 1)
    col = jnp.bitwise_and(lane, RWKV_HEAD - 1)
    head_id = lane // RWKV_HEAD
    heads = tuple(range(qw // RWKV_HEAD))
    heads_swapped = (1, 0, 3, 2)
    eye_q = (row == col).astype(F32)
    row_p = lax.broadcasted_iota(jnp.int32, (tc, LANES), 0)
    lane_p = lax.broadcasted_iota(jnp.int32, (tc, LANES), 1)
    col_p = jnp.bitwise_and(lane_p, RWKV_HEAD - 1)
    lo_p = lane_p < RWKV_HEAD
    lo_p2 = lax.broadcasted_iota(jnp.int32, (2 * tc, LANES), 1) < RWKV_HEAD
    stril = row_p > col_p
    tril = row_p >= col_p
    state_bd = ((lax.broadcasted_iota(jnp.int32, (LANES, qw), 0) // RWKV_HEAD)
                == (jnp.bitwise_and(lax.broadcasted_iota(jnp.int32, (LANES, qw), 1), LANES - 1) // RWKV_HEAD))
    ones = _head_ones(qw).astype(BF16)
    tri = (lax.broadcasted_iota(jnp.int32, (tc, tc), 0)
           >= lax.broadcasted_iota(jnp.int32, (tc, tc), 1)).astype(BF16)
    bf = lambda t: t.astype(BF16)
    up, dn = slice(0, tc), slice(tc, 2 * tc)

    def prepare(chunks):
        units = []
        for c in chunks:
            rows = slice(c * tc, (c + 1) * tc)
            lw_all = lw_ref[0, rows, :]
            g_all = _dot_exact_lhs(tri, lw_all)
            for q in range(RWKV_QUADS):
                qs = slice(qw * q, qw * (q + 1))
                units.append(dict(slot=c % RWKV_RING, q=q, rows=rows, qs=qs, lw=lw_all[:, qs], g=g_all[:, qs]))
        yield
        for d in units:
            rows, qs, g = d["rows"], d["qs"], d["g"]
            m = 0.5 * g[tc - 1:tc, :]
            e_out = jnp.exp(m - g)
            em = jnp.exp(m)
            rt = r_ref[0, rows, qs] * jnp.exp(g - m)
            bt = b_ref[0, rows, qs] * e_out
            kt = k_ref[0, rows, qs] * e_out
            d["em"] = em
            d["at"] = bf(-kk_ref[0, rows, qs] * jnp.exp(g - d["lw"] - m))
            d["rt"], d["bt"], d["kt"] = bf(rt), bf(bt), bf(kt)
            d["rts"] = bf(rt * em)
            d["vsw"] = bf(_head_block_diag(v_ref[0, rows, qs], head_id, heads_swapped))
            bkt, e2 = [], []
            for j in range(2):
                ps = slice(LANES * j, LANES * (j + 1))
                bks = jnp.concatenate([bt[:, ps], kt[:, ps]], axis=0) * em[:, ps]
                bkt.append(bf(jnp.transpose(bks)))
                e2.append(jnp.transpose(jnp.broadcast_to(em[:, ps] * em[:, ps], (LANES, LANES))))
            bkt_ref[d["slot"], d["q"]] = jnp.concatenate(bkt, axis=1)
            e2_ref[d["slot"], d["q"]] = jnp.concatenate(e2, axis=1)
        yield
        for d in units:
            d["o"] = []
            for j in range(2):
                ps = slice(LANES * j, LANES * (j + 1))
                bk = jnp.concatenate([d["bt"][:, ps], d["kt"][:, ps]], axis=0)
                kb = jnp.concatenate([d["kt"][:, ps], d["bt"][:, ps]], axis=0)
                rhs = jnp.concatenate([jnp.where(lo_p2, bk, 0.0), jnp.where(lo_p2, 0.0, kb)], axis=0)
                lhs = jnp.concatenate([d["at"][:, ps], d["rt"][:, ps]], axis=0)
                d["o"].append(_dot_nt(lhs, rhs))
        yield
        for d in units:
            h0 = [o[:, :LANES] for o in d["o"]]
            h1 = [o[:, LANES:] for o in d["o"]]

            def pick(rws, first, second, mask):
                return jnp.concatenate(
                    [jnp.where(mask, jnp.where(lo_p, first[j][rws], second[j][rws]), 0.0) for j in range(2)],
                    axis=1)

            d["ap"] = pick(up, h0, h1, stril)
            h0, h1 = [bf(h) for h in h0], [bf(h) for h in h1]
            d["p_ak"] = pick(up, h1, h0, stril)
            d["p_rk"] = pick(dn, h1, h0, tril)
            prb_ref[d["slot"], d["q"]] = pick(dn, h0, h1, tril)
            d["tinv"] = eye_q + d["ap"]
        yield
        levels = int(math.log2(tc)) - 1
        for d in units:
            ap = bf(d["ap"])
            d["ap"] = bf(_dot(ap, _head_block_diag(ap, head_id, heads)))
        for lvl in range(levels):
            yield
            for d in units:
                rhs = _head_block_diag(d["ap"], head_id, heads)
                if lvl < levels - 1:
                    out = _dot(jnp.concatenate([bf(d["tinv"]), d["ap"]], axis=0), rhs)
                    d["tinv"], d["ap"] = d["tinv"] + out[:tc], bf(out[tc:])
                else:
                    d["tinv"] = d["tinv"] + _dot(bf(d["tinv"]), rhs)
        yield
        for d in units:
            d["tinv"] = bf(d["tinv"])
            out = _dot(jnp.concatenate([d["p_ak"], d["p_rk"]], axis=0), d["vsw"])
            d["xv"] = bf(out[:tc])
            yv_ref[d["slot"], d["q"]] = out[tc:]
        yield
        for d in units:
            w1s, w2 = [], []
            for j in range(2):
                ps = slice(LANES * j, LANES * (j + 1))
                rhs = jnp.concatenate([_stack_heads(d["at"][:, ps], lo_p),
                                       _stack_heads(d["xv"][:, ps], lo_p)], axis=1)
                w = _dot(d["tinv"][:, ps], rhs)
                w1s.append(bf(w[:, :LANES] * d["em"][:, ps]))
                w2.append(w[:, LANES:])
            wr_ref[d["slot"], d["q"]] = jnp.concatenate([jnp.concatenate(w1s, axis=1), d["rts"]], axis=0)
            w2_ref[d["slot"], d["q"]] = jnp.concatenate(w2, axis=1)

    def recur_step(c):
        rows, slot = slice(c * tc, (c + 1) * tc), c % RWKV_RING
        units = [dict(q=q, qs=slice(qw * q, qw * (q + 1))) for q in range(RWKV_QUADS)]
        for d in units:
            d["h0"] = s_ref[d["q"]]
        for d in units:
            out = _dot(wr_ref[slot, d["q"]], _pair_block_diag(bf(d["h0"])))
            d["u"] = out[:tc] + w2_ref[slot, d["q"]]
            d["yr"] = out[tc:]
        yield
        for d in units:
            q = d["q"]
            uv = jnp.concatenate([bf(d["u"]), v_ref[0, rows, d["qs"]]], axis=0)
            upd = _dot(bkt_ref[slot, q], _pair_block_diag(uv))
            s_ref[q] = d["h0"] * e2_ref[slot, q] + jnp.where(state_bd, upd, 0.0)
            u_ref[slot, q] = bf(d["u"])
            yv_ref[slot, q] = yv_ref[slot, q] + d["yr"]

    def emit_step(c):
        rows, slot = slice(c * tc, (c + 1) * tc), c % RWKV_RING
        units = [dict(q=q, qs=slice(qw * q, qw * (q + 1))) for q in range(RWKV_QUADS)]
        for d in units:
            q = d["q"]
            d["y"] = yv_ref[slot, q] + _dot(prb_ref[slot, q], _head_block_diag(u_ref[slot, q], head_id, heads))
        yield
        stacked = []
        for d in units:
            qs = d["qs"]
            stacked += [d["y"], r_ref[0, rows, qs].astype(F32) * k_ref[0, rows, qs] * rk_ref[:, qs]]
        sums = _head_sums(jnp.concatenate(stacked, axis=0), ones)
        for i, d in enumerate(units):
            d["dev"] = d["y"] - sums[2 * i * tc:(2 * i + 1) * tc] * (1.0 / RWKV_HEAD)
            d["bonus"] = sums[(2 * i + 1) * tc:(2 * i + 2) * tc]
        yield
        var = _head_sums(jnp.concatenate([d["dev"] * d["dev"] for d in units], axis=0), ones)
        for i, d in enumerate(units):
            qs = d["qs"]
            rstd = lax.rsqrt(var[i * tc:(i + 1) * tc] * (1.0 / RWKV_HEAD) + GN_EPS)
            yn = d["dev"] * rstd * lnw_ref[:, qs] + lnb_ref[:, qs]
            y_ref[0, rows, qs] = bf((yn + d["bonus"] * v_ref[0, rows, qs]) * gate_ref[0, rows, qs])

    def chain(chunks):
        for c in chunks:
            yield from _alternating(*([emit_step(c - 1)] if c > 0 else []), recur_step(c))

    groups = [list(range(g, g + RWKV_PREP_CHUNKS)) for g in range(0, nchunks, RWKV_PREP_CHUNKS)]
    _alternate(prepare(groups[0]))
    for prev, nxt in zip(groups[:-1], groups[1:]):
        _alternate(prepare(nxt), chain(prev))
    _alternate(chain(groups[-1]))
    _alternate(emit_step(nchunks - 1))


def _rwkv(r, lw, k, v, kkn, b, gate, r_k, ln_w, ln_b):
    bsz, seqlen, _ = r.shape
    nchunks = RWKV_TB // RWKV_CHUNK
    tok = pl.BlockSpec((1, RWKV_TB, D_RWKV), lambda bi, t: (bi, t, 0))
    gate_spec = pl.BlockSpec((1, RWKV_TB, D_RWKV), lambda bi, t: (bi, t, 1))
    vec = pl.BlockSpec((1, D_RWKV), lambda bi, t: (0, 0))
    unit = lambda rows, cols, dt: pltpu.VMEM((RWKV_RING, RWKV_QUADS, rows, cols), dt)
    return pl.pallas_call(
        _rwkv_kernel,
        grid=(bsz, seqlen // RWKV_TB),
        in_specs=[tok] * 6 + [gate_spec] + [vec] * 3,
        out_specs=tok,
        out_shape=jax.ShapeDtypeStruct(r.shape, BF16),
        scratch_shapes=[
            pltpu.VMEM((RWKV_QUADS, LANES, 2 * LANES), F32),
            unit(2 * RWKV_CHUNK, 2 * LANES, BF16),
            unit(RWKV_CHUNK, 2 * LANES, F32),
            unit(LANES, 4 * RWKV_CHUNK, BF16),
            unit(RWKV_CHUNK, 2 * LANES, BF16),
            unit(RWKV_CHUNK, 2 * LANES, F32),
            unit(LANES, 2 * LANES, F32),
            unit(RWKV_CHUNK, 2 * LANES, BF16),
        ],
        compiler_params=pltpu.CompilerParams(
            dimension_semantics=("arbitrary", "arbitrary"), vmem_limit_bytes=VMEM_LIMIT),
        name="rwkv7",
    )(r, lw, k, v, kkn, b, gate, r_k, ln_w, ln_b)


def _out_kernel(y5_ref, gate_ref, yr_ref, x_ref, gluw_ref, glub_ref, wout_ref, fg_ref, o_ref):
    def sub_tile(s):
        rs = slice(OUT_SUB * s, OUT_SUB * (s + 1))
        y = y5_ref[0, rs, :]
        half = 0.5 * y
        y = half + half * jnp.tanh(y * (GELU_C + (GELU_C * 0.044715) * (y * y)))
        yield
        glu = _sigmoid(_dot(y.astype(BF16), gluw_ref[...]) + glub_ref[...])
        ys = y * glu * gate_ref[0, rs, :]
        ycat = jnp.concatenate([ys.astype(BF16), yr_ref[0, rs, :]], axis=-1)
        yield
        xn = x_ref[0, rs, :] + _dot(ycat, wout_ref[...])
        yield
        ms = jnp.mean(xn * xn, axis=-1, keepdims=True)
        o_ref[0, rs, :] = xn * lax.rsqrt(ms + NORM_EPS) * fg_ref[...]

    _alternate(*[_delayed(sub_tile(s), SUB_TILE_SKEW * s) for s in range(TM_OUT // OUT_SUB)])


def _outproj(y5, gate, yr, x, glu_w, glu_b, w_out, final_g):
    bsz, seqlen, _ = x.shape
    tok = lambda w: pl.BlockSpec((1, TM_OUT, w), lambda b, t: (b, t, 0))
    full = lambda a: pl.BlockSpec(a.shape, lambda b, t: (0,) * a.ndim)
    return pl.pallas_call(
        _out_kernel,
        grid=(bsz, seqlen // TM_OUT),
        in_specs=[tok(D_S5), tok(D_S5), tok(D_RWKV), tok(D_MODEL),
                  full(glu_w), full(glu_b), full(w_out), full(final_g)],
        out_specs=tok(D_MODEL),
        out_shape=jax.ShapeDtypeStruct(x.shape, F32),
        compiler_params=pltpu.CompilerParams(
            dimension_semantics=("arbitrary", "arbitrary"), vmem_limit_bytes=VMEM_LIMIT),
        name="outproj",
    )(y5, gate, yr, x, glu_w, glu_b, w_out, final_g)


@jax.jit
def _forward(x, norm_g, w_in, s5_lam_re, s5_lam_im, s5_log_dt, s5_b_re, s5_b_im, s5_c_re, s5_c_im,
             s5_d, s5_glu_w, s5_glu_b, rwkv_mu, rwkv_w0, rwkv_w2, rwkv_a0, rwkv_a2, rwkv_k_k, rwkv_k_a,
             rwkv_r_k, rwkv_ln_w, rwkv_ln_b, w_out, final_g):
    assert w_in.shape[0] == 1, "single-layer block"
    for l in range(1):
        zeros = jnp.zeros((LOW_RANK, D_RWKV), F32)
        w2a2 = jnp.concatenate(
            [jnp.concatenate([rwkv_w2[l], zeros], axis=1), jnp.concatenate([zeros, rwkv_a2[l]], axis=1)],
            axis=0).astype(BF16)
        row = lambda a: a.reshape(1, -1).astype(F32)
        u, gate, r, lw, k, v, kkn, b = _inproj(
            x, row(norm_g[l]), w_in[l].astype(BF16), row(rwkv_mu[l]), w2a2, row(rwkv_w0[l]),
            row(rwkv_a0[l]), row(rwkv_k_k[l]), row(rwkv_k_a[l]))
        ein, toep, fout, lam16 = _s5_operators(
            s5_lam_re[l], s5_lam_im[l], s5_log_dt[l], s5_b_re[l], s5_b_im[l], s5_c_re[l], s5_c_im[l])
        y5 = _s5(u, ein, toep, fout, lam16, row(s5_d[l]))
        yr = _rwkv(r, lw, k, v, kkn, b, gate, row(rwkv_r_k[l]), row(rwkv_ln_w[l]), row(rwkv_ln_b[l]))
        x = _outproj(y5, gate, yr, x, s5_glu_w[l].astype(BF16), row(s5_glu_b[l]),
                     w_out[l].astype(BF16), row(final_g))
    return x


def kernel(x, norm_g, w_in, s5_lam_re, s5_lam_im, s5_log_dt, s5_b_re, s5_b_im, s5_c_re, s5_c_im, s5_d, s5_glu_w, s5_glu_b, rwkv_mu, rwkv_w0, rwkv_w2, rwkv_a0, rwkv_a2, rwkv_k_k, rwkv_k_a, rwkv_r_k, rwkv_ln_w, rwkv_ln_b, w_out, final_g):
    return _forward(x, norm_g, w_in, s5_lam_re, s5_lam_im, s5_log_dt, s5_b_re, s5_b_im, s5_c_re, s5_c_im,
                    s5_d, s5_glu_w, s5_glu_b, rwkv_mu, rwkv_w0, rwkv_w2, rwkv_a0, rwkv_a2, rwkv_k_k,
                    rwkv_k_a, rwkv_r_k, rwkv_ln_w, rwkv_ln_b, w_out, final_g)
```

```python
import math

import jax
import jax.numpy as jnp
from jax import lax
from jax.experimental import pallas as pl
from jax.experimental.pallas import tpu as pltpu

F32 = jnp.float32
BF16 = jnp.bfloat16

D_MODEL = 1024
D_S5 = 512
D_RWKV = 512
S5_GROUP = 16
S5_GROUPS = 32
S5_STATE = 64
RWKV_HEAD = 64
LOW_RANK = 64
D_SHIFT = 3 * D_RWKV + 2 * LOW_RANK
D_IN = 2 * D_S5 + D_SHIFT + D_RWKV
NORM_EPS = 1e-6
GN_EPS = 64e-5
GELU_C = math.sqrt(2.0 / math.pi)

LANES = 128
TM_IN = 1024
IN_SUB = 256
SUB_TILE_SKEW = 2
TM_OUT = 1024
OUT_SUB = 512
S5_CHUNK = 16
S5_BATCH = 2
S5_PIPE_SKEW = 6
S5_LANE_BLOCKS = D_S5 // LANES
S5_GROUPS_PER_BLOCK = LANES // S5_GROUP
S5_BLOCK_STATE = 2 * S5_GROUPS_PER_BLOCK * S5_STATE
RWKV_CHUNK = 64
RWKV_TB = 2048
RWKV_PREP_CHUNKS = 4
RWKV_RING = 3 * RWKV_PREP_CHUNKS
HEAD_PAIRS = D_RWKV // LANES
RWKV_QUADS = HEAD_PAIRS // 2
VMEM_LIMIT = 56 * 1024 * 1024


def _dot(a, b):
    return jnp.dot(a, b, preferred_element_type=F32)


def _dot_nt(a, b):
    return lax.dot_general(a, b, (((1,), (1,)), ((), ())), preferred_element_type=F32)


def _dot_tn(a, b):
    return lax.dot_general(a, b, (((0,), (0,)), ((), ())), preferred_element_type=F32)


def _split2(x):
    hi = x.astype(BF16)
    return hi, (x - hi.astype(F32)).astype(BF16)


def _head_sums(x, ones):
    return _dot(x.astype(BF16), ones)


def _dot_exact_lhs(m, x):
    hi, lo = _split2(x)
    return _dot(m, hi) + _dot(m, lo)


def _sigmoid(x):
    return 0.5 + 0.5 * jnp.tanh(0.5 * x)


def _head_ones(width):
    r = lax.broadcasted_iota(jnp.int32, (width, width), 0)
    c = lax.broadcasted_iota(jnp.int32, (width, width), 1)
    return (r // RWKV_HEAD) == (c // RWKV_HEAD)


def _alternating(*stage_generators):
    pending = list(stage_generators)
    while pending:
        for g in list(pending):
            if next(g, StopIteration) is StopIteration:
                pending.remove(g)
            else:
                yield


def _alternate(*stage_generators):
    for _ in _alternating(*stage_generators):
        pass


def _delayed(stage_generator, stages):
    for _ in range(stages):
        yield
    yield from stage_generator


def _inproj_kernel(x_ref, g_ref, win_ref, mu_ref, w2a2_ref, w0_ref, a0_ref, kk_ref, ka_ref,
                   u_ref, gate_ref, r_ref, lw_ref, k_ref, v_ref, kkn_ref, b_ref, carry_ref):
    t = pl.program_id(1)

    @pl.when(t == 0)
    def _():
        carry_ref[...] = jnp.zeros_like(carry_ref)

    ones = _head_ones(2 * LANES).astype(BF16)
    carries = {-1: carry_ref[...]}

    def sub_tile(s):
        rs = slice(IN_SUB * s, IN_SUB * (s + 1))
        x = x_ref[0, rs, :]
        ms = jnp.mean(x * x, axis=-1, keepdims=True)
        h = (x * lax.rsqrt(ms + NORM_EPS) * g_ref[...]).astype(BF16)

        def proj(lo, hi):
            return _dot(h, win_ref[:, lo:hi])

        yield
        u_ref[0, rs, :] = proj(0, D_S5)
        yield
        z5 = proj(D_S5, 2 * D_S5)
        gate_ref[0, rs, :D_S5] = (z5 * _sigmoid(z5)).astype(BF16)
        yield
        zr = proj(2 * D_S5 + D_SHIFT, D_IN)
        gate_ref[0, rs, D_S5:] = (zr * _sigmoid(zr)).astype(BF16)
        yield
        rw = proj(2 * D_S5, 2 * D_S5 + D_SHIFT)
        carries[s] = rw[IN_SUB - 1:IN_SUB, :]
        yield
        prev = pltpu.roll(rw, 1, axis=0)
        row = lax.broadcasted_iota(jnp.int32, rw.shape, 0)
        prev = jnp.where(row == 0, carries[s - 1], prev)
        sh = rw + (prev - rw) * mu_ref[...]
        r = sh[:, 0:D_RWKV]
        k = sh[:, D_RWKV:2 * D_RWKV]
        v = sh[:, 2 * D_RWKV:3 * D_RWKV]
        wa = sh[:, 3 * D_RWKV:]
        lane = lax.broadcasted_iota(jnp.int32, wa.shape, 1)
        wa = jnp.where(lane < LOW_RANK, jnp.tanh(wa), wa)
        za = _dot(wa.astype(BF16), w2a2_ref[...])
        r_ref[0, rs, :] = r.astype(BF16)
        v_ref[0, rs, :] = v.astype(BF16)
        yield
        lw_ref[0, rs, :] = -math.exp(-0.5) * _sigmoid(w0_ref[...] + za[:, :D_RWKV])
        iclr = _sigmoid(a0_ref[...] + za[:, D_RWKV:])
        k_ref[0, rs, :] = (k * (1.0 + (iclr - 1.0) * ka_ref[...])).astype(BF16)
        kk = k * kk_ref[...]
        sq = kk * kk
        ss = jnp.concatenate(
            [_head_sums(sq[:, 2 * LANES * q:2 * LANES * (q + 1)], ones) for q in range(RWKV_QUADS)], axis=1)
        yield
        kkn = kk * lax.rsqrt(jnp.maximum(ss, 1e-24))
        kkn_ref[0, rs, :] = kkn.astype(BF16)
        b_ref[0, rs, :] = (kkn * iclr).astype(BF16)

    n_sub = TM_IN // IN_SUB
    _alternate(*[_delayed(sub_tile(s), SUB_TILE_SKEW * s) for s in range(n_sub)])
    carry_ref[...] = carries[n_sub - 1]


def _inproj(x, norm_g, w_in, mu, w2a2, w0, a0, k_k, k_a):
    bsz, seqlen, _ = x.shape
    tok = lambda w: pl.BlockSpec((1, TM_IN, w), lambda b, t: (b, t, 0))
    full = lambda a: pl.BlockSpec(a.shape, lambda b, t: (0,) * a.ndim)
    out = lambda w, dt: jax.ShapeDtypeStruct((bsz, seqlen, w), dt)
    params = (norm_g, w_in, mu, w2a2, w0, a0, k_k, k_a)
    return pl.pallas_call(
        _inproj_kernel,
        grid=(bsz, seqlen // TM_IN),
        in_specs=[tok(D_MODEL)] + [full(a) for a in params],
        out_specs=[tok(D_S5), tok(2 * D_S5)] + [tok(D_RWKV)] * 6,
        out_shape=[out(D_S5, F32), out(2 * D_S5, BF16)]
        + [out(D_RWKV, F32 if i == 1 else BF16) for i in range(6)],
        scratch_shapes=[pltpu.VMEM((1, D_SHIFT), F32)],
        compiler_params=pltpu.CompilerParams(
            dimension_semantics=("arbitrary", "arbitrary"), vmem_limit_bytes=VMEM_LIMIT),
        name="inproj",
    )(x, *params)


def _s5_kernel(u_ref, ein_ref, toep_ref, fout_ref, lam_ref, d_ref, y_ref, e_ref, sp_ref):
    nch = e_ref.shape[1]
    half = S5_BLOCK_STATE // 2
    pairs = S5_CHUNK // 2
    lre = lam_ref[0, :, :half]
    lim = lam_ref[0, :, half:]
    d = d_ref[...]

    def pipeline(bi):
        xs = [u_ref[bi, pl.ds(j, nch, stride=S5_CHUNK), :] for j in range(S5_CHUNK)]
        xcat = jnp.concatenate([x.astype(BF16) for x in xs], axis=1)
        yield
        e_ref[bi] = _dot(xcat, ein_ref[0])
        yield
        intra = {}

        def scan():
            sre = sim = jnp.zeros((1, half), F32)
            for c in range(nch):
                sp_ref[bi, c:c + 1, :half] = sre
                sp_ref[bi, c:c + 1, half:] = sim
                e = e_ref[bi, c:c + 1, :]
                sre, sim = lre * sre - lim * sim + e[:, :half], lre * sim + lim * sre + e[:, half:]
                if (c + 1) % (nch // pairs) == 0:
                    yield

        def toeplitz():
            for tp in range(pairs):
                intra[tp] = _dot(xcat[:, :2 * LANES * (tp + 1)],
                                 toep_ref[0, 2 * LANES * (pairs - 1 - tp):, :])
                yield

        yield from _alternating(scan(), toeplitz())
        sp = sp_ref[bi].astype(BF16)
        for tp in range(pairs):
            y = intra[tp] + _dot(sp, fout_ref[0, :, 2 * LANES * tp:2 * LANES * (tp + 1)])
            for i in range(2):
                t = 2 * tp + i
                y_ref[bi, pl.ds(t, nch, stride=S5_CHUNK), :] = y[:, LANES * i:LANES * (i + 1)] + d * xs[t]
            yield

    _alternate(*[_delayed(pipeline(bi), S5_PIPE_SKEW * bi) for bi in range(S5_BATCH)])


def _cmul(ar, ai, br, bi):
    return ar * br - ai * bi, ar * bi + ai * br


def _block_diag_expand(compact, width, inner, transposed):
    k = compact.shape[1] if transposed else compact.shape[2]
    group_w = S5_GROUPS_PER_BLOCK * inner
    kk = jnp.arange(k)
    ww = jnp.arange(width)
    sel = ((kk[:, None] // inner == ww[None, :] // group_w)
           & (kk[:, None] % inner == ww[None, :] % inner)).astype(F32)
    wide_group = (ww % group_w) // inner
    if transposed:
        full = jnp.matmul(sel.T, compact)
        narrow_group = (jnp.arange(compact.shape[2]) % LANES) // S5_GROUP
        keep = wide_group[:, None] == narrow_group[None, :]
    else:
        full = jnp.matmul(compact, sel)
        narrow_group = (jnp.arange(compact.shape[1]) % LANES) // S5_GROUP
        keep = narrow_group[:, None] == wide_group[None, :]
    return jnp.where(keep, full, 0.0)


def _s5_operators(lam_re, lam_im, log_dt, b_re, b_im, c_re, c_im):
    c16, q4, gl = S5_CHUNK, S5_LANE_BLOCKS, S5_GROUPS_PER_BLOCK
    rows = c16 * LANES
    dt = jnp.exp(log_dt)[:, None]
    n = jnp.arange(c16 + 1, dtype=F32)[:, None, None]
    mag = jnp.exp(n * (lam_re * dt)[None])
    ang = n * (lam_im * dt)[None]
    pw_r, pw_i = mag * jnp.cos(ang), mag * jnp.sin(ang)
    den = lam_re * lam_re + lam_im * lam_im
    q_r, q_i = _cmul(pw_r[1] - 1.0, pw_i[1], lam_re / den, -lam_im / den)
    bt_r, bt_i = jnp.swapaxes(b_re, 1, 2), jnp.swapaxes(b_im, 1, 2)
    bb_r, bb_i = _cmul(q_r[:, None, :], q_i[:, None, :], bt_r, bt_i)
    e_r, e_i = _cmul(pw_r[:c16, :, None, :], pw_i[:c16, :, None, :], bb_r[None], bb_i[None])

    def by_block(a):
        a = a.reshape(a.shape[0], q4, gl * S5_GROUP, 2 * S5_STATE)
        return jnp.swapaxes(a, 0, 1).reshape(q4, -1, 2 * S5_STATE)

    ein = by_block(jnp.concatenate([e_r, e_i], axis=-1)[::-1])
    ein = _block_diag_expand(ein, S5_BLOCK_STATE, S5_STATE, False)

    f_r, f_i = _cmul(c_re[None], c_im[None], pw_r[1:, :, None, :], pw_i[1:, :, None, :])
    fo = jnp.swapaxes(by_block(jnp.concatenate([f_r, -f_i], axis=-1)), 1, 2)
    fo = _block_diag_expand(fo, S5_BLOCK_STATE, S5_STATE, True)

    ee = jnp.concatenate([e_r, -e_i], axis=-1)
    cc = jnp.concatenate([c_re, c_im], axis=-1).reshape(q4, LANES, 2 * S5_STATE)
    row_g = (jnp.arange(rows) % LANES) // S5_GROUP
    col_g = jnp.arange(LANES) // S5_GROUP
    halves = []
    for t2 in range(2):
        lags = [2 * d + t2 - j2 for d in reversed(range(c16 // 2)) for j2 in range(2)]
        a = jnp.stack([ee[lag] if lag >= 0 else jnp.zeros_like(ee[0]) for lag in lags])
        full = jnp.einsum('qrk,qck->qrc', by_block(a), cc, precision=lax.Precision.HIGHEST)
        halves.append(jnp.where(row_g[:, None] == col_g[None, :], full, 0.0))
    toep = jnp.concatenate(halves, axis=-1)

    lam16 = jnp.concatenate([pw_r[c16].reshape(q4, 1, gl * S5_STATE),
                             pw_i[c16].reshape(q4, 1, gl * S5_STATE)], axis=-1)
    return ein.astype(BF16), toep.astype(BF16), fo.astype(BF16), lam16


def _s5(u, ein, toep, fout, lam16, d):
    bsz, seqlen, _ = u.shape
    nch = seqlen // S5_CHUNK
    per_q = lambda a: pl.BlockSpec((1,) + a.shape[1:], lambda q, b: (q,) + (0,) * (a.ndim - 1))
    seq = pl.BlockSpec((S5_BATCH, seqlen, LANES), lambda q, b: (b, 0, q))
    state = pltpu.VMEM((S5_BATCH, nch, S5_BLOCK_STATE), F32)
    return pl.pallas_call(
        _s5_kernel,
        grid=(S5_LANE_BLOCKS, bsz // S5_BATCH),
        in_specs=[seq, per_q(ein), per_q(toep), per_q(fout), per_q(lam16),
                  pl.BlockSpec((1, LANES), lambda q, b: (0, q))],
        out_specs=seq,
        out_shape=jax.ShapeDtypeStruct(u.shape, F32),
        scratch_shapes=[state, state],
        compiler_params=pltpu.CompilerParams(
            dimension_semantics=("arbitrary", "arbitrary"), vmem_limit_bytes=VMEM_LIMIT),
        name="s5",
    )(u, ein, toep, fout, lam16, d)


def _stack_heads(x, lo):
    return jnp.concatenate([jnp.where(lo, x, 0.0), jnp.where(lo, 0.0, x)], axis=0)


def _pair_block_diag(x):
    zero = jnp.zeros((x.shape[0], LANES), x.dtype)
    return jnp.concatenate([jnp.concatenate([x[:, :LANES], zero], axis=1),
                            jnp.concatenate([zero, x[:, LANES:]], axis=1)], axis=0)


def _head_block_diag(x, head_id, order):
    return jnp.concatenate([jnp.where(head_id == h, x, 0.0) for h in order], axis=0)


def _rwkv_kernel(r_ref, lw_ref, k_ref, v_ref, kk_ref, b_ref, gate_ref, rk_ref, lnw_ref, lnb_ref,
                 y_ref, s_ref, wr_ref, w2_ref, bkt_ref, prb_ref, yv_ref, e2_ref, u_ref):
    tc = RWKV_CHUNK
    nchunks = RWKV_TB // tc
    qw = 2 * LANES

    @pl.when(pl.program_id(1) == 0)
    def _():
        s_ref[...] = jnp.zeros_like(s_ref)

    row = lax.broadcasted_iota(jnp.int32, (tc, qw), 0)
    lane = lax.broadcasted_iota(jnp.int32, (tc, qw), 1)
    col = jnp.bitwise_and(lane, RWKV_HEAD - 1)
    head_id = lane // RWKV_HEAD
    heads = tuple(range(qw // RWKV_HEAD))
    heads_swapped = (1, 0, 3, 2)
    eye_q = (row == col).astype(F32)
    row_p = lax.broadcasted_iota(jnp.int32, (tc, LANES), 0)
    lane_p = lax.broadcasted_iota(jnp.int32, (tc, LANES), 1)
    col_p = jnp.bitwise_and(lane_p, RWKV_HEAD - 1)
    lo_p = lane_p < RWKV_HEAD
    lo_p2 = lax.broadcasted_iota(jnp.int32, (2 * tc, LANES), 1) < RWKV_HEAD
    stril = row_p > col_p
    tril = row_p >= col_p
    state_bd = ((lax.broadcasted_iota(jnp.int32, (LANES, qw), 0) // RWKV_HEAD)
                == (jnp.bitwise_and(lax.broadcasted_iota(jnp.int32, (LANES, qw), 1), LANES - 1) // RWKV_HEAD))
    ones = _head_ones(qw).astype(BF16)
    tri = (lax.broadcasted_iota(jnp.int32, (tc, tc), 0)
           >= lax.broadcasted_iota(jnp.int32, (tc, tc), 1)).astype(BF16)
    bf = lambda t: t.astype(BF16)
    up, dn = slice(0, tc), slice(tc, 2 * tc)

    def prepare(chunks):
        units = []
        for c in chunks:
            rows = slice(c * tc, (c + 1) * tc)
            lw_all = lw_ref[0, rows, :]
            g_all = _dot_exact_lhs(tri, lw_all)
            for q in range(RWKV_QUADS):
                qs = slice(qw * q, qw * (q + 1))
                units.append(dict(slot=c % RWKV_RING, q=q, rows=rows, qs=qs, lw=lw_all[:, qs], g=g_all[:, qs]))
        yield
        for d in units:
            rows, qs, g = d["rows"], d["qs"], d["g"]
            m = 0.5 * g[tc - 1:tc, :]
            e_out = jnp.exp(m - g)
            em = jnp.exp(m)
            rt = r_ref[0, rows, qs] * jnp.exp(g - m)
            bt = b_ref[0, rows, qs] * e_out
            kt = k_ref[0, rows, qs] * e_out
            d["em"] = em
            d["at"] = bf(-kk_ref[0, rows, qs] * jnp.exp(g - d["lw"] - m))
            d["rt"], d["bt"], d["kt"] = bf(rt), bf(bt), bf(kt)
            wr_ref[d["slot"], d["q"], tc:, :] = bf(rt * em)
            bkt, e2 = [], []
            for j in range(2):
                ps = slice(LANES * j, LANES * (j + 1))
                bks = jnp.concatenate([bt[:, ps], kt[:, ps]], axis=0) * em[:, ps]
                bkt.append(bf(jnp.transpose(bks)))
                e2.append(jnp.transpose(jnp.broadcast_to(em[:, ps] * em[:, ps], (LANES, LANES))))
            bkt_ref[d["slot"], d["q"]] = jnp.concatenate(bkt, axis=1)
            e2_ref[d["slot"], d["q"]] = jnp.concatenate(e2, axis=1)
        yield
        for d in units:
            d["o"] = []
            for j in range(2):
                ps = slice(LANES * j, LANES * (j + 1))
                bk = jnp.concatenate([d["bt"][:, ps], d["kt"][:, ps]], axis=0)
                kb = jnp.concatenate([d["kt"][:, ps], d["bt"][:, ps]], axis=0)
                rhs = jnp.concatenate([jnp.where(lo_p2, bk, 0.0), jnp.where(lo_p2, 0.0, kb)], axis=0)
                lhs = jnp.concatenate([d["at"][:, ps], d["rt"][:, ps]], axis=0)
                d["o"].append(_dot_nt(lhs, rhs))
        yield
        for d in units:
            h0 = [o[:, :LANES] for o in d["o"]]
            h1 = [o[:, LANES:] for o in d["o"]]

            def pick(rws, first, second, mask):
                return jnp.concatenate(
                    [jnp.where(mask, jnp.where(lo_p, first[j][rws], second[j][rws]), 0.0) for j in range(2)],
                    axis=1)

            d["ap"] = pick(up, h0, h1, stril)
            h0, h1 = [bf(h) for h in h0], [bf(h) for h in h1]
            d["p_ak"] = pick(up, h1, h0, stril)
            d["p_rk"] = pick(dn, h1, h0, tril)
            prb_ref[d["slot"], d["q"]] = pick(dn, h0, h1, tril)
            d["tinv"] = eye_q + d["ap"]
        yield
        levels = int(math.log2(tc)) - 1
        for d in units:
            ap = bf(d["ap"])
            d["ap"] = bf(_dot(ap, _head_block_diag(ap, head_id, heads)))
        for lvl in range(levels):
            yield
            for d in units:
                rhs = _head_block_diag(d["ap"], head_id, heads)
                if lvl < levels - 1:
                    out = _dot(jnp.concatenate([bf(d["tinv"]), d["ap"]], axis=0), rhs)
                    d["tinv"], d["ap"] = d["tinv"] + out[:tc], bf(out[tc:])
                else:
                    d["tinv"] = d["tinv"] + _dot(bf(d["tinv"]), rhs)
        yield
        for d in units:
            d["tinv"] = bf(d["tinv"])
            vsw = _head_block_diag(v_ref[0, d["rows"], d["qs"]], head_id, heads_swapped)
            out = _dot(jnp.concatenate([d["p_ak"], d["p_rk"]], axis=0), vsw)
            d["xv"] = bf(out[:tc])
            yv_ref[d["slot"], d["q"]] = out[tc:]
        yield
        for d in units:
            w1s, w2 = [], []
            for j in range(2):
                ps = slice(LANES * j, LANES * (j + 1))
                rhs = jnp.concatenate([_stack_heads(d["at"][:, ps], lo_p),
                                       _stack_heads(d["xv"][:, ps], lo_p)], axis=1)
                w = _dot(d["tinv"][:, ps], rhs)
                w1s.append(bf(w[:, :LANES] * d["em"][:, ps]))
                w2.append(w[:, LANES:])
            wr_ref[d["slot"], d["q"], :tc, :] = jnp.concatenate(w1s, axis=1)
            w2_ref[d["slot"], d["q"]] = jnp.concatenate(w2, axis=1)

    def recur_step(c):
        rows, slot = slice(c * tc, (c + 1) * tc), c % RWKV_RING
        units = [dict(q=q, qs=slice(qw * q, qw * (q + 1))) for q in range(RWKV_QUADS)]
        for d in units:
            d["h0"] = s_ref[d["q"]]
        for d in units:
            out = _dot(wr_ref[slot, d["q"]], _pair_block_diag(bf(d["h0"])))
            d["u"] = out[:tc] + w2_ref[slot, d["q"]]
            d["yr"] = out[tc:]
        yield
        for d in units:
            q = d["q"]
            uv = jnp.concatenate([bf(d["u"]), v_ref[0, rows, d["qs"]]], axis=0)
            upd = _dot(bkt_ref[slot, q], _pair_block_diag(uv))
            s_ref[q] = d["h0"] * e2_ref[slot, q] + jnp.where(state_bd, upd, 0.0)
            u_ref[slot, q] = bf(d["u"])
            yv_ref[slot, q] = yv_ref[slot, q] + d["yr"]

    def emit_step(c):
        rows, slot = slice(c * tc, (c + 1) * tc), c % RWKV_RING
        units = [dict(q=q, qs=slice(qw * q, qw * (q + 1))) for q in range(RWKV_QUADS)]
        for d in units:
            q = d["q"]
            d["y"] = yv_ref[slot, q] + _dot(prb_ref[slot, q], _head_block_diag(u_ref[slot, q], head_id, heads))
        yield
        stacked = []
        for d in units:
            qs = d["qs"]
            stacked += [d["y"], r_ref[0, rows, qs].astype(F32) * k_ref[0, rows, qs] * rk_ref[:, qs]]
        sums = _head_sums(jnp.concatenate(stacked, axis=0), ones)
        for i, d in enumerate(units):
            d["dev"] = d["y"] - sums[2 * i * tc:(2 * i + 1) * tc] * (1.0 / RWKV_HEAD)
            d["bonus"] = sums[(2 * i + 1) * tc:(2 * i + 2) * tc]
        yield
        var = _head_sums(jnp.concatenate([d["dev"] * d["dev"] for d in units], axis=0), ones)
        for i, d in enumerate(units):
            qs = d["qs"]
            rstd = lax.rsqrt(var[i * tc:(i + 1) * tc] * (1.0 / RWKV_HEAD) + GN_EPS)
            yn = d["dev"] * rstd * lnw_ref[:, qs] + lnb_ref[:, qs]
            y_ref[0, rows, qs] = bf((yn + d["bonus"] * v_ref[0, rows, qs]) * gate_ref[0, rows, qs])

    def chain(chunks):
        for c in chunks:
            yield from _alternating(*([emit_step(c - 1)] if c > 0 else []), recur_step(c))

    groups = [list(range(g, g + RWKV_PREP_CHUNKS)) for g in range(0, nchunks, RWKV_PREP_CHUNKS)]
    _alternate(prepare(groups[0]))
    for prev, nxt in zip(groups[:-1], groups[1:]):
        _alternate(prepare(nxt), chain(prev))
    _alternate(chain(groups[-1]))
    _alternate(emit_step(nchunks - 1))


def _rwkv(r, lw, k, v, kkn, b, gate, r_k, ln_w, ln_b):
    bsz, seqlen, _ = r.shape
    nchunks = RWKV_TB // RWKV_CHUNK
    tok = pl.BlockSpec((1, RWKV_TB, D_RWKV), lambda bi, t: (bi, t, 0))
    gate_spec = pl.BlockSpec((1, RWKV_TB, D_RWKV), lambda bi, t: (bi, t, 1))
    vec = pl.BlockSpec((1, D_RWKV), lambda bi, t: (0, 0))
    unit = lambda rows, cols, dt: pltpu.VMEM((RWKV_RING, RWKV_QUADS, rows, cols), dt)
    return pl.pallas_call(
        _rwkv_kernel,
        grid=(bsz, seqlen // RWKV_TB),
        in_specs=[tok] * 6 + [gate_spec] + [vec] * 3,
        out_specs=tok,
        out_shape=jax.ShapeDtypeStruct(r.shape, BF16),
        scratch_shapes=[
            pltpu.VMEM((RWKV_QUADS, LANES, 2 * LANES), F32),
            unit(2 * RWKV_CHUNK, 2 * LANES, BF16),
            unit(RWKV_CHUNK, 2 * LANES, F32),
            unit(LANES, 4 * RWKV_CHUNK, BF16),
            unit(RWKV_CHUNK, 2 * LANES, BF16),
            unit(RWKV_CHUNK, 2 * LANES, F32),
            unit(LANES, 2 * LANES, F32),
            unit(RWKV_CHUNK, 2 * LANES, BF16),
        ],
        compiler_params=pltpu.CompilerParams(
            dimension_semantics=("arbitrary", "arbitrary"), vmem_limit_bytes=VMEM_LIMIT),
        name="rwkv7",
    )(r, lw, k, v, kkn, b, gate, r_k, ln_w, ln_b)


def _out_kernel(y5_ref, gate_ref, yr_ref, x_ref, gluw_ref, glub_ref, wout_ref, fg_ref, o_ref):
    def sub_tile(s):
        rs = slice(OUT_SUB * s, OUT_SUB * (s + 1))
        y = y5_ref[0, rs, :]
        half = 0.5 * y
        y = half + half * jnp.tanh(y * (GELU_C + (GELU_C * 0.044715) * (y * y)))
        yield
        glu = _sigmoid(_dot(y.astype(BF16), gluw_ref[...]) + glub_ref[...])
        ys = y * glu * gate_ref[0, rs, :]
        ycat = jnp.concatenate([ys.astype(BF16), yr_ref[0, rs, :]], axis=-1)
        yield
        xn = x_ref[0, rs, :] + _dot(ycat, wout_ref[...])
        yield
        ms = jnp.mean(xn * xn, axis=-1, keepdims=True)
        o_ref[0, rs, :] = xn * lax.rsqrt(ms + NORM_EPS) * fg_ref[...]

    _alternate(*[_delayed(sub_tile(s), SUB_TILE_SKEW * s) for s in range(TM_OUT // OUT_SUB)])


def _outproj(y5, gate, yr, x, glu_w, glu_b, w_out, final_g):
    bsz, seqlen, _ = x.shape
    tok = lambda w: pl.BlockSpec((1, TM_OUT, w), lambda b, t: (b, t, 0))
    full = lambda a: pl.BlockSpec(a.shape, lambda b, t: (0,) * a.ndim)
    return pl.pallas_call(
        _out_kernel,
        grid=(bsz, seqlen // TM_OUT),
        in_specs=[tok(D_S5), tok(D_S5), tok(D_RWKV), tok(D_MODEL),
                  full(glu_w), full(glu_b), full(w_out), full(final_g)],
        out_specs=tok(D_MODEL),
        out_shape=jax.ShapeDtypeStruct(x.shape, F32),
        compiler_params=pltpu.CompilerParams(
            dimension_semantics=("arbitrary", "arbitrary"), vmem_limit_bytes=VMEM_LIMIT),
        name="outproj",
    )(y5, gate, yr, x, glu_w, glu_b, w_out, final_g)


@jax.jit
def _forward(x, norm_g, w_in, s5_lam_re, s5_lam_im, s5_log_dt, s5_b_re, s5_b_im, s5_c_re, s5_c_im,
             s5_d, s5_glu_w, s5_glu_b, rwkv_mu, rwkv_w0, rwkv_w2, rwkv_a0, rwkv_a2, rwkv_k_k, rwkv_k_a,
             rwkv_r_k, rwkv_ln_w, rwkv_ln_b, w_out, final_g):
    assert w_in.shape[0] == 1, "single-layer block"
    for l in range(1):
        zeros = jnp.zeros((LOW_RANK, D_RWKV), F32)
        w2a2 = jnp.concatenate(
            [jnp.concatenate([rwkv_w2[l], zeros], axis=1), jnp.concatenate([zeros, rwkv_a2[l]], axis=1)],
            axis=0).astype(BF16)
        row = lambda a: a.reshape(1, -1).astype(F32)
        u, gate, r, lw, k, v, kkn, b = _inproj(
            x, row(norm_g[l]), w_in[l].astype(BF16), row(rwkv_mu[l]), w2a2, row(rwkv_w0[l]),
            row(rwkv_a0[l]), row(rwkv_k_k[l]), row(rwkv_k_a[l]))
        ein, toep, fout, lam16 = _s5_operators(
            s5_lam_re[l], s5_lam_im[l], s5_log_dt[l], s5_b_re[l], s5_b_im[l], s5_c_re[l], s5_c_im[l])
        y5 = _s5(u, ein, toep, fout, lam16, row(s5_d[l]))
        yr = _rwkv(r, lw, k, v, kkn, b, gate, row(rwkv_r_k[l]), row(rwkv_ln_w[l]), row(rwkv_ln_b[l]))
        x = _outproj(y5, gate, yr, x, s5_glu_w[l].astype(BF16), row(s5_glu_b[l]),
                     w_out[l].astype(BF16), row(final_g))
    return x


def kernel(x, norm_g, w_in, s5_lam_re, s5_lam_im, s5_log_dt, s5_b_re, s5_b_im, s5_c_re, s5_c_im, s5_d, s5_glu_w, s5_glu_b, rwkv_mu, rwkv_w0, rwkv_w2, rwkv_a0, rwkv_a2, rwkv_k_k, rwkv_k_a, rwkv_r_k, rwkv_ln_w, rwkv_ln_b, w_out, final_g):
    return _forward(x, norm_g, w_in, s5_lam_re, s5_lam_im, s5_log_dt, s5_b_re, s5_b_im, s5_c_re, s5_c_im,
                    s5_d, s5_glu_w, s5_glu_b, rwkv_mu, rwkv_w0, rwkv_w2, rwkv_a0, rwkv_a2, rwkv_k_k,
                    rwkv_k_a, rwkv_r_k, rwkv_ln_w, rwkv_ln_b, w_out, final_g)
```

```python
import math

import jax
import jax.numpy as jnp
from jax import lax
from jax.experimental import pallas as pl
from jax.experimental.pallas import tpu as pltpu

F32 = jnp.float32
BF16 = jnp.bfloat16

D_MODEL = 1024
D_S5 = 512
D_RWKV = 512
S5_GROUP = 16
S5_STATE = 64
RWKV_HEAD = 64
LOW_RANK = 64
D_SHIFT = 3 * D_RWKV + 2 * LOW_RANK
D_IN = 2 * D_S5 + D_SHIFT + D_RWKV
NORM_EPS = 1e-6
GN_EPS = 64e-5
GELU_C = math.sqrt(2.0 / math.pi)

LANES = 128
TM_IN = 1024
IN_SUB = 256
SUB_TILE_SKEW = 2
TM_OUT = 1024
OUT_SUB = 512
S5_CHUNK = 16
S5_BATCH = 2
S5_PIPE_SKEW = 6
S5_LANE_BLOCKS = D_S5 // LANES
S5_GROUPS_PER_BLOCK = LANES // S5_GROUP
S5_BLOCK_STATE = 2 * S5_GROUPS_PER_BLOCK * S5_STATE
RWKV_CHUNK = 64
RWKV_TB = 2048
RWKV_PREP_CHUNKS = 4
RWKV_RING = 3 * RWKV_PREP_CHUNKS
HEAD_PAIRS = D_RWKV // LANES
RWKV_QUADS = HEAD_PAIRS // 2
VMEM_LIMIT = 56 * 1024 * 1024


def _dot(a, b):
    return jnp.dot(a, b, preferred_element_type=F32)


def _dot_nt(a, b):
    return lax.dot_general(a, b, (((1,), (1,)), ((), ())), preferred_element_type=F32)


def _split2(x):
    hi = x.astype(BF16)
    return hi, (x - hi.astype(F32)).astype(BF16)


def _head_sums(x, ones):
    return _dot(x.astype(BF16), ones)


def _dot_exact_lhs(m, x):
    hi, lo = _split2(x)
    return _dot(m, hi) + _dot(m, lo)


def _sigmoid(x):
    return 0.5 + 0.5 * jnp.tanh(0.5 * x)


def _head_ones(width):
    r = lax.broadcasted_iota(jnp.int32, (width, width), 0)
    c = lax.broadcasted_iota(jnp.int32, (width, width), 1)
    return (r // RWKV_HEAD) == (c // RWKV_HEAD)


def _alternating(*stage_generators):
    pending = list(stage_generators)
    while pending:
        for g in list(pending):
            if next(g, StopIteration) is StopIteration:
                pending.remove(g)
            else:
                yield


def _alternate(*stage_generators):
    for _ in _alternating(*stage_generators):
        pass


def _delayed(stage_generator, stages):
    for _ in range(stages):
        yield
    yield from stage_generator


def _inproj_kernel(x_ref, g_ref, win_ref, mu_ref, w2a2_ref, w0_ref, a0_ref, kk_ref, ka_ref,
                   u_ref, gate_ref, r_ref, lw_ref, k_ref, v_ref, kkn_ref, b_ref, carry_ref):
    t = pl.program_id(1)

    @pl.when(t == 0)
    def _():
        carry_ref[...] = jnp.zeros_like(carry_ref)

    ones = _head_ones(2 * LANES).astype(BF16)
    carries = {-1: carry_ref[...]}

    def sub_tile(s):
        rs = slice(IN_SUB * s, IN_SUB * (s + 1))
        x = x_ref[0, rs, :]
        ms = jnp.mean(x * x, axis=-1, keepdims=True)
        h = (x * lax.rsqrt(ms + NORM_EPS) * g_ref[...]).astype(BF16)

        def proj(lo, hi):
            return _dot(h, win_ref[:, lo:hi])

        yield
        u_ref[0, rs, :] = proj(0, D_S5)
        yield
        z5 = proj(D_S5, 2 * D_S5)
        gate_ref[0, rs, :D_S5] = (z5 * _sigmoid(z5)).astype(BF16)
        yield
        zr = proj(2 * D_S5 + D_SHIFT, D_IN)
        gate_ref[0, rs, D_S5:] = (zr * _sigmoid(zr)).astype(BF16)
        yield
        rw = proj(2 * D_S5, 2 * D_S5 + D_SHIFT)
        carries[s] = rw[IN_SUB - 1:IN_SUB, :]
        yield
        prev = pltpu.roll(rw, 1, axis=0)
        row = lax.broadcasted_iota(jnp.int32, rw.shape, 0)
        prev = jnp.where(row == 0, carries[s - 1], prev)
        sh = rw + (prev - rw) * mu_ref[...]
        r = sh[:, 0:D_RWKV]
        k = sh[:, D_RWKV:2 * D_RWKV]
        v = sh[:, 2 * D_RWKV:3 * D_RWKV]
        wa = sh[:, 3 * D_RWKV:]
        lane = lax.broadcasted_iota(jnp.int32, wa.shape, 1)
        wa = jnp.where(lane < LOW_RANK, jnp.tanh(wa), wa)
        za = _dot(wa.astype(BF16), w2a2_ref[...])
        r_ref[0, rs, :] = r.astype(BF16)
        v_ref[0, rs, :] = v.astype(BF16)
        yield
        lw_ref[0, rs, :] = -math.exp(-0.5) * _sigmoid(w0_ref[...] + za[:, :D_RWKV])
        iclr = _sigmoid(a0_ref[...] + za[:, D_RWKV:])
        k_ref[0, rs, :] = (k * (1.0 + (iclr - 1.0) * ka_ref[...])).astype(BF16)
        kk = k * kk_ref[...]
        sq = kk * kk
        ss = jnp.concatenate(
            [_head_sums(sq[:, 2 * LANES * q:2 * LANES * (q + 1)], ones) for q in range(RWKV_QUADS)], axis=1)
        yield
        kkn = kk * lax.rsqrt(jnp.maximum(ss, 1e-24))
        kkn_ref[0, rs, :] = kkn.astype(BF16)
        b_ref[0, rs, :] = (kkn * iclr).astype(BF16)

    n_sub = TM_IN // IN_SUB
    _alternate(*[_delayed(sub_tile(s), SUB_TILE_SKEW * s) for s in range(n_sub)])
    carry_ref[...] = carries[n_sub - 1]


def _inproj(x, norm_g, w_in, mu, w2a2, w0, a0, k_k, k_a):
    bsz, seqlen, _ = x.shape
    tok = lambda w: pl.BlockSpec((1, TM_IN, w), lambda b, t: (b, t, 0))
    full = lambda a: pl.BlockSpec(a.shape, lambda b, t: (0,) * a.ndim)
    out = lambda w, dt: jax.ShapeDtypeStruct((bsz, seqlen, w), dt)
    params = (norm_g, w_in, mu, w2a2, w0, a0, k_k, k_a)
    return pl.pallas_call(
        _inproj_kernel,
        grid=(bsz, seqlen // TM_IN),
        in_specs=[tok(D_MODEL)] + [full(a) for a in params],
        out_specs=[tok(D_S5), tok(2 * D_S5)] + [tok(D_RWKV)] * 6,
        out_shape=[out(D_S5, F32), out(2 * D_S5, BF16)]
        + [out(D_RWKV, F32 if i == 1 else BF16) for i in range(6)],
        scratch_shapes=[pltpu.VMEM((1, D_SHIFT), F32)],
        compiler_params=pltpu.CompilerParams(
            dimension_semantics=("arbitrary", "arbitrary"), vmem_limit_bytes=VMEM_LIMIT),
        name="inproj",
    )(x, *params)


def _s5_kernel(u_ref, ein_ref, toep_ref, fout_ref, lam_ref, d_ref, y_ref, e_ref, sp_ref):
    nch = e_ref.shape[1]
    half = S5_BLOCK_STATE // 2
    pairs = S5_CHUNK // 2
    lre = lam_ref[0, :, :half]
    lim = lam_ref[0, :, half:]
    d = d_ref[...]

    def pipeline(bi):
        xs = [u_ref[bi, pl.ds(j, nch, stride=S5_CHUNK), :] for j in range(S5_CHUNK)]
        xcat = jnp.concatenate([x.astype(BF16) for x in xs], axis=1)
        yield
        e_ref[bi] = _dot(xcat, ein_ref[0])
        yield
        intra = {}

        def scan():
            sre = sim = jnp.zeros((1, half), F32)
            for c in range(nch):
                sp_ref[bi, c:c + 1, :half] = sre
                sp_ref[bi, c:c + 1, half:] = sim
                e = e_ref[bi, c:c + 1, :]
                sre, sim = lre * sre - lim * sim + e[:, :half], lre * sim + lim * sre + e[:, half:]
                if (c + 1) % (nch // pairs) == 0:
                    yield

        def toeplitz():
            for tp in range(pairs):
                intra[tp] = _dot(xcat[:, :2 * LANES * (tp + 1)],
                                 toep_ref[0, 2 * LANES * (pairs - 1 - tp):, :])
                yield

        yield from _alternating(scan(), toeplitz())
        sp = sp_ref[bi].astype(BF16)
        for tp in range(pairs):
            y = intra[tp] + _dot(sp, fout_ref[0, :, 2 * LANES * tp:2 * LANES * (tp + 1)])
            for i in range(2):
                t = 2 * tp + i
                y_ref[bi, pl.ds(t, nch, stride=S5_CHUNK), :] = y[:, LANES * i:LANES * (i + 1)] + d * xs[t]
            yield

    _alternate(*[_delayed(pipeline(bi), S5_PIPE_SKEW * bi) for bi in range(S5_BATCH)])


def _cmul(ar, ai, br, bi):
    return ar * br - ai * bi, ar * bi + ai * br


def _block_diag_expand(compact, width, inner, transposed):
    k = compact.shape[1] if transposed else compact.shape[2]
    group_w = S5_GROUPS_PER_BLOCK * inner
    kk = jnp.arange(k)
    ww = jnp.arange(width)
    sel = ((kk[:, None] // inner == ww[None, :] // group_w)
           & (kk[:, None] % inner == ww[None, :] % inner)).astype(F32)
    wide_group = (ww % group_w) // inner
    if transposed:
        full = jnp.matmul(sel.T, compact)
        narrow_group = (jnp.arange(compact.shape[2]) % LANES) // S5_GROUP
        keep = wide_group[:, None] == narrow_group[None, :]
    else:
        full = jnp.matmul(compact, sel)
        narrow_group = (jnp.arange(compact.shape[1]) % LANES) // S5_GROUP
        keep = narrow_group[:, None] == wide_group[None, :]
    return jnp.where(keep, full, 0.0)


def _s5_operators(lam_re, lam_im, log_dt, b_re, b_im, c_re, c_im):
    c16, q4, gl = S5_CHUNK, S5_LANE_BLOCKS, S5_GROUPS_PER_BLOCK
    rows = c16 * LANES
    dt = jnp.exp(log_dt)[:, None]
    n = jnp.arange(c16 + 1, dtype=F32)[:, None, None]
    mag = jnp.exp(n * (lam_re * dt)[None])
    ang = n * (lam_im * dt)[None]
    pw_r, pw_i = mag * jnp.cos(ang), mag * jnp.sin(ang)
    den = lam_re * lam_re + lam_im * lam_im
    q_r, q_i = _cmul(pw_r[1] - 1.0, pw_i[1], lam_re / den, -lam_im / den)
    bt_r, bt_i = jnp.swapaxes(b_re, 1, 2), jnp.swapaxes(b_im, 1, 2)
    bb_r, bb_i = _cmul(q_r[:, None, :], q_i[:, None, :], bt_r, bt_i)
    e_r, e_i = _cmul(pw_r[:c16, :, None, :], pw_i[:c16, :, None, :], bb_r[None], bb_i[None])

    def by_block(a):
        a = a.reshape(a.shape[0], q4, gl * S5_GROUP, 2 * S5_STATE)
        return jnp.swapaxes(a, 0, 1).reshape(q4, -1, 2 * S5_STATE)

    ein = by_block(jnp.concatenate([e_r, e_i], axis=-1)[::-1])
    ein = _block_diag_expand(ein, S5_BLOCK_STATE, S5_STATE, False)

    f_r, f_i = _cmul(c_re[None], c_im[None], pw_r[1:, :, None, :], pw_i[1:, :, None, :])
    fo = jnp.swapaxes(by_block(jnp.concatenate([f_r, -f_i], axis=-1)), 1, 2)
    fo = _block_diag_expand(fo, S5_BLOCK_STATE, S5_STATE, True)

    ee = jnp.concatenate([e_r, -e_i], axis=-1)
    cc = jnp.concatenate([c_re, c_im], axis=-1).reshape(q4, LANES, 2 * S5_STATE)
    row_g = (jnp.arange(rows) % LANES) // S5_GROUP
    col_g = jnp.arange(LANES) // S5_GROUP
    halves = []
    for t2 in range(2):
        lags = [2 * d + t2 - j2 for d in reversed(range(c16 // 2)) for j2 in range(2)]
        a = jnp.stack([ee[lag] if lag >= 0 else jnp.zeros_like(ee[0]) for lag in lags])
        full = jnp.einsum('qrk,qck->qrc', by_block(a), cc, precision=lax.Precision.HIGHEST)
        halves.append(jnp.where(row_g[:, None] == col_g[None, :], full, 0.0))
    toep = jnp.concatenate(halves, axis=-1)

    lam16 = jnp.concatenate([pw_r[c16].reshape(q4, 1, gl * S5_STATE),
                             pw_i[c16].reshape(q4, 1, gl * S5_STATE)], axis=-1)
    return ein.astype(BF16), toep.astype(BF16), fo.astype(BF16), lam16


def _s5(u, ein, toep, fout, lam16, d):
    bsz, seqlen, _ = u.shape
    nch = seqlen // S5_CHUNK
    per_q = lambda a: pl.BlockSpec((1,) + a.shape[1:], lambda q, b: (q,) + (0,) * (a.ndim - 1))
    seq = pl.BlockSpec((S5_BATCH, seqlen, LANES), lambda q, b: (b, 0, q))
    state = pltpu.VMEM((S5_BATCH, nch, S5_BLOCK_STATE), F32)
    return pl.pallas_call(
        _s5_kernel,
        grid=(S5_LANE_BLOCKS, bsz // S5_BATCH),
        in_specs=[seq, per_q(ein), per_q(toep), per_q(fout), per_q(lam16),
                  pl.BlockSpec((1, LANES), lambda q, b: (0, q))],
        out_specs=seq,
        out_shape=jax.ShapeDtypeStruct(u.shape, F32),
        scratch_shapes=[state, state],
        compiler_params=pltpu.CompilerParams(
            dimension_semantics=("arbitrary", "arbitrary"), vmem_limit_bytes=VMEM_LIMIT),
        name="s5",
    )(u, ein, toep, fout, lam16, d)


def _stack_heads(x, lo):
    return jnp.concatenate([jnp.where(lo, x, 0.0), jnp.where(lo, 0.0, x)], axis=0)


def _pair_block_diag(x):
    zero = jnp.zeros((x.shape[0], LANES), x.dtype)
    return jnp.concatenate([jnp.concatenate([x[:, :LANES], zero], axis=1),
                            jnp.concatenate([zero, x[:, LANES:]], axis=1)], axis=0)


def _head_block_diag(x, head_id, order):
    return jnp.concatenate([jnp.where(head_id == h, x, 0.0) for h in order], axis=0)


def _rwkv_kernel(r_ref, lw_ref, k_ref, v_ref, kk_ref, b_ref, gate_ref, rk_ref, lnw_ref, lnb_ref,
                 y_ref, s_ref, wr_ref, w2_ref, bkt_ref, prb_ref, yv_ref, e2_ref, u_ref):
    tc = RWKV_CHUNK
    nchunks = RWKV_TB // tc
    qw = 2 * LANES

    @pl.when(pl.program_id(1) == 0)
    def _():
        s_ref[...] = jnp.zeros_like(s_ref)

    row = lax.broadcasted_iota(jnp.int32, (tc, qw), 0)
    lane = lax.broadcasted_iota(jnp.int32, (tc, qw), 1)
    col = jnp.bitwise_and(lane, RWKV_HEAD - 1)
    head_id = lane // RWKV_HEAD
    heads = tuple(range(qw // RWKV_HEAD))
    heads_swapped = (1, 0, 3, 2)
    eye_q = (row == col).astype(F32)
    row_p = lax.broadcasted_iota(jnp.int32, (tc, LANES), 0)
    lane_p = lax.broadcasted_iota(jnp.int32, (tc, LANES), 1)
    col_p = jnp.bitwise_and(lane_p, RWKV_HEAD - 1)
    lo_p = lane_p < RWKV_HEAD
    lo_p2 = lax.broadcasted_iota(jnp.int32, (2 * tc, LANES), 1) < RWKV_HEAD
    stril = row_p > col_p
    tril = row_p >= col_p
    state_bd = ((lax.broadcasted_iota(jnp.int32, (LANES, qw), 0) // RWKV_HEAD)
                == (jnp.bitwise_and(lax.broadcasted_iota(jnp.int32, (LANES, qw), 1), LANES - 1) // RWKV_HEAD))
    ones = _head_ones(qw).astype(BF16)
    tri = (lax.broadcasted_iota(jnp.int32, (tc, tc), 0)
           >= lax.broadcasted_iota(jnp.int32, (tc, tc), 1)).astype(BF16)
    bf = lambda t: t.astype(BF16)
    up, dn = slice(0, tc), slice(tc, 2 * tc)

    def prepare(chunks):
        units = []
        for c in chunks:
            rows = slice(c * tc, (c + 1) * tc)
            lw_all = lw_ref[0, rows, :]
            g_all = _dot_exact_lhs(tri, lw_all)
            for q in range(RWKV_QUADS):
                qs = slice(qw * q, qw * (q + 1))
                units.append(dict(slot=c % RWKV_RING, q=q, rows=rows, qs=qs, lw=lw_all[:, qs], g=g_all[:, qs]))
        yield
        for d in units:
            rows, qs, g = d["rows"], d["qs"], d["g"]
            m = 0.5 * g[tc - 1:tc, :]
            e_out = jnp.exp(m - g)
            em = jnp.exp(m)
            rt = r_ref[0, rows, qs] * jnp.exp(g - m)
            bt = b_ref[0, rows, qs] * e_out
            kt = k_ref[0, rows, qs] * e_out
            d["em"] = em
            d["at"] = bf(-kk_ref[0, rows, qs] * jnp.exp(g - d["lw"] - m))
            d["rt"], d["bt"], d["kt"] = bf(rt), bf(bt), bf(kt)
            d["rts"] = bf(rt * em)
            d["vsw"] = bf(_head_block_diag(v_ref[0, rows, qs], head_id, heads_swapped))
            bkt, e2 = [], []
            for j in range(2):
                ps = slice(LANES * j, LANES * (j + 1))
                bks = jnp.concatenate([bt[:, ps], kt[:, ps]], axis=0) * em[:, ps]
                bkt.append(bf(jnp.transpose(bks)))
                e2.append(jnp.transpose(jnp.broadcast_to(em[:, ps] * em[:, ps], (LANES, LANES))))
            bkt_ref[d["slot"], d["q"]] = jnp.concatenate(bkt, axis=1)
            e2_ref[d["slot"], d["q"]] = jnp.concatenate(e2, axis=1)
        yield
        for d in units:
            d["o"] = []
            for j in range(2):
                ps = slice(LANES * j, LANES * (j + 1))
                bk = jnp.concatenate([d["bt"][:, ps], d["kt"][:, ps]], axis=0)
                kb = jnp.concatenate([d["kt"][:, ps], d["bt"][:, ps]], axis=0)
                rhs = jnp.concatenate([jnp.where(lo_p2, bk, 0.0), jnp.where(lo_p2, 0.0, kb)], axis=0)
                lhs = jnp.concatenate([d["at"][:, ps], d["rt"][:, ps]], axis=0)
                d["o"].append(_dot_nt(lhs, rhs))
        yield
        for d in units:
            h0 = [o[:, :LANES] for o in d["o"]]
            h1 = [o[:, LANES:] for o in d["o"]]

            def pick(rws, first, second, mask):
                return jnp.concatenate(
                    [jnp.where(mask, jnp.where(lo_p, first[j][rws], second[j][rws]), 0.0) for j in range(2)],
                    axis=1)

            d["ap"] = pick(up, h0, h1, stril)
            h0, h1 = [bf(h) for h in h0], [bf(h) for h in h1]
            d["p_ak"] = pick(up, h1, h0, stril)
            d["p_rk"] = pick(dn, h1, h0, tril)
            prb_ref[d["slot"], d["q"]] = pick(dn, h0, h1, tril)
            d["tinv"] = eye_q + d["ap"]
        yield
        levels = int(math.log2(tc)) - 1
        for d in units:
            ap = bf(d["ap"])
            d["ap"] = bf(_dot(ap, _head_block_diag(ap, head_id, heads)))
        for lvl in range(levels):
            yield
            for d in units:
                rhs = _head_block_diag(d["ap"], head_id, heads)
                if lvl < levels - 1:
                    out = _dot(jnp.concatenate([bf(d["tinv"]), d["ap"]], axis=0), rhs)
                    d["tinv"], d["ap"] = d["tinv"] + out[:tc], bf(out[tc:])
                else:
                    d["tinv"] = d["tinv"] + _dot(bf(d["tinv"]), rhs)
        yield
        for d in units:
            d["tinv"] = bf(d["tinv"])
            out = _dot(jnp.concatenate([d["p_ak"], d["p_rk"]], axis=0), d["vsw"])
            d["xv"] = bf(out[:tc])
            yv_ref[d["slot"], d["q"]] = out[tc:]
        yield
        for d in units:
            w1s, w2 = [], []
            for j in range(2):
                ps = slice(LANES * j, LANES * (j + 1))
                rhs = jnp.concatenate([_stack_heads(d["at"][:, ps], lo_p),
                                       _stack_heads(d["xv"][:, ps], lo_p)], axis=1)
                w = _dot(d["tinv"][:, ps], rhs)
                w1s.append(bf(w[:, :LANES] * d["em"][:, ps]))
                w2.append(w[:, LANES:])
            wr_ref[d["slot"], d["q"]] = jnp.concatenate([jnp.concatenate(w1s, axis=1), d["rts"]], axis=0)
            w2_ref[d["slot"], d["q"]] = jnp.concatenate(w2, axis=1)

    def recur_step(c):
        rows, slot = slice(c * tc, (c + 1) * tc), c % RWKV_RING
        units = [dict(q=q, qs=slice(qw * q, qw * (q + 1))) for q in range(RWKV_QUADS)]
        for d in units:
            d["h0"] = s_ref[d["q"]]
        for d in units:
            out = _dot(wr_ref[slot, d["q"]], _pair_block_diag(bf(d["h0"])))
            d["u"] = out[:tc] + w2_ref[slot, d["q"]]
            d["yr"] = out[tc:]
        yield
        for d in units:
            q = d["q"]
            uv = jnp.concatenate([bf(d["u"]), v_ref[0, rows, d["qs"]]], axis=0)
            upd = _dot(bkt_ref[slot, q], _pair_block_diag(uv))
            s_ref[q] = d["h0"] * e2_ref[slot, q] + jnp.where(state_bd, upd, 0.0)
            u_ref[slot, q] = bf(d["u"])
            yv_ref[slot, q] = yv_ref[slot, q] + d["yr"]

    def emit_step(c):
        rows, slot = slice(c * tc, (c + 1) * tc), c % RWKV_RING
        units = [dict(q=q, qs=slice(qw * q, qw * (q + 1))) for q in range(RWKV_QUADS)]
        for d in units:
            q = d["q"]
            d["y"] = yv_ref[slot, q] + _dot(prb_ref[slot, q], _head_block_diag(u_ref[slot, q], head_id, heads))
        yield
        stacked = []
        for d in units:
            qs = d["qs"]
            stacked += [d["y"], r_ref[0, rows, qs].astype(F32) * k_ref[0, rows, qs] * rk_ref[:, qs]]
        sums = _head_sums(jnp.concatenate(stacked, axis=0), ones)
        for i, d in enumerate(units):
            d["dev"] = d["y"] - sums[2 * i * tc:(2 * i + 1) * tc] * (1.0 / RWKV_HEAD)
            d["bonus"] = sums[(2 * i + 1) * tc:(2 * i + 2) * tc]
        yield
        var = _head_sums(jnp.concatenate([d["dev"] * d["dev"] for d in units], axis=0), ones)
        for i, d in enumerate(units):
            qs = d["qs"]
            rstd = lax.rsqrt(var[i * tc:(i + 1) * tc] * (1.0 / RWKV_HEAD) + GN_EPS)
            yn = d["dev"] * rstd * lnw_ref[:, qs] + lnb_ref[:, qs]
            y_ref[0, rows, qs] = bf((yn + d["bonus"] * v_ref[0, rows, qs]) * gate_ref[0, rows, qs])

    def chain(chunks):
        for c in chunks:
            yield from _alternating(*([emit_step(c - 1)] if c > 0 else []), recur_step(c))

    groups = [list(range(g, g + RWKV_PREP_CHUNKS)) for g in range(0, nchunks, RWKV_PREP_CHUNKS)]
    _alternate(prepare(groups[0]))
    for prev, nxt in zip(groups[:-1], groups[1:]):
        _alternate(prepare(nxt), chain(prev))
    _alternate(chain(groups[-1]))
    _alternate(emit_step(nchunks - 1))


def _rwkv(r, lw, k, v, kkn, b, gate, r_k, ln_w, ln_b):
    bsz, seqlen, _ = r.shape
    nchunks = RWKV_TB // RWKV_CHUNK
    tok = pl.BlockSpec((1, RWKV_TB, D_RWKV), lambda bi, t: (bi, t, 0))
    gate_spec = pl.BlockSpec((1, RWKV_TB, D_RWKV), lambda bi, t: (bi, t, 1))
    vec = pl.BlockSpec((1, D_RWKV), lambda bi, t: (0, 0))
    unit = lambda rows, cols, dt: pltpu.VMEM((RWKV_RING, RWKV_QUADS, rows, cols), dt)
    return pl.pallas_call(
        _rwkv_kernel,
        grid=(bsz, seqlen // RWKV_TB),
        in_specs=[tok] * 6 + [gate_spec] + [vec] * 3,
        out_specs=tok,
        out_shape=jax.ShapeDtypeStruct(r.shape, BF16),
        scratch_shapes=[
            pltpu.VMEM((RWKV_QUADS, LANES, 2 * LANES), F32),
            unit(2 * RWKV_CHUNK, 2 * LANES, BF16),
            unit(RWKV_CHUNK, 2 * LANES, F32),
            unit(LANES, 4 * RWKV_CHUNK, BF16),
            unit(RWKV_CHUNK, 2 * LANES, BF16),
            unit(RWKV_CHUNK, 2 * LANES, F32),
            unit(LANES, 2 * LANES, F32),
            unit(RWKV_CHUNK, 2 * LANES, BF16),
        ],
        compiler_params=pltpu.CompilerParams(
            dimension_semantics=("arbitrary", "arbitrary"), vmem_limit_bytes=VMEM_LIMIT),
        name="rwkv7",
    )(r, lw, k, v, kkn, b, gate, r_k, ln_w, ln_b)


def _out_kernel(y5_ref, gate_ref, yr_ref, x_ref, gluw_ref, glub_ref, wout_ref, fg_ref, o_ref):
    def sub_tile(s):
        rs = slice(OUT_SUB * s, OUT_SUB * (s + 1))
        y = y5_ref[0, rs, :]
        half = 0.5 * y
        y = half + half * jnp.tanh(y * (GELU_C + (GELU_C * 0.044715) * (y * y)))
        yield
        glu = _sigmoid(_dot(y.astype(BF16), gluw_ref[...]) + glub_ref[...])
        ys = y * glu * gate_ref[0, rs, :]
        ycat = jnp.concatenate([ys.astype(BF16), yr_ref[0, rs, :]], axis=-1)
        yield
        xn = x_ref[0, rs, :] + _dot(ycat, wout_ref[...])
        yield
        ms = jnp.mean(xn * xn, axis=-1, keepdims=True)
        o_ref[0, rs, :] = xn * lax.rsqrt(ms + NORM_EPS) * fg_ref[...]

    _alternate(*[_delayed(sub_tile(s), SUB_TILE_SKEW * s) for s in range(TM_OUT // OUT_SUB)])


def _outproj(y5, gate, yr, x, glu_w, glu_b, w_out, final_g):
    bsz, seqlen, _ = x.shape
    tok = lambda w: pl.BlockSpec((1, TM_OUT, w), lambda b, t: (b, t, 0))
    full = lambda a: pl.BlockSpec(a.shape, lambda b, t: (0,) * a.ndim)
    return pl.pallas_call(
        _out_kernel,
        grid=(bsz, seqlen // TM_OUT),
        in_specs=[tok(D_S5), tok(D_S5), tok(D_RWKV), tok(D_MODEL),
                  full(glu_w), full(glu_b), full(w_out), full(final_g)],
        out_specs=tok(D_MODEL),
        out_shape=jax.ShapeDtypeStruct(x.shape, F32),
        compiler_params=pltpu.CompilerParams(
            dimension_semantics=("arbitrary", "arbitrary"), vmem_limit_bytes=VMEM_LIMIT),
        name="outproj",
    )(y5, gate, yr, x, glu_w, glu_b, w_out, final_g)


@jax.jit
def _forward(x, norm_g, w_in, s5_lam_re, s5_lam_im, s5_log_dt, s5_b_re, s5_b_im, s5_c_re, s5_c_im,
             s5_d, s5_glu_w, s5_glu_b, rwkv_mu, rwkv_w0, rwkv_w2, rwkv_a0, rwkv_a2, rwkv_k_k, rwkv_k_a,
             rwkv_r_k, rwkv_ln_w, rwkv_ln_b, w_out, final_g):
    assert w_in.shape[0] == 1, "single-layer block"
    for l in range(1):
        zeros = jnp.zeros((LOW_RANK, D_RWKV), F32)
        w2a2 = jnp.concatenate(
            [jnp.concatenate([rwkv_w2[l], zeros], axis=1), jnp.concatenate([zeros, rwkv_a2[l]], axis=1)],
            axis=0).astype(BF16)
        row = lambda a: a.reshape(1, -1).astype(F32)
        u, gate, r, lw, k, v, kkn, b = _inproj(
            x, row(norm_g[l]), w_in[l].astype(BF16), row(rwkv_mu[l]), w2a2, row(rwkv_w0[l]),
            row(rwkv_a0[l]), row(rwkv_k_k[l]), row(rwkv_k_a[l]))
        ein, toep, fout, lam16 = _s5_operators(
            s5_lam_re[l], s5_lam_im[l], s5_log_dt[l], s5_b_re[l], s5_b_im[l], s5_c_re[l], s5_c_im[l])
        y5 = _s5(u, ein, toep, fout, lam16, row(s5_d[l]))
        yr = _rwkv(r, lw, k, v, kkn, b, gate, row(rwkv_r_k[l]), row(rwkv_ln_w[l]), row(rwkv_ln_b[l]))
        x = _outproj(y5, gate, yr, x, s5_glu_w[l].astype(BF16), row(s5_glu_b[l]),
                     w_out[l].astype(BF16), row(final_g))
    return x


def kernel(x, norm_g, w_in, s5_lam_re, s5_lam_im, s5_log_dt, s5_b_re, s5_b_im, s5_c_re, s5_c_im, s5_d, s5_glu_w, s5_glu_b, rwkv_mu, rwkv_w0, rwkv_w2, rwkv_a0, rwkv_a2, rwkv_k_k, rwkv_k_a, rwkv_r_k, rwkv_ln_w, rwkv_ln_b, w_out, final_g):
    return _forward(x, norm_g, w_in, s5_lam_re, s5_lam_im, s5_log_dt, s5_b_re, s5_b_im, s5_c_re, s5_c_im,
                    s5_d, s5_glu_w, s5_glu_b, rwkv_mu, rwkv_w0, rwkv_w2, rwkv_a0, rwkv_a2, rwkv_k_k,
                    rwkv_k_a, rwkv_r_k, rwkv_ln_w, rwkv_ln_b, w_out, final_g)
```

```python
import math

import jax
import jax.numpy as jnp
from jax import lax
from jax.experimental import pallas as pl
from jax.experimental.pallas import tpu as pltpu

F32 = jnp.float32
BF16 = jnp.bfloat16

D_MODEL = 1024
D_S5 = 512
D_RWKV = 512
S5_GROUP = 16
S5_STATE = 64
RWKV_HEAD = 64
LOW_RANK = 64
D_SHIFT = 3 * D_RWKV + 2 * LOW_RANK
D_IN = 2 * D_S5 + D_SHIFT + D_RWKV
NORM_EPS = 1e-6
GN_EPS = 64e-5
GELU_C = math.sqrt(2.0 / math.pi)

LANES = 128
TM_IN = 1024
IN_SUB = 256
SUB_TILE_SKEW = 2
TM_OUT = 1024
OUT_SUB = 512
S5_CHUNK = 16
S5_BATCH = 2
S5_PIPE_SKEW = 6
S5_LANE_BLOCKS = D_S5 // LANES
S5_GROUPS_PER_BLOCK = LANES // S5_GROUP
S5_BLOCK_STATE = 2 * S5_GROUPS_PER_BLOCK * S5_STATE
RWKV_CHUNK = 64
RWKV_TB = 2048
RWKV_PREP_CHUNKS = 4
RWKV_RING = 3 * RWKV_PREP_CHUNKS
HEAD_PAIRS = D_RWKV // LANES
RWKV_QUADS = HEAD_PAIRS // 2
VMEM_LIMIT = 56 * 1024 * 1024


def _dot(a, b):
    return jnp.dot(a, b, preferred_element_type=F32)


def _dot_nt(a, b):
    return lax.dot_general(a, b, (((1,), (1,)), ((), ())), preferred_element_type=F32)


def _split2(x):
    hi = x.astype(BF16)
    return hi, (x - hi.astype(F32)).astype(BF16)


def _head_sums(x, ones):
    return _dot(x.astype(BF16), ones)


def _cumsum_rows(tri2, x):
    hi, lo = _split2(x)
    return _dot(tri2, jnp.concatenate([hi, lo], axis=0))


def _sigmoid(x):
    return 0.5 + 0.5 * jnp.tanh(0.5 * x)


def _head_ones(width):
    r = lax.broadcasted_iota(jnp.int32, (width, width), 0)
    c = lax.broadcasted_iota(jnp.int32, (width, width), 1)
    return (r // RWKV_HEAD) == (c // RWKV_HEAD)


def _alternating(*stage_generators):
    pending = list(stage_generators)
    while pending:
        for g in list(pending):
            if next(g, StopIteration) is StopIteration:
                pending.remove(g)
            else:
                yield


def _alternate(*stage_generators):
    for _ in _alternating(*stage_generators):
        pass


def _delayed(stage_generator, stages):
    for _ in range(stages):
        yield
    yield from stage_generator


def _inproj_kernel(x_ref, g_ref, win_ref, mu_ref, w2a2_ref, w0_ref, a0_ref, kk_ref, ka_ref,
                   u_ref, gate_ref, r_ref, lw_ref, k_ref, v_ref, kkn_ref, b_ref, carry_ref):
    t = pl.program_id(1)

    @pl.when(t == 0)
    def _():
        carry_ref[...] = jnp.zeros_like(carry_ref)

    ones = _head_ones(2 * LANES).astype(BF16)
    carries = {-1: carry_ref[...]}

    def sub_tile(s):
        rs = slice(IN_SUB * s, IN_SUB * (s + 1))
        x = x_ref[0, rs, :]
        ms = jnp.mean(x * x, axis=-1, keepdims=True)
        h = (x * lax.rsqrt(ms + NORM_EPS) * g_ref[...]).astype(BF16)

        def proj(lo, hi):
            return _dot(h, win_ref[:, lo:hi])

        yield
        u_ref[0, rs, :] = proj(0, D_S5)
        yield
        z5 = proj(D_S5, 2 * D_S5)
        gate_ref[0, rs, :D_S5] = (z5 * _sigmoid(z5)).astype(BF16)
        yield
        zr = proj(2 * D_S5 + D_SHIFT, D_IN)
        gate_ref[0, rs, D_S5:] = (zr * _sigmoid(zr)).astype(BF16)
        yield
        rw = proj(2 * D_S5, 2 * D_S5 + D_SHIFT)
        carries[s] = rw[IN_SUB - 1:IN_SUB, :]
        yield
        prev = pltpu.roll(rw, 1, axis=0)
        row = lax.broadcasted_iota(jnp.int32, rw.shape, 0)
        prev = jnp.where(row == 0, carries[s - 1], prev)
        sh = rw + (prev - rw) * mu_ref[...]
        r = sh[:, 0:D_RWKV]
        k = sh[:, D_RWKV:2 * D_RWKV]
        v = sh[:, 2 * D_RWKV:3 * D_RWKV]
        wa = sh[:, 3 * D_RWKV:]
        lane = lax.broadcasted_iota(jnp.int32, wa.shape, 1)
        wa = jnp.where(lane < LOW_RANK, jnp.tanh(wa), wa)
        za = _dot(wa.astype(BF16), w2a2_ref[...])
        r_ref[0, rs, :] = r.astype(BF16)
        v_ref[0, rs, :] = v.astype(BF16)
        yield
        lw_ref[0, rs, :] = -math.exp(-0.5) * _sigmoid(w0_ref[...] + za[:, :D_RWKV])
        iclr = _sigmoid(a0_ref[...] + za[:, D_RWKV:])
        k_ref[0, rs, :] = (k * (1.0 + (iclr - 1.0) * ka_ref[...])).astype(BF16)
        kk = k * kk_ref[...]
        sq = kk * kk
        ss = jnp.concatenate(
            [_head_sums(sq[:, 2 * LANES * q:2 * LANES * (q + 1)], ones) for q in range(RWKV_QUADS)], axis=1)
        yield
        kkn = kk * lax.rsqrt(jnp.maximum(ss, 1e-24))
        kkn_ref[0, rs, :] = kkn.astype(BF16)
        b_ref[0, rs, :] = (kkn * iclr).astype(BF16)

    n_sub = TM_IN // IN_SUB
    _alternate(*[_delayed(sub_tile(s), SUB_TILE_SKEW * s) for s in range(n_sub)])
    carry_ref[...] = carries[n_sub - 1]


def _inproj(x, norm_g, w_in, mu, w2a2, w0, a0, k_k, k_a):
    bsz, seqlen, _ = x.shape
    tok = lambda w: pl.BlockSpec((1, TM_IN, w), lambda b, t: (b, t, 0))
    full = lambda a: pl.BlockSpec(a.shape, lambda b, t: (0,) * a.ndim)
    out = lambda w, dt: jax.ShapeDtypeStruct((bsz, seqlen, w), dt)
    params = (norm_g, w_in, mu, w2a2, w0, a0, k_k, k_a)
    return pl.pallas_call(
        _inproj_kernel,
        grid=(bsz, seqlen // TM_IN),
        in_specs=[tok(D_MODEL)] + [full(a) for a in params],
        out_specs=[tok(D_S5), tok(2 * D_S5)] + [tok(D_RWKV)] * 6,
        out_shape=[out(D_S5, F32), out(2 * D_S5, BF16)]
        + [out(D_RWKV, F32 if i == 1 else BF16) for i in range(6)],
        scratch_shapes=[pltpu.VMEM((1, D_SHIFT), F32)],
        compiler_params=pltpu.CompilerParams(
            dimension_semantics=("arbitrary", "arbitrary"), vmem_limit_bytes=VMEM_LIMIT),
        name="inproj",
    )(x, *params)


def _s5_kernel(u_ref, ein_ref, toep_ref, fout_ref, lam_ref, d_ref, y_ref, e_ref, sp_ref):
    nch = e_ref.shape[1]
    half = S5_BLOCK_STATE // 2
    pairs = S5_CHUNK // 2
    lre = lam_ref[0, :, :half]
    lim = lam_ref[0, :, half:]
    d = d_ref[...]

    def pipeline(bi):
        xs = [u_ref[bi, pl.ds(j, nch, stride=S5_CHUNK), :] for j in range(S5_CHUNK)]
        xcat = jnp.concatenate([x.astype(BF16) for x in xs], axis=1)
        yield
        e_ref[bi] = _dot(xcat, ein_ref[0])
        yield
        intra = {}

        def scan():
            sre = sim = jnp.zeros((1, half), F32)
            for c in range(nch):
                sp_ref[bi, c:c + 1, :half] = sre
                sp_ref[bi, c:c + 1, half:] = sim
                e = e_ref[bi, c:c + 1, :]
                sre, sim = lre * sre - lim * sim + e[:, :half], lre * sim + lim * sre + e[:, half:]
                if (c + 1) % (nch // pairs) == 0:
                    yield

        def toeplitz():
            for tp in range(pairs):
                intra[tp] = _dot(xcat[:, :2 * LANES * (tp + 1)],
                                 toep_ref[0, 2 * LANES * (pairs - 1 - tp):, :])
                yield

        yield from _alternating(scan(), toeplitz())
        sp = sp_ref[bi].astype(BF16)
        for tp in range(pairs):
            y = intra[tp] + _dot(sp, fout_ref[0, :, 2 * LANES * tp:2 * LANES * (tp + 1)])
            for i in range(2):
                t = 2 * tp + i
                y_ref[bi, pl.ds(t, nch, stride=S5_CHUNK), :] = y[:, LANES * i:LANES * (i + 1)] + d * xs[t]
            yield

    _alternate(*[_delayed(pipeline(bi), S5_PIPE_SKEW * bi) for bi in range(S5_BATCH)])


def _cmul(ar, ai, br, bi):
    return ar * br - ai * bi, ar * bi + ai * br


def _block_diag_expand(compact, width, inner, transposed):
    k = compact.shape[1] if transposed else compact.shape[2]
    group_w = S5_GROUPS_PER_BLOCK * inner
    kk = jnp.arange(k)
    ww = jnp.arange(width)
    sel = ((kk[:, None] // inner == ww[None, :] // group_w)
           & (kk[:, None] % inner == ww[None, :] % inner)).astype(F32)
    wide_group = (ww % group_w) // inner
    if transposed:
        full = jnp.matmul(sel.T, compact)
        narrow_group = (jnp.arange(compact.shape[2]) % LANES) // S5_GROUP
        keep = wide_group[:, None] == narrow_group[None, :]
    else:
        full = jnp.matmul(compact, sel)
        narrow_group = (jnp.arange(compact.shape[1]) % LANES) // S5_GROUP
        keep = narrow_group[:, None] == wide_group[None, :]
    return jnp.where(keep, full, 0.0)


def _s5_operators(lam_re, lam_im, log_dt, b_re, b_im, c_re, c_im):
    c16, q4, gl = S5_CHUNK, S5_LANE_BLOCKS, S5_GROUPS_PER_BLOCK
    rows = c16 * LANES
    dt = jnp.exp(log_dt)[:, None]
    n = jnp.arange(c16 + 1, dtype=F32)[:, None, None]
    mag = jnp.exp(n * (lam_re * dt)[None])
    ang = n * (lam_im * dt)[None]
    pw_r, pw_i = mag * jnp.cos(ang), mag * jnp.sin(ang)
    den = lam_re * lam_re + lam_im * lam_im
    q_r, q_i = _cmul(pw_r[1] - 1.0, pw_i[1], lam_re / den, -lam_im / den)
    bt_r, bt_i = jnp.swapaxes(b_re, 1, 2), jnp.swapaxes(b_im, 1, 2)
    bb_r, bb_i = _cmul(q_r[:, None, :], q_i[:, None, :], bt_r, bt_i)
    e_r, e_i = _cmul(pw_r[:c16, :, None, :], pw_i[:c16, :, None, :], bb_r[None], bb_i[None])

    def by_block(a):
        a = a.reshape(a.shape[0], q4, gl * S5_GROUP, 2 * S5_STATE)
        return jnp.swapaxes(a, 0, 1).reshape(q4, -1, 2 * S5_STATE)

    ein = by_block(jnp.concatenate([e_r, e_i], axis=-1)[::-1])
    ein = _block_diag_expand(ein, S5_BLOCK_STATE, S5_STATE, False)

    f_r, f_i = _cmul(c_re[None], c_im[None], pw_r[1:, :, None, :], pw_i[1:, :, None, :])
    fo = jnp.swapaxes(by_block(jnp.concatenate([f_r, -f_i], axis=-1)), 1, 2)
    fo = _block_diag_expand(fo, S5_BLOCK_STATE, S5_STATE, True)

    ee = jnp.concatenate([e_r, -e_i], axis=-1)
    cc = jnp.concatenate([c_re, c_im], axis=-1).reshape(q4, LANES, 2 * S5_STATE)
    row_g = (jnp.arange(rows) % LANES) // S5_GROUP
    col_g = jnp.arange(LANES) // S5_GROUP
    halves = []
    for t2 in range(2):
        lags = [2 * d + t2 - j2 for d in reversed(range(c16 // 2)) for j2 in range(2)]
        a = jnp.stack([ee[lag] if lag >= 0 else jnp.zeros_like(ee[0]) for lag in lags])
        full = jnp.einsum('qrk,qck->qrc', by_block(a), cc, precision=lax.Precision.HIGHEST)
        halves.append(jnp.where(row_g[:, None] == col_g[None, :], full, 0.0))
    toep = jnp.concatenate(halves, axis=-1)

    lam16 = jnp.concatenate([pw_r[c16].reshape(q4, 1, gl * S5_STATE),
                             pw_i[c16].reshape(q4, 1, gl * S5_STATE)], axis=-1)
    return ein.astype(BF16), toep.astype(BF16), fo.astype(BF16), lam16


def _s5(u, ein, toep, fout, lam16, d):
    bsz, seqlen, _ = u.shape
    nch = seqlen // S5_CHUNK
    per_q = lambda a: pl.BlockSpec((1,) + a.shape[1:], lambda q, b: (q,) + (0,) * (a.ndim - 1))
    seq = pl.BlockSpec((S5_BATCH, seqlen, LANES), lambda q, b: (b, 0, q))
    state = pltpu.VMEM((S5_BATCH, nch, S5_BLOCK_STATE), F32)
    return pl.pallas_call(
        _s5_kernel,
        grid=(S5_LANE_BLOCKS, bsz // S5_BATCH),
        in_specs=[seq, per_q(ein), per_q(toep), per_q(fout), per_q(lam16),
                  pl.BlockSpec((1, LANES), lambda q, b: (0, q))],
        out_specs=seq,
        out_shape=jax.ShapeDtypeStruct(u.shape, F32),
        scratch_shapes=[state, state],
        compiler_params=pltpu.CompilerParams(
            dimension_semantics=("arbitrary", "arbitrary"), vmem_limit_bytes=VMEM_LIMIT),
        name="s5",
    )(u, ein, toep, fout, lam16, d)


def _stack_heads(x, lo):
    return jnp.concatenate([jnp.where(lo, x, 0.0), jnp.where(lo, 0.0, x)], axis=0)


def _pair_block_diag(x):
    zero = jnp.zeros((x.shape[0], LANES), x.dtype)
    return jnp.concatenate([jnp.concatenate([x[:, :LANES], zero], axis=1),
                            jnp.concatenate([zero, x[:, LANES:]], axis=1)], axis=0)


def _head_block_diag(x, head_id, order):
    return jnp.concatenate([jnp.where(head_id == h, x, 0.0) for h in order], axis=0)


def _rwkv_kernel(r_ref, lw_ref, k_ref, v_ref, kk_ref, b_ref, gate_ref, rk_ref, lnw_ref, lnb_ref,
                 y_ref, s_ref, wr_ref, w2_ref, bkt_ref, prb_ref, yv_ref, e2_ref, u_ref):
    tc = RWKV_CHUNK
    nchunks = RWKV_TB // tc
    qw = 2 * LANES

    @pl.when(pl.program_id(1) == 0)
    def _():
        s_ref[...] = jnp.zeros_like(s_ref)

    row = lax.broadcasted_iota(jnp.int32, (tc, qw), 0)
    lane = lax.broadcasted_iota(jnp.int32, (tc, qw), 1)
    col = jnp.bitwise_and(lane, RWKV_HEAD - 1)
    head_id = lane // RWKV_HEAD
    heads = tuple(range(qw // RWKV_HEAD))
    heads_swapped = (1, 0, 3, 2)
    eye_q = (row == col).astype(F32)
    row_p = lax.broadcasted_iota(jnp.int32, (tc, LANES), 0)
    lane_p = lax.broadcasted_iota(jnp.int32, (tc, LANES), 1)
    col_p = jnp.bitwise_and(lane_p, RWKV_HEAD - 1)
    lo_p = lane_p < RWKV_HEAD
    lo_p2 = lax.broadcasted_iota(jnp.int32, (2 * tc, LANES), 1) < RWKV_HEAD
    stril = row_p > col_p
    tril = row_p >= col_p
    state_bd = ((lax.broadcasted_iota(jnp.int32, (LANES, qw), 0) // RWKV_HEAD)
                == (jnp.bitwise_and(lax.broadcasted_iota(jnp.int32, (LANES, qw), 1), LANES - 1) // RWKV_HEAD))
    ones = _head_ones(qw).astype(BF16)
    tri2 = (lax.broadcasted_iota(jnp.int32, (tc, 2 * tc), 0)
            >= jnp.bitwise_and(lax.broadcasted_iota(jnp.int32, (tc, 2 * tc), 1), tc - 1)).astype(BF16)
    bf = lambda t: t.astype(BF16)
    up, dn = slice(0, tc), slice(tc, 2 * tc)

    def prepare(chunks):
        units = []
        for c in chunks:
            rows = slice(c * tc, (c + 1) * tc)
            lw_all = lw_ref[0, rows, :]
            g_all = _cumsum_rows(tri2, lw_all)
            for q in range(RWKV_QUADS):
                qs = slice(qw * q, qw * (q + 1))
                units.append(dict(slot=c % RWKV_RING, q=q, rows=rows, qs=qs, lw=lw_all[:, qs], g=g_all[:, qs]))
        yield
        for d in units:
            rows, qs, g = d["rows"], d["qs"], d["g"]
            m = 0.5 * g[tc - 1:tc, :]
            e_out = jnp.exp(m - g)
            em = jnp.exp(m)
            rt = r_ref[0, rows, qs] * jnp.exp(g - m)
            bt = b_ref[0, rows, qs] * e_out
            kt = k_ref[0, rows, qs] * e_out
            d["em"] = em
            d["at"] = bf(-kk_ref[0, rows, qs] * jnp.exp(g - d["lw"] - m))
            d["rt"], d["bt"], d["kt"] = bf(rt), bf(bt), bf(kt)
            d["rts"] = bf(rt * em)
            d["vsw"] = bf(_head_block_diag(v_ref[0, rows, qs], head_id, heads_swapped))
            bkt, e2 = [], []
            for j in range(2):
                ps = slice(LANES * j, LANES * (j + 1))
                bks = jnp.concatenate([bt[:, ps], kt[:, ps]], axis=0) * em[:, ps]
                bkt.append(bf(jnp.transpose(bks)))
                e2.append(jnp.transpose(jnp.broadcast_to(em[:, ps] * em[:, ps], (LANES, LANES))))
            bkt_ref[d["slot"], d["q"]] = jnp.concatenate(bkt, axis=1)
            e2_ref[d["slot"], d["q"]] = jnp.concatenate(e2, axis=1)
        yield
        for d in units:
            d["o"] = []
            for j in range(2):
                ps = slice(LANES * j, LANES * (j + 1))
                bk = jnp.concatenate([d["bt"][:, ps], d["kt"][:, ps]], axis=0)
                kb = jnp.concatenate([d["kt"][:, ps], d["bt"][:, ps]], axis=0)
                rhs = jnp.concatenate([jnp.where(lo_p2, bk, 0.0), jnp.where(lo_p2, 0.0, kb)], axis=0)
                lhs = jnp.concatenate([d["at"][:, ps], d["rt"][:, ps]], axis=0)
                d["o"].append(_dot_nt(lhs, rhs))
        yield
        for d in units:
            h0 = [o[:, :LANES] for o in d["o"]]
            h1 = [o[:, LANES:] for o in d["o"]]

            def pick(rws, first, second, mask):
                return jnp.concatenate(
                    [jnp.where(mask, jnp.where(lo_p, first[j][rws], second[j][rws]), 0.0) for j in range(2)],
                    axis=1)

            d["ap"] = pick(up, h0, h1, stril)
            h0, h1 = [bf(h) for h in h0], [bf(h) for h in h1]
            d["p_ak"] = pick(up, h1, h0, stril)
            d["p_rk"] = pick(dn, h1, h0, tril)
            prb_ref[d["slot"], d["q"]] = pick(dn, h0, h1, tril)
            d["tinv"] = eye_q + d["ap"]
        yield
        levels = int(math.log2(tc)) - 1
        for d in units:
            ap = bf(d["ap"])
            d["ap"] = bf(_dot(ap, _head_block_diag(ap, head_id, heads)))
        for lvl in range(levels):
            yield
            for d in units:
                rhs = _head_block_diag(d["ap"], head_id, heads)
                if lvl < levels - 1:
                    out = _dot(jnp.concatenate([bf(d["tinv"]), d["ap"]], axis=0), rhs)
                    d["tinv"], d["ap"] = d["tinv"] + out[:tc], bf(out[tc:])
                else:
                    d["tinv"] = d["tinv"] + _dot(bf(d["tinv"]), rhs)
        yield
        for d in units:
            d["tinv"] = bf(d["tinv"])
            out = _dot(jnp.concatenate([d["p_ak"], d["p_rk"]], axis=0), d["vsw"])
            d["xv"] = bf(out[:tc])
            yv_ref[d["slot"], d["q"]] = out[tc:]
        yield
        for d in units:
            w1s, w2 = [], []
            for j in range(2):
                ps = slice(LANES * j, LANES * (j + 1))
                rhs = jnp.concatenate([_stack_heads(d["at"][:, ps], lo_p),
                                       _stack_heads(d["xv"][:, ps], lo_p)], axis=1)
                w = _dot(d["tinv"][:, ps], rhs)
                w1s.append(bf(w[:, :LANES] * d["em"][:, ps]))
                w2.append(w[:, LANES:])
            wr_ref[d["slot"], d["q"]] = jnp.concatenate([jnp.concatenate(w1s, axis=1), d["rts"]], axis=0)
            w2_ref[d["slot"], d["q"]] = jnp.concatenate(w2, axis=1)

    def recur_step(c):
        rows, slot = slice(c * tc, (c + 1) * tc), c % RWKV_RING
        units = [dict(q=q, qs=slice(qw * q, qw * (q + 1))) for q in range(RWKV_QUADS)]
        for d in units:
            d["h0"] = s_ref[d["q"]]
        for d in units:
            out = _dot(wr_ref[slot, d["q"]], _pair_block_diag(bf(d["h0"])))
            d["u"] = out[:tc] + w2_ref[slot, d["q"]]
            d["yr"] = out[tc:]
        yield
        for d in units:
            q = d["q"]
            uv = jnp.concatenate([bf(d["u"]), v_ref[0, rows, d["qs"]]], axis=0)
            upd = _dot(bkt_ref[slot, q], _pair_block_diag(uv))
            s_ref[q] = d["h0"] * e2_ref[slot, q] + jnp.where(state_bd, upd, 0.0)
            u_ref[slot, q] = bf(d["u"])
            yv_ref[slot, q] = yv_ref[slot, q] + d["yr"]

    def emit_step(c):
        rows, slot = slice(c * tc, (c + 1) * tc), c % RWKV_RING
        units = [dict(q=q, qs=slice(qw * q, qw * (q + 1))) for q in range(RWKV_QUADS)]
        for d in units:
            q = d["q"]
            d["y"] = yv_ref[slot, q] + _dot(prb_ref[slot, q], _head_block_diag(u_ref[slot, q], head_id, heads))
        yield
        stacked = []
        for d in units:
            qs = d["qs"]
            stacked += [d["y"], r_ref[0, rows, qs].astype(F32) * k_ref[0, rows, qs] * rk_ref[:, qs]]
        sums = _head_sums(jnp.concatenate(stacked, axis=0), ones)
        for i, d in enumerate(units):
            d["dev"] = d["y"] - sums[2 * i * tc:(2 * i + 1) * tc] * (1.0 / RWKV_HEAD)
            d["bonus"] = sums[(2 * i + 1) * tc:(2 * i + 2) * tc]
        yield
        var = _head_sums(jnp.concatenate([d["dev"] * d["dev"] for d in units], axis=0), ones)
        for i, d in enumerate(units):
            qs = d["qs"]
            rstd = lax.rsqrt(var[i * tc:(i + 1) * tc] * (1.0 / RWKV_HEAD) + GN_EPS)
            yn = d["dev"] * rstd * lnw_ref[:, qs] + lnb_ref[:, qs]
            y_ref[0, rows, qs] = bf((yn + d["bonus"] * v_ref[0, rows, qs]) * gate_ref[0, rows, qs])

    def chain(chunks):
        for c in chunks:
            yield from _alternating(*([emit_step(c - 1)] if c > 0 else []), recur_step(c))

    groups = [list(range(g, g + RWKV_PREP_CHUNKS)) for g in range(0, nchunks, RWKV_PREP_CHUNKS)]
    _alternate(prepare(groups[0]))
    for prev, nxt in zip(groups[:-1], groups[1:]):
        _alternate(prepare(nxt), chain(prev))
    _alternate(chain(groups[-1]))
    _alternate(emit_step(nchunks - 1))


def _rwkv(r, lw, k, v, kkn, b, gate, r_k, ln_w, ln_b):
    bsz, seqlen, _ = r.shape
    nchunks = RWKV_TB // RWKV_CHUNK
    tok = pl.BlockSpec((1, RWKV_TB, D_RWKV), lambda bi, t: (bi, t, 0))
    gate_spec = pl.BlockSpec((1, RWKV_TB, D_RWKV), lambda bi, t: (bi, t, 1))
    vec = pl.BlockSpec((1, D_RWKV), lambda bi, t: (0, 0))
    unit = lambda rows, cols, dt: pltpu.VMEM((RWKV_RING, RWKV_QUADS, rows, cols), dt)
    return pl.pallas_call(
        _rwkv_kernel,
        grid=(bsz, seqlen // RWKV_TB),
        in_specs=[tok] * 6 + [gate_spec] + [vec] * 3,
        out_specs=tok,
        out_shape=jax.ShapeDtypeStruct(r.shape, BF16),
        scratch_shapes=[
            pltpu.VMEM((RWKV_QUADS, LANES, 2 * LANES), F32),
            unit(2 * RWKV_CHUNK, 2 * LANES, BF16),
            unit(RWKV_CHUNK, 2 * LANES, F32),
            unit(LANES, 4 * RWKV_CHUNK, BF16),
            unit(RWKV_CHUNK, 2 * LANES, BF16),
            unit(RWKV_CHUNK, 2 * LANES, F32),
            unit(LANES, 2 * LANES, F32),
            unit(RWKV_CHUNK, 2 * LANES, BF16),
        ],
        compiler_params=pltpu.CompilerParams(
            dimension_semantics=("arbitrary", "arbitrary"), vmem_limit_bytes=VMEM_LIMIT),
        name="rwkv7",
    )(r, lw, k, v, kkn, b, gate, r_k, ln_w, ln_b)


def _out_kernel(y5_ref, gate_ref, yr_ref, x_ref, gluw_ref, glub_ref, wout_ref, fg_ref, o_ref):
    def sub_tile(s):
        rs = slice(OUT_SUB * s, OUT_SUB * (s + 1))
        y = y5_ref[0, rs, :]
        half = 0.5 * y
        y = half + half * jnp.tanh(y * (GELU_C + (GELU_C * 0.044715) * (y * y)))
        yield
        glu = _sigmoid(_dot(y.astype(BF16), gluw_ref[...]) + glub_ref[...])
        ys = y * glu * gate_ref[0, rs, :]
        ycat = jnp.concatenate([ys.astype(BF16), yr_ref[0, rs, :]], axis=-1)
        yield
        xn = x_ref[0, rs, :] + _dot(ycat, wout_ref[...])
        yield
        ms = jnp.mean(xn * xn, axis=-1, keepdims=True)
        o_ref[0, rs, :] = xn * lax.rsqrt(ms + NORM_EPS) * fg_ref[...]

    _alternate(*[_delayed(sub_tile(s), SUB_TILE_SKEW * s) for s in range(TM_OUT // OUT_SUB)])


def _outproj(y5, gate, yr, x, glu_w, glu_b, w_out, final_g):
    bsz, seqlen, _ = x.shape
    tok = lambda w: pl.BlockSpec((1, TM_OUT, w), lambda b, t: (b, t, 0))
    full = lambda a: pl.BlockSpec(a.shape, lambda b, t: (0,) * a.ndim)
    return pl.pallas_call(
        _out_kernel,
        grid=(bsz, seqlen // TM_OUT),
        in_specs=[tok(D_S5), tok(D_S5), tok(D_RWKV), tok(D_MODEL),
                  full(glu_w), full(glu_b), full(w_out), full(final_g)],
        out_specs=tok(D_MODEL),
        out_shape=jax.ShapeDtypeStruct(x.shape, F32),
        compiler_params=pltpu.CompilerParams(
            dimension_semantics=("arbitrary", "arbitrary"), vmem_limit_bytes=VMEM_LIMIT),
        name="outproj",
    )(y5, gate, yr, x, glu_w, glu_b, w_out, final_g)


@jax.jit
def _forward(x, norm_g, w_in, s5_lam_re, s5_lam_im, s5_log_dt, s5_b_re, s5_b_im, s5_c_re, s5_c_im,
             s5_d, s5_glu_w, s5_glu_b, rwkv_mu, rwkv_w0, rwkv_w2, rwkv_a0, rwkv_a2, rwkv_k_k, rwkv_k_a,
             rwkv_r_k, rwkv_ln_w, rwkv_ln_b, w_out, final_g):
    assert w_in.shape[0] == 1, "single-layer block"
    for l in range(1):
        zeros = jnp.zeros((LOW_RANK, D_RWKV), F32)
        w2a2 = jnp.concatenate(
            [jnp.concatenate([rwkv_w2[l], zeros], axis=1), jnp.concatenate([zeros, rwkv_a2[l]], axis=1)],
            axis=0).astype(BF16)
        row = lambda a: a.reshape(1, -1).astype(F32)
        u, gate, r, lw, k, v, kkn, b = _inproj(
            x, row(norm_g[l]), w_in[l].astype(BF16), row(rwkv_mu[l]), w2a2, row(rwkv_w0[l]),
            row(rwkv_a0[l]), row(rwkv_k_k[l]), row(rwkv_k_a[l]))
        ein, toep, fout, lam16 = _s5_operators(
            s5_lam_re[l], s5_lam_im[l], s5_log_dt[l], s5_b_re[l], s5_b_im[l], s5_c_re[l], s5_c_im[l])
        y5 = _s5(u, ein, toep, fout, lam16, row(s5_d[l]))
        yr = _rwkv(r, lw, k, v, kkn, b, gate, row(rwkv_r_k[l]), row(rwkv_ln_w[l]), row(rwkv_ln_b[l]))
        x = _outproj(y5, gate, yr, x, s5_glu_w[l].astype(BF16), row(s5_glu_b[l]),
                     w_out[l].astype(BF16), row(final_g))
    return x


def kernel(x, norm_g, w_in, s5_lam_re, s5_lam_im, s5_log_dt, s5_b_re, s5_b_im, s5_c_re, s5_c_im, s5_d, s5_glu_w, s5_glu_b, rwkv_mu, rwkv_w0, rwkv_w2, rwkv_a0, rwkv_a2, rwkv_k_k, rwkv_k_a, rwkv_r_k, rwkv_ln_w, rwkv_ln_b, w_out, final_g):
    return _forward(x, norm_g, w_in, s5_lam_re, s5_lam_im, s5_log_dt, s5_b_re, s5_b_im, s5_c_re, s5_c_im,
                    s5_d, s5_glu_w, s5_glu_b, rwkv_mu, rwkv_w0, rwkv_w2, rwkv_a0, rwkv_a2, rwkv_k_k,
                    rwkv_k_a, rwkv_r_k, rwkv_ln_w, rwkv_ln_b, w_out, final_g)
```

```python
import math

import jax
import jax.numpy as jnp
from jax import lax
from jax.experimental import pallas as pl
from jax.experimental.pallas import tpu as pltpu

F32 = jnp.float32
BF16 = jnp.bfloat16

D_MODEL = 1024
D_S5 = 512
D_RWKV = 512
S5_GROUP = 16
S5_GROUPS = 32
S5_STATE = 64
RWKV_HEAD = 64
LOW_RANK = 64
D_SHIFT = 3 * D_RWKV + 2 * LOW_RANK
D_IN = 2 * D_S5 + D_SHIFT + D_RWKV
NORM_EPS = 1e-6
GN_EPS = 64e-5
GELU_C = math.sqrt(2.0 / math.pi)

LANES = 128
TM_IN = 1024
IN_SUB = 256
SUB_TILE_SKEW = 2
TM_OUT = 1024
OUT_SUB = 512
S5_CHUNK = 16
S5_BATCH = 2
S5_PIPE_SKEW = 6
S5_LANE_BLOCKS = D_S5 // LANES
S5_GROUPS_PER_BLOCK = LANES // S5_GROUP
S5_BLOCK_STATE = 2 * S5_GROUPS_PER_BLOCK * S5_STATE
RWKV_CHUNK = 64
RWKV_TB = 2048
RWKV_PREP_CHUNKS = 4
RWKV_RING = 3 * RWKV_PREP_CHUNKS
HEAD_PAIRS = D_RWKV // LANES
RWKV_QUADS = HEAD_PAIRS // 2
VMEM_LIMIT = 56 * 1024 * 1024


def _dot(a, b):
    return jnp.dot(a, b, preferred_element_type=F32)


def _dot_nt(a, b):
    return lax.dot_general(a, b, (((1,), (1,)), ((), ())), preferred_element_type=F32)


def _dot_tn(a, b):
    return lax.dot_general(a, b, (((0,), (0,)), ((), ())), preferred_element_type=F32)


def _split2(x):
    hi = x.astype(BF16)
    return hi, (x - hi.astype(F32)).astype(BF16)


def _head_sums(x, ones):
    return _dot(x.astype(BF16), ones)


def _dot_exact_lhs(m, x):
    hi, lo = _split2(x)
    return _dot(m, hi) + _dot(m, lo)


def _sigmoid(x):
    return 0.5 + 0.5 * jnp.tanh(0.5 * x)


def _head_ones(width):
    r = lax.broadcasted_iota(jnp.int32, (width, width), 0)
    c = lax.broadcasted_iota(jnp.int32, (width, width), 1)
    return (r // RWKV_HEAD) == (c // RWKV_HEAD)


def _alternating(*stage_generators):
    pending = list(stage_generators)
    while pending:
        for g in list(pending):
            if next(g, StopIteration) is StopIteration:
                pending.remove(g)
            else:
                yield


def _alternate(*stage_generators):
    for _ in _alternating(*stage_generators):
        pass


def _delayed(stage_generator, stages):
    for _ in range(stages):
        yield
    yield from stage_generator


def _inproj_kernel(x_ref, g_ref, win_ref, mu_ref, w2a2_ref, w0_ref, a0_ref, kk_ref, ka_ref,
                   u_ref, gate_ref, r_ref, lw_ref, k_ref, v_ref, kkn_ref, b_ref, carry_ref):
    t = pl.program_id(1)

    @pl.when(t == 0)
    def _():
        carry_ref[...] = jnp.zeros_like(carry_ref)

    ones = _head_ones(2 * LANES).astype(BF16)
    carries = {-1: carry_ref[...]}

    def sub_tile(s):
        rs = slice(IN_SUB * s, IN_SUB * (s + 1))
        x = x_ref[0, rs, :]
        ms = jnp.mean(x * x, axis=-1, keepdims=True)
        h = (x * lax.rsqrt(ms + NORM_EPS) * g_ref[...]).astype(BF16)

        def proj(lo, hi):
            return _dot(h, win_ref[:, lo:hi])

        yield
        rw = proj(2 * D_S5, 2 * D_S5 + D_SHIFT)
        carries[s] = rw[IN_SUB - 1:IN_SUB, :]
        yield
        u_ref[0, rs, :] = proj(0, D_S5)
        prev = pltpu.roll(rw, 1, axis=0)
        row = lax.broadcasted_iota(jnp.int32, rw.shape, 0)
        prev = jnp.where(row == 0, carries[s - 1], prev)
        sh = rw + (prev - rw) * mu_ref[...]
        r = sh[:, 0:D_RWKV]
        k = sh[:, D_RWKV:2 * D_RWKV]
        v = sh[:, 2 * D_RWKV:3 * D_RWKV]
        wa = sh[:, 3 * D_RWKV:]
        lane = lax.broadcasted_iota(jnp.int32, wa.shape, 1)
        wa = jnp.where(lane < LOW_RANK, jnp.tanh(wa), wa)
        za = _dot(wa.astype(BF16), w2a2_ref[...])
        r_ref[0, rs, :] = r.astype(BF16)
        v_ref[0, rs, :] = v.astype(BF16)
        yield
        z5 = proj(D_S5, 2 * D_S5)
        gate_ref[0, rs, :D_S5] = (z5 * _sigmoid(z5)).astype(BF16)
        lw_ref[0, rs, :] = -math.exp(-0.5) * _sigmoid(w0_ref[...] + za[:, :D_RWKV])
        iclr = _sigmoid(a0_ref[...] + za[:, D_RWKV:])
        k_ref[0, rs, :] = (k * (1.0 + (iclr - 1.0) * ka_ref[...])).astype(BF16)
        kk = k * kk_ref[...]
        sq = kk * kk
        ss = jnp.concatenate(
            [_head_sums(sq[:, 2 * LANES * q:2 * LANES * (q + 1)], ones) for q in range(RWKV_QUADS)], axis=1)
        yield
        zr = proj(2 * D_S5 + D_SHIFT, D_IN)
        gate_ref[0, rs, D_S5:] = (zr * _sigmoid(zr)).astype(BF16)
        kkn = kk * lax.rsqrt(jnp.maximum(ss, 1e-24))
        kkn_ref[0, rs, :] = kkn.astype(BF16)
        b_ref[0, rs, :] = (kkn * iclr).astype(BF16)

    n_sub = TM_IN // IN_SUB
    _alternate(*[_delayed(sub_tile(s), SUB_TILE_SKEW * s) for s in range(n_sub)])
    carry_ref[...] = carries[n_sub - 1]


def _inproj(x, norm_g, w_in, mu, w2a2, w0, a0, k_k, k_a):
    bsz, seqlen, _ = x.shape
    tok = lambda w: pl.BlockSpec((1, TM_IN, w), lambda b, t: (b, t, 0))
    full = lambda a: pl.BlockSpec(a.shape, lambda b, t: (0,) * a.ndim)
    out = lambda w, dt: jax.ShapeDtypeStruct((bsz, seqlen, w), dt)
    params = (norm_g, w_in, mu, w2a2, w0, a0, k_k, k_a)
    return pl.pallas_call(
        _inproj_kernel,
        grid=(bsz, seqlen // TM_IN),
        in_specs=[tok(D_MODEL)] + [full(a) for a in params],
        out_specs=[tok(D_S5), tok(2 * D_S5)] + [tok(D_RWKV)] * 6,
        out_shape=[out(D_S5, F32), out(2 * D_S5, BF16)]
        + [out(D_RWKV, F32 if i == 1 else BF16) for i in range(6)],
        scratch_shapes=[pltpu.VMEM((1, D_SHIFT), F32)],
        compiler_params=pltpu.CompilerParams(
            dimension_semantics=("arbitrary", "arbitrary"), vmem_limit_bytes=VMEM_LIMIT),
        name="inproj",
    )(x, *params)


def _s5_kernel(u_ref, ein_ref, toep_ref, fout_ref, lam_ref, d_ref, y_ref, e_ref, sp_ref):
    nch = e_ref.shape[1]
    half = S5_BLOCK_STATE // 2
    pairs = S5_CHUNK // 2
    lre = lam_ref[0, :, :half]
    lim = lam_ref[0, :, half:]
    d = d_ref[...]

    def pipeline(bi):
        xs = [u_ref[bi, pl.ds(j, nch, stride=S5_CHUNK), :] for j in range(S5_CHUNK)]
        xcat = jnp.concatenate([x.astype(BF16) for x in xs], axis=1)
        yield
        e_ref[bi] = _dot(xcat, ein_ref[0])
        yield
        intra = {}

        def scan():
            sre = sim = jnp.zeros((1, half), F32)
            for c in range(nch):
                sp_ref[bi, c:c + 1, :half] = sre
                sp_ref[bi, c:c + 1, half:] = sim
                e = e_ref[bi, c:c + 1, :]
                sre, sim = lre * sre - lim * sim + e[:, :half], lre * sim + lim * sre + e[:, half:]
                if (c + 1) % (nch // pairs) == 0:
                    yield

        def toeplitz():
            for tp in range(pairs):
                intra[tp] = _dot(xcat[:, :2 * LANES * (tp + 1)],
                                 toep_ref[0, 2 * LANES * (pairs - 1 - tp):, :])
                yield

        yield from _alternating(scan(), toeplitz())
        sp = sp_ref[bi].astype(BF16)
        for tp in range(pairs):
            y = intra[tp] + _dot(sp, fout_ref[0, :, 2 * LANES * tp:2 * LANES * (tp + 1)])
            for i in range(2):
                t = 2 * tp + i
                y_ref[bi, pl.ds(t, nch, stride=S5_CHUNK), :] = y[:, LANES * i:LANES * (i + 1)] + d * xs[t]
            yield

    _alternate(*[_delayed(pipeline(bi), S5_PIPE_SKEW * bi) for bi in range(S5_BATCH)])


def _cmul(ar, ai, br, bi):
    return ar * br - ai * bi, ar * bi + ai * br


def _block_diag_expand(compact, width, inner, transposed):
    k = compact.shape[1] if transposed else compact.shape[2]
    group_w = S5_GROUPS_PER_BLOCK * inner
    kk = jnp.arange(k)
    ww = jnp.arange(width)
    sel = ((kk[:, None] // inner == ww[None, :] // group_w)
           & (kk[:, None] % inner == ww[None, :] % inner)).astype(F32)
    wide_group = (ww % group_w) // inner
    if transposed:
        full = jnp.matmul(sel.T, compact)
        narrow_group = (jnp.arange(compact.shape[2]) % LANES) // S5_GROUP
        keep = wide_group[:, None] == narrow_group[None, :]
    else:
        full = jnp.matmul(compact, sel)
        narrow_group = (jnp.arange(compact.shape[1]) % LANES) // S5_GROUP
        keep = narrow_group[:, None] == wide_group[None, :]
    return jnp.where(keep, full, 0.0)


def _s5_operators(lam_re, lam_im, log_dt, b_re, b_im, c_re, c_im):
    c16, q4, gl = S5_CHUNK, S5_LANE_BLOCKS, S5_GROUPS_PER_BLOCK
    rows = c16 * LANES
    dt = jnp.exp(log_dt)[:, None]
    n = jnp.arange(c16 + 1, dtype=F32)[:, None, None]
    mag = jnp.exp(n * (lam_re * dt)[None])
    ang = n * (lam_im * dt)[None]
    pw_r, pw_i = mag * jnp.cos(ang), mag * jnp.sin(ang)
    den = lam_re * lam_re + lam_im * lam_im
    q_r, q_i = _cmul(pw_r[1] - 1.0, pw_i[1], lam_re / den, -lam_im / den)
    bt_r, bt_i = jnp.swapaxes(b_re, 1, 2), jnp.swapaxes(b_im, 1, 2)
    bb_r, bb_i = _cmul(q_r[:, None, :], q_i[:, None, :], bt_r, bt_i)
    e_r, e_i = _cmul(pw_r[:c16, :, None, :], pw_i[:c16, :, None, :], bb_r[None], bb_i[None])

    def by_block(a):
        a = a.reshape(a.shape[0], q4, gl * S5_GROUP, 2 * S5_STATE)
        return jnp.swapaxes(a, 0, 1).reshape(q4, -1, 2 * S5_STATE)

    ein = by_block(jnp.concatenate([e_r, e_i], axis=-1)[::-1])
    ein = _block_diag_expand(ein, S5_BLOCK_STATE, S5_STATE, False)

    f_r, f_i = _cmul(c_re[None], c_im[None], pw_r[1:, :, None, :], pw_i[1:, :, None, :])
    fo = jnp.swapaxes(by_block(jnp.concatenate([f_r, -f_i], axis=-1)), 1, 2)
    fo = _block_diag_expand(fo, S5_BLOCK_STATE, S5_STATE, True)

    ee = jnp.concatenate([e_r, -e_i], axis=-1)
    cc = jnp.concatenate([c_re, c_im], axis=-1).reshape(q4, LANES, 2 * S5_STATE)
    row_g = (jnp.arange(rows) % LANES) // S5_GROUP
    col_g = jnp.arange(LANES) // S5_GROUP
    halves = []
    for t2 in range(2):
        lags = [2 * d + t2 - j2 for d in reversed(range(c16 // 2)) for j2 in range(2)]
        a = jnp.stack([ee[lag] if lag >= 0 else jnp.zeros_like(ee[0]) for lag in lags])
        full = jnp.einsum('qrk,qck->qrc', by_block(a), cc, precision=lax.Precision.HIGHEST)
        halves.append(jnp.where(row_g[:, None] == col_g[None, :], full, 0.0))
    toep = jnp.concatenate(halves, axis=-1)

    lam16 = jnp.concatenate([pw_r[c16].reshape(q4, 1, gl * S5_STATE),
                             pw_i[c16].reshape(q4, 1, gl * S5_STATE)], axis=-1)
    return ein.astype(BF16), toep.astype(BF16), fo.astype(BF16), lam16


def _s5(u, ein, toep, fout, lam16, d):
    bsz, seqlen, _ = u.shape
    nch = seqlen // S5_CHUNK
    per_q = lambda a: pl.BlockSpec((1,) + a.shape[1:], lambda q, b: (q,) + (0,) * (a.ndim - 1))
    seq = pl.BlockSpec((S5_BATCH, seqlen, LANES), lambda q, b: (b, 0, q))
    state = pltpu.VMEM((S5_BATCH, nch, S5_BLOCK_STATE), F32)
    return pl.pallas_call(
        _s5_kernel,
        grid=(S5_LANE_BLOCKS, bsz // S5_BATCH),
        in_specs=[seq, per_q(ein), per_q(toep), per_q(fout), per_q(lam16),
                  pl.BlockSpec((1, LANES), lambda q, b: (0, q))],
        out_specs=seq,
        out_shape=jax.ShapeDtypeStruct(u.shape, F32),
        scratch_shapes=[state, state],
        compiler_params=pltpu.CompilerParams(
            dimension_semantics=("arbitrary", "arbitrary"), vmem_limit_bytes=VMEM_LIMIT),
        name="s5",
    )(u, ein, toep, fout, lam16, d)


def _stack_heads(x, lo):
    return jnp.concatenate([jnp.where(lo, x, 0.0), jnp.where(lo, 0.0, x)], axis=0)


def _pair_block_diag(x):
    zero = jnp.zeros((x.shape[0], LANES), x.dtype)
    return jnp.concatenate([jnp.concatenate([x[:, :LANES], zero], axis=1),
                            jnp.concatenate([zero, x[:, LANES:]], axis=1)], axis=0)


def _head_block_diag(x, head_id, order):
    return jnp.concatenate([jnp.where(head_id == h, x, 0.0) for h in order], axis=0)


def _rwkv_kernel(r_ref, lw_ref, k_ref, v_ref, kk_ref, b_ref, gate_ref, rk_ref, lnw_ref, lnb_ref,
                 y_ref, s_ref, wr_ref, w2_ref, bkt_ref, prb_ref, yv_ref, e2_ref, u_ref):
    tc = RWKV_CHUNK
    nchunks = RWKV_TB // tc
    qw = 2 * LANES

    @pl.when(pl.program_id(1) == 0)
    def _():
        s_ref[...] = jnp.zeros_like(s_ref)

    row = lax.broadcasted_iota(jnp.int32, (tc, qw), 0)
    lane = lax.broadcasted_iota(jnp.int32, (tc, qw), 1)
    col = jnp.bitwise_and(lane, RWKV_HEAD - 1)
    head_id = lane // RWKV_HEAD
    heads = tuple(range(qw // RWKV_HEAD))
    heads_swapped = (1, 0, 3, 2)
    eye_q = (row == col).astype(F32)
    row_p = lax.broadcasted_iota(jnp.int32, (tc, LANES), 0)
    lane_p = lax.broadcasted_iota(jnp.int32, (tc, LANES), 1)
    col_p = jnp.bitwise_and(lane_p, RWKV_HEAD - 1)
    lo_p = lane_p < RWKV_HEAD
    lo_p2 = lax.broadcasted_iota(jnp.int32, (2 * tc, LANES), 1) < RWKV_HEAD
    stril = row_p > col_p
    tril = row_p >= col_p
    state_bd = ((lax.broadcasted_iota(jnp.int32, (LANES, qw), 0) // RWKV_HEAD)
                == (jnp.bitwise_and(lax.broadcasted_iota(jnp.int32, (LANES, qw), 1), LANES - 1) // RWKV_HEAD))
    ones = _head_ones(qw).astype(BF16)
    tri = (lax.broadcasted_iota(jnp.int32, (tc, tc), 0)
           >= lax.broadcasted_iota(jnp.int32, (tc, tc), 1)).astype(BF16)
    bf = lambda t: t.astype(BF16)
    up, dn = slice(0, tc), slice(tc, 2 * tc)

    def prepare(chunks):
        units = []
        for c in chunks:
            rows = slice(c * tc, (c + 1) * tc)
            lw_all = lw_ref[0, rows, :]
            g_all = _dot_exact_lhs(tri, lw_all)
            for q in range(RWKV_QUADS):
                qs = slice(qw * q, qw * (q + 1))
                units.append(dict(slot=c % RWKV_RING, q=q, rows=rows, qs=qs, lw=lw_all[:, qs], g=g_all[:, qs]))
        yield
        for d in units:
            rows, qs, g = d["rows"], d["qs"], d["g"]
            m = 0.5 * g[tc - 1:tc, :]
            e_out = jnp.exp(m - g)
            em = jnp.exp(m)
            rt = r_ref[0, rows, qs] * jnp.exp(g - m)
            bt = b_ref[0, rows, qs] * e_out
            kt = k_ref[0, rows, qs] * e_out
            d["em"] = em
            d["at"] = bf(-kk_ref[0, rows, qs] * jnp.exp(g - d["lw"] - m))
            d["rt"], d["bt"], d["kt"] = bf(rt), bf(bt), bf(kt)
            d["rts"] = bf(rt * em)
            d["vsw"] = bf(_head_block_diag(v_ref[0, rows, qs], head_id, heads_swapped))
            bkt, e2 = [], []
            for j in range(2):
                ps = slice(LANES * j, LANES * (j + 1))
                bks = jnp.concatenate([bt[:, ps], kt[:, ps]], axis=0) * em[:, ps]
                bkt.append(bf(jnp.transpose(bks)))
                e2.append(jnp.transpose(jnp.broadcast_to(em[:, ps] * em[:, ps], (LANES, LANES))))
            bkt_ref[d["slot"], d["q"]] = jnp.concatenate(bkt, axis=1)
            e2_ref[d["slot"], d["q"]] = jnp.concatenate(e2, axis=1)
        yield
        for d in units:
            d["o"] = []
            for j in range(2):
                ps = slice(LANES * j, LANES * (j + 1))
                bk = jnp.concatenate([d["bt"][:, ps], d["kt"][:, ps]], axis=0)
                kb = jnp.concatenate([d["kt"][:, ps], d["bt"][:, ps]], axis=0)
                rhs = jnp.concatenate([jnp.where(lo_p2, bk, 0.0), jnp.where(lo_p2, 0.0, kb)], axis=0)
                lhs = jnp.concatenate([d["at"][:, ps], d["rt"][:, ps]], axis=0)
                d["o"].append(_dot_nt(lhs, rhs))
        yield
        for d in units:
            h0 = [o[:, :LANES] for o in d["o"]]
            h1 = [o[:, LANES:] for o in d["o"]]

            def pick(rws, first, second, mask):
                return jnp.concatenate(
                    [jnp.where(mask, jnp.where(lo_p, first[j][rws], second[j][rws]), 0.0) for j in range(2)],
                    axis=1)

            d["ap"] = pick(up, h0, h1, stril)
            h0, h1 = [bf(h) for h in h0], [bf(h) for h in h1]
            d["p_ak"] = pick(up, h1, h0, stril)
            d["p_rk"] = pick(dn, h1, h0, tril)
            prb_ref[d["slot"], d["q"]] = pick(dn, h0, h1, tril)
            d["tinv"] = eye_q + d["ap"]
        yield
        levels = int(math.log2(tc)) - 1
        for d in units:
            ap = bf(d["ap"])
            d["ap"] = bf(_dot(ap, _head_block_diag(ap, head_id, heads)))
        for lvl in range(levels):
            yield
            for d in units:
                rhs = _head_block_diag(d["ap"], head_id, heads)
                if lvl < levels - 1:
                    out = _dot(jnp.concatenate([bf(d["tinv"]), d["ap"]], axis=0), rhs)
                    d["tinv"], d["ap"] = d["tinv"] + out[:tc], bf(out[tc:])
                else:
                    d["tinv"] = d["tinv"] + _dot(bf(d["tinv"]), rhs)
        yield
        for d in units:
            d["tinv"] = bf(d["tinv"])
            out = _dot(jnp.concatenate([d["p_ak"], d["p_rk"]], axis=0), d["vsw"])
            d["xv"] = bf(out[:tc])
            yv_ref[d["slot"], d["q"]] = out[tc:]
        yield
        for d in units:
            w1s, w2 = [], []
            for j in range(2):
                ps = slice(LANES * j, LANES * (j + 1))
                rhs = jnp.concatenate([_stack_heads(d["at"][:, ps], lo_p),
                                       _stack_heads(d["xv"][:, ps], lo_p)], axis=1)
                w = _dot(d["tinv"][:, ps], rhs)
                w1s.append(bf(w[:, :LANES] * d["em"][:, ps]))
                w2.append(w[:, LANES:])
            wr_ref[d["slot"], d["q"]] = jnp.concatenate([jnp.concatenate(w1s, axis=1), d["rts"]], axis=0)
            w2_ref[d["slot"], d["q"]] = jnp.concatenate(w2, axis=1)

    def recur_step(c):
        rows, slot = slice(c * tc, (c + 1) * tc), c % RWKV_RING
        units = [dict(q=q, qs=slice(qw * q, qw * (q + 1))) for q in range(RWKV_QUADS)]
        for d in units:
            d["h0"] = s_ref[d["q"]]
        for d in units:
            out = _dot(wr_ref[slot, d["q"]], _pair_block_diag(bf(d["h0"])))
            d["u"] = out[:tc] + w2_ref[slot, d["q"]]
            d["yr"] = out[tc:]
        yield
        for d in units:
            q = d["q"]
            uv = jnp.concatenate([bf(d["u"]), v_ref[0, rows, d["qs"]]], axis=0)
            upd = _dot(bkt_ref[slot, q], _pair_block_diag(uv))
            s_ref[q] = d["h0"] * e2_ref[slot, q] + jnp.where(state_bd, upd, 0.0)
            u_ref[slot, q] = bf(d["u"])
            yv_ref[slot, q] = yv_ref[slot, q] + d["yr"]

    def emit_step(c):
        rows, slot = slice(c * tc, (c + 1) * tc), c % RWKV_RING
        units = [dict(q=q, qs=slice(qw * q, qw * (q + 1))) for q in range(RWKV_QUADS)]
        for d in units:
            q = d["q"]
            d["y"] = yv_ref[slot, q] + _dot(prb_ref[slot, q], _head_block_diag(u_ref[slot, q], head_id, heads))
        yield
        stacked = []
        for d in units:
            qs = d["qs"]
            stacked += [d["y"], r_ref[0, rows, qs].astype(F32) * k_ref[0, rows, qs] * rk_ref[:, qs]]
        sums = _head_sums(jnp.concatenate(stacked, axis=0), ones)
        for i, d in enumerate(units):
            d["dev"] = d["y"] - sums[2 * i * tc:(2 * i + 1) * tc] * (1.0 / RWKV_HEAD)
            d["bonus"] = sums[(2 * i + 1) * tc:(2 * i + 2) * tc]
        yield
        var = _head_sums(jnp.concatenate([d["dev"] * d["dev"] for d in units], axis=0), ones)
        for i, d in enumerate(units):
            qs = d["qs"]
            rstd = lax.rsqrt(var[i * tc:(i + 1) * tc] * (1.0 / RWKV_HEAD) + GN_EPS)
            yn = d["dev"] * rstd * lnw_ref[:, qs] + lnb_ref[:, qs]
            y_ref[0, rows, qs] = bf((yn + d["bonus"] * v_ref[0, rows, qs]) * gate_ref[0, rows, qs])

    def chain(chunks):
        for c in chunks:
            yield from _alternating(*([emit_step(c - 1)] if c > 0 else []), recur_step(c))

    groups = [list(range(g, g + RWKV_PREP_CHUNKS)) for g in range(0, nchunks, RWKV_PREP_CHUNKS)]
    _alternate(prepare(groups[0]))
    for prev, nxt in zip(groups[:-1], groups[1:]):
        _alternate(prepare(nxt), chain(prev))
    _alternate(chain(groups[-1]))
    _alternate(emit_step(nchunks - 1))


def _rwkv(r, lw, k, v, kkn, b, gate, r_k, ln_w, ln_b):
    bsz, seqlen, _ = r.shape
    nchunks = RWKV_TB // RWKV_CHUNK
    tok = pl.BlockSpec((1, RWKV_TB, D_RWKV), lambda bi, t: (bi, t, 0))
    gate_spec = pl.BlockSpec((1, RWKV_TB, D_RWKV), lambda bi, t: (bi, t, 1))
    vec = pl.BlockSpec((1, D_RWKV), lambda bi, t: (0, 0))
    unit = lambda rows, cols, dt: pltpu.VMEM((RWKV_RING, RWKV_QUADS, rows, cols), dt)
    return pl.pallas_call(
        _rwkv_kernel,
        grid=(bsz, seqlen // RWKV_TB),
        in_specs=[tok] * 6 + [gate_spec] + [vec] * 3,
        out_specs=tok,
        out_shape=jax.ShapeDtypeStruct(r.shape, BF16),
        scratch_shapes=[
            pltpu.VMEM((RWKV_QUADS, LANES, 2 * LANES), F32),
            unit(2 * RWKV_CHUNK, 2 * LANES, BF16),
            unit(RWKV_CHUNK, 2 * LANES, F32),
            unit(LANES, 4 * RWKV_CHUNK, BF16),
            unit(RWKV_CHUNK, 2 * LANES, BF16),
            unit(RWKV_CHUNK, 2 * LANES, F32),
            unit(LANES, 2 * LANES, F32),
            unit(RWKV_CHUNK, 2 * LANES, BF16),
        ],
        compiler_params=pltpu.CompilerParams(
            dimension_semantics=("arbitrary", "arbitrary"), vmem_limit_bytes=VMEM_LIMIT),
        name="rwkv7",
    )(r, lw, k, v, kkn, b, gate, r_k, ln_w, ln_b)


def _out_kernel(y5_ref, gate_ref, yr_ref, x_ref, gluw_ref, glub_ref, wout_ref, fg_ref, o_ref):
    def sub_tile(s):
        rs = slice(OUT_SUB * s, OUT_SUB * (s + 1))
        y = y5_ref[0, rs, :]
        half = 0.5 * y
        y = half + half * jnp.tanh(y * (GELU_C + (GELU_C * 0.044715) * (y * y)))
        yield
        glu = _sigmoid(_dot(y.astype(BF16), gluw_ref[...]) + glub_ref[...])
        ys = y * glu * gate_ref[0, rs, :]
        ycat = jnp.concatenate([ys.astype(BF16), yr_ref[0, rs, :]], axis=-1)
        yield
        xn = x_ref[0, rs, :] + _dot(ycat, wout_ref[...])
        yield
        ms = jnp.mean(xn * xn, axis=-1, keepdims=True)
        o_ref[0, rs, :] = xn * lax.rsqrt(ms + NORM_EPS) * fg_ref[...]

    _alternate(*[_delayed(sub_tile(s), SUB_TILE_SKEW * s) for s in range(TM_OUT // OUT_SUB)])


def _outproj(y5, gate, yr, x, glu_w, glu_b, w_out, final_g):
    bsz, seqlen, _ = x.shape
    tok = lambda w: pl.BlockSpec((1, TM_OUT, w), lambda b, t: (b, t, 0))
    full = lambda a: pl.BlockSpec(a.shape, lambda b, t: (0,) * a.ndim)
    return pl.pallas_call(
        _out_kernel,
        grid=(bsz, seqlen // TM_OUT),
        in_specs=[tok(D_S5), tok(D_S5), tok(D_RWKV), tok(D_MODEL),
                  full(glu_w), full(glu_b), full(w_out), full(final_g)],
        out_specs=tok(D_MODEL),
        out_shape=jax.ShapeDtypeStruct(x.shape, F32),
        compiler_params=pltpu.CompilerParams(
            dimension_semantics=("arbitrary", "arbitrary"), vmem_limit_bytes=VMEM_LIMIT),
        name="outproj",
    )(y5, gate, yr, x, glu_w, glu_b, w_out, final_g)


@jax.jit
def _forward(x, norm_g, w_in, s5_lam_re, s5_lam_im, s5_log_dt, s5_b_re, s5_b_im, s5_c_re, s5_c_im,
             s5_d, s5_glu_w, s5_glu_b, rwkv_mu, rwkv_w0, rwkv_w2, rwkv_a0, rwkv_a2, rwkv_k_k, rwkv_k_a,
             rwkv_r_k, rwkv_ln_w, rwkv_ln_b, w_out, final_g):
    assert w_in.shape[0] == 1, "single-layer block"
    for l in range(1):
        zeros = jnp.zeros((LOW_RANK, D_RWKV), F32)
        w2a2 = jnp.concatenate(
            [jnp.concatenate([rwkv_w2[l], zeros], axis=1), jnp.concatenate([zeros, rwkv_a2[l]], axis=1)],
            axis=0).astype(BF16)
        row = lambda a: a.reshape(1, -1).astype(F32)
        u, gate, r, lw, k, v, kkn, b = _inproj(
            x, row(norm_g[l]), w_in[l].astype(BF16), row(rwkv_mu[l]), w2a2, row(rwkv_w0[l]),
            row(rwkv_a0[l]), row(rwkv_k_k[l]), row(rwkv_k_a[l]))
        ein, toep, fout, lam16 = _s5_operators(
            s5_lam_re[l], s5_lam_im[l], s5_log_dt[l], s5_b_re[l], s5_b_im[l], s5_c_re[l], s5_c_im[l])
        y5 = _s5(u, ein, toep, fout, lam16, row(s5_d[l]))
        yr = _rwkv(r, lw, k, v, kkn, b, gate, row(rwkv_r_k[l]), row(rwkv_ln_w[l]), row(rwkv_ln_b[l]))
        x = _outproj(y5, gate, yr, x, s5_glu_w[l].astype(BF16), row(s5_glu_b[l]),
                     w_out[l].astype(BF16), row(final_g))
    return x


def kernel(x, norm_g, w_in, s5_lam_re, s5_lam_im, s5_log_dt, s5_b_re, s5_b_im, s5_c_re, s5_c_im, s5_d, s5_glu_w, s5_glu_b, rwkv_mu, rwkv_w0, rwkv_w2, rwkv_a0, rwkv_a2, rwkv_k_k, rwkv_k_a, rwkv_r_k, rwkv_ln_w, rwkv_ln_b, w_out, final_g):
    return _forward(x, norm_g, w_in, s5_lam_re, s5_lam_im, s5_log_dt, s5_b_re, s5_b_im, s5_c_re, s5_c_im,
                    s5_d, s5_glu_w, s5_glu_b, rwkv_mu, rwkv_w0, rwkv_w2, rwkv_a0, rwkv_a2, rwkv_k_k,
                    rwkv_k_a, rwkv_r_k, rwkv_ln_w, rwkv_ln_b, w_out, final_g)
```

```python
import math

import jax
import jax.numpy as jnp
from jax import lax
from jax.experimental import pallas as pl
from jax.experimental.pallas import tpu as pltpu

F32 = jnp.float32
BF16 = jnp.bfloat16

D_MODEL = 1024
D_S5 = 512
D_RWKV = 512
S5_GROUP = 16
S5_GROUPS = 32
S5_STATE = 64
RWKV_HEAD = 64
LOW_RANK = 64
D_SHIFT = 3 * D_RWKV + 2 * LOW_RANK
D_IN = 2 * D_S5 + D_SHIFT + D_RWKV
NORM_EPS = 1e-6
GN_EPS = 64e-5
GELU_C = math.sqrt(2.0 / math.pi)

LANES = 128
TM_IN = 1024
IN_SUB = 256
SUB_TILE_SKEW = 2
TM_OUT = 2048
OUT_SUB = 512
S5_CHUNK = 16
S5_BATCH = 2
S5_PIPE_SKEW = 6
S5_LANE_BLOCKS = D_S5 // LANES
S5_GROUPS_PER_BLOCK = LANES // S5_GROUP
S5_BLOCK_STATE = 2 * S5_GROUPS_PER_BLOCK * S5_STATE
RWKV_CHUNK = 64
RWKV_TB = 2048
RWKV_PREP_CHUNKS = 4
RWKV_RING = 3 * RWKV_PREP_CHUNKS
HEAD_PAIRS = D_RWKV // LANES
RWKV_QUADS = HEAD_PAIRS // 2
VMEM_LIMIT = 56 * 1024 * 1024


def _dot(a, b):
    return jnp.dot(a, b, preferred_element_type=F32)


def _dot_nt(a, b):
    return lax.dot_general(a, b, (((1,), (1,)), ((), ())), preferred_element_type=F32)


def _dot_tn(a, b):
    return lax.dot_general(a, b, (((0,), (0,)), ((), ())), preferred_element_type=F32)


def _split2(x):
    hi = x.astype(BF16)
    return hi, (x - hi.astype(F32)).astype(BF16)


def _head_sums(x, ones):
    return _dot(x.astype(BF16), ones)


def _dot_exact_lhs(m, x):
    hi, lo = _split2(x)
    return _dot(m, hi) + _dot(m, lo)


def _sigmoid(x):
    return 0.5 + 0.5 * jnp.tanh(0.5 * x)


def _head_ones(width):
    r = lax.broadcasted_iota(jnp.int32, (width, width), 0)
    c = lax.broadcasted_iota(jnp.int32, (width, width), 1)
    return (r // RWKV_HEAD) == (c // RWKV_HEAD)


def _alternating(*stage_generators):
    pending = list(stage_generators)
    while pending:
        for g in list(pending):
            if next(g, StopIteration) is StopIteration:
                pending.remove(g)
            else:
                yield


def _alternate(*stage_generators):
    for _ in _alternating(*stage_generators):
        pass


def _delayed(stage_generator, stages):
    for _ in range(stages):
        yield
    yield from stage_generator


def _inproj_kernel(x_ref, g_ref, win_ref, mu_ref, w2a2_ref, w0_ref, a0_ref, kk_ref, ka_ref,
                   u_ref, gate_ref, r_ref, lw_ref, k_ref, v_ref, kkn_ref, b_ref, carry_ref):
    t = pl.program_id(1)

    @pl.when(t == 0)
    def _():
        carry_ref[...] = jnp.zeros_like(carry_ref)

    ones = _head_ones(2 * LANES).astype(BF16)
    carries = {-1: carry_ref[...]}

    def sub_tile(s):
        rs = slice(IN_SUB * s, IN_SUB * (s + 1))
        x = x_ref[0, rs, :]
        ms = jnp.mean(x * x, axis=-1, keepdims=True)
        h = (x * lax.rsqrt(ms + NORM_EPS) * g_ref[...]).astype(BF16)

        def proj(lo, hi):
            return _dot(h, win_ref[:, lo:hi])

        yield
        u_ref[0, rs, :] = proj(0, D_S5)
        yield
        z5 = proj(D_S5, 2 * D_S5)
        gate_ref[0, rs, :D_S5] = (z5 * _sigmoid(z5)).astype(BF16)
        yield
        zr = proj(2 * D_S5 + D_SHIFT, D_IN)
        gate_ref[0, rs, D_S5:] = (zr * _sigmoid(zr)).astype(BF16)
        yield
        rw = proj(2 * D_S5, 2 * D_S5 + D_SHIFT)
        carries[s] = rw[IN_SUB - 1:IN_SUB, :]
        yield
        prev = pltpu.roll(rw, 1, axis=0)
        row = lax.broadcasted_iota(jnp.int32, rw.shape, 0)
        prev = jnp.where(row == 0, carries[s - 1], prev)
        sh = rw + (prev - rw) * mu_ref[...]
        r = sh[:, 0:D_RWKV]
        k = sh[:, D_RWKV:2 * D_RWKV]
        v = sh[:, 2 * D_RWKV:3 * D_RWKV]
        wa = sh[:, 3 * D_RWKV:]
        lane = lax.broadcasted_iota(jnp.int32, wa.shape, 1)
        wa = jnp.where(lane < LOW_RANK, jnp.tanh(wa), wa)
        za = _dot(wa.astype(BF16), w2a2_ref[...])
        r_ref[0, rs, :] = r.astype(BF16)
        v_ref[0, rs, :] = v.astype(BF16)
        yield
        lw_ref[0, rs, :] = -math.exp(-0.5) * _sigmoid(w0_ref[...] + za[:, :D_RWKV])
        iclr = _sigmoid(a0_ref[...] + za[:, D_RWKV:])
        k_ref[0, rs, :] = (k * (1.0 + (iclr - 1.0) * ka_ref[...])).astype(BF16)
        kk = k * kk_ref[...]
        sq = kk * kk
        ss = jnp.concatenate(
            [_head_sums(sq[:, 2 * LANES * q:2 * LANES * (q + 1)], ones) for q in range(RWKV_QUADS)], axis=1)
        yield
        kkn = kk * lax.rsqrt(jnp.maximum(ss, 1e-24))
        kkn_ref[0, rs, :] = kkn.astype(BF16)
        b_ref[0, rs, :] = (kkn * iclr).astype(BF16)

    n_sub = TM_IN // IN_SUB
    _alternate(*[_delayed(sub_tile(s), SUB_TILE_SKEW * s) for s in range(n_sub)])
    carry_ref[...] = carries[n_sub - 1]


def _inproj(x, norm_g, w_in, mu, w2a2, w0, a0, k_k, k_a):
    bsz, seqlen, _ = x.shape
    tok = lambda w: pl.BlockSpec((1, TM_IN, w), lambda b, t: (b, t, 0))
    full = lambda a: pl.BlockSpec(a.shape, lambda b, t: (0,) * a.ndim, pipeline_mode=pl.Buffered(1))
    out = lambda w, dt: jax.ShapeDtypeStruct((bsz, seqlen, w), dt)
    params = (norm_g, w_in, mu, w2a2, w0, a0, k_k, k_a)
    return pl.pallas_call(
        _inproj_kernel,
        grid=(bsz, seqlen // TM_IN),
        in_specs=[tok(D_MODEL)] + [full(a) for a in params],
        out_specs=[tok(D_S5), tok(2 * D_S5)] + [tok(D_RWKV)] * 6,
        out_shape=[out(D_S5, F32), out(2 * D_S5, BF16)]
        + [out(D_RWKV, F32 if i == 1 else BF16) for i in range(6)],
        scratch_shapes=[pltpu.VMEM((1, D_SHIFT), F32)],
        compiler_params=pltpu.CompilerParams(
            dimension_semantics=("arbitrary", "arbitrary"), vmem_limit_bytes=VMEM_LIMIT),
        name="inproj",
    )(x, *params)


def _s5_kernel(u_ref, ein_ref, toep_ref, fout_ref, lam_ref, d_ref, y_ref, e_ref, sp_ref):
    nch = e_ref.shape[1]
    half = S5_BLOCK_STATE // 2
    pairs = S5_CHUNK // 2
    lre = lam_ref[0, :, :half]
    lim = lam_ref[0, :, half:]
    d = d_ref[...]

    def pipeline(bi):
        xs = [u_ref[bi, pl.ds(j, nch, stride=S5_CHUNK), :] for j in range(S5_CHUNK)]
        xcat = jnp.concatenate([x.astype(BF16) for x in xs], axis=1)
        yield
        e_ref[bi] = _dot(xcat, ein_ref[0])
        yield
        intra = {}

        def scan():
            sre = sim = jnp.zeros((1, half), F32)
            for c in range(nch):
                sp_ref[bi, c:c + 1, :half] = sre
                sp_ref[bi, c:c + 1, half:] = sim
                e = e_ref[bi, c:c + 1, :]
                sre, sim = lre * sre - lim * sim + e[:, :half], lre * sim + lim * sre + e[:, half:]
                if (c + 1) % (nch // pairs) == 0:
                    yield

        def toeplitz():
            for tp in range(pairs):
                intra[tp] = _dot(xcat[:, :2 * LANES * (tp + 1)],
                                 toep_ref[0, 2 * LANES * (pairs - 1 - tp):, :])
                yield

        yield from _alternating(scan(), toeplitz())
        sp = sp_ref[bi].astype(BF16)
        for tp in range(pairs):
            y = intra[tp] + _dot(sp, fout_ref[0, :, 2 * LANES * tp:2 * LANES * (tp + 1)])
            for i in range(2):
                t = 2 * tp + i
                y_ref[bi, pl.ds(t, nch, stride=S5_CHUNK), :] = y[:, LANES * i:LANES * (i + 1)] + d * xs[t]
            yield

    _alternate(*[_delayed(pipeline(bi), S5_PIPE_SKEW * bi) for bi in range(S5_BATCH)])


def _cmul(ar, ai, br, bi):
    return ar * br - ai * bi, ar * bi + ai * br


def _block_diag_expand(compact, width, inner, transposed):
    k = compact.shape[1] if transposed else compact.shape[2]
    group_w = S5_GROUPS_PER_BLOCK * inner
    kk = jnp.arange(k)
    ww = jnp.arange(width)
    sel = ((kk[:, None] // inner == ww[None, :] // group_w)
           & (kk[:, None] % inner == ww[None, :] % inner)).astype(F32)
    wide_group = (ww % group_w) // inner
    if transposed:
        full = jnp.matmul(sel.T, compact)
        narrow_group = (jnp.arange(compact.shape[2]) % LANES) // S5_GROUP
        keep = wide_group[:, None] == narrow_group[None, :]
    else:
        full = jnp.matmul(compact, sel)
        narrow_group = (jnp.arange(compact.shape[1]) % LANES) // S5_GROUP
        keep = narrow_group[:, None] == wide_group[None, :]
    return jnp.where(keep, full, 0.0)


def _s5_operators(lam_re, lam_im, log_dt, b_re, b_im, c_re, c_im):
    c16, q4, gl = S5_CHUNK, S5_LANE_BLOCKS, S5_GROUPS_PER_BLOCK
    rows = c16 * LANES
    dt = jnp.exp(log_dt)[:, None]
    n = jnp.arange(c16 + 1, dtype=F32)[:, None, None]
    mag = jnp.exp(n * (lam_re * dt)[None])
    ang = n * (lam_im * dt)[None]
    pw_r, pw_i = mag * jnp.cos(ang), mag * jnp.sin(ang)
    den = lam_re * lam_re + lam_im * lam_im
    q_r, q_i = _cmul(pw_r[1] - 1.0, pw_i[1], lam_re / den, -lam_im / den)
    bt_r, bt_i = jnp.swapaxes(b_re, 1, 2), jnp.swapaxes(b_im, 1, 2)
    bb_r, bb_i = _cmul(q_r[:, None, :], q_i[:, None, :], bt_r, bt_i)
    e_r, e_i = _cmul(pw_r[:c16, :, None, :], pw_i[:c16, :, None, :], bb_r[None], bb_i[None])

    def by_block(a):
        a = a.reshape(a.shape[0], q4, gl * S5_GROUP, 2 * S5_STATE)
        return jnp.swapaxes(a, 0, 1).reshape(q4, -1, 2 * S5_STATE)

    ein = by_block(jnp.concatenate([e_r, e_i], axis=-1)[::-1])
    ein = _block_diag_expand(ein, S5_BLOCK_STATE, S5_STATE, False)

    f_r, f_i = _cmul(c_re[None], c_im[None], pw_r[1:, :, None, :], pw_i[1:, :, None, :])
    fo = jnp.swapaxes(by_block(jnp.concatenate([f_r, -f_i], axis=-1)), 1, 2)
    fo = _block_diag_expand(fo, S5_BLOCK_STATE, S5_STATE, True)

    ee = jnp.concatenate([e_r, -e_i], axis=-1)
    cc = jnp.concatenate([c_re, c_im], axis=-1).reshape(q4, LANES, 2 * S5_STATE)
    row_g = (jnp.arange(rows) % LANES) // S5_GROUP
    col_g = jnp.arange(LANES) // S5_GROUP
    halves = []
    for t2 in range(2):
        lags = [2 * d + t2 - j2 for d in reversed(range(c16 // 2)) for j2 in range(2)]
        a = jnp.stack([ee[lag] if lag >= 0 else jnp.zeros_like(ee[0]) for lag in lags])
        full = jnp.einsum('qrk,qck->qrc', by_block(a), cc, precision=lax.Precision.HIGHEST)
        halves.append(jnp.where(row_g[:, None] == col_g[None, :], full, 0.0))
    toep = jnp.concatenate(halves, axis=-1)

    lam16 = jnp.concatenate([pw_r[c16].reshape(q4, 1, gl * S5_STATE),
                             pw_i[c16].reshape(q4, 1, gl * S5_STATE)], axis=-1)
    return ein.astype(BF16), toep.astype(BF16), fo.astype(BF16), lam16


def _s5(u, ein, toep, fout, lam16, d):
    bsz, seqlen, _ = u.shape
    nch = seqlen // S5_CHUNK
    per_q = lambda a: pl.BlockSpec((1,) + a.shape[1:], lambda q, b: (q,) + (0,) * (a.ndim - 1))
    seq = pl.BlockSpec((S5_BATCH, seqlen, LANES), lambda q, b: (b, 0, q))
    state = pltpu.VMEM((S5_BATCH, nch, S5_BLOCK_STATE), F32)
    return pl.pallas_call(
        _s5_kernel,
        grid=(S5_LANE_BLOCKS, bsz // S5_BATCH),
        in_specs=[seq, per_q(ein), per_q(toep), per_q(fout), per_q(lam16),
                  pl.BlockSpec((1, LANES), lambda q, b: (0, q))],
        out_specs=seq,
        out_shape=jax.ShapeDtypeStruct(u.shape, F32),
        scratch_shapes=[state, state],
        compiler_params=pltpu.CompilerParams(
            dimension_semantics=("arbitrary", "arbitrary"), vmem_limit_bytes=VMEM_LIMIT),
        name="s5",
    )(u, ein, toep, fout, lam16, d)


def _stack_heads(x, lo):
    return jnp.concatenate([jnp.where(lo, x, 0.0), jnp.where(lo, 0.0, x)], axis=0)


def _pair_block_diag(x):
    zero = jnp.zeros((x.shape[0], LANES), x.dtype)
    return jnp.concatenate([jnp.concatenate([x[:, :LANES], zero], axis=1),
                            jnp.concatenate([zero, x[:, LANES:]], axis=1)], axis=0)


def _head_block_diag(x, head_id, order):
    return jnp.concatenate([jnp.where(head_id == h, x, 0.0) for h in order], axis=0)


def _rwkv_kernel(r_ref, lw_ref, k_ref, v_ref, kk_ref, b_ref, gate_ref, rk_ref, lnw_ref, lnb_ref,
                 y_ref, s_ref, wr_ref, w2_ref, bkt_ref, prb_ref, yv_ref, e2_ref, u_ref):
    tc = RWKV_CHUNK
    nchunks = RWKV_TB // tc
    qw = 2 * LANES

    @pl.when(pl.program_id(1) == 0)
    def _():
        s_ref[...] = jnp.zeros_like(s_ref)

    row = lax.broadcasted_iota(jnp.int32, (tc, qw), 0)
    lane = lax.broadcasted_iota(jnp.int32, (tc, qw), 1)
    col = jnp.bitwise_and(lane, RWKV_HEAD - 1)
    head_id = lane // RWKV_HEAD
    heads = tuple(range(qw // RWKV_HEAD))
    heads_swapped = (1, 0, 3, 2)
    eye_q = (row == col).astype(F32)
    row_p = lax.broadcasted_iota(jnp.int32, (tc, LANES), 0)
    lane_p = lax.broadcasted_iota(jnp.int32, (tc, LANES), 1)
    col_p = jnp.bitwise_and(lane_p, RWKV_HEAD - 1)
    lo_p = lane_p < RWKV_HEAD
    lo_p2 = lax.broadcasted_iota(jnp.int32, (2 * tc, LANES), 1) < RWKV_HEAD
    stril = row_p > col_p
    tril = row_p >= col_p
    state_bd = ((lax.broadcasted_iota(jnp.int32, (LANES, qw), 0) // RWKV_HEAD)
                == (jnp.bitwise_and(lax.broadcasted_iota(jnp.int32, (LANES, qw), 1), LANES - 1) // RWKV_HEAD))
    ones = _head_ones(qw).astype(BF16)
    tri = (lax.broadcasted_iota(jnp.int32, (tc, tc), 0)
           >= lax.broadcasted_iota(jnp.int32, (tc, tc), 1)).astype(BF16)
    bf = lambda t: t.astype(BF16)
    up, dn = slice(0, tc), slice(tc, 2 * tc)

    def prepare(chunks):
        units = []
        for c in chunks:
            rows = slice(c * tc, (c + 1) * tc)
            lw_all = lw_ref[0, rows, :]
            g_all = _dot_exact_lhs(tri, lw_all)
            for q in range(RWKV_QUADS):
                qs = slice(qw * q, qw * (q + 1))
                units.append(dict(slot=c % RWKV_RING, q=q, rows=rows, qs=qs, lw=lw_all[:, qs], g=g_all[:, qs]))
        yield
        for d in units:
            rows, qs, g = d["rows"], d["qs"], d["g"]
            m = 0.5 * g[tc - 1:tc, :]
            e_out = jnp.exp(m - g)
            em = jnp.exp(m)
            rt = r_ref[0, rows, qs] * jnp.exp(g - m)
            bt = b_ref[0, rows, qs] * e_out
            kt = k_ref[0, rows, qs] * e_out
            d["em"] = em
            d["at"] = bf(-kk_ref[0, rows, qs] * jnp.exp(g - d["lw"] - m))
            d["rt"], d["bt"], d["kt"] = bf(rt), bf(bt), bf(kt)
            d["rts"] = bf(rt * em)
            d["vsw"] = bf(_head_block_diag(v_ref[0, rows, qs], head_id, heads_swapped))
            bkt, e2 = [], []
            for j in range(2):
                ps = slice(LANES * j, LANES * (j + 1))
                bks = jnp.concatenate([bt[:, ps], kt[:, ps]], axis=0) * em[:, ps]
                bkt.append(bf(jnp.transpose(bks)))
                e2.append(jnp.transpose(jnp.broadcast_to(em[:, ps] * em[:, ps], (LANES, LANES))))
            bkt_ref[d["slot"], d["q"]] = jnp.concatenate(bkt, axis=1)
            e2_ref[d["slot"], d["q"]] = jnp.concatenate(e2, axis=1)
        yield
        for d in units:
            d["o"] = []
            for j in range(2):
                ps = slice(LANES * j, LANES * (j + 1))
                bk = jnp.concatenate([d["bt"][:, ps], d["kt"][:, ps]], axis=0)
                kb = jnp.concatenate([d["kt"][:, ps], d["bt"][:, ps]], axis=0)
                rhs = jnp.concatenate([jnp.where(lo_p2, bk, 0.0), jnp.where(lo_p2, 0.0, kb)], axis=0)
                lhs = jnp.concatenate([d["at"][:, ps], d["rt"][:, ps]], axis=0)
                d["o"].append(_dot_nt(lhs, rhs))
        yield
        for d in units:
            h0 = [o[:, :LANES] for o in d["o"]]
            h1 = [o[:, LANES:] for o in d["o"]]

            def pick(rws, first, second, mask):
                return jnp.concatenate(
                    [jnp.where(mask, jnp.where(lo_p, first[j][rws], second[j][rws]), 0.0) for j in range(2)],
                    axis=1)

            d["ap"] = pick(up, h0, h1, stril)
            h0, h1 = [bf(h) for h in h0], [bf(h) for h in h1]
            d["p_ak"] = pick(up, h1, h0, stril)
            d["p_rk"] = pick(dn, h1, h0, tril)
            prb_ref[d["slot"], d["q"]] = pick(dn, h0, h1, tril)
            d["tinv"] = eye_q + d["ap"]
        yield
        levels = int(math.log2(tc)) - 1
        for d in units:
            ap = bf(d["ap"])
            d["ap"] = bf(_dot(ap, _head_block_diag(ap, head_id, heads)))
        for lvl in range(levels):
            yield
            for d in units:
                rhs = _head_block_diag(d["ap"], head_id, heads)
                if lvl < levels - 1:
                    out = _dot(jnp.concatenate([bf(d["tinv"]), d["ap"]], axis=0), rhs)
                    d["tinv"], d["ap"] = d["tinv"] + out[:tc], bf(out[tc:])
                else:
                    d["tinv"] = d["tinv"] + _dot(bf(d["tinv"]), rhs)
        yield
        for d in units:
            d["tinv"] = bf(d["tinv"])
            out = _dot(jnp.concatenate([d["p_ak"], d["p_rk"]], axis=0), d["vsw"])
            d["xv"] = bf(out[:tc])
            yv_ref[d["slot"], d["q"]] = out[tc:]
        yield
        for d in units:
            w1s, w2 = [], []
            for j in range(2):
                ps = slice(LANES * j, LANES * (j + 1))
                rhs = jnp.concatenate([_stack_heads(d["at"][:, ps], lo_p),
                                       _stack_heads(d["xv"][:, ps], lo_p)], axis=1)
                w = _dot(d["tinv"][:, ps], rhs)
                w1s.append(bf(w[:, :LANES] * d["em"][:, ps]))
                w2.append(w[:, LANES:])
            wr_ref[d["slot"], d["q"]] = jnp.concatenate([jnp.concatenate(w1s, axis=1), d["rts"]], axis=0)
            w2_ref[d["slot"], d["q"]] = jnp.concatenate(w2, axis=1)

    def recur_step(c):
        rows, slot = slice(c * tc, (c + 1) * tc), c % RWKV_RING
        units = [dict(q=q, qs=slice(qw * q, qw * (q + 1))) for q in range(RWKV_QUADS)]
        for d in units:
            d["h0"] = s_ref[d["q"]]
        for d in units:
            out = _dot(wr_ref[slot, d["q"]], _pair_block_diag(bf(d["h0"])))
            d["u"] = out[:tc] + w2_ref[slot, d["q"]]
            d["yr"] = out[tc:]
        yield
        for d in units:
            q = d["q"]
            uv = jnp.concatenate([bf(d["u"]), v_ref[0, rows, d["qs"]]], axis=0)
            upd = _dot(bkt_ref[slot, q], _pair_block_diag(uv))
            s_ref[q] = d["h0"] * e2_ref[slot, q] + jnp.where(state_bd, upd, 0.0)
            u_ref[slot, q] = bf(d["u"])
            yv_ref[slot, q] = yv_ref[slot, q] + d["yr"]

    def emit_step(c):
        rows, slot = slice(c * tc, (c + 1) * tc), c % RWKV_RING
        units = [dict(q=q, qs=slice(qw * q, qw * (q + 1))) for q in range(RWKV_QUADS)]
        for d in units:
            q = d["q"]
            d["y"] = yv_ref[slot, q] + _dot(prb_ref[slot, q], _head_block_diag(u_ref[slot, q], head_id, heads))
        yield
        stacked = []
        for d in units:
            qs = d["qs"]
            stacked += [d["y"], r_ref[0, rows, qs].astype(F32) * k_ref[0, rows, qs] * rk_ref[:, qs]]
        sums = _head_sums(jnp.concatenate(stacked, axis=0), ones)
        for i, d in enumerate(units):
            d["dev"] = d["y"] - sums[2 * i * tc:(2 * i + 1) * tc] * (1.0 / RWKV_HEAD)
            d["bonus"] = sums[(2 * i + 1) * tc:(2 * i + 2) * tc]
        yield
        var = _head_sums(jnp.concatenate([d["dev"] * d["dev"] for d in units], axis=0), ones)
        for i, d in enumerate(units):
            qs = d["qs"]
            rstd = lax.rsqrt(var[i * tc:(i + 1) * tc] * (1.0 / RWKV_HEAD) + GN_EPS)
            yn = d["dev"] * rstd * lnw_ref[:, qs] + lnb_ref[:, qs]
            y_ref[0, rows, qs] = bf((yn + d["bonus"] * v_ref[0, rows, qs]) * gate_ref[0, rows, qs])

    def chain(chunks):
        for c in chunks:
            yield from _alternating(*([emit_step(c - 1)] if c > 0 else []), recur_step(c))

    groups = [list(range(g, g + RWKV_PREP_CHUNKS)) for g in range(0, nchunks, RWKV_PREP_CHUNKS)]
    _alternate(prepare(groups[0]))
    for prev, nxt in zip(groups[:-1], groups[1:]):
        _alternate(prepare(nxt), chain(prev))
    _alternate(chain(groups[-1]))
    _alternate(emit_step(nchunks - 1))


def _rwkv(r, lw, k, v, kkn, b, gate, r_k, ln_w, ln_b):
    bsz, seqlen, _ = r.shape
    nchunks = RWKV_TB // RWKV_CHUNK
    tok = pl.BlockSpec((1, RWKV_TB, D_RWKV), lambda bi, t: (bi, t, 0))
    gate_spec = pl.BlockSpec((1, RWKV_TB, D_RWKV), lambda bi, t: (bi, t, 1))
    vec = pl.BlockSpec((1, D_RWKV), lambda bi, t: (0, 0))
    unit = lambda rows, cols, dt: pltpu.VMEM((RWKV_RING, RWKV_QUADS, rows, cols), dt)
    return pl.pallas_call(
        _rwkv_kernel,
        grid=(bsz, seqlen // RWKV_TB),
        in_specs=[tok] * 6 + [gate_spec] + [vec] * 3,
        out_specs=tok,
        out_shape=jax.ShapeDtypeStruct(r.shape, BF16),
        scratch_shapes=[
            pltpu.VMEM((RWKV_QUADS, LANES, 2 * LANES), F32),
            unit(2 * RWKV_CHUNK, 2 * LANES, BF16),
            unit(RWKV_CHUNK, 2 * LANES, F32),
            unit(LANES, 4 * RWKV_CHUNK, BF16),
            unit(RWKV_CHUNK, 2 * LANES, BF16),
            unit(RWKV_CHUNK, 2 * LANES, F32),
            unit(LANES, 2 * LANES, F32),
            unit(RWKV_CHUNK, 2 * LANES, BF16),
        ],
        compiler_params=pltpu.CompilerParams(
            dimension_semantics=("arbitrary", "arbitrary"), vmem_limit_bytes=VMEM_LIMIT),
        name="rwkv7",
    )(r, lw, k, v, kkn, b, gate, r_k, ln_w, ln_b)


def _out_kernel(y5_ref, gate_ref, yr_ref, x_ref, gluw_ref, glub_ref, wout_ref, fg_ref, o_ref):
    def sub_tile(s):
        rs = slice(OUT_SUB * s, OUT_SUB * (s + 1))
        y = y5_ref[0, rs, :]
        half = 0.5 * y
        y = half + half * jnp.tanh(y * (GELU_C + (GELU_C * 0.044715) * (y * y)))
        yield
        glu = _sigmoid(_dot(y.astype(BF16), gluw_ref[...]) + glub_ref[...])
        ys = y * glu * gate_ref[0, rs, :]
        ycat = jnp.concatenate([ys.astype(BF16), yr_ref[0, rs, :]], axis=-1)
        yield
        xn = x_ref[0, rs, :] + _dot(ycat, wout_ref[...])
        yield
        ms = jnp.mean(xn * xn, axis=-1, keepdims=True)
        o_ref[0, rs, :] = xn * lax.rsqrt(ms + NORM_EPS) * fg_ref[...]

    _alternate(*[_delayed(sub_tile(s), SUB_TILE_SKEW * s) for s in range(TM_OUT // OUT_SUB)])


def _outproj(y5, gate, yr, x, glu_w, glu_b, w_out, final_g):
    bsz, seqlen, _ = x.shape
    tok = lambda w: pl.BlockSpec((1, TM_OUT, w), lambda b, t: (b, t, 0))
    full = lambda a: pl.BlockSpec(a.shape, lambda b, t: (0,) * a.ndim, pipeline_mode=pl.Buffered(1))
    return pl.pallas_call(
        _out_kernel,
        grid=(bsz, seqlen // TM_OUT),
        in_specs=[tok(D_S5), tok(D_S5), tok(D_RWKV), tok(D_MODEL),
                  full(glu_w), full(glu_b), full(w_out), full(final_g)],
        out_specs=tok(D_MODEL),
        out_shape=jax.ShapeDtypeStruct(x.shape, F32),
        compiler_params=pltpu.CompilerParams(
            dimension_semantics=("arbitrary", "arbitrary"), vmem_limit_bytes=VMEM_LIMIT),
        name="outproj",
    )(y5, gate, yr, x, glu_w, glu_b, w_out, final_g)


@jax.jit
def _forward(x, norm_g, w_in, s5_lam_re, s5_lam_im, s5_log_dt, s5_b_re, s5_b_im, s5_c_re, s5_c_im,
             s5_d, s5_glu_w, s5_glu_b, rwkv_mu, rwkv_w0, rwkv_w2, rwkv_a0, rwkv_a2, rwkv_k_k, rwkv_k_a,
             rwkv_r_k, rwkv_ln_w, rwkv_ln_b, w_out, final_g):
    assert w_in.shape[0] == 1, "single-layer block"
    for l in range(1):
        zeros = jnp.zeros((LOW_RANK, D_RWKV), F32)
        w2a2 = jnp.concatenate(
            [jnp.concatenate([rwkv_w2[l], zeros], axis=1), jnp.concatenate([zeros, rwkv_a2[l]], axis=1)],
            axis=0).astype(BF16)
        row = lambda a: a.reshape(1, -1).astype(F32)
        u, gate, r, lw, k, v, kkn, b = _inproj(
            x, row(norm_g[l]), w_in[l].astype(BF16), row(rwkv_mu[l]), w2a2, row(rwkv_w0[l]),
            row(rwkv_a0[l]), row(rwkv_k_k[l]), row(rwkv_k_a[l]))
        ein, toep, fout, lam16 = _s5_operators(
            s5_lam_re[l], s5_lam_im[l], s5_log_dt[l], s5_b_re[l], s5_b_im[l], s5_c_re[l], s5_c_im[l])
        y5 = _s5(u, ein, toep, fout, lam16, row(s5_d[l]))
        yr = _rwkv(r, lw, k, v, kkn, b, gate, row(rwkv_r_k[l]), row(rwkv_ln_w[l]), row(rwkv_ln_b[l]))
        x = _outproj(y5, gate, yr, x, s5_glu_w[l].astype(BF16), row(s5_glu_b[l]),
                     w_out[l].astype(BF16), row(final_g))
    return x


def kernel(x, norm_g, w_in, s5_lam_re, s5_lam_im, s5_log_dt, s5_b_re, s5_b_im, s5_c_re, s5_c_im, s5_d, s5_glu_w, s5_glu_b, rwkv_mu, rwkv_w0, rwkv_w2, rwkv_a0, rwkv_a2, rwkv_k_k, rwkv_k_a, rwkv_r_k, rwkv_ln_w, rwkv_ln_b, w_out, final_g):
    return _forward(x, norm_g, w_in, s5_lam_re, s5_lam_im, s5_log_dt, s5_b_re, s5_b_im, s5_c_re, s5_c_im,
                    s5_d, s5_glu_w, s5_glu_b, rwkv_mu, rwkv_w0, rwkv_w2, rwkv_a0, rwkv_a2, rwkv_k_k,
                    rwkv_k_a, rwkv_r_k, rwkv_ln_w, rwkv_ln_b, w_out, final_g)
```
